```python
import math
import jax, jax.numpy as jnp
from jax import lax
import numpy as np

D_MODEL = 1024
BATCH = 2
SEQ = 8192
DEPTH = 1
DEC_BATCH = 8
DEC_SEQ = 16
PAST_LEN = 2048

CHUNK = 64
Q_BLOCK = 128
HEAD_V = 128
H_A = D_MODEL // (2 * HEAD_V)
DQK_A = HEAD_V // 2
DV_A = HEAD_V
H_B = D_MODEL // (2 * HEAD_V)
DK_B = HEAD_V
DV_B = HEAD_V
MIX_A = H_A * DV_A
MIX_B = H_B * DV_B
MIX = MIX_A + MIX_B
CONV_W = 4
CONV_CH = H_B * (2 * DK_B + DV_B)
IN_COLS = 2 * (H_A * 2 * DQK_A) + MIX_A + CONV_CH + MIX_B + 2 * H_B
ALIBI_MAX = 8.0
N_EXPERTS = 32
TOP_K = 4
D_FF = D_MODEL
SWIGLU_LIMIT = 7.0
SWIGLU_ALPHA = 1.702
MOE_BLOCK = 128
DEEPNORM_ALPHA = (2 * DEPTH) ** 0.25
DEEPNORM_BETA = (8 * DEPTH) ** -0.25
LN_EPS = 1e-5
SUBLN_EPS = 1e-5
GATED_NORM_EPS = 1e-6

kernel_name = 'hymba_diffattn_gdn_moe_stream_step'


def _layernorm(x, g, b):
    xf = x.astype(jnp.float32)
    mu = jnp.mean(xf, axis=-1, keepdims=True)
    var = jnp.mean(jnp.square(xf - mu), axis=-1, keepdims=True)
    return ((xf - mu) * lax.rsqrt(var + LN_EPS) * g.astype(jnp.float32) + b.astype(jnp.float32)).astype(x.dtype)


def _rmsnorm(x, w, eps):
    xf = x.astype(jnp.float32)
    return xf * lax.rsqrt(jnp.mean(jnp.square(xf), axis=-1, keepdims=True) + eps) * w.astype(jnp.float32)


def _l2norm(x):
    return x * lax.rsqrt(jnp.sum(jnp.square(x), axis=-1, keepdims=True) + 1e-6)


def _diff_attn_block(q, k, v, q_pos, k_pos, lam):
    slopes = 2.0 ** (-ALIBI_MAX * jnp.arange(1, H_A + 1, dtype=jnp.float32) / H_A)
    qf = q.astype(jnp.float32) * (DQK_A ** -0.5)
    kf = k.astype(jnp.float32)
    rel = jnp.abs(q_pos[:, None] - k_pos[None, :]).astype(jnp.float32)
    visible = (k_pos[None, :] // CHUNK) <= (q_pos[:, None] // CHUNK)
    bias = jnp.where(visible[None], -slopes[:, None, None] * rel[None], -jnp.inf)
    s1 = jnp.einsum('bqhd,bshd->bhqs', qf[..., :DQK_A], kf[..., :DQK_A]) + bias
    s2 = jnp.einsum('bqhd,bshd->bhqs', qf[..., DQK_A:], kf[..., DQK_A:]) + bias
    p = jax.nn.softmax(s1, axis=-1) - lam * jax.nn.softmax(s2, axis=-1)
    return jnp.einsum('bhqs,bshd->bqhd', p, v.astype(jnp.float32))


def _diff_attention(q, k, v, q_pos, k_pos, lam):
    bsz, L = q.shape[0], q.shape[1]
    if L > Q_BLOCK and L % Q_BLOCK == 0:
        nb = L // Q_BLOCK
        qb = jnp.swapaxes(q.reshape(bsz, nb, Q_BLOCK, H_A, 2 * DQK_A), 0, 1)
        pb = q_pos.reshape(nb, Q_BLOCK)
        out = lax.map(lambda a: _diff_attn_block(a[0], k, v, a[1], k_pos, lam), (qb, pb))
        return jnp.swapaxes(out, 0, 1).reshape(bsz, L, H_A, DV_A)
    return _diff_attn_block(q, k, v, q_pos, k_pos, lam)


def _gated_delta(q, k, v, beta, g, s0, chunk):
    bsz, L, H = q.shape[0], q.shape[1], q.shape[2]
    dv = v.shape[-1]
    nc = L // chunk

    def blocks(t):
        t = t.astype(jnp.float32).reshape((bsz, nc, chunk) + t.shape[2:])
        return jnp.swapaxes(t, 2, 3)

    q, k, v, beta, g = (blocks(t) for t in (q, k, v, beta, g))
    gc = jnp.cumsum(g, axis=-1)
    i = jnp.arange(chunk)
    incl = i[:, None] >= i[None, :]
    strict = i[:, None] > i[None, :]
    gamma = jnp.exp(jnp.where(incl, gc[..., :, None] - gc[..., None, :], -jnp.inf))
    kb = k * beta[..., None]
    a = jnp.where(strict, jnp.einsum('bnhid,bnhjd->bnhij', kb, k) * gamma, 0.0)
    rhs = jnp.concatenate([v * beta[..., None], kb * jnp.exp(gc)[..., None]], axis=-1)
    sol = lax.linalg.triangular_solve(a + jnp.eye(chunk, dtype=jnp.float32), rhs,
                                      left_side=True, lower=True, unit_diagonal=True)
    u, w = sol[..., :dv], sol[..., dv:]
    qk = jnp.einsum('bnhid,bnhjd->bnhij', q, k) * gamma
    q_dec = q * jnp.exp(gc)[..., None]
    k_dec = k * jnp.exp(gc[..., -1:] - gc)[..., None]
    g_last = jnp.exp(gc[..., -1])

    def step(s, inp):
        u_c, w_c, qk_c, qd_c, kd_c, gl_c = inp
        v_new = u_c - jnp.einsum('bhcd,bhde->bhce', w_c, s)
        o = jnp.einsum('bhcd,bhde->bhce', qd_c, s) + jnp.einsum('bhij,bhje->bhie', qk_c, v_new)
        s = s * gl_c[..., None, None] + jnp.einsum('bhcd,bhce->bhde', kd_c, v_new)
        return s, o

    xs = tuple(jnp.moveaxis(t, 1, 0) for t in (u, w, qk, q_dec, k_dec, g_last))
    s_final, o = lax.scan(step, s0.astype(jnp.float32), xs)
    o = o.transpose(1, 0, 3, 2, 4).reshape(bsz, L, H, dv)
    return o, s_final


def _moe(x2d, router_w, router_b, w_gu, b_gu, w_down, b_down):
    T, D = x2d.shape
    logits = jnp.dot(x2d, router_w).astype(jnp.float32) + router_b.astype(jnp.float32)
    top_val, top_idx = lax.top_k(logits, TOP_K)
    gates = jax.nn.softmax(top_val, axis=-1)
    n = T * TOP_K
    flat_e = top_idx.reshape(n)
    flat_t = jnp.arange(n, dtype=jnp.int32) // TOP_K
    flat_g = gates.reshape(n)
    order = jnp.argsort(flat_e)
    se, st, sg = flat_e[order], flat_t[order], flat_g[order]
    counts = jnp.zeros((N_EXPERTS,), jnp.int32).at[flat_e].add(1)
    padded = (counts + MOE_BLOCK - 1) // MOE_BLOCK * MOE_BLOCK
    pad_end = jnp.cumsum(padded)
    pad_start = pad_end - padded
    start = jnp.cumsum(counts) - counts
    dest = pad_start[se] + (jnp.arange(n, dtype=jnp.int32) - start[se])
    n_blocks = -(-n // MOE_BLOCK) + N_EXPERTS
    rows = n_blocks * MOE_BLOCK
    buf_t = jnp.zeros((rows,), jnp.int32).at[dest].set(st)
    buf_g = jnp.zeros((rows,), jnp.float32).at[dest].set(sg)
    blk_start = jnp.arange(n_blocks, dtype=jnp.int32) * MOE_BLOCK
    blk_e = jnp.minimum(jnp.searchsorted(pad_end, blk_start, side='right'), N_EXPERTS - 1)
    xb = x2d[buf_t].reshape(n_blocks, MOE_BLOCK, D)

    def expert_block(args):
        xe, e = args
        h = jnp.dot(xe, w_gu[e]) + b_gu[e]
        gate = jnp.minimum(h[:, :D_FF], SWIGLU_LIMIT)
        up = jnp.clip(h[:, D_FF:], -SWIGLU_LIMIT, SWIGLU_LIMIT)
        act = (up + 1.0) * (gate * jax.nn.sigmoid(SWIGLU_ALPHA * gate))
        return jnp.dot(act, w_down[e]) + b_down[e]

    yb = lax.map(expert_block, (xb, blk_e))
    y = jnp.zeros((T, D), jnp.float32).at[buf_t].add(yb.reshape(rows, D).astype(jnp.float32) * buf_g[:, None])
    return y.astype(x2d.dtype)


def _layer(x, k_past, v_past, conv_past, s_past, lam_init,
           w_in, conv_w, a_log, dt_bias, delta_norm_w,
           lambda_q1, lambda_k1, lambda_q2, lambda_k2, subln_w, w_out, ln1_g, ln1_b,
           router_w, router_b, w_gu, b_gu, w_down, b_down, ln2_g, ln2_b):
    bsz, L, _ = x.shape
    P = k_past.shape[1]
    h = jnp.dot(x, w_in)
    sizes = (2 * H_A * DQK_A, 2 * H_A * DQK_A, MIX_A, CONV_CH, MIX_B, H_B)
    cuts = [sum(sizes[:j + 1]) for j in range(len(sizes))]
    a_q, a_k, a_v, d_qkv, d_z, d_a, d_b = jnp.split(h, cuts, axis=-1)

    q = a_q.reshape(bsz, L, H_A, 2 * DQK_A)
    k = a_k.reshape(bsz, L, H_A, 2 * DQK_A)
    v = a_v.reshape(bsz, L, H_A, DV_A)
    k_all = jnp.concatenate([k_past.astype(k.dtype), k], axis=1)
    v_all = jnp.concatenate([v_past.astype(v.dtype), v], axis=1)
    q_pos = P + jnp.arange(L, dtype=jnp.int32)
    k_pos = jnp.arange(P + L, dtype=jnp.int32)
    lam = (jnp.exp(jnp.sum(lambda_q1.astype(jnp.float32) * lambda_k1.astype(jnp.float32)))
           - jnp.exp(jnp.sum(lambda_q2.astype(jnp.float32) * lambda_k2.astype(jnp.float32))) + lam_init)
    att = _diff_attention(q, k_all, v_all, q_pos, k_pos, lam)
    att = _rmsnorm(att, subln_w, SUBLN_EPS) * (1.0 - lam_init)

    conv_in = jnp.concatenate([conv_past.astype(x.dtype), d_qkv], axis=1)
    conv_out = jax.nn.silu(sum(conv_in[:, j:j + L] * conv_w[j] for j in range(CONV_W)).astype(jnp.float32))
    dq, dk, dv = jnp.split(conv_out, [H_B * DK_B, 2 * H_B * DK_B], axis=-1)
    dq = _l2norm(dq.reshape(bsz, L, H_B, DK_B)) * (DK_B ** -0.5)
    dk = _l2norm(dk.reshape(bsz, L, H_B, DK_B))
    dv = dv.reshape(bsz, L, H_B, DV_B)
    beta = jax.nn.sigmoid(d_b.astype(jnp.float32))
    g = -jnp.exp(a_log.astype(jnp.float32)) * jax.nn.softplus(d_a.astype(jnp.float32) + dt_bias.astype(jnp.float32))
    chunk = CHUNK if L % CHUNK == 0 else L
    o, s_new = _gated_delta(dq, dk, dv, beta, g, s_past, chunk)
    z = d_z.reshape(bsz, L, H_B, DV_B).astype(jnp.float32)
    o = _rmsnorm(o, delta_norm_w, GATED_NORM_EPS) * jax.nn.silu(z)

    mix = jnp.concatenate([att.reshape(bsz, L, MIX_A), o.reshape(bsz, L, MIX_B)], axis=-1).astype(x.dtype)
    x = _layernorm(DEEPNORM_ALPHA * x + jnp.dot(mix, w_out), ln1_g, ln1_b)
    y = _moe(x.reshape(bsz * L, D_MODEL), router_w, router_b, w_gu, b_gu, w_down, b_down).reshape(bsz, L, D_MODEL)
    x = _layernorm(DEEPNORM_ALPHA * x + y, ln2_g, ln2_b)
    conv_new = conv_in[:, conv_in.shape[1] - (CONV_W - 1):]
    return x, k, v, conv_new, s_new


def setup_inputs(seed: int = 0) -> dict:
    key = jax.random.key(seed)
    ks = jax.random.split(key, 32)
    f32 = jnp.float32

    def nrm(k, shape, s):
        return jax.random.normal(k, shape, f32) * s

    dt = jnp.exp(jax.random.uniform(ks[9], (DEPTH, H_B), f32, math.log(1e-3), math.log(1e-1)))
    return {
        'x_prompt': nrm(ks[0], (BATCH, SEQ, D_MODEL), 1.0),
        'x_sample': nrm(ks[1], (DEC_BATCH, DEC_SEQ, D_MODEL), 1.0),
        'cache_k': nrm(ks[2], (DEPTH, DEC_BATCH, PAST_LEN, H_A, 2 * DQK_A), 1.0),
        'cache_v': nrm(ks[3], (DEPTH, DEC_BATCH, PAST_LEN, H_A, DV_A), 1.0),
        'state_conv': nrm(ks[4], (DEPTH, DEC_BATCH, CONV_W - 1, CONV_CH), 1.0),
        'state_delta': nrm(ks[5], (DEPTH, DEC_BATCH, H_B, DK_B, DV_B), DK_B ** -0.5),
        'w_in': nrm(ks[6], (DEPTH, D_MODEL, IN_COLS), D_MODEL ** -0.5),
        'conv_w': nrm(ks[7], (DEPTH, CONV_W, CONV_CH), CONV_W ** -0.5),
        'a_log': jnp.log(jax.random.uniform(ks[8], (DEPTH, H_B), f32, 1.0, 16.0)),
        'dt_bias': dt + jnp.log(-jnp.expm1(-dt)),
        'delta_norm_w': 1.0 + nrm(ks[10], (DEPTH, DV_B), 0.02),
        'lambda_q1': nrm(ks[11], (DEPTH, DQK_A), 0.1),
        'lambda_k1': nrm(ks[12], (DEPTH, DQK_A), 0.1),
        'lambda_q2': nrm(ks[13], (DEPTH, DQK_A), 0.1),
        'lambda_k2': nrm(ks[14], (DEPTH, DQK_A), 0.1),
        'subln_w': 1.0 + nrm(ks[15], (DEPTH, DV_A), 0.02),
        'w_out': nrm(ks[16], (DEPTH, MIX, D_MODEL), (MIX ** -0.5) * DEEPNORM_BETA),
        'ln1_g': 1.0 + nrm(ks[17], (DEPTH, D_MODEL), 0.02),
        'ln1_b': nrm(ks[18], (DEPTH, D_MODEL), 0.02),
        'router_w': nrm(ks[19], (DEPTH, D_MODEL, N_EXPERTS), D_MODEL ** -0.5),
        'router_b': nrm(ks[20], (DEPTH, N_EXPERTS), 0.01),
        'w_gu': nrm(ks[21], (DEPTH, N_EXPERTS, D_MODEL, 2 * D_FF), D_MODEL ** -0.5),
        'b_gu': nrm(ks[22], (DEPTH, N_EXPERTS, 2 * D_FF), 0.02),
        'w_down': nrm(ks[23], (DEPTH, N_EXPERTS, D_FF, D_MODEL), (D_FF ** -0.5) * DEEPNORM_BETA),
        'b_down': nrm(ks[24], (DEPTH, N_EXPERTS, D_MODEL), 0.02),
        'ln2_g': 1.0 + nrm(ks[25], (DEPTH, D_MODEL), 0.02),
        'ln2_b': nrm(ks[26], (DEPTH, D_MODEL), 0.02),
    }


def reference(x_prompt, x_sample, cache_k, cache_v, state_conv, state_delta,
              w_in, conv_w, a_log, dt_bias, delta_norm_w,
              lambda_q1, lambda_k1, lambda_q2, lambda_k2, subln_w, w_out, ln1_g, ln1_b,
              router_w, router_b, w_gu, b_gu, w_down, b_down, ln2_g, ln2_b):
    yp, ys = x_prompt, x_sample
    empty_k = jnp.zeros((BATCH, 0, H_A, 2 * DQK_A), x_prompt.dtype)
    empty_v = jnp.zeros((BATCH, 0, H_A, DV_A), x_prompt.dtype)
    zero_conv = jnp.zeros((BATCH, CONV_W - 1, CONV_CH), x_prompt.dtype)
    zero_s = jnp.zeros((BATCH, H_B, DK_B, DV_B), jnp.float32)
    kp_l, vp_l, cp_l, sp_l, ks_l, vs_l, cs_l, ss_l = [], [], [], [], [], [], [], []
    for l in range(DEPTH):
        lam_init = 0.8 - 0.6 * math.exp(-0.3 * l)
        w = (w_in[l], conv_w[l], a_log[l], dt_bias[l], delta_norm_w[l],
             lambda_q1[l], lambda_k1[l], lambda_q2[l], lambda_k2[l], subln_w[l], w_out[l], ln1_g[l], ln1_b[l],
             router_w[l], router_b[l], w_gu[l], b_gu[l], w_down[l], b_down[l], ln2_g[l], ln2_b[l])
        yp, kp, vp, cp, sp = _layer(yp, empty_k, empty_v, zero_conv, zero_s, lam_init, *w)
        ys, kk, vv, cc, ss = _layer(ys, cache_k[l], cache_v[l], state_conv[l], state_delta[l], lam_init, *w)
        kp_l.append(kp); vp_l.append(vp); cp_l.append(cp); sp_l.append(sp)
        ks_l.append(kk); vs_l.append(vv); cs_l.append(cc); ss_l.append(ss)
    return (yp, ys,
            jnp.stack(kp_l), jnp.stack(vp_l), jnp.stack(cp_l), jnp.stack(sp_l).astype(state_delta.dtype),
            jnp.stack(ks_l), jnp.stack(vs_l), jnp.stack(cs_l), jnp.stack(ss_l).astype(state_delta.dtype))
```

```python
import functools
import math

import jax
import jax.numpy as jnp
from jax import lax
from jax.experimental import pallas as pl
from jax.experimental.pallas import tpu as pltpu

F32 = jnp.float32
BF16 = jnp.bfloat16

D_MODEL = 1024
HEAD = 128
N_HEADS = 4
DQK = HEAD // 2
GROUP = N_HEADS * HEAD
CONV_W = 4
CONV_CH = 3 * GROUP
CHUNK = 64
ALIBI_MAX = 8.0
N_EXPERTS = 32
TOP_K = 4
D_FF = D_MODEL
SWIGLU_LIMIT = 7.0
SWIGLU_ALPHA = 1.702
DEPTH = 1
DEEPNORM_ALPHA = (2 * DEPTH) ** 0.25
LN_EPS = 1e-5
SUBLN_EPS = 1e-5
GATED_NORM_EPS = 1e-6
L2_EPS = 1e-6

LANES = 128
SUBLANES = 8
VMEM_LIMIT = 56 * 1024 * 1024

COL_Q, COL_K, COL_V, COL_CONV = 0, GROUP, 2 * GROUP, 3 * GROUP
COL_Z = COL_CONV + CONV_CH
COL_AB = COL_Z + GROUP
IN_COLS = COL_AB + 2 * N_HEADS
IN_COLS_PAD = COL_AB + LANES


def _params(*sem):
    return pltpu.CompilerParams(dimension_semantics=sem, vmem_limit_bytes=VMEM_LIMIT)


def _in_proj_kernel(x_ref, w_ref, q_ref, kf_ref, vf_ref, kb_ref, vb_ref, c_ref, z_ref, ab_ref):
    xb = x_ref[...].astype(BF16)

    def section(lo, hi):
        return jnp.dot(xb, w_ref[:, lo:hi], preferred_element_type=F32)

    q_ref[...] = (section(COL_Q, COL_K) * (DQK ** -0.5)).astype(BF16)
    k = section(COL_K, COL_V)
    kf_ref[...] = k
    kb_ref[...] = k.astype(BF16)
    v = section(COL_V, COL_CONV)
    vf_ref[...] = v
    vb_ref[...] = v.astype(BF16)
    c_ref[...] = section(COL_CONV, COL_Z)
    z_ref[...] = section(COL_Z, COL_AB)
    ab_ref[...] = section(COL_AB, IN_COLS_PAD)


def _in_proj(x2d, w_bf, tm):
    t = x2d.shape[0]
    row = lambda i: (i, 0)
    widths = (GROUP, GROUP, GROUP, GROUP, GROUP, CONV_CH, GROUP, LANES)
    dtypes = (BF16, F32, F32, BF16, BF16, F32, F32, F32)
    return pl.pallas_call(
        _in_proj_kernel,
        grid=(t // tm,),
        in_specs=[pl.BlockSpec((tm, D_MODEL), row),
                  pl.BlockSpec((D_MODEL, IN_COLS_PAD), lambda i: (0, 0))],
        out_specs=[pl.BlockSpec((tm, w), row) for w in widths],
        out_shape=[jax.ShapeDtypeStruct((t, w), d) for w, d in zip(widths, dtypes)],
        compiler_params=_params("parallel"),
        name="in_proj",
    )(x2d, w_bf)


def _alibi_slopes():
    return [2.0 ** (-ALIBI_MAX * (h + 1) / N_HEADS) for h in range(N_HEADS)]


def _stack_halves(q):
    lane = lax.broadcasted_iota(jnp.int32, q.shape, 1)
    zero = jnp.zeros_like(q)
    return jnp.concatenate([jnp.where(lane < DQK, q, zero), jnp.where(lane < DQK, zero, q)], axis=0)


def _diff_norm(acc, l, lam, w, lam_init, rows):
    o = acc[:rows] / l[:rows] - lam * (acc[rows:] / l[rows:])
    ms = jnp.mean(o * o, axis=-1, keepdims=True)
    return o * lax.rsqrt(ms + SUBLN_EPS) * w * (1.0 - lam_init)


def _attn_prompt_kernel(lam_ref, q_ref, k_ref, v_ref, base_ref, diag_ref, w_ref, o_ref,
                        qz_ref, m_ref, l_ref, acc_ref, *, blk, lam_init):
    i = pl.program_id(2)
    h = pl.program_id(1)
    qz_ref[...] = _stack_halves(q_ref[...])

    def tile(j0, bias):
        kt = k_ref[pl.ds(j0, blk), :]
        s = lax.dot_general(qz_ref[...], kt, (((1,), (1,)), ((), ())), preferred_element_type=F32)
        return s + jnp.concatenate([bias, bias], axis=0)

    def update(s, j0, first):
        vt = v_ref[pl.ds(j0, blk), :]
        m_tile = jnp.max(s, axis=-1, keepdims=True)
        if first:
            m_new = m_tile
        else:
            m_old = m_ref[...]
            m_new = jnp.maximum(m_old, m_tile)
        p = jnp.exp(s - m_new)
        pv = jnp.dot(p.astype(BF16), vt, preferred_element_type=F32)
        l_tile = jnp.sum(p, axis=-1, keepdims=True)
        if first:
            l_ref[...] = l_tile
            acc_ref[...] = pv
        else:
            alpha = jnp.exp(m_old - m_new)
            l_ref[...] = alpha * l_ref[...] + l_tile
            acc_ref[...] = alpha * acc_ref[...] + pv
        m_ref[...] = m_new

    j_diag = pl.multiple_of(i * blk, blk)
    update(tile(j_diag, diag_ref[0]), j_diag, True)

    slope = jnp.where(h == 0, _alibi_slopes()[0],
                      jnp.where(h == 1, _alibi_slopes()[1],
                                jnp.where(h == 2, _alibi_slopes()[2], _alibi_slopes()[3]))).astype(F32)

    def body(j, carry):
        j0 = pl.multiple_of(j * blk, blk)
        shift = -slope * ((i - j) * blk).astype(F32)
        update(tile(j0, base_ref[0]) + shift, j0, False)
        return carry

    lax.fori_loop(0, i, body, 0)
    o = _diff_norm(acc_ref[...], l_ref[...], lam_ref[0], w_ref[...], lam_init, blk)
    o_ref[...] = o.astype(o_ref.dtype)


def _attn_prompt(q_bf, k_bf, v_bf, lam, subln_w, bsz, seq, blk, lam_init):
    nq = seq // blk
    a = jnp.arange(blk, dtype=jnp.int32)
    rel = (a[:, None] - a[None, :]).astype(F32)
    slopes = jnp.asarray(_alibi_slopes(), F32)[:, None, None]
    base = -slopes * rel[None]
    visible = (a[None, :] // CHUNK) <= (a[:, None] // CHUNK)
    diag = jnp.where(visible[None], -slopes * jnp.abs(rel)[None], -jnp.inf)
    kern = functools.partial(_attn_prompt_kernel, blk=blk, lam_init=lam_init)
    return pl.pallas_call(
        kern,
        grid_spec=pltpu.PrefetchScalarGridSpec(
            num_scalar_prefetch=0,
            grid=(bsz, N_HEADS, nq),
            in_specs=[pl.BlockSpec(memory_space=pltpu.SMEM),
                      pl.BlockSpec((blk, HEAD), lambda b, h, i: (b * nq + i, h)),
                      pl.BlockSpec((seq, HEAD), lambda b, h, i: (b, h)),
                      pl.BlockSpec((seq, HEAD), lambda b, h, i: (b, h)),
                      pl.BlockSpec((1, blk, blk), lambda b, h, i: (h, 0, 0)),
                      pl.BlockSpec((1, blk, blk), lambda b, h, i: (h, 0, 0)),
                      pl.BlockSpec((1, HEAD), lambda b, h, i: (0, 0))],
            out_specs=pl.BlockSpec((blk, HEAD), lambda b, h, i: (b * nq + i, h)),
            scratch_shapes=[pltpu.VMEM((2 * blk, HEAD), BF16),
                            pltpu.VMEM((2 * blk, 1), F32),
                            pltpu.VMEM((2 * blk, 1), F32),
                            pltpu.VMEM((2 * blk, HEAD), F32)]),
        out_shape=jax.ShapeDtypeStruct((bsz * seq, GROUP), BF16),
        compiler_params=_params("parallel", "parallel", "arbitrary"),
        name="attn_prompt",
    )(lam, q_bf, k_bf, v_bf, base, diag, subln_w)


def _attn_sample_kernel(lam_ref, q_ref, kn_ref, vn_ref, kc_ref, vc_ref, w_ref, o_ref, *, seq, past, lam_init):
    h = pl.program_id(1)
    slope = jnp.where(h == 0, _alibi_slopes()[0],
                      jnp.where(h == 1, _alibi_slopes()[1],
                                jnp.where(h == 2, _alibi_slopes()[2], _alibi_slopes()[3]))).astype(F32)
    qz = _stack_halves(q_ref[...])
    nt = (((1,), (1,)), ((), ()))
    qpos = past + lax.broadcasted_iota(jnp.int32, (2 * seq, 1), 0) % seq
    kpos_c = lax.broadcasted_iota(jnp.int32, (1, past), 1)
    kpos_n = past + lax.broadcasted_iota(jnp.int32, (1, seq), 1)
    s_c = lax.dot_general(qz, kc_ref[0].astype(BF16), nt, preferred_element_type=F32)
    s_c = s_c - slope * jnp.abs(qpos - kpos_c).astype(F32)
    s_n = lax.dot_general(qz, kn_ref[...], nt, preferred_element_type=F32)
    s_n = s_n - slope * jnp.abs(qpos - kpos_n).astype(F32)
    m = jnp.maximum(jnp.max(s_c, axis=-1, keepdims=True), jnp.max(s_n, axis=-1, keepdims=True))
    p_c = jnp.exp(s_c - m)
    p_n = jnp.exp(s_n - m)
    l = jnp.sum(p_c, axis=-1, keepdims=True) + jnp.sum(p_n, axis=-1, keepdims=True)
    acc = (jnp.dot(p_c.astype(BF16), vc_ref[0].astype(BF16), preferred_element_type=F32)
           + jnp.dot(p_n.astype(BF16), vn_ref[...], preferred_element_type=F32))
    o_ref[...] = _diff_norm(acc, l, lam_ref[0], w_ref[...], lam_init, seq).astype(o_ref.dtype)


def _attn_sample(q_bf, k_bf, v_bf, cache_k, cache_v, lam, subln_w, bsz, seq, lam_init):
    past = cache_k.shape[1]
    kern = functools.partial(_attn_sample_kernel, seq=seq, past=past, lam_init=lam_init)
    new = pl.BlockSpec((seq, HEAD), lambda b, h: (b, h))
    cache = pl.BlockSpec((1, past, HEAD), lambda b, h: (b, 0, h))
    return pl.pallas_call(
        kern,
        grid=(bsz, N_HEADS),
        in_specs=[pl.BlockSpec(memory_space=pltpu.SMEM), new, new, new, cache, cache,
                  pl.BlockSpec((1, HEAD), lambda b, h: (0, 0))],
        out_specs=new,
        out_shape=jax.ShapeDtypeStruct((bsz * seq, GROUP), BF16),
        compiler_params=_params("parallel", "parallel"),
        name="attn_sample",
    )(lam, q_bf, k_bf, v_bf, cache_k.reshape(bsz, past, GROUP), cache_v.reshape(bsz, past, GROUP), subln_w)


def _split3(x):
    hi = x.astype(BF16)
    r1 = x - hi.astype(F32)
    mid = r1.astype(BF16)
    lo = (r1 - mid.astype(F32)).astype(BF16)
    return hi, mid, lo


def _unit_lower_inverse(a_strict, eye):
    c = a_strict.shape[0]
    t = eye - a_strict
    p = a_strict
    steps = int(math.log2(c)) - 1
    for _ in range(steps):
        pb = p.astype(BF16)
        p = jnp.dot(pb, pb, preferred_element_type=F32)
        t = t + jnp.dot(t.astype(BF16), p.astype(BF16), preferred_element_type=F32)
    return t


def _gdn_prep_kernel(cin_ref, halo_ref, past_ref, ab_ref, cw_ref, alog_ref, dtb_ref,
                     u_ref, w_ref, qd_ref, kd_ref, qk_ref, gl_ref, *, chunk, n_sub):
    c_idx = pl.program_id(1)
    rows = chunk * n_sub
    prev = jnp.where(c_idx == 0, past_ref[0], halo_ref[...])
    xin = jnp.concatenate([prev, cin_ref[...]], axis=0)
    conv = sum(xin[SUBLANES - (CONV_W - 1) + j: SUBLANES - (CONV_W - 1) + j + rows] * cw_ref[j:j + 1, :]
               for j in range(CONV_W))
    conv = conv * jax.nn.sigmoid(conv)

    ab = ab_ref[...]
    lane = lax.broadcasted_iota(jnp.int32, ab.shape, 1)
    pre = ab + dtb_ref[...]
    softplus = jnp.maximum(pre, 0.0) + jnp.log(1.0 + jnp.exp(-jnp.abs(pre)))
    g_all = -jnp.exp(alog_ref[...]) * softplus
    g_all = jnp.where(lane < N_HEADS, g_all, 0.0)
    beta_all = jax.nn.sigmoid(ab)

    ri = lax.broadcasted_iota(jnp.int32, (chunk, chunk), 0)
    ci = lax.broadcasted_iota(jnp.int32, (chunk, chunk), 1)
    incl = ri >= ci
    strict = ri > ci
    tril = jnp.where(incl, 1.0, 0.0).astype(BF16)
    eye = jnp.where(ri == ci, 1.0, 0.0).astype(F32)
    nt = (((1,), (1,)), ((), ()))

    for sc in range(n_sub):
        r0 = sc * chunk
        g = g_all[r0:r0 + chunk]
        gc = sum(jnp.dot(tril, part, preferred_element_type=F32) for part in _split3(g))
        gct = gc.T
        gl_ref[sc] = gc[chunk - 1:chunk, :]
        for h in range(N_HEADS):
            col = slice(h * HEAD, (h + 1) * HEAD)
            qh = conv[r0:r0 + chunk, h * HEAD:(h + 1) * HEAD]
            kh = conv[r0:r0 + chunk, GROUP + h * HEAD:GROUP + (h + 1) * HEAD]
            vh = conv[r0:r0 + chunk, 2 * GROUP + h * HEAD:2 * GROUP + (h + 1) * HEAD]
            qh = qh * lax.rsqrt(jnp.sum(qh * qh, axis=-1, keepdims=True) + L2_EPS) * (HEAD ** -0.5)
            kh = kh * lax.rsqrt(jnp.sum(kh * kh, axis=-1, keepdims=True) + L2_EPS)
            beta = beta_all[r0:r0 + chunk, N_HEADS + h:N_HEADS + h + 1]
            gcol = gc[:, h:h + 1]
            grow = gct[h:h + 1, :]
            gamma = jnp.exp(jnp.where(incl, gcol - grow, -jnp.inf))
            egc = jnp.exp(gcol)
            kb = kh * beta
            kbb = kb.astype(BF16)
            khb = kh.astype(BF16)
            a = jnp.where(strict, lax.dot_general(kbb, khb, nt, preferred_element_type=F32) * gamma, 0.0)
            t_inv = _unit_lower_inverse(a, eye).astype(BF16)
            u_ref[r0:r0 + chunk, col] = jnp.dot(t_inv, (vh * beta).astype(BF16), preferred_element_type=F32)
            w_ref[r0:r0 + chunk, col] = jnp.dot(t_inv, (kb * egc).astype(BF16),
                                                preferred_element_type=F32).astype(BF16)
            qk = lax.dot_general(qh.astype(BF16), khb, nt, preferred_element_type=F32) * gamma
            qk_ref[h, r0:r0 + chunk, :] = qk.astype(BF16)
            qd_ref[r0:r0 + chunk, col] = (qh * egc).astype(BF16)
            kd_ref[r0:r0 + chunk, col] = (kh * jnp.exp(gc[chunk - 1:chunk, h:h + 1] - gcol)).astype(BF16)


def _gdn_prep(cin, ab, past8, conv_w, a_log, dt_bias, bsz, seq, chunk, n_sub):
    rows = chunk * n_sub
    nblk = seq // rows
    t = bsz * seq
    lanes = lambda v: jnp.pad(v.reshape(1, N_HEADS).astype(F32), ((0, 0), (0, LANES - N_HEADS)))
    kern = functools.partial(_gdn_prep_kernel, chunk=chunk, n_sub=n_sub)
    rowblk = lambda b, c: (b * nblk + c, 0)
    halo = lambda b, c: (jnp.maximum((b * nblk + c) * (rows // SUBLANES) - 1, 0), 0)
    const = lambda b, c: (0, 0)
    return pl.pallas_call(
        kern,
        grid=(bsz, nblk),
        in_specs=[pl.BlockSpec((rows, CONV_CH), rowblk),
                  pl.BlockSpec((SUBLANES, CONV_CH), halo),
                  pl.BlockSpec((1, SUBLANES, CONV_CH), lambda b, c: (b, 0, 0)),
                  pl.BlockSpec((rows, LANES), rowblk),
                  pl.BlockSpec((CONV_W, CONV_CH), const),
                  pl.BlockSpec((1, LANES), const),
                  pl.BlockSpec((1, LANES), const)],
        out_specs=[pl.BlockSpec((rows, GROUP), rowblk),
                   pl.BlockSpec((rows, GROUP), rowblk),
                   pl.BlockSpec((rows, GROUP), rowblk),
                   pl.BlockSpec((rows, GROUP), rowblk),
                   pl.BlockSpec((N_HEADS, rows, chunk), lambda b, c: (0, b * nblk + c, 0)),
                   pl.BlockSpec((n_sub, 1, LANES), lambda b, c: (b * nblk + c, 0, 0))],
        out_shape=[jax.ShapeDtypeStruct((t, GROUP), F32),
                   jax.ShapeDtypeStruct((t, GROUP), BF16),
                   jax.ShapeDtypeStruct((t, GROUP), BF16),
                   jax.ShapeDtypeStruct((t, GROUP), BF16),
                   jax.ShapeDtypeStruct((N_HEADS, t, chunk), BF16),
                   jax.ShapeDtypeStruct((t // chunk, 1, LANES), F32)],
        compiler_params=_params("parallel", "parallel"),
        name="gdn_prep",
    )(cin, cin, past8, ab, conv_w, lanes(a_log), lanes(dt_bias))


def _gdn_scan_kernel(u_ref, w_ref, qd_ref, kd_ref, qk_ref, gl_ref, z_ref, s0_ref, nw_ref,
                     o_ref, sf_ref, s_ref, *, bsz, chunk):
    c_idx = pl.program_id(0)

    @pl.when(c_idx == 0)
    def _():
        s_ref[...] = s0_ref[...]

    tn = (((0,), (0,)), ((), ()))
    for b in range(bsz):
        glast = jnp.exp(gl_ref[b, 0])
        for h in range(N_HEADS):
            col = slice(h * HEAD, (h + 1) * HEAD)
            s = s_ref[b, h]
            wq = jnp.concatenate([w_ref[b, :, col], qd_ref[b, :, col]], axis=0)
            r = jnp.dot(wq, s.astype(BF16), preferred_element_type=F32)
            v_new = (u_ref[b, :, col] - r[:chunk]).astype(BF16)
            o = r[chunk:] + jnp.dot(qk_ref[h, b], v_new, preferred_element_type=F32)
            s_ref[b, h] = s * glast[:, h:h + 1] + lax.dot_general(kd_ref[b, :, col], v_new, tn,
                                                                 preferred_element_type=F32)
            zh = z_ref[b, :, col]
            ms = jnp.mean(o * o, axis=-1, keepdims=True)
            o = o * lax.rsqrt(ms + GATED_NORM_EPS) * nw_ref[...] * (zh * jax.nn.sigmoid(zh))
            o_ref[b, :, col] = o.astype(o_ref.dtype)

    @pl.when(c_idx == pl.num_programs(0) - 1)
    def _():
        sf_ref[...] = s_ref[...]


def _gdn_scan(u, w, qd, kd, qk, gl, z, s0, norm_w, bsz, seq, chunk):
    nc = seq // chunk
    kern = functools.partial(_gdn_scan_kernel, bsz=bsz, chunk=chunk)
    tok = pl.BlockSpec((bsz, chunk, GROUP), lambda c: (0, c, 0))
    state = pl.BlockSpec((bsz, N_HEADS, HEAD, HEAD), lambda c: (0, 0, 0, 0))
    o, s_final = pl.pallas_call(
        kern,
        grid=(nc,),
        in_specs=[tok, tok, tok, tok,
                  pl.BlockSpec((N_HEADS, bsz, chunk, chunk), lambda c: (0, 0, c, 0)),
                  pl.BlockSpec((bsz, 1, 1, LANES), lambda c: (0, c, 0, 0)),
                  tok, state,
                  pl.BlockSpec((1, HEAD), lambda c: (0, 0))],
        out_specs=[tok, state],
        out_shape=[jax.ShapeDtypeStruct((bsz, seq, GROUP), BF16),
                   jax.ShapeDtypeStruct((bsz, N_HEADS, HEAD, HEAD), F32)],
        scratch_shapes=[pltpu.VMEM((bsz, N_HEADS, HEAD, HEAD), F32)],
        compiler_params=_params("arbitrary"),
        name="gdn_scan",
    )(u.reshape(bsz, seq, GROUP), w.reshape(bsz, seq, GROUP), qd.reshape(bsz, seq, GROUP),
      kd.reshape(bsz, seq, GROUP), qk.reshape(N_HEADS, bsz, seq, chunk), gl.reshape(bsz, nc, 1, LANES),
      z.reshape(bsz, seq, GROUP), s0, norm_w)
    return o.reshape(bsz * seq, GROUP), s_final


def _layernorm(x, g, b):
    mu = jnp.mean(x, axis=-1, keepdims=True)
    xc = x - mu
    var = jnp.mean(xc * xc, axis=-1, keepdims=True)
    return xc * lax.rsqrt(var + LN_EPS) * g + b


def _out_router_kernel(att_ref, o_ref, x_ref, wo_ref, g_ref, b_ref, rw_ref, rb_ref,
                       x1_ref, x1b_ref, idx_ref, gate_ref, rank_ref, cnt_ref, carry_ref, *, tm):
    step = pl.program_id(0)

    @pl.when(step == 0)
    def _():
        carry_ref[...] = jnp.zeros_like(carry_ref)

    mix = (jnp.dot(att_ref[...], wo_ref[:GROUP, :], preferred_element_type=F32)
           + jnp.dot(o_ref[...], wo_ref[GROUP:, :], preferred_element_type=F32))
    x1 = _layernorm(DEEPNORM_ALPHA * x_ref[...] + mix, g_ref[...], b_ref[...])
    x1_ref[...] = x1
    x1b_ref[...] = x1.astype(BF16)

    logits = jnp.dot(x1, rw_ref[...], preferred_element_type=F32, precision=lax.Precision.HIGHEST) + rb_ref[...]
    lane = lax.broadcasted_iota(jnp.int32, logits.shape, 1)
    work = jnp.where(lane < N_EXPERTS, logits, -jnp.inf)
    vals, idxs = [], []
    for _ in range(TOP_K):
        m = jnp.max(work, axis=-1, keepdims=True)
        am = jnp.min(jnp.where(work == m, lane, LANES), axis=-1, keepdims=True)
        vals.append(m)
        idxs.append(am)
        work = jnp.where(lane == am, -jnp.inf, work)
    exps = [jnp.exp(v - vals[0]) for v in vals]
    denom = exps[0] + exps[1] + exps[2] + exps[3]
    chosen = jnp.zeros(logits.shape, F32)
    gate_out = jnp.zeros(logits.shape, F32)
    idx_out = jnp.zeros(logits.shape, jnp.int32)
    for k in range(TOP_K):
        chosen = jnp.where(lane == idxs[k], 1.0, chosen)
        gate_out = jnp.where(lane == k, exps[k] / denom, gate_out)
        idx_out = jnp.where(lane == k, idxs[k], idx_out)
    ri = lax.broadcasted_iota(jnp.int32, (tm, tm), 0)
    ci = lax.broadcasted_iota(jnp.int32, (tm, tm), 1)
    before = jnp.where(ri > ci, 1.0, 0.0).astype(BF16)
    prefix = jnp.dot(before, chosen.astype(BF16), preferred_element_type=F32) + carry_ref[...]
    rank_out = jnp.zeros(logits.shape, F32)
    for k in range(TOP_K):
        r = jnp.sum(jnp.where(lane == idxs[k], prefix, 0.0), axis=-1, keepdims=True)
        rank_out = jnp.where(lane == k, r, rank_out)
    carry_ref[...] = carry_ref[...] + jnp.sum(chosen, axis=0, keepdims=True)
    idx_ref[...] = idx_out
    gate_ref[...] = gate_out
    rank_ref[...] = rank_out.astype(jnp.int32)
    cnt_ref[...] = carry_ref[...].astype(jnp.int32)


def _out_router(att, o, x2d, w_out_bf, ln_g, ln_b, router_w, router_b, tm):
    t = x2d.shape[0]
    row = lambda i: (i, 0)
    const = lambda i: (0, 0)
    rw = jnp.pad(router_w, ((0, 0), (0, LANES - N_EXPERTS)))
    rb = jnp.pad(router_b.reshape(1, N_EXPERTS), ((0, 0), (0, LANES - N_EXPERTS)))
    kern = functools.partial(_out_router_kernel, tm=tm)
    return pl.pallas_call(
        kern,
        grid=(t // tm,),
        in_specs=[pl.BlockSpec((tm, GROUP), row), pl.BlockSpec((tm, GROUP), row),
                  pl.BlockSpec((tm, D_MODEL), row),
                  pl.BlockSpec((2 * GROUP, D_MODEL), const),
                  pl.BlockSpec((1, D_MODEL), const), pl.BlockSpec((1, D_MODEL), const),
                  pl.BlockSpec((D_MODEL, LANES), const), pl.BlockSpec((1, LANES), const)],
        out_specs=[pl.BlockSpec((tm, D_MODEL), row), pl.BlockSpec((tm, D_MODEL), row),
                   pl.BlockSpec((tm, LANES), row), pl.BlockSpec((tm, LANES), row),
                   pl.BlockSpec((tm, LANES), row), pl.BlockSpec((1, LANES), const)],
        out_shape=[jax.ShapeDtypeStruct((t, D_MODEL), F32), jax.ShapeDtypeStruct((t, D_MODEL), BF16),
                   jax.ShapeDtypeStruct((t, LANES), jnp.int32), jax.ShapeDtypeStruct((t, LANES), F32),
                   jax.ShapeDtypeStruct((t, LANES), jnp.int32), jax.ShapeDtypeStruct((1, LANES), jnp.int32)],
        scratch_shapes=[pltpu.VMEM((1, LANES), F32)],
        compiler_params=_params("arbitrary"),
        name="out_router",
    )(att, o, x2d, w_out_bf, ln_g.reshape(1, D_MODEL), ln_b.reshape(1, D_MODEL), rw, rb)


def _expert_kernel(be_ref, nb_ref, x_ref, g_ref, wgu_ref, bgu_ref, wd_ref, bd_ref, y_ref):
    blk = pl.program_id(0)

    @pl.when(blk < nb_ref[0])
    def _():
        x = x_ref[...]
        h = jnp.dot(x, wgu_ref[0].astype(BF16), preferred_element_type=F32) + bgu_ref[0]
        gate = jnp.minimum(h[:, :D_FF], SWIGLU_LIMIT)
        up = jnp.clip(h[:, D_FF:], -SWIGLU_LIMIT, SWIGLU_LIMIT)
        act = (up + 1.0) * (gate * jax.nn.sigmoid(SWIGLU_ALPHA * gate))
        y = jnp.dot(act.astype(BF16), wd_ref[0].astype(BF16), preferred_element_type=F32) + bd_ref[0]
        y_ref[...] = y * g_ref[...]

    @pl.when(blk >= nb_ref[0])
    def _():
        y_ref[...] = jnp.zeros_like(y_ref)


def _experts(xb, gates_rows, blk_e, n_used, w_gu, b_gu, w_down, b_down, bm):
    rows = xb.shape[0]
    n_blocks = rows // bm
    return pl.pallas_call(
        _expert_kernel,
        grid_spec=pltpu.PrefetchScalarGridSpec(
            num_scalar_prefetch=2,
            grid=(n_blocks,),
            in_specs=[pl.BlockSpec((bm, D_MODEL), lambda i, be, nb: (i, 0)),
                      pl.BlockSpec((bm, 1), lambda i, be, nb: (i, 0)),
                      pl.BlockSpec((1, D_MODEL, 2 * D_FF), lambda i, be, nb: (be[i], 0, 0)),
                      pl.BlockSpec((1, 1, 2 * D_FF), lambda i, be, nb: (be[i], 0, 0)),
                      pl.BlockSpec((1, D_FF, D_MODEL), lambda i, be, nb: (be[i], 0, 0)),
                      pl.BlockSpec((1, 1, D_MODEL), lambda i, be, nb: (be[i], 0, 0))],
            out_specs=pl.BlockSpec((bm, D_MODEL), lambda i, be, nb: (i, 0))),
        out_shape=jax.ShapeDtypeStruct((rows, D_MODEL), F32),
        compiler_params=_params("arbitrary"),
        name="experts",
    )(blk_e, n_used, xb, gates_rows, w_gu, b_gu.reshape(N_EXPERTS, 1, 2 * D_FF),
      w_down, b_down.reshape(N_EXPERTS, 1, D_MODEL))


def _final_ln_kernel(x_ref, y_ref, g_ref, b_ref, o_ref):
    y = y_ref[:, 0, :] + y_ref[:, 1, :] + y_ref[:, 2, :] + y_ref[:, 3, :]
    o_ref[...] = _layernorm(DEEPNORM_ALPHA * x_ref[...] + y, g_ref[...], b_ref[...])


def _final_ln(x1, y4, ln_g, ln_b, tm):
    t = x1.shape[0]
    row = lambda i: (i, 0)
    const = lambda i: (0, 0)
    return pl.pallas_call(
        _final_ln_kernel,
        grid=(t // tm,),
        in_specs=[pl.BlockSpec((tm, D_MODEL), row),
                  pl.BlockSpec((tm, TOP_K, D_MODEL), lambda i: (i, 0, 0)),
                  pl.BlockSpec((1, D_MODEL), const), pl.BlockSpec((1, D_MODEL), const)],
        out_specs=pl.BlockSpec((tm, D_MODEL), row),
        out_shape=jax.ShapeDtypeStruct((t, D_MODEL), F32),
        compiler_params=_params("parallel"),
        name="final_ln",
    )(x1, y4, ln_g.reshape(1, D_MODEL), ln_b.reshape(1, D_MODEL))


MOE_BM = 256
ATTN_BLK = 256
PROJ_TM = 256
GDN_SUB = 2


def _mixers(x2d, bsz, seq, cache_k, cache_v, conv_past, s0, lam, lam_init, w_in_bf, conv_w, a_log, dt_bias,
            delta_norm_w, subln_w):
    tm = min(PROJ_TM, x2d.shape[0])
    q_bf, k_f, v_f, k_bf, v_bf, cin, z, ab = _in_proj(x2d, w_in_bf, tm)
    sub_w = subln_w.reshape(1, HEAD)
    if cache_k is None:
        att = _attn_prompt(q_bf, k_bf, v_bf, lam, sub_w, bsz, seq, ATTN_BLK, lam_init)
    else:
        att = _attn_sample(q_bf, k_bf, v_bf, cache_k, cache_v, lam, sub_w, bsz, seq, lam_init)
    chunk = CHUNK if seq % CHUNK == 0 else seq
    n_sub = GDN_SUB if (seq // chunk) % GDN_SUB == 0 else 1
    past8 = jnp.pad(conv_past, ((0, 0), (SUBLANES - (CONV_W - 1), 0), (0, 0)))
    u, w, qd, kd, qk, gl = _gdn_prep(cin, ab, past8, conv_w, a_log, dt_bias, bsz, seq, chunk, n_sub)
    o, s_new = _gdn_scan(u, w, qd, kd, qk, gl, z, s0, delta_norm_w.reshape(1, HEAD), bsz, seq, chunk)
    return att, o, k_f, v_f, cin, s_new


def _moe(x1, x1b, idx, gates, rank, counts, w_gu, b_gu, w_down, b_down, bm):
    t = x1.shape[0]
    n = t * TOP_K
    counts = counts[0, :N_EXPERTS]
    padded = (counts + bm - 1) // bm * bm
    pad_end = jnp.cumsum(padded)
    pad_start = pad_end - padded
    top_idx = idx[:, :TOP_K]
    dest = pad_start[top_idx] + rank[:, :TOP_K]
    n_blocks = -(-n // bm) + N_EXPERTS
    rows = n_blocks * bm
    flat_dest = dest.reshape(n)
    buf_t = jnp.zeros((rows,), jnp.int32).at[flat_dest].set(jnp.arange(n, dtype=jnp.int32) // TOP_K)
    buf_g = jnp.zeros((rows,), F32).at[flat_dest].set(gates[:, :TOP_K].reshape(n))
    blk_start = jnp.arange(n_blocks, dtype=jnp.int32) * bm
    blk_e = jnp.minimum(jnp.searchsorted(pad_end, blk_start, side='right'), N_EXPERTS - 1).astype(jnp.int32)
    n_used = (pad_end[-1] // bm).astype(jnp.int32).reshape(1)
    xb = x1b[buf_t]
    yb = _experts(xb, buf_g.reshape(rows, 1), blk_e, n_used, w_gu, b_gu, w_down, b_down, bm)
    return yb[flat_dest].reshape(t, TOP_K, D_MODEL)


def kernel(x_prompt, x_sample, cache_k, cache_v, state_conv, state_delta, w_in, conv_w, a_log, dt_bias,
           delta_norm_w, lambda_q1, lambda_k1, lambda_q2, lambda_k2, subln_w, w_out, ln1_g, ln1_b,
           router_w, router_b, w_gu, b_gu, w_down, b_down, ln2_g, ln2_b):
    bp, lp, _ = x_prompt.shape
    bs, ls, _ = x_sample.shape
    l = 0
    lam_init = 0.8 - 0.6 * math.exp(-0.3 * l)
    lam = (jnp.exp(jnp.sum(lambda_q1[l] * lambda_k1[l])) - jnp.exp(jnp.sum(lambda_q2[l] * lambda_k2[l]))
           + lam_init).reshape(1).astype(F32)
    w_in_bf = jnp.pad(w_in[l], ((0, 0), (0, IN_COLS_PAD - IN_COLS))).astype(BF16)
    shared = (lam, lam_init, w_in_bf, conv_w[l], a_log[l], dt_bias[l], delta_norm_w[l], subln_w[l])

    xp = x_prompt.reshape(bp * lp, D_MODEL)
    xs = x_sample.reshape(bs * ls, D_MODEL)
    zero_conv = jnp.zeros((bp, CONV_W - 1, CONV_CH), F32)
    zero_s = jnp.zeros((bp, N_HEADS, HEAD, HEAD), F32)
    att_p, o_p, k_p, v_p, cin_p, s_p = _mixers(xp, bp, lp, None, None, zero_conv, zero_s, *shared)
    att_s, o_s, k_s, v_s, cin_s, s_s = _mixers(xs, bs, ls, cache_k[l], cache_v[l], state_conv[l],
                                               state_delta[l], *shared)

    att = jnp.concatenate([att_p, att_s], axis=0)
    o = jnp.concatenate([o_p, o_s], axis=0)
    x_all = jnp.concatenate([xp, xs], axis=0)
    t = x_all.shape[0]
    tm = next(c for c in (512, 384, 256, 128) if t % c == 0)
    x1, x1b, idx, gates, rank, counts = _out_router(att, o, x_all, w_out[l].astype(BF16), ln1_g[l], ln1_b[l],
                                                    router_w[l], router_b[l], tm)
    y4 = _moe(x1, x1b, idx, gates, rank, counts, w_gu[l], b_gu[l], w_down[l], b_down[l], MOE_BM)
    y = _final_ln(x1, y4, ln2_g[l], ln2_b[l], 128)

    tp = bp * lp
    conv_tail = lambda cin, b, s: cin.reshape(b, s, CONV_CH)[:, s - (CONV_W - 1):][None]
    return (y[:tp].reshape(bp, lp, D_MODEL), y[tp:].reshape(bs, ls, D_MODEL),
            k_p.reshape(1, bp, lp, N_HEADS, HEAD), v_p.reshape(1, bp, lp, N_HEADS, HEAD),
            conv_tail(cin_p, bp, lp), s_p[None].astype(state_delta.dtype),
            k_s.reshape(1, bs, ls, N_HEADS, HEAD), v_s.reshape(1, bs, ls, N_HEADS, HEAD),
            conv_tail(cin_s, bs, ls), s_s[None].astype(state_delta.dtype))
```

```python
import functools
import math

import jax
import jax.numpy as jnp
from jax import lax
from jax.experimental import pallas as pl
from jax.experimental.pallas import tpu as pltpu

F32 = jnp.float32
BF16 = jnp.bfloat16

D_MODEL = 1024
HEAD = 128
N_HEADS = 4
DQK = HEAD // 2
GROUP = N_HEADS * HEAD
CONV_W = 4
CONV_CH = 3 * GROUP
CHUNK = 64
ALIBI_MAX = 8.0
N_EXPERTS = 32
TOP_K = 4
D_FF = D_MODEL
SWIGLU_LIMIT = 7.0
SWIGLU_ALPHA = 1.702
DEPTH = 1
DEEPNORM_ALPHA = (2 * DEPTH) ** 0.25
LN_EPS = 1e-5
SUBLN_EPS = 1e-5
GATED_NORM_EPS = 1e-6
L2_EPS = 1e-6

LANES = 128
SUBLANES = 8
BF16_EXACT_INT = 256
VMEM_LIMIT = 56 * 1024 * 1024

COL_Q, COL_K, COL_V, COL_CONV = 0, GROUP, 2 * GROUP, 3 * GROUP
COL_Z = COL_CONV + CONV_CH
COL_AB = COL_Z + GROUP
IN_COLS = COL_AB + 2 * N_HEADS
IN_COLS_PAD = COL_AB + LANES


def _params(*sem):
    return pltpu.CompilerParams(dimension_semantics=sem, vmem_limit_bytes=VMEM_LIMIT)


def _in_proj_kernel(x_ref, w_ref, q_ref, kf_ref, vf_ref, kb_ref, vb_ref, c_ref, z_ref, ab_ref, *, tm, transposed):
    xb = x_ref[...].astype(BF16)

    def section(lo, hi):
        return jnp.dot(xb, w_ref[:, lo:hi], preferred_element_type=F32)

    q = section(COL_Q, COL_K) * (DQK ** -0.5)
    k = section(COL_K, COL_V)
    kf_ref[...] = k
    kb_ref[...] = k.astype(BF16)
    v = section(COL_V, COL_CONV)
    vf_ref[...] = v
    if transposed:
        q_ref[0] = q.T.astype(BF16)
        vb_ref[0] = v.T.astype(BF16)
    else:
        q_ref[...] = q.astype(BF16)
        vb_ref[...] = v.astype(BF16)
    c_ref[...] = section(COL_CONV, COL_Z)
    z_ref[...] = section(COL_Z, COL_AB)
    ab_ref[...] = section(COL_AB, IN_COLS_PAD)


def _in_proj(x2d, w_bf, tm, transposed):
    t = x2d.shape[0]
    row = lambda i: (i, 0)
    widths = (GROUP, GROUP, GROUP, GROUP, GROUP, CONV_CH, GROUP, LANES)
    dtypes = (BF16, F32, F32, BF16, BF16, F32, F32, F32)
    out_specs = [pl.BlockSpec((tm, w), row) for w in widths]
    out_shape = [jax.ShapeDtypeStruct((t, w), d) for w, d in zip(widths, dtypes)]
    if transposed:
        for slot in (0, 4):
            out_specs[slot] = pl.BlockSpec((1, GROUP, tm), lambda i: (i, 0, 0))
            out_shape[slot] = jax.ShapeDtypeStruct((t // tm, GROUP, tm), BF16)
    return pl.pallas_call(
        functools.partial(_in_proj_kernel, tm=tm, transposed=transposed),
        grid=(t // tm,),
        in_specs=[pl.BlockSpec((tm, D_MODEL), row),
                  pl.BlockSpec((D_MODEL, IN_COLS_PAD), lambda i: (0, 0))],
        out_specs=out_specs,
        out_shape=out_shape,
        compiler_params=_params("parallel"),
        name="in_proj",
    )(x2d, w_bf)


def _alibi_slopes():
    return [2.0 ** (-ALIBI_MAX * (h + 1) / N_HEADS) for h in range(N_HEADS)]


def _stack_halves(q):
    lane = lax.broadcasted_iota(jnp.int32, q.shape, 1)
    zero = jnp.zeros_like(q)
    return jnp.concatenate([jnp.where(lane < DQK, q, zero), jnp.where(lane < DQK, zero, q)], axis=0)


def _diff_norm(acc, l, lam, w, lam_init, rows):
    o = acc[:rows] / l[:rows] - lam * (acc[rows:] / l[rows:])
    ms = jnp.mean(o * o, axis=-1, keepdims=True)
    return o * lax.rsqrt(ms + SUBLN_EPS) * w * (1.0 - lam_init)


def _head_slope(h):
    s = _alibi_slopes()
    return jnp.where(h == 0, s[0], jnp.where(h == 1, s[1], jnp.where(h == 2, s[2], s[3]))).astype(F32)


def _attn_prompt_kernel(lam_ref, qt_ref, k_ref, vt_ref, diag_ref, w_ref, o_ref,
                        kaug_ref, qz_ref, m_ref, l_ref, acc_ref, *, blk, lam_init):
    i = pl.program_id(2)
    slope = _head_slope(pl.program_id(1))
    rows = 2 * blk
    n_kv = k_ref.shape[0] // blk

    @pl.when(i == 0)
    def _():
        lane = lax.broadcasted_iota(jnp.int32, (blk, HEAD), 1)
        pos = lax.broadcasted_iota(jnp.int32, (blk, HEAD), 0)
        lo = slope * (pos % BF16_EXACT_INT).astype(F32)
        hi = slope * (pos - pos % BF16_EXACT_INT).astype(F32)
        extra = jnp.where(lane < 2, 1.0, jnp.where(lane == 2, lo, jnp.where(lane == 3, hi, 0.0))).astype(BF16)

        def fill(j, carry):
            j0 = pl.multiple_of(j * blk, blk)
            kaug_ref[pl.ds(j0, blk), :HEAD] = k_ref[pl.ds(j0, blk), :]
            kaug_ref[pl.ds(j0, blk), HEAD:] = extra
            return carry

        lax.fori_loop(0, n_kv, fill, 0)

    qt = qt_ref[0]
    d = lax.broadcasted_iota(jnp.int32, qt.shape, 0)
    zero = jnp.zeros_like(qt)
    qz_ref[:HEAD, :] = jnp.concatenate([jnp.where(d < DQK, qt, zero), jnp.where(d < DQK, zero, qt)], axis=1)
    a = lax.broadcasted_iota(jnp.int32, (HEAD, rows), 1) % blk
    a_lo = -slope * (a % BF16_EXACT_INT).astype(F32)
    a_hi = -slope * (a - a % BF16_EXACT_INT).astype(F32)
    r = lax.broadcasted_iota(jnp.int32, (HEAD, rows), 0)
    qz_ref[HEAD:, :] = jnp.where(r == 0, a_lo, jnp.where(r == 1, a_hi, jnp.where(r < 4, 1.0, 0.0))).astype(BF16)

    def update(s, vt, shift, first):
        m_tile = jnp.max(s, axis=0, keepdims=True) + shift
        if first:
            m_new = m_tile
        else:
            m_old = m_ref[...]
            m_new = jnp.maximum(m_old, m_tile)
        p = jnp.exp(s - (m_new - shift))
        pv = jnp.dot(vt, p.astype(BF16), preferred_element_type=F32)
        l_tile = jnp.sum(p, axis=0, keepdims=True)
        if first:
            l_ref[...] = l_tile
            acc_ref[...] = pv
        else:
            alpha = jnp.exp(m_old - m_new)
            l_ref[...] = alpha * l_ref[...] + l_tile
            acc_ref[...] = alpha * acc_ref[...] + pv
        m_ref[...] = m_new

    j_diag = pl.multiple_of(i * blk, blk)
    s_diag = jnp.dot(k_ref[pl.ds(j_diag, blk), :], qz_ref[:HEAD, :], preferred_element_type=F32) + diag_ref[0]
    update(s_diag, vt_ref[i], 0.0, True)

    def body(j, carry):
        j0 = pl.multiple_of(j * blk, blk)
        s = jnp.dot(kaug_ref[pl.ds(j0, blk), :], qz_ref[...], preferred_element_type=F32)
        update(s, vt_ref[j], -slope * ((i - j) * blk).astype(F32), False)
        return carry

    lax.fori_loop(0, i, body, 0)
    acc = acc_ref[...]
    l = l_ref[...]
    ot = acc[:, :blk] / l[:, :blk] - lam_ref[0] * (acc[:, blk:] / l[:, blk:])
    ms = jnp.mean(ot * ot, axis=0, keepdims=True)
    ot = ot * lax.rsqrt(ms + SUBLN_EPS) * w_ref[...] * (1.0 - lam_init)
    o_ref[...] = ot.T.astype(o_ref.dtype)


def _attn_prompt(qt_bf, k_bf, vt_bf, lam, subln_w, bsz, seq, blk, lam_init):
    nq = seq // blk
    a = jnp.arange(blk, dtype=jnp.int32)
    rel = jnp.abs(a[None, :] - a[:, None]).astype(F32)
    visible = (a[:, None] // CHUNK) <= (a[None, :] // CHUNK)
    slopes = jnp.asarray(_alibi_slopes(), F32)[:, None, None]
    diag = jnp.where(visible[None], -slopes * rel[None], -jnp.inf)
    diag = jnp.concatenate([diag, diag], axis=2)
    kern = functools.partial(_attn_prompt_kernel, blk=blk, lam_init=lam_init)
    return pl.pallas_call(
        kern,
        grid=(bsz, N_HEADS, nq),
        in_specs=[pl.BlockSpec(memory_space=pltpu.SMEM),
                  pl.BlockSpec((1, HEAD, blk), lambda b, h, i: (b * nq + i, h, 0)),
                  pl.BlockSpec((seq, HEAD), lambda b, h, i: (b, h)),
                  pl.BlockSpec((nq, HEAD, blk), lambda b, h, i: (b, h, 0)),
                  pl.BlockSpec((1, blk, 2 * blk), lambda b, h, i: (h, 0, 0)),
                  pl.BlockSpec((HEAD, 1), lambda b, h, i: (0, 0))],
        out_specs=pl.BlockSpec((blk, HEAD), lambda b, h, i: (b * nq + i, h)),
        scratch_shapes=[pltpu.VMEM((seq, 2 * HEAD), BF16),
                        pltpu.VMEM((2 * HEAD, 2 * blk), BF16),
                        pltpu.VMEM((1, 2 * blk), F32),
                        pltpu.VMEM((1, 2 * blk), F32),
                        pltpu.VMEM((HEAD, 2 * blk), F32)],
        out_shape=jax.ShapeDtypeStruct((bsz * seq, GROUP), BF16),
        compiler_params=_params("parallel", "parallel", "arbitrary"),
        name="attn_prompt",
    )(lam, qt_bf, k_bf, vt_bf, diag, subln_w.reshape(HEAD, 1))


def _attn_sample_kernel(lam_ref, q_ref, kn_ref, vn_ref, kc_ref, vc_ref, w_ref, o_ref, *, seq, past, lam_init):
    slope = _head_slope(pl.program_id(1))
    qz = _stack_halves(q_ref[...])
    nt = (((1,), (1,)), ((), ()))
    qpos = past + lax.broadcasted_iota(jnp.int32, (2 * seq, 1), 0) % seq
    kpos_c = lax.broadcasted_iota(jnp.int32, (1, past), 1)
    kpos_n = past + lax.broadcasted_iota(jnp.int32, (1, seq), 1)
    s_c = lax.dot_general(qz, kc_ref[0].astype(BF16), nt, preferred_element_type=F32)
    s_c = s_c - slope * jnp.abs(qpos - kpos_c).astype(F32)
    s_n = lax.dot_general(qz, kn_ref[...], nt, preferred_element_type=F32)
    s_n = s_n - slope * jnp.abs(qpos - kpos_n).astype(F32)
    m = jnp.maximum(jnp.max(s_c, axis=-1, keepdims=True), jnp.max(s_n, axis=-1, keepdims=True))
    p_c = jnp.exp(s_c - m)
    p_n = jnp.exp(s_n - m)
    l = jnp.sum(p_c, axis=-1, keepdims=True) + jnp.sum(p_n, axis=-1, keepdims=True)
    acc = (jnp.dot(p_c.astype(BF16), vc_ref[0].astype(BF16), preferred_element_type=F32)
           + jnp.dot(p_n.astype(BF16), vn_ref[...], preferred_element_type=F32))
    o_ref[...] = _diff_norm(acc, l, lam_ref[0], w_ref[...], lam_init, seq).astype(o_ref.dtype)


def _attn_sample(q_bf, k_bf, v_bf, cache_k, cache_v, lam, subln_w, bsz, seq, lam_init):
    past = cache_k.shape[1]
    kern = functools.partial(_attn_sample_kernel, seq=seq, past=past, lam_init=lam_init)
    new = pl.BlockSpec((seq, HEAD), lambda b, h: (b, h))
    cache = pl.BlockSpec((1, past, HEAD), lambda b, h: (b, 0, h))
    return pl.pallas_call(
        kern,
        grid=(bsz, N_HEADS),
        in_specs=[pl.BlockSpec(memory_space=pltpu.SMEM), new, new, new, cache, cache,
                  pl.BlockSpec((1, HEAD), lambda b, h: (0, 0))],
        out_specs=new,
        out_shape=jax.ShapeDtypeStruct((bsz * seq, GROUP), BF16),
        compiler_params=_params("parallel", "parallel"),
        name="attn_sample",
    )(lam, q_bf, k_bf, v_bf, cache_k.reshape(bsz, past, GROUP), cache_v.reshape(bsz, past, GROUP), subln_w)


def _split3(x):
    hi = x.astype(BF16)
    r1 = x - hi.astype(F32)
    mid = r1.astype(BF16)
    lo = (r1 - mid.astype(F32)).astype(BF16)
    return hi, mid, lo


def _unit_lower_inverse(a_strict, eye):
    c = a_strict.shape[0]
    t = eye - a_strict
    p = a_strict
    steps = int(math.log2(c)) - 1
    for _ in range(steps):
        pb = p.astype(BF16)
        p = jnp.dot(pb, pb, preferred_element_type=F32)
        t = t + jnp.dot(t.astype(BF16), p.astype(BF16), preferred_element_type=F32)
    return t


def _gdn_prep_kernel(cin_ref, halo_ref, past_ref, ab_ref, cw_ref, alog_ref, dtb_ref,
                     u_ref, w_ref, qd_ref, kd_ref, qk_ref, gl_ref, *, chunk, n_sub):
    c_idx = pl.program_id(1)
    rows = chunk * n_sub
    prev = jnp.where(c_idx == 0, past_ref[0], halo_ref[...])
    xin = jnp.concatenate([prev, cin_ref[...]], axis=0)
    conv = sum(xin[SUBLANES - (CONV_W - 1) + j: SUBLANES - (CONV_W - 1) + j + rows] * cw_ref[j:j + 1, :]
               for j in range(CONV_W))
    conv = conv * jax.nn.sigmoid(conv)

    ab = ab_ref[...]
    lane = lax.broadcasted_iota(jnp.int32, ab.shape, 1)
    pre = ab + dtb_ref[...]
    softplus = jnp.maximum(pre, 0.0) + jnp.log(1.0 + jnp.exp(-jnp.abs(pre)))
    g_all = -jnp.exp(alog_ref[...]) * softplus
    g_all = jnp.where(lane < N_HEADS, g_all, 0.0)
    beta_all = jax.nn.sigmoid(ab)

    ri = lax.broadcasted_iota(jnp.int32, (chunk, chunk), 0)
    ci = lax.broadcasted_iota(jnp.int32, (chunk, chunk), 1)
    incl = ri >= ci
    strict = ri > ci
    tril = jnp.where(incl, 1.0, 0.0).astype(BF16)
    eye = jnp.where(ri == ci, 1.0, 0.0).astype(F32)
    nt = (((1,), (1,)), ((), ()))

    for sc in range(n_sub):
        r0 = sc * chunk
        g = g_all[r0:r0 + chunk]
        gc = sum(jnp.dot(tril, part, preferred_element_type=F32) for part in _split3(g))
        gct = gc.T
        gl_ref[sc] = gc[chunk - 1:chunk, :]
        for h in range(N_HEADS):
            col = slice(h * HEAD, (h + 1) * HEAD)
            qh = conv[r0:r0 + chunk, h * HEAD:(h + 1) * HEAD]
            kh = conv[r0:r0 + chunk, GROUP + h * HEAD:GROUP + (h + 1) * HEAD]
            vh = conv[r0:r0 + chunk, 2 * GROUP + h * HEAD:2 * GROUP + (h + 1) * HEAD]
            qh = qh * lax.rsqrt(jnp.sum(qh * qh, axis=-1, keepdims=True) + L2_EPS) * (HEAD ** -0.5)
            kh = kh * lax.rsqrt(jnp.sum(kh * kh, axis=-1, keepdims=True) + L2_EPS)
            beta = beta_all[r0:r0 + chunk, N_HEADS + h:N_HEADS + h + 1]
            gcol = gc[:, h:h + 1]
            grow = gct[h:h + 1, :]
            gamma = jnp.exp(jnp.where(incl, gcol - grow, -jnp.inf))
            egc = jnp.exp(gcol)
            kb = kh * beta
            kbb = kb.astype(BF16)
            khb = kh.astype(BF16)
            a = jnp.where(strict, lax.dot_general(kbb, khb, nt, preferred_element_type=F32) * gamma, 0.0)
            t_inv = _unit_lower_inverse(a, eye).astype(BF16)
            u_ref[r0:r0 + chunk, col] = jnp.dot(t_inv, (vh * beta).astype(BF16), preferred_element_type=F32)
            w_ref[r0:r0 + chunk, col] = jnp.dot(t_inv, (kb * egc).astype(BF16),
                                                preferred_element_type=F32).astype(BF16)
            qk = lax.dot_general(qh.astype(BF16), khb, nt, preferred_element_type=F32) * gamma
            qk_ref[h, r0:r0 + chunk, :] = qk.astype(BF16)
            qd_ref[r0:r0 + chunk, col] = (qh * egc).astype(BF16)
            kd_ref[r0:r0 + chunk, col] = (kh * jnp.exp(gc[chunk - 1:chunk, h:h + 1] - gcol)).astype(BF16)


def _gdn_prep(cin, ab, past8, conv_w, a_log, dt_bias, bsz, seq, chunk, n_sub):
    rows = chunk * n_sub
    nblk = seq // rows
    t = bsz * seq
    lanes = lambda v: jnp.pad(v.reshape(1, N_HEADS).astype(F32), ((0, 0), (0, LANES - N_HEADS)))
    kern = functools.partial(_gdn_prep_kernel, chunk=chunk, n_sub=n_sub)
    rowblk = lambda b, c: (b * nblk + c, 0)
    halo = lambda b, c: (jnp.maximum((b * nblk + c) * (rows // SUBLANES) - 1, 0), 0)
    const = lambda b, c: (0, 0)
    return pl.pallas_call(
        kern,
        grid=(bsz, nblk),
        in_specs=[pl.BlockSpec((rows, CONV_CH), rowblk),
                  pl.BlockSpec((SUBLANES, CONV_CH), halo),
                  pl.BlockSpec((1, SUBLANES, CONV_CH), lambda b, c: (b, 0, 0)),
                  pl.BlockSpec((rows, LANES), rowblk),
                  pl.BlockSpec((CONV_W, CONV_CH), const),
                  pl.BlockSpec((1, LANES), const),
                  pl.BlockSpec((1, LANES), const)],
        out_specs=[pl.BlockSpec((rows, GROUP), rowblk),
                   pl.BlockSpec((rows, GROUP), rowblk),
                   pl.BlockSpec((rows, GROUP), rowblk),
                   pl.BlockSpec((rows, GROUP), rowblk),
                   pl.BlockSpec((N_HEADS, rows, chunk), lambda b, c: (0, b * nblk + c, 0)),
                   pl.BlockSpec((n_sub, 1, LANES), lambda b, c: (b * nblk + c, 0, 0))],
        out_shape=[jax.ShapeDtypeStruct((t, GROUP), F32),
                   jax.ShapeDtypeStruct((t, GROUP), BF16),
                   jax.ShapeDtypeStruct((t, GROUP), BF16),
                   jax.ShapeDtypeStruct((t, GROUP), BF16),
                   jax.ShapeDtypeStruct((N_HEADS, t, chunk), BF16),
                   jax.ShapeDtypeStruct((t // chunk, 1, LANES), F32)],
        compiler_params=_params("parallel", "parallel"),
        name="gdn_prep",
    )(cin, cin, past8, ab, conv_w, lanes(a_log), lanes(dt_bias))


def _gdn_scan_kernel(u_ref, w_ref, qd_ref, kd_ref, qk_ref, gl_ref, z_ref, s0_ref, nw_ref,
                     o_ref, sf_ref, s_ref, *, bsz, chunk):
    c_idx = pl.program_id(0)

    @pl.when(c_idx == 0)
    def _():
        s_ref[...] = s0_ref[...]

    tn = (((0,), (0,)), ((), ()))
    for b in range(bsz):
        glast = jnp.exp(gl_ref[b, 0])
        for h in range(N_HEADS):
            col = slice(h * HEAD, (h + 1) * HEAD)
            s = s_ref[b, h]
            wq = jnp.concatenate([w_ref[b, :, col], qd_ref[b, :, col]], axis=0)
            r = jnp.dot(wq, s.astype(BF16), preferred_element_type=F32)
            v_new = (u_ref[b, :, col] - r[:chunk]).astype(BF16)
            o = r[chunk:] + jnp.dot(qk_ref[h, b], v_new, preferred_element_type=F32)
            s_ref[b, h] = s * glast[:, h:h + 1] + lax.dot_general(kd_ref[b, :, col], v_new, tn,
                                                                 preferred_element_type=F32)
            zh = z_ref[b, :, col]
            ms = jnp.mean(o * o, axis=-1, keepdims=True)
            o = o * lax.rsqrt(ms + GATED_NORM_EPS) * nw_ref[...] * (zh * jax.nn.sigmoid(zh))
            o_ref[b, :, col] = o.astype(o_ref.dtype)

    @pl.when(c_idx == pl.num_programs(0) - 1)
    def _():
        sf_ref[...] = s_ref[...]


def _gdn_scan(u, w, qd, kd, qk, gl, z, s0, norm_w, bsz, seq, chunk):
    nc = seq // chunk
    kern = functools.partial(_gdn_scan_kernel, bsz=bsz, chunk=chunk)
    tok = pl.BlockSpec((bsz, chunk, GROUP), lambda c: (0, c, 0))
    state = pl.BlockSpec((bsz, N_HEADS, HEAD, HEAD), lambda c: (0, 0, 0, 0))
    o, s_final = pl.pallas_call(
        kern,
        grid=(nc,),
        in_specs=[tok, tok, tok, tok,
                  pl.BlockSpec((N_HEADS, bsz, chunk, chunk), lambda c: (0, 0, c, 0)),
                  pl.BlockSpec((bsz, 1, 1, LANES), lambda c: (0, c, 0, 0)),
                  tok, state,
                  pl.BlockSpec((1, HEAD), lambda c: (0, 0))],
        out_specs=[tok, state],
        out_shape=[jax.ShapeDtypeStruct((bsz, seq, GROUP), BF16),
                   jax.ShapeDtypeStruct((bsz, N_HEADS, HEAD, HEAD), F32)],
        scratch_shapes=[pltpu.VMEM((bsz, N_HEADS, HEAD, HEAD), F32)],
        compiler_params=_params("arbitrary"),
        name="gdn_scan",
    )(u.reshape(bsz, seq, GROUP), w.reshape(bsz, seq, GROUP), qd.reshape(bsz, seq, GROUP),
      kd.reshape(bsz, seq, GROUP), qk.reshape(N_HEADS, bsz, seq, chunk), gl.reshape(bsz, nc, 1, LANES),
      z.reshape(bsz, seq, GROUP), s0, norm_w)
    return o.reshape(bsz * seq, GROUP), s_final


def _layernorm(x, g, b):
    mu = jnp.mean(x, axis=-1, keepdims=True)
    xc = x - mu
    var = jnp.mean(xc * xc, axis=-1, keepdims=True)
    return xc * lax.rsqrt(var + LN_EPS) * g + b


def _out_router_kernel(att_ref, o_ref, x_ref, wo_ref, g_ref, b_ref, rw_ref, rb_ref,
                       x1_ref, x1b_ref, idx_ref, gate_ref, rank_ref, cnt_ref, carry_ref, *, tm):
    step = pl.program_id(0)

    @pl.when(step == 0)
    def _():
        carry_ref[...] = jnp.zeros_like(carry_ref)

    mix = (jnp.dot(att_ref[...], wo_ref[:GROUP, :], preferred_element_type=F32)
           + jnp.dot(o_ref[...], wo_ref[GROUP:, :], preferred_element_type=F32))
    x1 = _layernorm(DEEPNORM_ALPHA * x_ref[...] + mix, g_ref[...], b_ref[...])
    x1_ref[...] = x1
    x1b_ref[...] = x1.astype(BF16)

    logits = jnp.dot(x1, rw_ref[...], preferred_element_type=F32, precision=lax.Precision.HIGHEST) + rb_ref[...]
    lane = lax.broadcasted_iota(jnp.int32, logits.shape, 1)
    work = jnp.where(lane < N_EXPERTS, logits, -jnp.inf)
    vals, idxs = [], []
    for _ in range(TOP_K):
        m = jnp.max(work, axis=-1, keepdims=True)
        am = jnp.min(jnp.where(work == m, lane, LANES), axis=-1, keepdims=True)
        vals.append(m)
        idxs.append(am)
        work = jnp.where(lane == am, -jnp.inf, work)
    exps = [jnp.exp(v - vals[0]) for v in vals]
    denom = exps[0] + exps[1] + exps[2] + exps[3]
    chosen = jnp.zeros(logits.shape, F32)
    gate_out = jnp.zeros(logits.shape, F32)
    idx_out = jnp.zeros(logits.shape, jnp.int32)
    for k in range(TOP_K):
        chosen = jnp.where(lane == idxs[k], 1.0, chosen)
        gate_out = jnp.where(lane == k, exps[k] / denom, gate_out)
        idx_out = jnp.where(lane == k, idxs[k], idx_out)
    ri = lax.broadcasted_iota(jnp.int32, (tm, tm), 0)
    ci = lax.broadcasted_iota(jnp.int32, (tm, tm), 1)
    before = jnp.where(ri > ci, 1.0, 0.0).astype(BF16)
    prefix = jnp.dot(before, chosen.astype(BF16), preferred_element_type=F32) + carry_ref[...]
    rank_out = jnp.zeros(logits.shape, F32)
    for k in range(TOP_K):
        r = jnp.sum(jnp.where(lane == idxs[k], prefix, 0.0), axis=-1, keepdims=True)
        rank_out = jnp.where(lane == k, r, rank_out)
    carry_ref[...] = carry_ref[...] + jnp.sum(chosen, axis=0, keepdims=True)
    idx_ref[...] = idx_out
    gate_ref[...] = gate_out
    rank_ref[...] = rank_out.astype(jnp.int32)
    cnt_ref[...] = carry_ref[...].astype(jnp.int32)


def _out_router(att, o, x2d, w_out_bf, ln_g, ln_b, router_w, router_b, tm):
    t = x2d.shape[0]
    row = lambda i: (i, 0)
    const = lambda i: (0, 0)
    rw = jnp.pad(router_w, ((0, 0), (0, LANES - N_EXPERTS)))
    rb = jnp.pad(router_b.reshape(1, N_EXPERTS), ((0, 0), (0, LANES - N_EXPERTS)))
    kern = functools.partial(_out_router_kernel, tm=tm)
    return pl.pallas_call(
        kern,
        grid=(t // tm,),
        in_specs=[pl.BlockSpec((tm, GROUP), row), pl.BlockSpec((tm, GROUP), row),
                  pl.BlockSpec((tm, D_MODEL), row),
                  pl.BlockSpec((2 * GROUP, D_MODEL), const),
                  pl.BlockSpec((1, D_MODEL), const), pl.BlockSpec((1, D_MODEL), const),
                  pl.BlockSpec((D_MODEL, LANES), const), pl.BlockSpec((1, LANES), const)],
        out_specs=[pl.BlockSpec((tm, D_MODEL), row), pl.BlockSpec((tm, D_MODEL), row),
                   pl.BlockSpec((tm, LANES), row), pl.BlockSpec((tm, LANES), row),
                   pl.BlockSpec((tm, LANES), row), pl.BlockSpec((1, LANES), const)],
        out_shape=[jax.ShapeDtypeStruct((t, D_MODEL), F32), jax.ShapeDtypeStruct((t, D_MODEL), BF16),
                   jax.ShapeDtypeStruct((t, LANES), jnp.int32), jax.ShapeDtypeStruct((t, LANES), F32),
                   jax.ShapeDtypeStruct((t, LANES), jnp.int32), jax.ShapeDtypeStruct((1, LANES), jnp.int32)],
        scratch_shapes=[pltpu.VMEM((1, LANES), F32)],
        compiler_params=_params("arbitrary"),
        name="out_router",
    )(att, o, x2d, w_out_bf, ln_g.reshape(1, D_MODEL), ln_b.reshape(1, D_MODEL), rw, rb)


def _expert_kernel(be_ref, nb_ref, x_ref, g_ref, wgu_ref, bgu_ref, wd_ref, bd_ref, y_ref):
    blk = pl.program_id(0)

    @pl.when(blk < nb_ref[0])
    def _():
        x = x_ref[...]
        h = jnp.dot(x, wgu_ref[0].astype(BF16), preferred_element_type=F32) + bgu_ref[0]
        gate = jnp.minimum(h[:, :D_FF], SWIGLU_LIMIT)
        up = jnp.clip(h[:, D_FF:], -SWIGLU_LIMIT, SWIGLU_LIMIT)
        act = (up + 1.0) * (gate * jax.nn.sigmoid(SWIGLU_ALPHA * gate))
        y = jnp.dot(act.astype(BF16), wd_ref[0].astype(BF16), preferred_element_type=F32) + bd_ref[0]
        y_ref[...] = y * g_ref[...]

    @pl.when(blk >= nb_ref[0])
    def _():
        y_ref[...] = jnp.zeros_like(y_ref)


def _experts(xb, gates_rows, blk_e, n_used, w_gu, b_gu, w_down, b_down, bm):
    rows = xb.shape[0]
    n_blocks = rows // bm
    return pl.pallas_call(
        _expert_kernel,
        grid_spec=pltpu.PrefetchScalarGridSpec(
            num_scalar_prefetch=2,
            grid=(n_blocks,),
            in_specs=[pl.BlockSpec((bm, D_MODEL), lambda i, be, nb: (i, 0)),
                      pl.BlockSpec((bm, 1), lambda i, be, nb: (i, 0)),
                      pl.BlockSpec((1, D_MODEL, 2 * D_FF), lambda i, be, nb: (be[i], 0, 0)),
                      pl.BlockSpec((1, 1, 2 * D_FF), lambda i, be, nb: (be[i], 0, 0)),
                      pl.BlockSpec((1, D_FF, D_MODEL), lambda i, be, nb: (be[i], 0, 0)),
                      pl.BlockSpec((1, 1, D_MODEL), lambda i, be, nb: (be[i], 0, 0))],
            out_specs=pl.BlockSpec((bm, D_MODEL), lambda i, be, nb: (i, 0))),
        out_shape=jax.ShapeDtypeStruct((rows, D_MODEL), F32),
        compiler_params=_params("arbitrary"),
        name="experts",
    )(blk_e, n_used, xb, gates_rows, w_gu, b_gu.reshape(N_EXPERTS, 1, 2 * D_FF),
      w_down, b_down.reshape(N_EXPERTS, 1, D_MODEL))


def _final_ln_kernel(x_ref, y_ref, g_ref, b_ref, o_ref):
    y = y_ref[:, 0, :] + y_ref[:, 1, :] + y_ref[:, 2, :] + y_ref[:, 3, :]
    o_ref[...] = _layernorm(DEEPNORM_ALPHA * x_ref[...] + y, g_ref[...], b_ref[...])


def _final_ln(x1, y4, ln_g, ln_b, tm):
    t = x1.shape[0]
    row = lambda i: (i, 0)
    const = lambda i: (0, 0)
    return pl.pallas_call(
        _final_ln_kernel,
        grid=(t // tm,),
        in_specs=[pl.BlockSpec((tm, D_MODEL), row),
                  pl.BlockSpec((tm, TOP_K, D_MODEL), lambda i: (i, 0, 0)),
                  pl.BlockSpec((1, D_MODEL), const), pl.BlockSpec((1, D_MODEL), const)],
        out_specs=pl.BlockSpec((tm, D_MODEL), row),
        out_shape=jax.ShapeDtypeStruct((t, D_MODEL), F32),
        compiler_params=_params("parallel"),
        name="final_ln",
    )(x1, y4, ln_g.reshape(1, D_MODEL), ln_b.reshape(1, D_MODEL))


MOE_BM = 256
ATTN_BLK = 512
GDN_SUB = 2


def _mixers(x2d, bsz, seq, cache_k, cache_v, conv_past, s0, lam, lam_init, w_in_bf, conv_w, a_log, dt_bias,
            delta_norm_w, subln_w):
    prompt = cache_k is None
    tm = ATTN_BLK if prompt else x2d.shape[0]
    q_bf, k_f, v_f, k_bf, v_bf, cin, z, ab = _in_proj(x2d, w_in_bf, tm, prompt)
    if prompt:
        att = _attn_prompt(q_bf, k_bf, v_bf, lam, subln_w, bsz, seq, ATTN_BLK, lam_init)
    else:
        att = _attn_sample(q_bf, k_bf, v_bf, cache_k, cache_v, lam, subln_w.reshape(1, HEAD), bsz, seq, lam_init)
    chunk = CHUNK if seq % CHUNK == 0 else seq
    n_sub = GDN_SUB if (seq // chunk) % GDN_SUB == 0 else 1
    past8 = jnp.pad(conv_past, ((0, 0), (SUBLANES - (CONV_W - 1), 0), (0, 0)))
    u, w, qd, kd, qk, gl = _gdn_prep(cin, ab, past8, conv_w, a_log, dt_bias, bsz, seq, chunk, n_sub)
    o, s_new = _gdn_scan(u, w, qd, kd, qk, gl, z, s0, delta_norm_w.reshape(1, HEAD), bsz, seq, chunk)
    return att, o, k_f, v_f, cin, s_new


def _moe(x1, x1b, idx, gates, rank, counts, w_gu, b_gu, w_down, b_down, bm):
    t = x1.shape[0]
    n = t * TOP_K
    counts = counts[0, :N_EXPERTS]
    padded = (counts + bm - 1) // bm * bm
    pad_end = jnp.cumsum(padded)
    pad_start = pad_end - padded
    top_idx = idx[:, :TOP_K]
    dest = pad_start[top_idx] + rank[:, :TOP_K]
    n_blocks = -(-n // bm) + N_EXPERTS
    rows = n_blocks * bm
    flat_dest = dest.reshape(n)
    buf_t = jnp.zeros((rows,), jnp.int32).at[flat_dest].set(jnp.arange(n, dtype=jnp.int32) // TOP_K)
    buf_g = jnp.zeros((rows,), F32).at[flat_dest].set(gates[:, :TOP_K].reshape(n))
    blk_start = jnp.arange(n_blocks, dtype=jnp.int32) * bm
    blk_e = jnp.minimum(jnp.sum(pad_end[None, :] <= blk_start[:, None], axis=1), N_EXPERTS - 1).astype(jnp.int32)
    n_used = (pad_end[-1] // bm).astype(jnp.int32).reshape(1)
    xb = x1b[buf_t]
    yb = _experts(xb, buf_g.reshape(rows, 1), blk_e, n_used, w_gu, b_gu, w_down, b_down, bm)
    return yb[flat_dest].reshape(t, TOP_K, D_MODEL)


def kernel(x_prompt, x_sample, cache_k, cache_v, state_conv, state_delta, w_in, conv_w, a_log, dt_bias,
           delta_norm_w, lambda_q1, lambda_k1, lambda_q2, lambda_k2, subln_w, w_out, ln1_g, ln1_b,
           router_w, router_b, w_gu, b_gu, w_down, b_down, ln2_g, ln2_b):
    bp, lp, _ = x_prompt.shape
    bs, ls, _ = x_sample.shape
    l = 0
    lam_init = 0.8 - 0.6 * math.exp(-0.3 * l)
    lam = (jnp.exp(jnp.sum(lambda_q1[l] * lambda_k1[l])) - jnp.exp(jnp.sum(lambda_q2[l] * lambda_k2[l]))
           + lam_init).reshape(1).astype(F32)
    w_in_bf = jnp.pad(w_in[l], ((0, 0), (0, IN_COLS_PAD - IN_COLS))).astype(BF16)
    shared = (lam, lam_init, w_in_bf, conv_w[l], a_log[l], dt_bias[l], delta_norm_w[l], subln_w[l])

    xp = x_prompt.reshape(bp * lp, D_MODEL)
    xs = x_sample.reshape(bs * ls, D_MODEL)
    zero_conv = jnp.zeros((bp, CONV_W - 1, CONV_CH), F32)
    zero_s = jnp.zeros((bp, N_HEADS, HEAD, HEAD), F32)
    att_p, o_p, k_p, v_p, cin_p, s_p = _mixers(xp, bp, lp, None, None, zero_conv, zero_s, *shared)
    att_s, o_s, k_s, v_s, cin_s, s_s = _mixers(xs, bs, ls, cache_k[l], cache_v[l], state_conv[l],
                                               state_delta[l], *shared)

    att = jnp.concatenate([att_p, att_s], axis=0)
    o = jnp.concatenate([o_p, o_s], axis=0)
    x_all = jnp.concatenate([xp, xs], axis=0)
    t = x_all.shape[0]
    tm = next(c for c in (512, 384, 256, 128) if t % c == 0)
    x1, x1b, idx, gates, rank, counts = _out_router(att, o, x_all, w_out[l].astype(BF16), ln1_g[l], ln1_b[l],
                                                    router_w[l], router_b[l], tm)
    y4 = _moe(x1, x1b, idx, gates, rank, counts, w_gu[l], b_gu[l], w_down[l], b_down[l], MOE_BM)
    y = _final_ln(x1, y4, ln2_g[l], ln2_b[l], 128)

    tp = bp * lp
    conv_tail = lambda cin, b, s: cin.reshape(b, s, CONV_CH)[:, s - (CONV_W - 1):][None]
    return (y[:tp].reshape(bp, lp, D_MODEL), y[tp:].reshape(bs, ls, D_MODEL),
            k_p.reshape(1, bp, lp, N_HEADS, HEAD), v_p.reshape(1, bp, lp, N_HEADS, HEAD),
            conv_tail(cin_p, bp, lp), s_p[None].astype(state_delta.dtype),
            k_s.reshape(1, bs, ls, N_HEADS, HEAD), v_s.reshape(1, bs, ls, N_HEADS, HEAD),
            conv_tail(cin_s, bs, ls), s_s[None].astype(state_delta.dtype))
```

```python
import functools
import math

import jax
import jax.numpy as jnp
from jax import lax
from jax.experimental import pallas as pl
from jax.experimental.pallas import tpu as pltpu

F32 = jnp.float32
BF16 = jnp.bfloat16

D_MODEL = 1024
HEAD = 128
N_HEADS = 4
DQK = HEAD // 2
GROUP = N_HEADS * HEAD
CONV_W = 4
CONV_CH = 3 * GROUP
CHUNK = 64
ALIBI_MAX = 8.0
N_EXPERTS = 32
TOP_K = 4
D_FF = D_MODEL
SWIGLU_LIMIT = 7.0
SWIGLU_ALPHA = 1.702
DEPTH = 1
DEEPNORM_ALPHA = (2 * DEPTH) ** 0.25
LN_EPS = 1e-5
SUBLN_EPS = 1e-5
GATED_NORM_EPS = 1e-6
L2_EPS = 1e-6

LANES = 128
SUBLANES = 8
BF16_EXACT_INT = 256
VMEM_LIMIT = 56 * 1024 * 1024

COL_Q, COL_K, COL_V, COL_CONV = 0, GROUP, 2 * GROUP, 3 * GROUP
COL_Z = COL_CONV + CONV_CH
COL_AB = COL_Z + GROUP
IN_COLS = COL_AB + 2 * N_HEADS
IN_COLS_PAD = COL_AB + LANES


def _params(*sem):
    return pltpu.CompilerParams(dimension_semantics=sem, vmem_limit_bytes=VMEM_LIMIT)


def _in_proj_kernel(x_ref, w_ref, q_ref, kf_ref, vf_ref, kb_ref, vb_ref, c_ref, z_ref, ab_ref, *, tm, transposed):
    xb = x_ref[...].astype(BF16)

    def section(lo, hi):
        return jnp.dot(xb, w_ref[:, lo:hi], preferred_element_type=F32)

    q = section(COL_Q, COL_K) * (DQK ** -0.5)
    k = section(COL_K, COL_V)
    kf_ref[...] = k
    kb_ref[...] = k.astype(BF16)
    v = section(COL_V, COL_CONV)
    vf_ref[...] = v
    if transposed:
        q_ref[0] = q.T.astype(BF16)
        vb_ref[0] = v.T.astype(BF16)
    else:
        q_ref[...] = q.astype(BF16)
        vb_ref[...] = v.astype(BF16)
    c_ref[...] = section(COL_CONV, COL_Z)
    z_ref[...] = section(COL_Z, COL_AB)
    ab_ref[...] = section(COL_AB, IN_COLS_PAD)


def _in_proj(x2d, w_bf, tm, transposed):
    t = x2d.shape[0]
    row = lambda i: (i, 0)
    widths = (GROUP, GROUP, GROUP, GROUP, GROUP, CONV_CH, GROUP, LANES)
    dtypes = (BF16, F32, F32, BF16, BF16, F32, F32, F32)
    out_specs = [pl.BlockSpec((tm, w), row) for w in widths]
    out_shape = [jax.ShapeDtypeStruct((t, w), d) for w, d in zip(widths, dtypes)]
    if transposed:
        for slot in (0, 4):
            out_specs[slot] = pl.BlockSpec((1, GROUP, tm), lambda i: (i, 0, 0))
            out_shape[slot] = jax.ShapeDtypeStruct((t // tm, GROUP, tm), BF16)
    return pl.pallas_call(
        functools.partial(_in_proj_kernel, tm=tm, transposed=transposed),
        grid=(t // tm,),
        in_specs=[pl.BlockSpec((tm, D_MODEL), row),
                  pl.BlockSpec((D_MODEL, IN_COLS_PAD), lambda i: (0, 0))],
        out_specs=out_specs,
        out_shape=out_shape,
        compiler_params=_params("parallel"),
        name="in_proj",
    )(x2d, w_bf)


def _alibi_slopes():
    return [2.0 ** (-ALIBI_MAX * (h + 1) / N_HEADS) for h in range(N_HEADS)]


def _stack_halves(q):
    lane = lax.broadcasted_iota(jnp.int32, q.shape, 1)
    zero = jnp.zeros_like(q)
    return jnp.concatenate([jnp.where(lane < DQK, q, zero), jnp.where(lane < DQK, zero, q)], axis=0)


def _diff_norm(acc, l, lam, w, lam_init, rows):
    o = acc[:rows] / l[:rows] - lam * (acc[rows:] / l[rows:])
    ms = jnp.mean(o * o, axis=-1, keepdims=True)
    return o * lax.rsqrt(ms + SUBLN_EPS) * w * (1.0 - lam_init)


def _head_slope(h):
    s = _alibi_slopes()
    return jnp.where(h == 0, s[0], jnp.where(h == 1, s[1], jnp.where(h == 2, s[2], s[3]))).astype(F32)


def _attn_prompt_kernel(lam_ref, qt_ref, k_ref, vt_ref, diag_ref, w_ref, o_ref,
                        kaug_ref, qz_ref, m_ref, l_ref, acc_ref, *, blk, lam_init):
    i = pl.program_id(2)
    slope = _head_slope(pl.program_id(1))
    rows = 2 * blk
    n_kv = k_ref.shape[0] // blk

    @pl.when(i == 0)
    def _():
        lane = lax.broadcasted_iota(jnp.int32, (blk, HEAD), 1)
        pos = lax.broadcasted_iota(jnp.int32, (blk, HEAD), 0)
        lo = slope * (pos % BF16_EXACT_INT).astype(F32)
        hi = slope * (pos - pos % BF16_EXACT_INT).astype(F32)
        extra = jnp.where(lane < 2, 1.0, jnp.where(lane == 2, lo, jnp.where(lane == 3, hi, 0.0))).astype(BF16)

        def fill(j, carry):
            j0 = pl.multiple_of(j * blk, blk)
            kaug_ref[pl.ds(j0, blk), :HEAD] = k_ref[pl.ds(j0, blk), :]
            kaug_ref[pl.ds(j0, blk), HEAD:] = extra
            return carry

        lax.fori_loop(0, n_kv, fill, 0)

    qt = qt_ref[0]
    d = lax.broadcasted_iota(jnp.int32, qt.shape, 0)
    zero = jnp.zeros_like(qt)
    qz_ref[:HEAD, :] = jnp.concatenate([jnp.where(d < DQK, qt, zero), jnp.where(d < DQK, zero, qt)], axis=1)
    a = lax.broadcasted_iota(jnp.int32, (HEAD, rows), 1) % blk
    a_lo = -slope * (a % BF16_EXACT_INT).astype(F32)
    a_hi = -slope * (a - a % BF16_EXACT_INT).astype(F32)
    r = lax.broadcasted_iota(jnp.int32, (HEAD, rows), 0)
    qz_ref[HEAD:, :] = jnp.where(r == 0, a_lo, jnp.where(r == 1, a_hi, jnp.where(r < 4, 1.0, 0.0))).astype(BF16)

    def update(s, vt, shift, first):
        m_tile = jnp.max(s, axis=0, keepdims=True) + shift
        if first:
            m_new = m_tile
        else:
            m_old = m_ref[...]
            m_new = jnp.maximum(m_old, m_tile)
        p = jnp.exp(s - (m_new - shift))
        pv = jnp.dot(vt, p.astype(BF16), preferred_element_type=F32)
        l_tile = jnp.sum(p, axis=0, keepdims=True)
        if first:
            l_ref[...] = l_tile
            acc_ref[...] = pv
        else:
            alpha = jnp.exp(m_old - m_new)
            l_ref[...] = alpha * l_ref[...] + l_tile
            acc_ref[...] = alpha * acc_ref[...] + pv
        m_ref[...] = m_new

    j_diag = pl.multiple_of(i * blk, blk)
    s_diag = jnp.dot(k_ref[pl.ds(j_diag, blk), :], qz_ref[:HEAD, :], preferred_element_type=F32) + diag_ref[0]
    update(s_diag, vt_ref[i], 0.0, True)

    def body(j, carry):
        j0 = pl.multiple_of(j * blk, blk)
        s = jnp.dot(kaug_ref[pl.ds(j0, blk), :], qz_ref[...], preferred_element_type=F32)
        update(s, vt_ref[j], -slope * ((i - j) * blk).astype(F32), False)
        return carry

    lax.fori_loop(0, i, body, 0)
    acc = acc_ref[...]
    l = l_ref[...]
    ot = acc[:, :blk] / l[:, :blk] - lam_ref[0] * (acc[:, blk:] / l[:, blk:])
    ms = jnp.mean(ot * ot, axis=0, keepdims=True)
    ot = ot * lax.rsqrt(ms + SUBLN_EPS) * w_ref[...] * (1.0 - lam_init)
    o_ref[...] = ot.T.astype(o_ref.dtype)


def _attn_prompt(qt_bf, k_bf, vt_bf, lam, subln_w, bsz, seq, blk, lam_init):
    nq = seq // blk
    a = jnp.arange(blk, dtype=jnp.int32)
    rel = jnp.abs(a[None, :] - a[:, None]).astype(F32)
    visible = (a[:, None] // CHUNK) <= (a[None, :] // CHUNK)
    slopes = jnp.asarray(_alibi_slopes(), F32)[:, None, None]
    diag = jnp.where(visible[None], -slopes * rel[None], -jnp.inf)
    diag = jnp.concatenate([diag, diag], axis=2)
    kern = functools.partial(_attn_prompt_kernel, blk=blk, lam_init=lam_init)
    return pl.pallas_call(
        kern,
        grid=(bsz, N_HEADS, nq),
        in_specs=[pl.BlockSpec(memory_space=pltpu.SMEM),
                  pl.BlockSpec((1, HEAD, blk), lambda b, h, i: (b * nq + i, h, 0)),
                  pl.BlockSpec((seq, HEAD), lambda b, h, i: (b, h)),
                  pl.BlockSpec((nq, HEAD, blk), lambda b, h, i: (b, h, 0)),
                  pl.BlockSpec((1, blk, 2 * blk), lambda b, h, i: (h, 0, 0)),
                  pl.BlockSpec((HEAD, 1), lambda b, h, i: (0, 0))],
        out_specs=pl.BlockSpec((blk, HEAD), lambda b, h, i: (b * nq + i, h)),
        scratch_shapes=[pltpu.VMEM((seq, 2 * HEAD), BF16),
                        pltpu.VMEM((2 * HEAD, 2 * blk), BF16),
                        pltpu.VMEM((1, 2 * blk), F32),
                        pltpu.VMEM((1, 2 * blk), F32),
                        pltpu.VMEM((HEAD, 2 * blk), F32)],
        out_shape=jax.ShapeDtypeStruct((bsz * seq, GROUP), BF16),
        compiler_params=_params("parallel", "parallel", "arbitrary"),
        name="attn_prompt",
    )(lam, qt_bf, k_bf, vt_bf, diag, subln_w.reshape(HEAD, 1))


def _attn_sample_kernel(lam_ref, q_ref, kn_ref, vn_ref, kc_ref, vc_ref, w_ref, o_ref, *, seq, past, lam_init):
    slope = _head_slope(pl.program_id(1))
    qz = _stack_halves(q_ref[...])
    nt = (((1,), (1,)), ((), ()))
    qpos = past + lax.broadcasted_iota(jnp.int32, (2 * seq, 1), 0) % seq
    kpos_c = lax.broadcasted_iota(jnp.int32, (1, past), 1)
    kpos_n = past + lax.broadcasted_iota(jnp.int32, (1, seq), 1)
    s_c = lax.dot_general(qz, kc_ref[0].astype(BF16), nt, preferred_element_type=F32)
    s_c = s_c - slope * jnp.abs(qpos - kpos_c).astype(F32)
    s_n = lax.dot_general(qz, kn_ref[...], nt, preferred_element_type=F32)
    s_n = s_n - slope * jnp.abs(qpos - kpos_n).astype(F32)
    m = jnp.maximum(jnp.max(s_c, axis=-1, keepdims=True), jnp.max(s_n, axis=-1, keepdims=True))
    p_c = jnp.exp(s_c - m)
    p_n = jnp.exp(s_n - m)
    l = jnp.sum(p_c, axis=-1, keepdims=True) + jnp.sum(p_n, axis=-1, keepdims=True)
    acc = (jnp.dot(p_c.astype(BF16), vc_ref[0].astype(BF16), preferred_element_type=F32)
           + jnp.dot(p_n.astype(BF16), vn_ref[...], preferred_element_type=F32))
    o_ref[...] = _diff_norm(acc, l, lam_ref[0], w_ref[...], lam_init, seq).astype(o_ref.dtype)


def _attn_sample(q_bf, k_bf, v_bf, cache_k, cache_v, lam, subln_w, bsz, seq, lam_init):
    past = cache_k.shape[1]
    kern = functools.partial(_attn_sample_kernel, seq=seq, past=past, lam_init=lam_init)
    new = pl.BlockSpec((seq, HEAD), lambda b, h: (b, h))
    cache = pl.BlockSpec((1, past, HEAD), lambda b, h: (b, 0, h))
    return pl.pallas_call(
        kern,
        grid=(bsz, N_HEADS),
        in_specs=[pl.BlockSpec(memory_space=pltpu.SMEM), new, new, new, cache, cache,
                  pl.BlockSpec((1, HEAD), lambda b, h: (0, 0))],
        out_specs=new,
        out_shape=jax.ShapeDtypeStruct((bsz * seq, GROUP), BF16),
        compiler_params=_params("parallel", "parallel"),
        name="attn_sample",
    )(lam, q_bf, k_bf, v_bf, cache_k.reshape(bsz, past, GROUP), cache_v.reshape(bsz, past, GROUP), subln_w)


def _split3(x):
    hi = x.astype(BF16)
    r1 = x - hi.astype(F32)
    mid = r1.astype(BF16)
    lo = (r1 - mid.astype(F32)).astype(BF16)
    return hi, mid, lo


def _unit_lower_inverse(a_strict, eye):
    c = a_strict.shape[0]
    t = eye - a_strict
    p = a_strict
    steps = int(math.log2(c)) - 1
    for _ in range(steps):
        pb = p.astype(BF16)
        p = jnp.dot(pb, pb, preferred_element_type=F32)
        t = t + jnp.dot(t.astype(BF16), p.astype(BF16), preferred_element_type=F32)
    return t


def _gdn_prep_kernel(cin_ref, halo_ref, past_ref, ab_ref, cw_ref, alog_ref, dtb_ref,
                     u_ref, w_ref, qd_ref, kd_ref, qk_ref, gl_ref, *, chunk, n_sub):
    c_idx = pl.program_id(1)
    rows = chunk * n_sub
    prev = jnp.where(c_idx == 0, past_ref[0], halo_ref[...])
    xin = jnp.concatenate([prev, cin_ref[...]], axis=0)
    conv = sum(xin[SUBLANES - (CONV_W - 1) + j: SUBLANES - (CONV_W - 1) + j + rows] * cw_ref[j:j + 1, :]
               for j in range(CONV_W))
    conv = conv * jax.nn.sigmoid(conv)

    ab = ab_ref[...]
    lane = lax.broadcasted_iota(jnp.int32, ab.shape, 1)
    pre = ab + dtb_ref[...]
    softplus = jnp.maximum(pre, 0.0) + jnp.log(1.0 + jnp.exp(-jnp.abs(pre)))
    g_all = -jnp.exp(alog_ref[...]) * softplus
    g_all = jnp.where(lane < N_HEADS, g_all, 0.0)
    beta_all = jax.nn.sigmoid(ab)

    ri = lax.broadcasted_iota(jnp.int32, (chunk, chunk), 0)
    ci = lax.broadcasted_iota(jnp.int32, (chunk, chunk), 1)
    incl = ri >= ci
    strict = ri > ci
    tril = jnp.where(incl, 1.0, 0.0).astype(BF16)
    eye = jnp.where(ri == ci, 1.0, 0.0).astype(F32)
    nt = (((1,), (1,)), ((), ()))

    for sc in range(n_sub):
        r0 = sc * chunk
        g = g_all[r0:r0 + chunk]
        gc = sum(jnp.dot(tril, part, preferred_element_type=F32) for part in _split3(g))
        gct = gc.T
        gl_ref[sc] = gc[chunk - 1:chunk, :]
        for h in range(N_HEADS):
            col = slice(h * HEAD, (h + 1) * HEAD)
            qh = conv[r0:r0 + chunk, h * HEAD:(h + 1) * HEAD]
            kh = conv[r0:r0 + chunk, GROUP + h * HEAD:GROUP + (h + 1) * HEAD]
            vh = conv[r0:r0 + chunk, 2 * GROUP + h * HEAD:2 * GROUP + (h + 1) * HEAD]
            qh = qh * lax.rsqrt(jnp.sum(qh * qh, axis=-1, keepdims=True) + L2_EPS) * (HEAD ** -0.5)
            kh = kh * lax.rsqrt(jnp.sum(kh * kh, axis=-1, keepdims=True) + L2_EPS)
            beta = beta_all[r0:r0 + chunk, N_HEADS + h:N_HEADS + h + 1]
            gcol = gc[:, h:h + 1]
            grow = gct[h:h + 1, :]
            gamma = jnp.exp(jnp.where(incl, gcol - grow, -jnp.inf))
            egc = jnp.exp(gcol)
            kb = kh * beta
            kbb = kb.astype(BF16)
            khb = kh.astype(BF16)
            a = jnp.where(strict, lax.dot_general(kbb, khb, nt, preferred_element_type=F32) * gamma, 0.0)
            t_inv = _unit_lower_inverse(a, eye).astype(BF16)
            u_ref[r0:r0 + chunk, col] = jnp.dot(t_inv, (vh * beta).astype(BF16), preferred_element_type=F32)
            w_ref[r0:r0 + chunk, col] = jnp.dot(t_inv, (kb * egc).astype(BF16),
                                                preferred_element_type=F32).astype(BF16)
            qk = lax.dot_general(qh.astype(BF16), khb, nt, preferred_element_type=F32) * gamma
            qk_ref[h, r0:r0 + chunk, :] = qk.astype(BF16)
            qd_ref[r0:r0 + chunk, col] = (qh * egc).astype(BF16)
            kd_ref[r0:r0 + chunk, col] = (kh * jnp.exp(gc[chunk - 1:chunk, h:h + 1] - gcol)).astype(BF16)


def _gdn_prep(cin, ab, past8, conv_w, a_log, dt_bias, bsz, seq, chunk, n_sub):
    rows = chunk * n_sub
    nblk = seq // rows
    t = bsz * seq
    lanes = lambda v: jnp.pad(v.reshape(1, N_HEADS).astype(F32), ((0, 0), (0, LANES - N_HEADS)))
    kern = functools.partial(_gdn_prep_kernel, chunk=chunk, n_sub=n_sub)
    rowblk = lambda b, c: (b * nblk + c, 0)
    halo = lambda b, c: (jnp.maximum((b * nblk + c) * (rows // SUBLANES) - 1, 0), 0)
    const = lambda b, c: (0, 0)
    return pl.pallas_call(
        kern,
        grid=(bsz, nblk),
        in_specs=[pl.BlockSpec((rows, CONV_CH), rowblk),
                  pl.BlockSpec((SUBLANES, CONV_CH), halo),
                  pl.BlockSpec((1, SUBLANES, CONV_CH), lambda b, c: (b, 0, 0)),
                  pl.BlockSpec((rows, LANES), rowblk),
                  pl.BlockSpec((CONV_W, CONV_CH), const),
                  pl.BlockSpec((1, LANES), const),
                  pl.BlockSpec((1, LANES), const)],
        out_specs=[pl.BlockSpec((rows, GROUP), rowblk),
                   pl.BlockSpec((rows, GROUP), rowblk),
                   pl.BlockSpec((rows, GROUP), rowblk),
                   pl.BlockSpec((rows, GROUP), rowblk),
                   pl.BlockSpec((N_HEADS, rows, chunk), lambda b, c: (0, b * nblk + c, 0)),
                   pl.BlockSpec((n_sub, 1, LANES), lambda b, c: (b * nblk + c, 0, 0))],
        out_shape=[jax.ShapeDtypeStruct((t, GROUP), F32),
                   jax.ShapeDtypeStruct((t, GROUP), BF16),
                   jax.ShapeDtypeStruct((t, GROUP), BF16),
                   jax.ShapeDtypeStruct((t, GROUP), BF16),
                   jax.ShapeDtypeStruct((N_HEADS, t, chunk), BF16),
                   jax.ShapeDtypeStruct((t // chunk, 1, LANES), F32)],
        compiler_params=_params("parallel", "parallel"),
        name="gdn_prep",
    )(cin, cin, past8, ab, conv_w, lanes(a_log), lanes(dt_bias))


def _gdn_scan_kernel(u_ref, w_ref, qd_ref, kd_ref, qk_ref, gl_ref, z_ref, s0_ref, nw_ref,
                     o_ref, sf_ref, s_ref, *, bsz, chunk):
    c_idx = pl.program_id(0)

    @pl.when(c_idx == 0)
    def _():
        s_ref[...] = s0_ref[...]

    tn = (((0,), (0,)), ((), ()))
    for b in range(bsz):
        glast = jnp.exp(gl_ref[b, 0])
        for h in range(N_HEADS):
            col = slice(h * HEAD, (h + 1) * HEAD)
            s = s_ref[b, h]
            wq = jnp.concatenate([w_ref[b, :, col], qd_ref[b, :, col]], axis=0)
            r = jnp.dot(wq, s.astype(BF16), preferred_element_type=F32)
            v_new = (u_ref[b, :, col] - r[:chunk]).astype(BF16)
            o = r[chunk:] + jnp.dot(qk_ref[h, b], v_new, preferred_element_type=F32)
            s_ref[b, h] = s * glast[:, h:h + 1] + lax.dot_general(kd_ref[b, :, col], v_new, tn,
                                                                 preferred_element_type=F32)
            zh = z_ref[b, :, col]
            ms = jnp.mean(o * o, axis=-1, keepdims=True)
            o = o * lax.rsqrt(ms + GATED_NORM_EPS) * nw_ref[...] * (zh * jax.nn.sigmoid(zh))
            o_ref[b, :, col] = o.astype(o_ref.dtype)

    @pl.when(c_idx == pl.num_programs(0) - 1)
    def _():
        sf_ref[...] = s_ref[...]


def _gdn_scan(u, w, qd, kd, qk, gl, z, s0, norm_w, bsz, seq, chunk):
    nc = seq // chunk
    kern = functools.partial(_gdn_scan_kernel, bsz=bsz, chunk=chunk)
    tok = pl.BlockSpec((bsz, chunk, GROUP), lambda c: (0, c, 0))
    state = pl.BlockSpec((bsz, N_HEADS, HEAD, HEAD), lambda c: (0, 0, 0, 0))
    o, s_final = pl.pallas_call(
        kern,
        grid=(nc,),
        in_specs=[tok, tok, tok, tok,
                  pl.BlockSpec((N_HEADS, bsz, chunk, chunk), lambda c: (0, 0, c, 0)),
                  pl.BlockSpec((bsz, 1, 1, LANES), lambda c: (0, c, 0, 0)),
                  tok, state,
                  pl.BlockSpec((1, HEAD), lambda c: (0, 0))],
        out_specs=[tok, state],
        out_shape=[jax.ShapeDtypeStruct((bsz, seq, GROUP), BF16),
                   jax.ShapeDtypeStruct((bsz, N_HEADS, HEAD, HEAD), F32)],
        scratch_shapes=[pltpu.VMEM((bsz, N_HEADS, HEAD, HEAD), F32)],
        compiler_params=_params("arbitrary"),
        name="gdn_scan",
    )(u.reshape(bsz, seq, GROUP), w.reshape(bsz, seq, GROUP), qd.reshape(bsz, seq, GROUP),
      kd.reshape(bsz, seq, GROUP), qk.reshape(N_HEADS, bsz, seq, chunk), gl.reshape(bsz, nc, 1, LANES),
      z.reshape(bsz, seq, GROUP), s0, norm_w)
    return o.reshape(bsz * seq, GROUP), s_final


def _layernorm(x, g, b):
    mu = jnp.mean(x, axis=-1, keepdims=True)
    xc = x - mu
    var = jnp.mean(xc * xc, axis=-1, keepdims=True)
    return xc * lax.rsqrt(var + LN_EPS) * g + b


def _out_router_kernel(att_ref, o_ref, x_ref, wo_ref, g_ref, b_ref, rw_ref, rb_ref,
                       x1_ref, x1b_ref, idx_ref, gate_ref, rank_ref, cnt_ref, carry_ref, *, tm):
    step = pl.program_id(0)

    @pl.when(step == 0)
    def _():
        carry_ref[...] = jnp.zeros_like(carry_ref)

    mix = (jnp.dot(att_ref[...], wo_ref[:GROUP, :], preferred_element_type=F32)
           + jnp.dot(o_ref[...], wo_ref[GROUP:, :], preferred_element_type=F32))
    x1 = _layernorm(DEEPNORM_ALPHA * x_ref[...] + mix, g_ref[...], b_ref[...])
    x1_ref[...] = x1
    x1b_ref[...] = x1.astype(BF16)

    logits = jnp.dot(x1, rw_ref[...], preferred_element_type=F32, precision=lax.Precision.HIGHEST) + rb_ref[...]
    lane = lax.broadcasted_iota(jnp.int32, logits.shape, 1)
    work = jnp.where(lane < N_EXPERTS, logits, -jnp.inf)
    vals, idxs = [], []
    for _ in range(TOP_K):
        m = jnp.max(work, axis=-1, keepdims=True)
        am = jnp.min(jnp.where(work == m, lane, LANES), axis=-1, keepdims=True)
        vals.append(m)
        idxs.append(am)
        work = jnp.where(lane == am, -jnp.inf, work)
    exps = [jnp.exp(v - vals[0]) for v in vals]
    denom = exps[0] + exps[1] + exps[2] + exps[3]
    chosen = jnp.zeros(logits.shape, F32)
    gate_out = jnp.zeros(logits.shape, F32)
    idx_out = jnp.zeros(logits.shape, jnp.int32)
    for k in range(TOP_K):
        chosen = jnp.where(lane == idxs[k], 1.0, chosen)
        gate_out = jnp.where(lane == k, exps[k] / denom, gate_out)
        idx_out = jnp.where(lane == k, idxs[k], idx_out)
    ri = lax.broadcasted_iota(jnp.int32, (tm, tm), 0)
    ci = lax.broadcasted_iota(jnp.int32, (tm, tm), 1)
    before = jnp.where(ri > ci, 1.0, 0.0).astype(BF16)
    prefix = jnp.dot(before, chosen.astype(BF16), preferred_element_type=F32) + carry_ref[...]
    rank_out = jnp.zeros(logits.shape, F32)
    for k in range(TOP_K):
        r = jnp.sum(jnp.where(lane == idxs[k], prefix, 0.0), axis=-1, keepdims=True)
        rank_out = jnp.where(lane == k, r, rank_out)
    carry_ref[...] = carry_ref[...] + jnp.sum(chosen, axis=0, keepdims=True)
    idx_ref[...] = idx_out
    gate_ref[...] = gate_out
    rank_ref[...] = rank_out.astype(jnp.int32)
    cnt_ref[...] = carry_ref[...].astype(jnp.int32)


def _out_router(att, o, x2d, w_out_bf, ln_g, ln_b, router_w, router_b, tm):
    t = x2d.shape[0]
    row = lambda i: (i, 0)
    const = lambda i: (0, 0)
    rw = jnp.pad(router_w, ((0, 0), (0, LANES - N_EXPERTS)))
    rb = jnp.pad(router_b.reshape(1, N_EXPERTS), ((0, 0), (0, LANES - N_EXPERTS)))
    kern = functools.partial(_out_router_kernel, tm=tm)
    return pl.pallas_call(
        kern,
        grid=(t // tm,),
        in_specs=[pl.BlockSpec((tm, GROUP), row), pl.BlockSpec((tm, GROUP), row),
                  pl.BlockSpec((tm, D_MODEL), row),
                  pl.BlockSpec((2 * GROUP, D_MODEL), const),
                  pl.BlockSpec((1, D_MODEL), const), pl.BlockSpec((1, D_MODEL), const),
                  pl.BlockSpec((D_MODEL, LANES), const), pl.BlockSpec((1, LANES), const)],
        out_specs=[pl.BlockSpec((tm, D_MODEL), row), pl.BlockSpec((tm, D_MODEL), row),
                   pl.BlockSpec((tm, LANES), row), pl.BlockSpec((tm, LANES), row),
                   pl.BlockSpec((tm, LANES), row), pl.BlockSpec((1, LANES), const)],
        out_shape=[jax.ShapeDtypeStruct((t, D_MODEL), F32), jax.ShapeDtypeStruct((t, D_MODEL), BF16),
                   jax.ShapeDtypeStruct((t, LANES), jnp.int32), jax.ShapeDtypeStruct((t, LANES), F32),
                   jax.ShapeDtypeStruct((t, LANES), jnp.int32), jax.ShapeDtypeStruct((1, LANES), jnp.int32)],
        scratch_shapes=[pltpu.VMEM((1, LANES), F32)],
        compiler_params=_params("arbitrary"),
        name="out_router",
    )(att, o, x2d, w_out_bf, ln_g.reshape(1, D_MODEL), ln_b.reshape(1, D_MODEL), rw, rb)


def _row_copy(src_hbm, src_row, dst, dst_row, sem):
    return pltpu.make_async_copy(src_hbm.at[pl.ds(src_row, 1), :], dst.at[pl.ds(dst_row, 1), :], sem)


def _dispatch_kernel(ps_ref, pe_ref, dest_ref, x_hbm, xb_hbm, zero_ref, sem, zsem, *, tm, bm):
    i = pl.program_id(0)

    def zero_block(row):
        return pltpu.make_async_copy(zero_ref, xb_hbm.at[pl.ds(pl.multiple_of(row, bm), bm), :], zsem)

    def fill(e):
        return zero_block(pe_ref[e] - bm)

    @pl.when(i == 0)
    def _():
        zero_ref[...] = jnp.zeros_like(zero_ref)
        for e in range(N_EXPERTS):
            @pl.when(pe_ref[e] > ps_ref[e])
            def _():
                fill(e).start()
        first_unused = pe_ref[N_EXPERTS - 1] // bm
        n_blocks = xb_hbm.shape[0] // bm
        lax.fori_loop(first_unused, n_blocks, lambda b, c: (zero_block(b * bm).start(), c)[1], 0)
        for e in range(N_EXPERTS):
            @pl.when(pe_ref[e] > ps_ref[e])
            def _():
                fill(e).wait()
        lax.fori_loop(first_unused, n_blocks, lambda b, c: (zero_block(b * bm).wait(), c)[1], 0)

    def issue(r, carry):
        for k in range(TOP_K):
            _row_copy(x_hbm, i * tm + r, xb_hbm, dest_ref[0, 0, r * TOP_K + k], sem).start()
        return carry

    lax.fori_loop(0, tm, issue, 0)
    pltpu.make_async_copy(xb_hbm.at[pl.ds(0, tm * TOP_K), :], xb_hbm.at[pl.ds(0, tm * TOP_K), :], sem).wait()


def _dispatch(x1, dest, pad_start, pad_end, rows, tm, bm):
    t = x1.shape[0]
    return pl.pallas_call(
        functools.partial(_dispatch_kernel, tm=tm, bm=bm),
        grid_spec=pltpu.PrefetchScalarGridSpec(
            num_scalar_prefetch=2,
            grid=(t // tm,),
            in_specs=[pl.BlockSpec((1, 1, tm * TOP_K), lambda i, ps, pe: (i, 0, 0), memory_space=pltpu.SMEM),
                      pl.BlockSpec(memory_space=pl.ANY)],
            out_specs=pl.BlockSpec(memory_space=pl.ANY),
            scratch_shapes=[pltpu.VMEM((bm, D_MODEL), F32),
                            pltpu.SemaphoreType.DMA(()), pltpu.SemaphoreType.DMA(())]),
        out_shape=jax.ShapeDtypeStruct((rows, D_MODEL), F32),
        compiler_params=_params("arbitrary"),
        name="dispatch",
    )(pad_start, pad_end, dest.reshape(t // tm, 1, tm * TOP_K), x1)


def _expert_kernel(be_ref, nb_ref, x_ref, wgu_ref, bgu_ref, wd_ref, bd_ref, y_ref):
    blk = pl.program_id(0)

    @pl.when(blk < nb_ref[0])
    def _():
        x = x_ref[...].astype(BF16)
        h = jnp.dot(x, wgu_ref[0].astype(BF16), preferred_element_type=F32) + bgu_ref[0]
        gate = jnp.minimum(h[:, :D_FF], SWIGLU_LIMIT)
        up = jnp.clip(h[:, D_FF:], -SWIGLU_LIMIT, SWIGLU_LIMIT)
        act = (up + 1.0) * (gate * jax.nn.sigmoid(SWIGLU_ALPHA * gate))
        y_ref[...] = jnp.dot(act.astype(BF16), wd_ref[0].astype(BF16), preferred_element_type=F32) + bd_ref[0]

    @pl.when(blk >= nb_ref[0])
    def _():
        y_ref[...] = jnp.zeros_like(y_ref)


def _experts(xb, blk_e, n_used, w_gu, b_gu, w_down, b_down, bm):
    rows = xb.shape[0]
    n_blocks = rows // bm
    used = lambda i, be, nb: (jnp.minimum(i, nb[0] - 1), 0)
    return pl.pallas_call(
        _expert_kernel,
        grid_spec=pltpu.PrefetchScalarGridSpec(
            num_scalar_prefetch=2,
            grid=(n_blocks,),
            in_specs=[pl.BlockSpec((bm, D_MODEL), used),
                      pl.BlockSpec((1, D_MODEL, 2 * D_FF), lambda i, be, nb: (be[i], 0, 0)),
                      pl.BlockSpec((1, 1, 2 * D_FF), lambda i, be, nb: (be[i], 0, 0)),
                      pl.BlockSpec((1, D_FF, D_MODEL), lambda i, be, nb: (be[i], 0, 0)),
                      pl.BlockSpec((1, 1, D_MODEL), lambda i, be, nb: (be[i], 0, 0))],
            out_specs=pl.BlockSpec((bm, D_MODEL), lambda i, be, nb: (i, 0))),
        out_shape=jax.ShapeDtypeStruct((rows, D_MODEL), F32),
        compiler_params=_params("arbitrary"),
        name="experts",
    )(blk_e, n_used, xb, w_gu, b_gu.reshape(N_EXPERTS, 1, 2 * D_FF), w_down, b_down.reshape(N_EXPERTS, 1, D_MODEL))


def _combine_ln_kernel(dcur_ref, dnext_ref, x_ref, gate_ref, g_ref, b_ref, yb_hbm, o_ref, ybuf, sem, *, tm):
    i = pl.program_id(0)
    slot = i % 2

    def issue(dref, s):
        def body(r, carry):
            for k in range(TOP_K):
                _row_copy(yb_hbm, dref[0, 0, r * TOP_K + k], ybuf.at[s, k], r, sem.at[s]).start()
            return carry
        lax.fori_loop(0, tm, body, 0)

    @pl.when(i == 0)
    def _():
        issue(dcur_ref, 0)

    @pl.when(i + 1 < pl.num_programs(0))
    def _():
        issue(dnext_ref, 1 - slot)

    pltpu.make_async_copy(ybuf.at[slot], ybuf.at[slot], sem.at[slot]).wait()
    gates = gate_ref[...]
    y = sum(gates[:, k:k + 1] * ybuf[slot, k] for k in range(TOP_K))
    o_ref[...] = _layernorm(DEEPNORM_ALPHA * x_ref[...] + y, g_ref[...], b_ref[...])


def _combine_ln(x1, gates, dest, yb, ln_g, ln_b, tm):
    t = x1.shape[0]
    n = t // tm
    row = lambda i: (i, 0)
    const = lambda i: (0, 0)
    d2 = dest.reshape(n, 1, tm * TOP_K)
    return pl.pallas_call(
        functools.partial(_combine_ln_kernel, tm=tm),
        grid=(n,),
        in_specs=[pl.BlockSpec((1, 1, tm * TOP_K), lambda i: (i, 0, 0), memory_space=pltpu.SMEM),
                  pl.BlockSpec((1, 1, tm * TOP_K), lambda i: (jnp.minimum(i + 1, n - 1), 0, 0),
                               memory_space=pltpu.SMEM),
                  pl.BlockSpec((tm, D_MODEL), row),
                  pl.BlockSpec((tm, LANES), row),
                  pl.BlockSpec((1, D_MODEL), const), pl.BlockSpec((1, D_MODEL), const),
                  pl.BlockSpec(memory_space=pl.ANY)],
        out_specs=pl.BlockSpec((tm, D_MODEL), row),
        out_shape=jax.ShapeDtypeStruct((t, D_MODEL), F32),
        scratch_shapes=[pltpu.VMEM((2, TOP_K, tm, D_MODEL), F32), pltpu.SemaphoreType.DMA((2,))],
        compiler_params=_params("arbitrary"),
        name="combine_ln",
    )(d2, d2, x1, gates, ln_g.reshape(1, D_MODEL), ln_b.reshape(1, D_MODEL), yb)


MOE_BM = 256
ATTN_BLK = 512
GDN_SUB = 2


def _mixers(x2d, bsz, seq, cache_k, cache_v, conv_past, s0, lam, lam_init, w_in_bf, conv_w, a_log, dt_bias,
            delta_norm_w, subln_w):
    prompt = cache_k is None
    tm = ATTN_BLK if prompt else x2d.shape[0]
    q_bf, k_f, v_f, k_bf, v_bf, cin, z, ab = _in_proj(x2d, w_in_bf, tm, prompt)
    if prompt:
        att = _attn_prompt(q_bf, k_bf, v_bf, lam, subln_w, bsz, seq, ATTN_BLK, lam_init)
    else:
        att = _attn_sample(q_bf, k_bf, v_bf, cache_k, cache_v, lam, subln_w.reshape(1, HEAD), bsz, seq, lam_init)
    chunk = CHUNK if seq % CHUNK == 0 else seq
    n_sub = GDN_SUB if (seq // chunk) % GDN_SUB == 0 else 1
    past8 = jnp.pad(conv_past, ((0, 0), (SUBLANES - (CONV_W - 1), 0), (0, 0)))
    u, w, qd, kd, qk, gl = _gdn_prep(cin, ab, past8, conv_w, a_log, dt_bias, bsz, seq, chunk, n_sub)
    o, s_new = _gdn_scan(u, w, qd, kd, qk, gl, z, s0, delta_norm_w.reshape(1, HEAD), bsz, seq, chunk)
    return att, o, k_f, v_f, cin, s_new


def _moe(x1, idx, gates, rank, counts, w_gu, b_gu, w_down, b_down, ln_g, ln_b, bm, tm):
    t = x1.shape[0]
    n = t * TOP_K
    counts = counts[0, :N_EXPERTS]
    padded = (counts + bm - 1) // bm * bm
    pad_end = jnp.cumsum(padded).astype(jnp.int32)
    pad_start = (pad_end - padded).astype(jnp.int32)
    dest = pad_start[idx[:, :TOP_K]] + rank[:, :TOP_K]
    n_blocks = -(-n // bm) + N_EXPERTS
    blk_start = jnp.arange(n_blocks, dtype=jnp.int32) * bm
    blk_e = jnp.minimum(jnp.sum(pad_end[None, :] <= blk_start[:, None], axis=1), N_EXPERTS - 1).astype(jnp.int32)
    n_used = (pad_end[-1] // bm).astype(jnp.int32).reshape(1)
    xb = _dispatch(x1, dest, pad_start, pad_end, n_blocks * bm, tm, bm)
    yb = _experts(xb, blk_e, n_used, w_gu, b_gu, w_down, b_down, bm)
    return _combine_ln(x1, gates, dest, yb, ln_g, ln_b, tm)


def kernel(x_prompt, x_sample, cache_k, cache_v, state_conv, state_delta, w_in, conv_w, a_log, dt_bias,
           delta_norm_w, lambda_q1, lambda_k1, lambda_q2, lambda_k2, subln_w, w_out, ln1_g, ln1_b,
           router_w, router_b, w_gu, b_gu, w_down, b_down, ln2_g, ln2_b):
    bp, lp, _ = x_prompt.shape
    bs, ls, _ = x_sample.shape
    l = 0
    lam_init = 0.8 - 0.6 * math.exp(-0.3 * l)
    lam = (jnp.exp(jnp.sum(lambda_q1[l] * lambda_k1[l])) - jnp.exp(jnp.sum(lambda_q2[l] * lambda_k2[l]))
           + lam_init).reshape(1).astype(F32)
    w_in_bf = jnp.pad(w_in[l], ((0, 0), (0, IN_COLS_PAD - IN_COLS))).astype(BF16)
    shared = (lam, lam_init, w_in_bf, conv_w[l], a_log[l], dt_bias[l], delta_norm_w[l], subln_w[l])

    xp = x_prompt.reshape(bp * lp, D_MODEL)
    xs = x_sample.reshape(bs * ls, D_MODEL)
    zero_conv = jnp.zeros((bp, CONV_W - 1, CONV_CH), F32)
    zero_s = jnp.zeros((bp, N_HEADS, HEAD, HEAD), F32)
    att_p, o_p, k_p, v_p, cin_p, s_p = _mixers(xp, bp, lp, None, None, zero_conv, zero_s, *shared)
    att_s, o_s, k_s, v_s, cin_s, s_s = _mixers(xs, bs, ls, cache_k[l], cache_v[l], state_conv[l],
                                               state_delta[l], *shared)

    att = jnp.concatenate([att_p, att_s], axis=0)
    o = jnp.concatenate([o_p, o_s], axis=0)
    x_all = jnp.concatenate([xp, xs], axis=0)
    t = x_all.shape[0]
    tm = next(c for c in (512, 384, 256, 128) if t % c == 0)
    x1, x1b, idx, gates, rank, counts = _out_router(att, o, x_all, w_out[l].astype(BF16), ln1_g[l], ln1_b[l],
                                                    router_w[l], router_b[l], tm)
    y = _moe(x1, idx, gates, rank, counts, w_gu[l], b_gu[l], w_down[l], b_down[l], ln2_g[l], ln2_b[l], MOE_BM, 128)

    tp = bp * lp
    conv_tail = lambda cin, b, s: cin.reshape(b, s, CONV_CH)[:, s - (CONV_W - 1):][None]
    return (y[:tp].reshape(bp, lp, D_MODEL), y[tp:].reshape(bs, ls, D_MODEL),
            k_p.reshape(1, bp, lp, N_HEADS, HEAD), v_p.reshape(1, bp, lp, N_HEADS, HEAD),
            conv_tail(cin_p, bp, lp), s_p[None].astype(state_delta.dtype),
            k_s.reshape(1, bs, ls, N_HEADS, HEAD), v_s.reshape(1, bs, ls, N_HEADS, HEAD),
            conv_tail(cin_s, bs, ls), s_s[None].astype(state_delta.dtype))
```

```python
import functools
import math

import jax
import jax.numpy as jnp
from jax import lax
from jax.experimental import pallas as pl
from jax.experimental.pallas import tpu as pltpu

F32 = jnp.float32
BF16 = jnp.bfloat16

D_MODEL = 1024
HEAD = 128
N_HEADS = 4
DQK = HEAD // 2
GROUP = N_HEADS * HEAD
CONV_W = 4
CONV_CH = 3 * GROUP
CHUNK = 64
ALIBI_MAX = 8.0
N_EXPERTS = 32
TOP_K = 4
D_FF = D_MODEL
SWIGLU_LIMIT = 7.0
SWIGLU_ALPHA = 1.702
DEPTH = 1
DEEPNORM_ALPHA = (2 * DEPTH) ** 0.25
LN_EPS = 1e-5
SUBLN_EPS = 1e-5
GATED_NORM_EPS = 1e-6
L2_EPS = 1e-6

LANES = 128
SUBLANES = 8
BF16_EXACT_INT = 256
VMEM_LIMIT = 56 * 1024 * 1024

COL_Q, COL_K, COL_V, COL_CONV = 0, GROUP, 2 * GROUP, 3 * GROUP
COL_Z = COL_CONV + CONV_CH
COL_AB = COL_Z + GROUP
IN_COLS = COL_AB + 2 * N_HEADS
IN_COLS_PAD = COL_AB + LANES


def _params(*sem):
    return pltpu.CompilerParams(dimension_semantics=sem, vmem_limit_bytes=VMEM_LIMIT)


def _in_proj_kernel(x_ref, w_ref, q_ref, kf_ref, vf_ref, kb_ref, vb_ref, c_ref, z_ref, ab_ref, *, tm, transposed):
    xb = x_ref[...].astype(BF16)

    def section(lo, hi):
        return jnp.dot(xb, w_ref[:, lo:hi], preferred_element_type=F32)

    q = section(COL_Q, COL_K) * (DQK ** -0.5)
    k = section(COL_K, COL_V)
    kf_ref[...] = k
    kb_ref[...] = k.astype(BF16)
    v = section(COL_V, COL_CONV)
    vf_ref[...] = v
    if transposed:
        q_ref[0] = q.T.astype(BF16)
        vb_ref[0] = v.T.astype(BF16)
    else:
        q_ref[...] = q.astype(BF16)
        vb_ref[...] = v.astype(BF16)
    c_ref[...] = section(COL_CONV, COL_Z)
    z_ref[...] = section(COL_Z, COL_AB)
    ab_ref[...] = section(COL_AB, IN_COLS_PAD)


def _in_proj(x2d, w_bf, tm, transposed):
    t = x2d.shape[0]
    row = lambda i: (i, 0)
    widths = (GROUP, GROUP, GROUP, GROUP, GROUP, CONV_CH, GROUP, LANES)
    dtypes = (BF16, F32, F32, BF16, BF16, F32, F32, F32)
    out_specs = [pl.BlockSpec((tm, w), row) for w in widths]
    out_shape = [jax.ShapeDtypeStruct((t, w), d) for w, d in zip(widths, dtypes)]
    if transposed:
        for slot in (0, 4):
            out_specs[slot] = pl.BlockSpec((1, GROUP, tm), lambda i: (i, 0, 0))
            out_shape[slot] = jax.ShapeDtypeStruct((t // tm, GROUP, tm), BF16)
    return pl.pallas_call(
        functools.partial(_in_proj_kernel, tm=tm, transposed=transposed),
        grid=(t // tm,),
        in_specs=[pl.BlockSpec((tm, D_MODEL), row),
                  pl.BlockSpec((D_MODEL, IN_COLS_PAD), lambda i: (0, 0))],
        out_specs=out_specs,
        out_shape=out_shape,
        compiler_params=_params("parallel"),
        name="in_proj",
    )(x2d, w_bf)


def _alibi_slopes():
    return [2.0 ** (-ALIBI_MAX * (h + 1) / N_HEADS) for h in range(N_HEADS)]


def _stack_halves(q):
    lane = lax.broadcasted_iota(jnp.int32, q.shape, 1)
    zero = jnp.zeros_like(q)
    return jnp.concatenate([jnp.where(lane < DQK, q, zero), jnp.where(lane < DQK, zero, q)], axis=0)


def _diff_norm(acc, l, lam, w, lam_init, rows):
    o = acc[:rows] / l[:rows] - lam * (acc[rows:] / l[rows:])
    ms = jnp.mean(o * o, axis=-1, keepdims=True)
    return o * lax.rsqrt(ms + SUBLN_EPS) * w * (1.0 - lam_init)


def _head_slope(h):
    s = _alibi_slopes()
    return jnp.where(h == 0, s[0], jnp.where(h == 1, s[1], jnp.where(h == 2, s[2], s[3]))).astype(F32)


def _attn_prompt_kernel(lam_ref, qt_ref, k_ref, vt_ref, diag_ref, w_ref, o_ref,
                        kaug_ref, qz_ref, m_ref, l_ref, acc_ref, *, blk, lam_init):
    i = pl.program_id(2)
    slope = _head_slope(pl.program_id(1))
    rows = 2 * blk
    n_kv = k_ref.shape[0] // blk

    @pl.when(i == 0)
    def _():
        lane = lax.broadcasted_iota(jnp.int32, (blk, HEAD), 1)
        pos = lax.broadcasted_iota(jnp.int32, (blk, HEAD), 0)
        lo = slope * (pos % BF16_EXACT_INT).astype(F32)
        hi = slope * (pos - pos % BF16_EXACT_INT).astype(F32)
        extra = jnp.where(lane < 2, 1.0, jnp.where(lane == 2, lo, jnp.where(lane == 3, hi, 0.0))).astype(BF16)

        def fill(j, carry):
            j0 = pl.multiple_of(j * blk, blk)
            kaug_ref[pl.ds(j0, blk), :HEAD] = k_ref[pl.ds(j0, blk), :]
            kaug_ref[pl.ds(j0, blk), HEAD:] = extra
            return carry

        lax.fori_loop(0, n_kv, fill, 0)

    qt = qt_ref[0]
    d = lax.broadcasted_iota(jnp.int32, qt.shape, 0)
    zero = jnp.zeros_like(qt)
    qz_ref[:HEAD, :] = jnp.concatenate([jnp.where(d < DQK, qt, zero), jnp.where(d < DQK, zero, qt)], axis=1)
    a = lax.broadcasted_iota(jnp.int32, (HEAD, rows), 1) % blk
    a_lo = -slope * (a % BF16_EXACT_INT).astype(F32)
    a_hi = -slope * (a - a % BF16_EXACT_INT).astype(F32)
    r = lax.broadcasted_iota(jnp.int32, (HEAD, rows), 0)
    qz_ref[HEAD:, :] = jnp.where(r == 0, a_lo, jnp.where(r == 1, a_hi, jnp.where(r < 4, 1.0, 0.0))).astype(BF16)

    def update(s, vt, shift, first):
        m_tile = jnp.max(s, axis=0, keepdims=True) + shift
        if first:
            m_new = m_tile
        else:
            m_old = m_ref[...]
            m_new = jnp.maximum(m_old, m_tile)
        p = jnp.exp(s - (m_new - shift))
        pv = jnp.dot(vt, p.astype(BF16), preferred_element_type=F32)
        l_tile = jnp.sum(p, axis=0, keepdims=True)
        if first:
            l_ref[...] = l_tile
            acc_ref[...] = pv
        else:
            alpha = jnp.exp(m_old - m_new)
            l_ref[...] = alpha * l_ref[...] + l_tile
            acc_ref[...] = alpha * acc_ref[...] + pv
        m_ref[...] = m_new

    j_diag = pl.multiple_of(i * blk, blk)
    s_diag = jnp.dot(k_ref[pl.ds(j_diag, blk), :], qz_ref[:HEAD, :], preferred_element_type=F32) + diag_ref[0]
    update(s_diag, vt_ref[i], 0.0, True)

    def body(j, carry):
        j0 = pl.multiple_of(j * blk, blk)
        s = jnp.dot(kaug_ref[pl.ds(j0, blk), :], qz_ref[...], preferred_element_type=F32)
        update(s, vt_ref[j], -slope * ((i - j) * blk).astype(F32), False)
        return carry

    lax.fori_loop(0, i, body, 0)
    acc = acc_ref[...]
    l = l_ref[...]
    ot = acc[:, :blk] / l[:, :blk] - lam_ref[0] * (acc[:, blk:] / l[:, blk:])
    ms = jnp.mean(ot * ot, axis=0, keepdims=True)
    ot = ot * lax.rsqrt(ms + SUBLN_EPS) * w_ref[...] * (1.0 - lam_init)
    o_ref[...] = ot.T.astype(o_ref.dtype)


def _attn_prompt(qt_bf, k_bf, vt_bf, lam, subln_w, bsz, seq, blk, lam_init):
    nq = seq // blk
    a = jnp.arange(blk, dtype=jnp.int32)
    rel = jnp.abs(a[None, :] - a[:, None]).astype(F32)
    visible = (a[:, None] // CHUNK) <= (a[None, :] // CHUNK)
    slopes = jnp.asarray(_alibi_slopes(), F32)[:, None, None]
    diag = jnp.where(visible[None], -slopes * rel[None], -jnp.inf)
    diag = jnp.concatenate([diag, diag], axis=2)
    kern = functools.partial(_attn_prompt_kernel, blk=blk, lam_init=lam_init)
    return pl.pallas_call(
        kern,
        grid=(bsz, N_HEADS, nq),
        in_specs=[pl.BlockSpec(memory_space=pltpu.SMEM),
                  pl.BlockSpec((1, HEAD, blk), lambda b, h, i: (b * nq + i, h, 0)),
                  pl.BlockSpec((seq, HEAD), lambda b, h, i: (b, h)),
                  pl.BlockSpec((nq, HEAD, blk), lambda b, h, i: (b, h, 0)),
                  pl.BlockSpec((1, blk, 2 * blk), lambda b, h, i: (h, 0, 0)),
                  pl.BlockSpec((HEAD, 1), lambda b, h, i: (0, 0))],
        out_specs=pl.BlockSpec((blk, HEAD), lambda b, h, i: (b * nq + i, h)),
        scratch_shapes=[pltpu.VMEM((seq, 2 * HEAD), BF16),
                        pltpu.VMEM((2 * HEAD, 2 * blk), BF16),
                        pltpu.VMEM((1, 2 * blk), F32),
                        pltpu.VMEM((1, 2 * blk), F32),
                        pltpu.VMEM((HEAD, 2 * blk), F32)],
        out_shape=jax.ShapeDtypeStruct((bsz * seq, GROUP), BF16),
        compiler_params=_params("parallel", "parallel", "arbitrary"),
        name="attn_prompt",
    )(lam, qt_bf, k_bf, vt_bf, diag, subln_w.reshape(HEAD, 1))


def _attn_sample_kernel(lam_ref, q_ref, kn_ref, vn_ref, kc_ref, vc_ref, w_ref, o_ref, *, seq, past, lam_init):
    slope = _head_slope(pl.program_id(1))
    qz = _stack_halves(q_ref[...])
    nt = (((1,), (1,)), ((), ()))
    qpos = past + lax.broadcasted_iota(jnp.int32, (2 * seq, 1), 0) % seq
    kpos_c = lax.broadcasted_iota(jnp.int32, (1, past), 1)
    kpos_n = past + lax.broadcasted_iota(jnp.int32, (1, seq), 1)
    s_c = lax.dot_general(qz, kc_ref[0].astype(BF16), nt, preferred_element_type=F32)
    s_c = s_c - slope * jnp.abs(qpos - kpos_c).astype(F32)
    s_n = lax.dot_general(qz, kn_ref[...], nt, preferred_element_type=F32)
    s_n = s_n - slope * jnp.abs(qpos - kpos_n).astype(F32)
    m = jnp.maximum(jnp.max(s_c, axis=-1, keepdims=True), jnp.max(s_n, axis=-1, keepdims=True))
    p_c = jnp.exp(s_c - m)
    p_n = jnp.exp(s_n - m)
    l = jnp.sum(p_c, axis=-1, keepdims=True) + jnp.sum(p_n, axis=-1, keepdims=True)
    acc = (jnp.dot(p_c.astype(BF16), vc_ref[0].astype(BF16), preferred_element_type=F32)
           + jnp.dot(p_n.astype(BF16), vn_ref[...], preferred_element_type=F32))
    o_ref[...] = _diff_norm(acc, l, lam_ref[0], w_ref[...], lam_init, seq).astype(o_ref.dtype)


def _attn_sample(q_bf, k_bf, v_bf, cache_k, cache_v, lam, subln_w, bsz, seq, lam_init):
    past = cache_k.shape[1]
    kern = functools.partial(_attn_sample_kernel, seq=seq, past=past, lam_init=lam_init)
    new = pl.BlockSpec((seq, HEAD), lambda b, h: (b, h))
    cache = pl.BlockSpec((1, past, HEAD), lambda b, h: (b, 0, h))
    return pl.pallas_call(
        kern,
        grid=(bsz, N_HEADS),
        in_specs=[pl.BlockSpec(memory_space=pltpu.SMEM), new, new, new, cache, cache,
                  pl.BlockSpec((1, HEAD), lambda b, h: (0, 0))],
        out_specs=new,
        out_shape=jax.ShapeDtypeStruct((bsz * seq, GROUP), BF16),
        compiler_params=_params("parallel", "parallel"),
        name="attn_sample",
    )(lam, q_bf, k_bf, v_bf, cache_k.reshape(bsz, past, GROUP), cache_v.reshape(bsz, past, GROUP), subln_w)


def _split3(x):
    hi = x.astype(BF16)
    r1 = x - hi.astype(F32)
    mid = r1.astype(BF16)
    lo = (r1 - mid.astype(F32)).astype(BF16)
    return hi, mid, lo


def _gdn_prep_kernel(cin_ref, halo_ref, past_ref, ab_ref, cw_ref, alog_ref, dtb_ref,
                     u_ref, w_ref, qd_ref, kd_ref, qk_ref, gl_ref, *, chunk, n_sub):
    c_idx = pl.program_id(1)
    rows = chunk * n_sub
    prev = jnp.where(c_idx == 0, past_ref[0], halo_ref[...])
    xin = jnp.concatenate([prev, cin_ref[...]], axis=0)
    conv = sum(xin[SUBLANES - (CONV_W - 1) + j: SUBLANES - (CONV_W - 1) + j + rows] * cw_ref[j:j + 1, :]
               for j in range(CONV_W))
    conv = conv * jax.nn.sigmoid(conv)

    ab = ab_ref[...]
    lane = lax.broadcasted_iota(jnp.int32, ab.shape, 1)
    pre = ab + dtb_ref[...]
    softplus = jnp.maximum(pre, 0.0) + jnp.log(1.0 + jnp.exp(-jnp.abs(pre)))
    g = jnp.where(lane < N_HEADS, -jnp.exp(alog_ref[...]) * softplus, 0.0)
    beta_all = jax.nn.sigmoid(ab)

    ri = lax.broadcasted_iota(jnp.int32, (rows, rows), 0)
    ci = lax.broadcasted_iota(jnp.int32, (rows, rows), 1)
    same = (ri // chunk) == (ci // chunk)
    incl = same & (ri >= ci)
    strict = same & (ri > ci)
    eye = jnp.where(ri == ci, 1.0, 0.0).astype(F32)
    nt = (((1,), (1,)), ((), ()))
    g_parts = _split3(g)
    ones_incl = jnp.where(incl, 1.0, 0.0).astype(BF16)
    ones_same = jnp.where(same, 1.0, 0.0).astype(BF16)
    gc = sum(jnp.dot(ones_incl, part, preferred_element_type=F32) for part in g_parts)
    g_end = sum(jnp.dot(ones_same, part, preferred_element_type=F32) for part in g_parts)
    gct = gc.T
    for sc in range(n_sub):
        gl_ref[sc] = g_end[sc * chunk:sc * chunk + 1, :]

    t_mats, p_mats, rhs = [], [], []
    for h in range(N_HEADS):
        col = slice(h * HEAD, (h + 1) * HEAD)
        qh = conv[:, h * HEAD:(h + 1) * HEAD]
        kh = conv[:, GROUP + h * HEAD:GROUP + (h + 1) * HEAD]
        vh = conv[:, 2 * GROUP + h * HEAD:2 * GROUP + (h + 1) * HEAD]
        qh = qh * lax.rsqrt(jnp.sum(qh * qh, axis=-1, keepdims=True) + L2_EPS) * (HEAD ** -0.5)
        kh = kh * lax.rsqrt(jnp.sum(kh * kh, axis=-1, keepdims=True) + L2_EPS)
        beta = beta_all[:, N_HEADS + h:N_HEADS + h + 1]
        gcol = gc[:, h:h + 1]
        grow = gct[h:h + 1, :]
        gamma = jnp.exp(jnp.where(incl, gcol - grow, -jnp.inf))
        egc = jnp.exp(gcol)
        kb = kh * beta
        khb = kh.astype(BF16)
        a = jnp.where(strict, lax.dot_general(kb.astype(BF16), khb, nt, preferred_element_type=F32) * gamma, 0.0)
        qk = (lax.dot_general(qh.astype(BF16), khb, nt, preferred_element_type=F32) * gamma).astype(BF16)
        for sc in range(n_sub):
            blk = slice(sc * chunk, (sc + 1) * chunk)
            qk_ref[h, blk, :] = qk[blk, blk]
        qd_ref[:, col] = (qh * egc).astype(BF16)
        kd_ref[:, col] = (kh * jnp.exp(g_end[:, h:h + 1] - gcol)).astype(BF16)
        t_mats.append(eye - a)
        p_mats.append(a)
        rhs.append(((vh * beta).astype(BF16), (kb * egc).astype(BF16)))

    for _ in range(int(math.log2(chunk)) - 1):
        for h in range(N_HEADS):
            pb = p_mats[h].astype(BF16)
            p_mats[h] = jnp.dot(pb, pb, preferred_element_type=F32)
        for h in range(N_HEADS):
            t_mats[h] = t_mats[h] + jnp.dot(t_mats[h].astype(BF16), p_mats[h].astype(BF16),
                                            preferred_element_type=F32)

    for h in range(N_HEADS):
        col = slice(h * HEAD, (h + 1) * HEAD)
        t_inv = t_mats[h].astype(BF16)
        u_ref[:, col] = jnp.dot(t_inv, rhs[h][0], preferred_element_type=F32)
        w_ref[:, col] = jnp.dot(t_inv, rhs[h][1], preferred_element_type=F32).astype(BF16)


def _gdn_prep(cin, ab, past8, conv_w, a_log, dt_bias, bsz, seq, chunk, n_sub):
    rows = chunk * n_sub
    nblk = seq // rows
    t = bsz * seq
    lanes = lambda v: jnp.pad(v.reshape(1, N_HEADS).astype(F32), ((0, 0), (0, LANES - N_HEADS)))
    kern = functools.partial(_gdn_prep_kernel, chunk=chunk, n_sub=n_sub)
    rowblk = lambda b, c: (b * nblk + c, 0)
    halo = lambda b, c: (jnp.maximum((b * nblk + c) * (rows // SUBLANES) - 1, 0), 0)
    const = lambda b, c: (0, 0)
    return pl.pallas_call(
        kern,
        grid=(bsz, nblk),
        in_specs=[pl.BlockSpec((rows, CONV_CH), rowblk),
                  pl.BlockSpec((SUBLANES, CONV_CH), halo),
                  pl.BlockSpec((1, SUBLANES, CONV_CH), lambda b, c: (b, 0, 0)),
                  pl.BlockSpec((rows, LANES), rowblk),
                  pl.BlockSpec((CONV_W, CONV_CH), const),
                  pl.BlockSpec((1, LANES), const),
                  pl.BlockSpec((1, LANES), const)],
        out_specs=[pl.BlockSpec((rows, GROUP), rowblk),
                   pl.BlockSpec((rows, GROUP), rowblk),
                   pl.BlockSpec((rows, GROUP), rowblk),
                   pl.BlockSpec((rows, GROUP), rowblk),
                   pl.BlockSpec((N_HEADS, rows, chunk), lambda b, c: (0, b * nblk + c, 0)),
                   pl.BlockSpec((n_sub, 1, LANES), lambda b, c: (b * nblk + c, 0, 0))],
        out_shape=[jax.ShapeDtypeStruct((t, GROUP), F32),
                   jax.ShapeDtypeStruct((t, GROUP), BF16),
                   jax.ShapeDtypeStruct((t, GROUP), BF16),
                   jax.ShapeDtypeStruct((t, GROUP), BF16),
                   jax.ShapeDtypeStruct((N_HEADS, t, chunk), BF16),
                   jax.ShapeDtypeStruct((t // chunk, 1, LANES), F32)],
        compiler_params=_params("parallel", "parallel"),
        name="gdn_prep",
    )(cin, cin, past8, ab, conv_w, lanes(a_log), lanes(dt_bias))


def _gdn_scan_kernel(u_ref, w_ref, qd_ref, kd_ref, qk_ref, gl_ref, z_ref, s0_ref, nw_ref,
                     o_ref, sf_ref, s_ref, *, bsz, chunk):
    c_idx = pl.program_id(0)

    @pl.when(c_idx == 0)
    def _():
        s_ref[...] = s0_ref[...]

    tn = (((0,), (0,)), ((), ()))
    for b in range(bsz):
        glast = jnp.exp(gl_ref[b, 0])
        for h in range(N_HEADS):
            col = slice(h * HEAD, (h + 1) * HEAD)
            s = s_ref[b, h]
            wq = jnp.concatenate([w_ref[b, :, col], qd_ref[b, :, col]], axis=0)
            r = jnp.dot(wq, s.astype(BF16), preferred_element_type=F32)
            v_new = (u_ref[b, :, col] - r[:chunk]).astype(BF16)
            o = r[chunk:] + jnp.dot(qk_ref[h, b], v_new, preferred_element_type=F32)
            s_ref[b, h] = s * glast[:, h:h + 1] + lax.dot_general(kd_ref[b, :, col], v_new, tn,
                                                                 preferred_element_type=F32)
            zh = z_ref[b, :, col]
            ms = jnp.mean(o * o, axis=-1, keepdims=True)
            o = o * lax.rsqrt(ms + GATED_NORM_EPS) * nw_ref[...] * (zh * jax.nn.sigmoid(zh))
            o_ref[b, :, col] = o.astype(o_ref.dtype)

    @pl.when(c_idx == pl.num_programs(0) - 1)
    def _():
        sf_ref[...] = s_ref[...]


def _gdn_scan(u, w, qd, kd, qk, gl, z, s0, norm_w, bsz, seq, chunk):
    nc = seq // chunk
    kern = functools.partial(_gdn_scan_kernel, bsz=bsz, chunk=chunk)
    tok = pl.BlockSpec((bsz, chunk, GROUP), lambda c: (0, c, 0))
    state = pl.BlockSpec((bsz, N_HEADS, HEAD, HEAD), lambda c: (0, 0, 0, 0))
    o, s_final = pl.pallas_call(
        kern,
        grid=(nc,),
        in_specs=[tok, tok, tok, tok,
                  pl.BlockSpec((N_HEADS, bsz, chunk, chunk), lambda c: (0, 0, c, 0)),
                  pl.BlockSpec((bsz, 1, 1, LANES), lambda c: (0, c, 0, 0)),
                  tok, state,
                  pl.BlockSpec((1, HEAD), lambda c: (0, 0))],
        out_specs=[tok, state],
        out_shape=[jax.ShapeDtypeStruct((bsz, seq, GROUP), BF16),
                   jax.ShapeDtypeStruct((bsz, N_HEADS, HEAD, HEAD), F32)],
        scratch_shapes=[pltpu.VMEM((bsz, N_HEADS, HEAD, HEAD), F32)],
        compiler_params=_params("arbitrary"),
        name="gdn_scan",
    )(u.reshape(bsz, seq, GROUP), w.reshape(bsz, seq, GROUP), qd.reshape(bsz, seq, GROUP),
      kd.reshape(bsz, seq, GROUP), qk.reshape(N_HEADS, bsz, seq, chunk), gl.reshape(bsz, nc, 1, LANES),
      z.reshape(bsz, seq, GROUP), s0, norm_w)
    return o.reshape(bsz * seq, GROUP), s_final


def _layernorm(x, g, b):
    mu = jnp.mean(x, axis=-1, keepdims=True)
    xc = x - mu
    var = jnp.mean(xc * xc, axis=-1, keepdims=True)
    return xc * lax.rsqrt(var + LN_EPS) * g + b


def _out_router_kernel(att_ref, o_ref, x_ref, wo_ref, g_ref, b_ref, rw_ref, rb_ref,
                       x1_ref, idx_ref, gate_ref, rank_ref, cnt_ref, carry_ref, *, tm):
    step = pl.program_id(0)

    @pl.when(step == 0)
    def _():
        carry_ref[...] = jnp.zeros_like(carry_ref)

    mix = (jnp.dot(att_ref[...], wo_ref[:GROUP, :], preferred_element_type=F32)
           + jnp.dot(o_ref[...], wo_ref[GROUP:, :], preferred_element_type=F32))
    x1 = _layernorm(DEEPNORM_ALPHA * x_ref[...] + mix, g_ref[...], b_ref[...])
    x1_ref[...] = x1

    logits = jnp.dot(x1, rw_ref[...], preferred_element_type=F32, precision=lax.Precision.HIGHEST) + rb_ref[...]
    lane = lax.broadcasted_iota(jnp.int32, logits.shape, 1)
    work = jnp.where(lane < N_EXPERTS, logits, -jnp.inf)
    vals, idxs = [], []
    for _ in range(TOP_K):
        m = jnp.max(work, axis=-1, keepdims=True)
        am = jnp.min(jnp.where(work == m, lane, LANES), axis=-1, keepdims=True)
        vals.append(m)
        idxs.append(am)
        work = jnp.where(lane == am, -jnp.inf, work)
    exps = [jnp.exp(v - vals[0]) for v in vals]
    denom = exps[0] + exps[1] + exps[2] + exps[3]
    chosen = jnp.zeros(logits.shape, F32)
    gate_out = jnp.zeros(logits.shape, F32)
    idx_out = jnp.zeros(logits.shape, jnp.int32)
    for k in range(TOP_K):
        chosen = jnp.where(lane == idxs[k], 1.0, chosen)
        gate_out = jnp.where(lane == k, exps[k] / denom, gate_out)
        idx_out = jnp.where(lane == k, idxs[k], idx_out)
    ri = lax.broadcasted_iota(jnp.int32, (tm, tm), 0)
    ci = lax.broadcasted_iota(jnp.int32, (tm, tm), 1)
    before = jnp.where(ri > ci, 1.0, 0.0).astype(BF16)
    prefix = jnp.dot(before, chosen.astype(BF16), preferred_element_type=F32) + carry_ref[...]
    rank_out = jnp.zeros(logits.shape, F32)
    for k in range(TOP_K):
        r = jnp.sum(jnp.where(lane == idxs[k], prefix, 0.0), axis=-1, keepdims=True)
        rank_out = jnp.where(lane == k, r, rank_out)
    carry_ref[...] = carry_ref[...] + jnp.sum(chosen, axis=0, keepdims=True)
    idx_ref[...] = idx_out
    gate_ref[...] = gate_out
    rank_ref[...] = rank_out.astype(jnp.int32)
    cnt_ref[...] = carry_ref[...].astype(jnp.int32)


def _out_router(att, o, x2d, w_out_bf, ln_g, ln_b, router_w, router_b, tm):
    t = x2d.shape[0]
    row = lambda i: (i, 0)
    const = lambda i: (0, 0)
    rw = jnp.pad(router_w, ((0, 0), (0, LANES - N_EXPERTS)))
    rb = jnp.pad(router_b.reshape(1, N_EXPERTS), ((0, 0), (0, LANES - N_EXPERTS)))
    kern = functools.partial(_out_router_kernel, tm=tm)
    return pl.pallas_call(
        kern,
        grid=(t // tm,),
        in_specs=[pl.BlockSpec((tm, GROUP), row), pl.BlockSpec((tm, GROUP), row),
                  pl.BlockSpec((tm, D_MODEL), row),
                  pl.BlockSpec((2 * GROUP, D_MODEL), const),
                  pl.BlockSpec((1, D_MODEL), const), pl.BlockSpec((1, D_MODEL), const),
                  pl.BlockSpec((D_MODEL, LANES), const), pl.BlockSpec((1, LANES), const)],
        out_specs=[pl.BlockSpec((tm, D_MODEL), row),
                   pl.BlockSpec((tm, LANES), row), pl.BlockSpec((tm, LANES), row),
                   pl.BlockSpec((tm, LANES), row), pl.BlockSpec((1, LANES), const)],
        out_shape=[jax.ShapeDtypeStruct((t, D_MODEL), F32),
                   jax.ShapeDtypeStruct((t, LANES), jnp.int32), jax.ShapeDtypeStruct((t, LANES), F32),
                   jax.ShapeDtypeStruct((t, LANES), jnp.int32), jax.ShapeDtypeStruct((1, LANES), jnp.int32)],
        scratch_shapes=[pltpu.VMEM((1, LANES), F32)],
        compiler_params=_params("arbitrary"),
        name="out_router",
    )(att, o, x2d, w_out_bf, ln_g.reshape(1, D_MODEL), ln_b.reshape(1, D_MODEL), rw, rb)


def _row_copy(src, src_row, dst, dst_row, sem):
    return pltpu.make_async_copy(src.at[pl.ds(src_row, 1), :], dst.at[pl.ds(dst_row, 1), :], sem)


def _dispatch_kernel(ps_ref, pe_ref, dest_ref, x_ref, xb_hbm, zero_ref, sem, zsem, *, tm, bm):
    i = pl.program_id(0)

    def zero_block(row):
        return pltpu.make_async_copy(zero_ref, xb_hbm.at[pl.ds(pl.multiple_of(row, bm), bm), :], zsem)

    def fill(e):
        return zero_block(pe_ref[e] - bm)

    @pl.when(i == 0)
    def _():
        zero_ref[...] = jnp.zeros_like(zero_ref)
        for e in range(N_EXPERTS):
            @pl.when(pe_ref[e] > ps_ref[e])
            def _():
                fill(e).start()
        first_unused = pe_ref[N_EXPERTS - 1] // bm
        n_blocks = xb_hbm.shape[0] // bm
        lax.fori_loop(first_unused, n_blocks, lambda b, c: (zero_block(b * bm).start(), c)[1], 0)
        for e in range(N_EXPERTS):
            @pl.when(pe_ref[e] > ps_ref[e])
            def _():
                fill(e).wait()
        lax.fori_loop(first_unused, n_blocks, lambda b, c: (zero_block(b * bm).wait(), c)[1], 0)

    def issue(r, carry):
        for k in range(TOP_K):
            _row_copy(x_ref, r, xb_hbm, dest_ref[0, 0, r * TOP_K + k], sem).start()
        return carry

    lax.fori_loop(0, tm, issue, 0)
    pltpu.make_async_copy(xb_hbm.at[pl.ds(0, tm * TOP_K), :], xb_hbm.at[pl.ds(0, tm * TOP_K), :], sem).wait()


def _dispatch(x1, dest, pad_start, pad_end, rows, tm, bm):
    t = x1.shape[0]
    return pl.pallas_call(
        functools.partial(_dispatch_kernel, tm=tm, bm=bm),
        grid_spec=pltpu.PrefetchScalarGridSpec(
            num_scalar_prefetch=2,
            grid=(t // tm,),
            in_specs=[pl.BlockSpec((1, 1, tm * TOP_K), lambda i, ps, pe: (i, 0, 0), memory_space=pltpu.SMEM),
                      pl.BlockSpec((tm, D_MODEL), lambda i, ps, pe: (i, 0))],
            out_specs=pl.BlockSpec(memory_space=pl.ANY),
            scratch_shapes=[pltpu.VMEM((bm, D_MODEL), F32),
                            pltpu.SemaphoreType.DMA(()), pltpu.SemaphoreType.DMA(())]),
        out_shape=jax.ShapeDtypeStruct((rows, D_MODEL), F32),
        compiler_params=_params("arbitrary"),
        name="dispatch",
    )(pad_start, pad_end, dest.reshape(t // tm, 1, tm * TOP_K), x1)


def _expert_kernel(be_ref, nb_ref, x_ref, wgu_ref, bgu_ref, wd_ref, bd_ref, y_ref):
    blk = pl.program_id(0)

    @pl.when(blk < nb_ref[0])
    def _():
        x = x_ref[...].astype(BF16)
        h = jnp.dot(x, wgu_ref[0].astype(BF16), preferred_element_type=F32) + bgu_ref[0]
        gate = jnp.minimum(h[:, :D_FF], SWIGLU_LIMIT)
        up = jnp.clip(h[:, D_FF:], -SWIGLU_LIMIT, SWIGLU_LIMIT)
        act = (up + 1.0) * (gate * jax.nn.sigmoid(SWIGLU_ALPHA * gate))
        y_ref[...] = jnp.dot(act.astype(BF16), wd_ref[0].astype(BF16), preferred_element_type=F32) + bd_ref[0]

    @pl.when(blk >= nb_ref[0])
    def _():
        y_ref[...] = jnp.zeros_like(y_ref)


def _experts(xb, blk_e, n_used, w_gu, b_gu, w_down, b_down, bm):
    rows = xb.shape[0]
    n_blocks = rows // bm
    used = lambda i, be, nb: (jnp.maximum(jnp.minimum(i, nb[0] - 1), 0), 0)
    return pl.pallas_call(
        _expert_kernel,
        grid_spec=pltpu.PrefetchScalarGridSpec(
            num_scalar_prefetch=2,
            grid=(n_blocks,),
            in_specs=[pl.BlockSpec((bm, D_MODEL), used),
                      pl.BlockSpec((1, D_MODEL, 2 * D_FF), lambda i, be, nb: (be[i], 0, 0)),
                      pl.BlockSpec((1, 1, 2 * D_FF), lambda i, be, nb: (be[i], 0, 0)),
                      pl.BlockSpec((1, D_FF, D_MODEL), lambda i, be, nb: (be[i], 0, 0)),
                      pl.BlockSpec((1, 1, D_MODEL), lambda i, be, nb: (be[i], 0, 0))],
            out_specs=pl.BlockSpec((bm, D_MODEL), lambda i, be, nb: (i, 0))),
        out_shape=jax.ShapeDtypeStruct((rows, D_MODEL), F32),
        compiler_params=_params("arbitrary"),
        name="experts",
    )(blk_e, n_used, xb, w_gu, b_gu.reshape(N_EXPERTS, 1, 2 * D_FF), w_down, b_down.reshape(N_EXPERTS, 1, D_MODEL))


def _combine_ln_kernel(dcur_ref, dnext_ref, x_ref, gate_ref, g_ref, b_ref, yb_hbm, o_ref, ybuf, sem, *, tm):
    i = pl.program_id(0)
    slot = i % 2

    def issue(dref, s):
        def body(r, carry):
            for k in range(TOP_K):
                _row_copy(yb_hbm, dref[0, 0, r * TOP_K + k], ybuf.at[s, k], r, sem.at[s]).start()
            return carry
        lax.fori_loop(0, tm, body, 0)

    @pl.when(i == 0)
    def _():
        issue(dcur_ref, 0)

    @pl.when(i + 1 < pl.num_programs(0))
    def _():
        issue(dnext_ref, 1 - slot)

    pltpu.make_async_copy(ybuf.at[slot], ybuf.at[slot], sem.at[slot]).wait()
    gates = gate_ref[...]
    y = sum(gates[:, k:k + 1] * ybuf[slot, k] for k in range(TOP_K))
    o_ref[...] = _layernorm(DEEPNORM_ALPHA * x_ref[...] + y, g_ref[...], b_ref[...])


def _combine_ln(x1, gates, dest, yb, ln_g, ln_b, tm):
    t = x1.shape[0]
    n = t // tm
    row = lambda i: (i, 0)
    const = lambda i: (0, 0)
    d2 = dest.reshape(n, 1, tm * TOP_K)
    return pl.pallas_call(
        functools.partial(_combine_ln_kernel, tm=tm),
        grid=(n,),
        in_specs=[pl.BlockSpec((1, 1, tm * TOP_K), lambda i: (i, 0, 0), memory_space=pltpu.SMEM),
                  pl.BlockSpec((1, 1, tm * TOP_K), lambda i: (jnp.minimum(i + 1, n - 1), 0, 0),
                               memory_space=pltpu.SMEM),
                  pl.BlockSpec((tm, D_MODEL), row),
                  pl.BlockSpec((tm, LANES), row),
                  pl.BlockSpec((1, D_MODEL), const), pl.BlockSpec((1, D_MODEL), const),
                  pl.BlockSpec(memory_space=pl.ANY)],
        out_specs=pl.BlockSpec((tm, D_MODEL), row),
        out_shape=jax.ShapeDtypeStruct((t, D_MODEL), F32),
        scratch_shapes=[pltpu.VMEM((2, TOP_K, tm, D_MODEL), F32), pltpu.SemaphoreType.DMA((2,))],
        compiler_params=_params("arbitrary"),
        name="combine_ln",
    )(d2, d2, x1, gates, ln_g.reshape(1, D_MODEL), ln_b.reshape(1, D_MODEL), yb)


MOE_BM = 256
ATTN_BLK = 512
GDN_SUB = 4


def _mixers(x2d, bsz, seq, cache_k, cache_v, conv_past, s0, lam, lam_init, w_in_bf, conv_w, a_log, dt_bias,
            delta_norm_w, subln_w):
    prompt = cache_k is None
    tm = ATTN_BLK if prompt else x2d.shape[0]
    q_bf, k_f, v_f, k_bf, v_bf, cin, z, ab = _in_proj(x2d, w_in_bf, tm, prompt)
    if prompt:
        att = _attn_prompt(q_bf, k_bf, v_bf, lam, subln_w, bsz, seq, ATTN_BLK, lam_init)
    else:
        att = _attn_sample(q_bf, k_bf, v_bf, cache_k, cache_v, lam, subln_w.reshape(1, HEAD), bsz, seq, lam_init)
    chunk = CHUNK if seq % CHUNK == 0 else seq
    n_sub = GDN_SUB if (seq // chunk) % GDN_SUB == 0 else 1
    past8 = jnp.pad(conv_past, ((0, 0), (SUBLANES - (CONV_W - 1), 0), (0, 0)))
    u, w, qd, kd, qk, gl = _gdn_prep(cin, ab, past8, conv_w, a_log, dt_bias, bsz, seq, chunk, n_sub)
    o, s_new = _gdn_scan(u, w, qd, kd, qk, gl, z, s0, delta_norm_w.reshape(1, HEAD), bsz, seq, chunk)
    return att, o, k_f, v_f, cin, s_new


def _moe(x1, idx, gates, rank, counts, w_gu, b_gu, w_down, b_down, ln_g, ln_b, bm, tm):
    t = x1.shape[0]
    n = t * TOP_K
    counts = counts[0, :N_EXPERTS]
    padded = (counts + bm - 1) // bm * bm
    pad_end = jnp.cumsum(padded).astype(jnp.int32)
    pad_start = (pad_end - padded).astype(jnp.int32)
    dest = pad_start[idx[:, :TOP_K]] + rank[:, :TOP_K]
    n_blocks = -(-n // bm) + N_EXPERTS
    blk_start = jnp.arange(n_blocks, dtype=jnp.int32) * bm
    blk_e = jnp.minimum(jnp.sum(pad_end[None, :] <= blk_start[:, None], axis=1), N_EXPERTS - 1).astype(jnp.int32)
    n_used = (pad_end[-1] // bm).astype(jnp.int32).reshape(1)
    xb = _dispatch(x1, dest, pad_start, pad_end, n_blocks * bm, tm, bm)
    yb = _experts(xb, blk_e, n_used, w_gu, b_gu, w_down, b_down, bm)
    return _combine_ln(x1, gates, dest, yb, ln_g, ln_b, tm)


def kernel(x_prompt, x_sample, cache_k, cache_v, state_conv, state_delta, w_in, conv_w, a_log, dt_bias,
           delta_norm_w, lambda_q1, lambda_k1, lambda_q2, lambda_k2, subln_w, w_out, ln1_g, ln1_b,
           router_w, router_b, w_gu, b_gu, w_down, b_down, ln2_g, ln2_b):
    bp, lp, _ = x_prompt.shape
    bs, ls, _ = x_sample.shape
    l = 0
    lam_init = 0.8 - 0.6 * math.exp(-0.3 * l)
    lam = (jnp.exp(jnp.sum(lambda_q1[l] * lambda_k1[l])) - jnp.exp(jnp.sum(lambda_q2[l] * lambda_k2[l]))
           + lam_init).reshape(1).astype(F32)
    w_in_bf = jnp.pad(w_in[l], ((0, 0), (0, IN_COLS_PAD - IN_COLS))).astype(BF16)
    shared = (lam, lam_init, w_in_bf, conv_w[l], a_log[l], dt_bias[l], delta_norm_w[l], subln_w[l])

    xp = x_prompt.reshape(bp * lp, D_MODEL)
    xs = x_sample.reshape(bs * ls, D_MODEL)
    zero_conv = jnp.zeros((bp, CONV_W - 1, CONV_CH), F32)
    zero_s = jnp.zeros((bp, N_HEADS, HEAD, HEAD), F32)
    att_p, o_p, k_p, v_p, cin_p, s_p = _mixers(xp, bp, lp, None, None, zero_conv, zero_s, *shared)
    att_s, o_s, k_s, v_s, cin_s, s_s = _mixers(xs, bs, ls, cache_k[l], cache_v[l], state_conv[l],
                                               state_delta[l], *shared)

    att = jnp.concatenate([att_p, att_s], axis=0)
    o = jnp.concatenate([o_p, o_s], axis=0)
    x_all = jnp.concatenate([xp, xs], axis=0)
    t = x_all.shape[0]
    tm = next(c for c in (512, 384, 256, 128) if t % c == 0)
    x1, idx, gates, rank, counts = _out_router(att, o, x_all, w_out[l].astype(BF16), ln1_g[l], ln1_b[l],
                                                    router_w[l], router_b[l], tm)
    y = _moe(x1, idx, gates, rank, counts, w_gu[l], b_gu[l], w_down[l], b_down[l], ln2_g[l], ln2_b[l], MOE_BM, tm)

    tp = bp * lp
    conv_tail = lambda cin, b, s: cin.reshape(b, s, CONV_CH)[:, s - (CONV_W - 1):][None]
    return (y[:tp].reshape(bp, lp, D_MODEL), y[tp:].reshape(bs, ls, D_MODEL),
            k_p.reshape(1, bp, lp, N_HEADS, HEAD), v_p.reshape(1, bp, lp, N_HEADS, HEAD),
            conv_tail(cin_p, bp, lp), s_p[None].astype(state_delta.dtype),
            k_s.reshape(1, bs, ls, N_HEADS, HEAD), v_s.reshape(1, bs, ls, N_HEADS, HEAD),
            conv_tail(cin_s, bs, ls), s_s[None].astype(state_delta.dtype))
```

```python
import functools
import math

import jax
import jax.numpy as jnp
from jax import lax
from jax.experimental import pallas as pl
from jax.experimental.pallas import tpu as pltpu

F32 = jnp.float32
BF16 = jnp.bfloat16

D_MODEL = 1024
HEAD = 128
N_HEADS = 4
DQK = HEAD // 2
GROUP = N_HEADS * HEAD
CONV_W = 4
CONV_CH = 3 * GROUP
CHUNK = 64
ALIBI_MAX = 8.0
N_EXPERTS = 32
TOP_K = 4
D_FF = D_MODEL
SWIGLU_LIMIT = 7.0
SWIGLU_ALPHA = 1.702
DEPTH = 1
DEEPNORM_ALPHA = (2 * DEPTH) ** 0.25
LN_EPS = 1e-5
SUBLN_EPS = 1e-5
GATED_NORM_EPS = 1e-6
L2_EPS = 1e-6

LANES = 128
SUBLANES = 8
BF16_EXACT_INT = 256
BF16_ROWS = 16
LOG2E = 1.4426950408889634
N_POS = 6
ONES_ROWS = BF16_ROWS
VMEM_LIMIT = 56 * 1024 * 1024

COL_Q, COL_K, COL_V, COL_CONV = 0, GROUP, 2 * GROUP, 3 * GROUP
COL_Z = COL_CONV + CONV_CH
COL_AB = COL_Z + GROUP
IN_COLS = COL_AB + 2 * N_HEADS
IN_COLS_PAD = COL_AB + LANES


def _params(*sem):
    return pltpu.CompilerParams(dimension_semantics=sem, vmem_limit_bytes=VMEM_LIMIT)


def _in_proj_kernel(x_ref, w_ref, q_ref, kf_ref, vf_ref, kb_ref, vb_ref, c_ref, z_ref, ab_ref, *, tm, transposed):
    xb = x_ref[...].astype(BF16)

    def section(lo, hi):
        return jnp.dot(xb, w_ref[:, lo:hi], preferred_element_type=F32)

    q = section(COL_Q, COL_K) * (DQK ** -0.5 * (LOG2E if transposed else 1.0))
    k = section(COL_K, COL_V)
    kf_ref[...] = k
    kb_ref[...] = k.astype(BF16)
    v = section(COL_V, COL_CONV)
    vf_ref[...] = v
    if transposed:
        q_ref[0] = q.T.astype(BF16)
        vb_ref[0] = v.T.astype(BF16)
    else:
        q_ref[...] = q.astype(BF16)
        vb_ref[...] = v.astype(BF16)
    c_ref[...] = section(COL_CONV, COL_Z)
    z_ref[...] = section(COL_Z, COL_AB)
    ab_ref[...] = section(COL_AB, IN_COLS_PAD)


def _in_proj(x2d, w_bf, tm, transposed):
    t = x2d.shape[0]
    row = lambda i: (i, 0)
    widths = (GROUP, GROUP, GROUP, GROUP, GROUP, CONV_CH, GROUP, LANES)
    dtypes = (BF16, F32, F32, BF16, BF16, F32, F32, F32)
    out_specs = [pl.BlockSpec((tm, w), row) for w in widths]
    out_shape = [jax.ShapeDtypeStruct((t, w), d) for w, d in zip(widths, dtypes)]
    if transposed:
        for slot in (0, 4):
            out_specs[slot] = pl.BlockSpec((1, GROUP, tm), lambda i: (i, 0, 0))
            out_shape[slot] = jax.ShapeDtypeStruct((t // tm, GROUP, tm), BF16)
    return pl.pallas_call(
        functools.partial(_in_proj_kernel, tm=tm, transposed=transposed),
        grid=(t // tm,),
        in_specs=[pl.BlockSpec((tm, D_MODEL), row),
                  pl.BlockSpec((D_MODEL, IN_COLS_PAD), lambda i: (0, 0))],
        out_specs=out_specs,
        out_shape=out_shape,
        compiler_params=_params("parallel"),
        name="in_proj",
    )(x2d, w_bf)


def _alibi_slopes():
    return [2.0 ** (-ALIBI_MAX * (h + 1) / N_HEADS) for h in range(N_HEADS)]


def _stack_halves(q):
    lane = lax.broadcasted_iota(jnp.int32, q.shape, 1)
    zero = jnp.zeros_like(q)
    return jnp.concatenate([jnp.where(lane < DQK, q, zero), jnp.where(lane < DQK, zero, q)], axis=0)


def _diff_norm(acc, l, lam, w, lam_init, rows):
    o = acc[:rows] / l[:rows] - lam * (acc[rows:] / l[rows:])
    ms = jnp.mean(o * o, axis=-1, keepdims=True)
    return o * lax.rsqrt(ms + SUBLN_EPS) * w * (1.0 - lam_init)


def _head_slope(h):
    s = _alibi_slopes()
    return jnp.where(h == 0, s[0], jnp.where(h == 1, s[1], jnp.where(h == 2, s[2], s[3]))).astype(F32)


def _attn_prompt_kernel(lam_ref, qt_ref, k_ref, vt_ref, w_ref, o_ref,
                        diag_ref, kaug_ref, vaug_ref, qz_ref, s0_ref, s1_ref, p0_ref, p1_ref, m_ref, acc_ref, *, blk, lam_init):
    i = pl.program_id(2)
    slope = _head_slope(pl.program_id(1)) * LOG2E
    rows = 2 * blk
    n_kv = k_ref.shape[0] // blk

    def pieces(x):
        lo = x % BF16_EXACT_INT
        return _split3(slope * lo.astype(F32)) + _split3(slope * (x - lo).astype(F32))

    @pl.when(i == 0)
    def _():
        lane = lax.broadcasted_iota(jnp.int32, (blk, HEAD), 1)
        extra = jnp.where(lane < N_POS, 1.0, 0.0).astype(BF16)
        for n, piece in enumerate(pieces(lax.broadcasted_iota(jnp.int32, (blk, 1), 0))):
            extra = jnp.where(lane == N_POS + n, piece, extra)
        ones = jnp.ones((ONES_ROWS, blk), BF16)
        c = lax.broadcasted_iota(jnp.int32, (blk, rows), 0)
        a = lax.broadcasted_iota(jnp.int32, (blk, rows), 1) % blk
        diag_ref[...] = jnp.where(c // CHUNK <= a // CHUNK, -slope * jnp.abs(a - c).astype(F32), -jnp.inf)

        def fill(j, carry):
            j0 = pl.multiple_of(j * blk, blk)
            kaug_ref[pl.ds(j0, blk), :HEAD] = k_ref[pl.ds(j0, blk), :]
            kaug_ref[pl.ds(j0, blk), HEAD:] = extra
            vaug_ref[j, :HEAD, :] = vt_ref[j]
            vaug_ref[j, HEAD:, :] = ones
            return carry

        lax.fori_loop(0, n_kv, fill, 0)

    qt = qt_ref[0]
    d = lax.broadcasted_iota(jnp.int32, qt.shape, 0)
    zero = jnp.zeros_like(qt)
    qz_ref[:HEAD, :] = jnp.concatenate([jnp.where(d < DQK, qt, zero), jnp.where(d < DQK, zero, qt)], axis=1)
    r = lax.broadcasted_iota(jnp.int32, (HEAD, rows), 0)
    qx = jnp.where((r >= N_POS) & (r < 2 * N_POS), 1.0, 0.0).astype(BF16)
    for n, piece in enumerate(pieces(-(lax.broadcasted_iota(jnp.int32, (1, rows), 1) % blk))):
        qx = jnp.where(r == n, piece, qx)
    qz_ref[HEAD:, :] = qx

    def tile_rows(j):
        return pl.ds(pl.multiple_of(j * blk, blk), blk)

    s = jnp.dot(k_ref[tile_rows(i), :], qz_ref[:HEAD, :], preferred_element_type=F32) + diag_ref[...]
    m = jnp.max(s, axis=0, keepdims=True)
    p1_ref[...] = jnp.exp2(s - m).astype(BF16)
    m_ref[...] = m
    acc_ref[...] = jnp.zeros_like(acc_ref)
    s0_ref[...] = jnp.dot(kaug_ref[tile_rows(0), :], qz_ref[...], preferred_element_type=F32)

    def step(j, s_cur, s_nxt, p_cur, p_nxt):
        s_nxt[...] = jnp.dot(kaug_ref[tile_rows(jnp.minimum(j + 1, i - 1)), :], qz_ref[...],
                             preferred_element_type=F32)
        pv = jnp.dot(vaug_ref[jnp.where(j == 0, i, j - 1)], p_nxt[...], preferred_element_type=F32)
        shift = -slope * ((i - j) * blk).astype(F32)
        m_old = m_ref[...]
        m_new = jnp.maximum(m_old, jnp.max(s_cur[...], axis=0, keepdims=True) + shift)
        p_cur[...] = jnp.exp2(s_cur[...] - (m_new - shift)).astype(BF16)
        acc_ref[...] = (acc_ref[...] + pv) * jnp.exp2(m_old - m_new)
        m_ref[...] = m_new

    def body(jj, carry):
        step(2 * jj, s0_ref, s1_ref, p0_ref, p1_ref)

        @pl.when(2 * jj + 1 < i)
        def _():
            step(2 * jj + 1, s1_ref, s0_ref, p1_ref, p0_ref)

        return carry

    lax.fori_loop(0, (i + 1) // 2, body, 0)
    p_last = jnp.where(i % 2 == 1, p0_ref[...], p1_ref[...])
    acc = acc_ref[...] + jnp.dot(vaug_ref[jnp.where(i > 0, i - 1, i)], p_last, preferred_element_type=F32)
    l = acc[HEAD:HEAD + 1, :]
    num = acc[:HEAD, :]
    ot = num[:, :blk] / l[:, :blk] - lam_ref[0] * (num[:, blk:] / l[:, blk:])
    ms = jnp.mean(ot * ot, axis=0, keepdims=True)
    ot = ot * lax.rsqrt(ms + SUBLN_EPS) * w_ref[...] * (1.0 - lam_init)
    o_ref[...] = ot.T.astype(o_ref.dtype)


def _attn_prompt(qt_bf, k_bf, vt_bf, lam, subln_w, bsz, seq, blk, lam_init):
    nq = seq // blk
    kern = functools.partial(_attn_prompt_kernel, blk=blk, lam_init=lam_init)
    return pl.pallas_call(
        kern,
        grid=(bsz, N_HEADS, nq),
        in_specs=[pl.BlockSpec(memory_space=pltpu.SMEM),
                  pl.BlockSpec((1, HEAD, blk), lambda b, h, i: (b * nq + i, h, 0)),
                  pl.BlockSpec((seq, HEAD), lambda b, h, i: (b, h)),
                  pl.BlockSpec((nq, HEAD, blk), lambda b, h, i: (b, h, 0)),
                  pl.BlockSpec((HEAD, 1), lambda b, h, i: (0, 0))],
        out_specs=pl.BlockSpec((blk, HEAD), lambda b, h, i: (b * nq + i, h)),
        scratch_shapes=[pltpu.VMEM((blk, 2 * blk), F32),
                        pltpu.VMEM((seq, 2 * HEAD), BF16),
                        pltpu.VMEM((nq, HEAD + ONES_ROWS, blk), BF16),
                        pltpu.VMEM((2 * HEAD, 2 * blk), BF16),
                        pltpu.VMEM((blk, 2 * blk), F32), pltpu.VMEM((blk, 2 * blk), F32),
                        pltpu.VMEM((blk, 2 * blk), BF16), pltpu.VMEM((blk, 2 * blk), BF16),
                        pltpu.VMEM((1, 2 * blk), F32),
                        pltpu.VMEM((HEAD + ONES_ROWS, 2 * blk), F32)],
        out_shape=jax.ShapeDtypeStruct((bsz * seq, GROUP), BF16),
        compiler_params=_params("parallel", "parallel", "arbitrary"),
        name="attn_prompt",
    )(lam, qt_bf, k_bf, vt_bf, subln_w.reshape(HEAD, 1))


def _attn_sample_kernel(lam_ref, q_ref, kn_ref, vn_ref, kc_ref, vc_ref, w_ref, o_ref, *, seq, past, lam_init):
    nt = (((1,), (1,)), ((), ()))
    qpos = past + lax.broadcasted_iota(jnp.int32, (2 * seq, 1), 0) % seq
    rel_c = jnp.abs(qpos - lax.broadcasted_iota(jnp.int32, (1, past), 1)).astype(F32)
    rel_n = jnp.abs(qpos - (past + lax.broadcasted_iota(jnp.int32, (1, seq), 1))).astype(F32)
    for h, slope in enumerate(_alibi_slopes()):
        col = slice(h * HEAD, (h + 1) * HEAD)
        qz = _stack_halves(q_ref[:, col])
        s_c = lax.dot_general(qz, kc_ref[0, 0, :, h, :].astype(BF16), nt, preferred_element_type=F32)
        s_c = s_c - slope * rel_c
        s_n = lax.dot_general(qz, kn_ref[:, col], nt, preferred_element_type=F32) - slope * rel_n
        m = jnp.maximum(jnp.max(s_c, axis=-1, keepdims=True), jnp.max(s_n, axis=-1, keepdims=True))
        p_c = jnp.exp(s_c - m)
        p_n = jnp.exp(s_n - m)
        l = jnp.sum(p_c, axis=-1, keepdims=True) + jnp.sum(p_n, axis=-1, keepdims=True)
        acc = (jnp.dot(p_c.astype(BF16), vc_ref[0, 0, :, h, :].astype(BF16), preferred_element_type=F32)
               + jnp.dot(p_n.astype(BF16), vn_ref[:, col], preferred_element_type=F32))
        o_ref[:, col] = _diff_norm(acc, l, lam_ref[0], w_ref[...], lam_init, seq).astype(o_ref.dtype)


def _attn_sample(q_bf, k_bf, v_bf, cache_k, cache_v, layer, lam, subln_w, bsz, seq, lam_init):
    past = cache_k.shape[2]
    kern = functools.partial(_attn_sample_kernel, seq=seq, past=past, lam_init=lam_init)
    new = pl.BlockSpec((seq, GROUP), lambda b: (b, 0))
    cache = pl.BlockSpec((1, 1, past, N_HEADS, HEAD), lambda b: (layer, b, 0, 0, 0))
    return pl.pallas_call(
        kern,
        grid=(bsz,),
        in_specs=[pl.BlockSpec(memory_space=pltpu.SMEM), new, new, new, cache, cache,
                  pl.BlockSpec((1, HEAD), lambda b: (0, 0))],
        out_specs=new,
        out_shape=jax.ShapeDtypeStruct((bsz * seq, GROUP), BF16),
        compiler_params=_params("parallel"),
        name="attn_sample",
    )(lam, q_bf, k_bf, v_bf, cache_k, cache_v, subln_w)


def _split3(x):
    hi = x.astype(BF16)
    r1 = x - hi.astype(F32)
    mid = r1.astype(BF16)
    lo = (r1 - mid.astype(F32)).astype(BF16)
    return hi, mid, lo


def _gdn_prep_kernel(cin_ref, halo_ref, past_ref, ab_ref, cw_ref, alog_ref, dtb_ref,
                     u_ref, w_ref, qd_ref, kd_ref, qk_ref, gl_ref, *, chunk, n_sub):
    c_idx = pl.program_id(1)
    rows = chunk * n_sub
    prev = jnp.where(c_idx == 0, past_ref[0], halo_ref[...])
    xin = jnp.concatenate([prev, cin_ref[...]], axis=0)
    conv = sum(xin[SUBLANES - (CONV_W - 1) + j: SUBLANES - (CONV_W - 1) + j + rows] * cw_ref[j:j + 1, :]
               for j in range(CONV_W))
    conv = conv * jax.nn.sigmoid(conv)

    ab = ab_ref[...]
    lane = lax.broadcasted_iota(jnp.int32, ab.shape, 1)
    pre = ab + dtb_ref[...]
    softplus = jnp.maximum(pre, 0.0) + jnp.log(1.0 + jnp.exp(-jnp.abs(pre)))
    g = jnp.where(lane < N_HEADS, -jnp.exp(alog_ref[...]) * softplus, 0.0)
    beta_all = jax.nn.sigmoid(ab)

    ri = lax.broadcasted_iota(jnp.int32, (rows, rows), 0)
    ci = lax.broadcasted_iota(jnp.int32, (rows, rows), 1)
    same = (ri // chunk) == (ci // chunk)
    incl = same & (ri >= ci)
    strict = same & (ri > ci)
    eye = jnp.where(ri == ci, 1.0, 0.0).astype(F32)
    nt = (((1,), (1,)), ((), ()))
    g_parts = _split3(g)
    ones_incl = jnp.where(incl, 1.0, 0.0).astype(BF16)
    ones_same = jnp.where(same, 1.0, 0.0).astype(BF16)
    gc = sum(jnp.dot(ones_incl, part, preferred_element_type=F32) for part in g_parts)
    g_end = sum(jnp.dot(ones_same, part, preferred_element_type=F32) for part in g_parts)
    gct = gc.T
    for sc in range(n_sub):
        gl_ref[sc] = g_end[sc * chunk:sc * chunk + 1, :]

    t_mats, p_mats, rhs = [], [], []
    for h in range(N_HEADS):
        col = slice(h * HEAD, (h + 1) * HEAD)
        qh = conv[:, h * HEAD:(h + 1) * HEAD]
        kh = conv[:, GROUP + h * HEAD:GROUP + (h + 1) * HEAD]
        vh = conv[:, 2 * GROUP + h * HEAD:2 * GROUP + (h + 1) * HEAD]
        qh = qh * lax.rsqrt(jnp.sum(qh * qh, axis=-1, keepdims=True) + L2_EPS) * (HEAD ** -0.5)
        kh = kh * lax.rsqrt(jnp.sum(kh * kh, axis=-1, keepdims=True) + L2_EPS)
        beta = beta_all[:, N_HEADS + h:N_HEADS + h + 1]
        gcol = gc[:, h:h + 1]
        grow = gct[h:h + 1, :]
        gamma = jnp.exp(jnp.where(incl, gcol - grow, -jnp.inf))
        egc = jnp.exp(gcol)
        kb = kh * beta
        khb = kh.astype(BF16)
        a = jnp.where(strict, lax.dot_general(kb.astype(BF16), khb, nt, preferred_element_type=F32) * gamma, 0.0)
        qk = (lax.dot_general(qh.astype(BF16), khb, nt, preferred_element_type=F32) * gamma).astype(BF16)
        for sc in range(n_sub):
            blk = slice(sc * chunk, (sc + 1) * chunk)
            qk_ref[h, blk, :] = qk[blk, blk]
        qd_ref[:, col] = (qh * egc).astype(BF16)
        kd_ref[:, col] = (kh * jnp.exp(g_end[:, h:h + 1] - gcol)).astype(BF16)
        t_mats.append(eye - a)
        p_mats.append(a)
        rhs.append(((vh * beta).astype(BF16), (kb * egc).astype(BF16)))

    for _ in range(int(math.log2(chunk)) - 1):
        for h in range(N_HEADS):
            pb = p_mats[h].astype(BF16)
            p_mats[h] = jnp.dot(pb, pb, preferred_element_type=F32)
        for h in range(N_HEADS):
            t_mats[h] = t_mats[h] + jnp.dot(t_mats[h].astype(BF16), p_mats[h].astype(BF16),
                                            preferred_element_type=F32)

    for h in range(N_HEADS):
        col = slice(h * HEAD, (h + 1) * HEAD)
        t_inv = t_mats[h].astype(BF16)
        u_ref[:, col] = jnp.dot(t_inv, rhs[h][0], preferred_element_type=F32)
        w_ref[:, col] = jnp.dot(t_inv, rhs[h][1], preferred_element_type=F32).astype(BF16)


def _gdn_prep(cin, ab, past8, conv_w, a_log, dt_bias, bsz, seq, chunk, n_sub):
    rows = chunk * n_sub
    nblk = seq // rows
    t = bsz * seq
    lanes = lambda v: jnp.pad(v.reshape(1, N_HEADS).astype(F32), ((0, 0), (0, LANES - N_HEADS)))
    kern = functools.partial(_gdn_prep_kernel, chunk=chunk, n_sub=n_sub)
    rowblk = lambda b, c: (b * nblk + c, 0)
    halo = lambda b, c: (jnp.maximum((b * nblk + c) * (rows // SUBLANES) - 1, 0), 0)
    const = lambda b, c: (0, 0)
    return pl.pallas_call(
        kern,
        grid=(bsz, nblk),
        in_specs=[pl.BlockSpec((rows, CONV_CH), rowblk),
                  pl.BlockSpec((SUBLANES, CONV_CH), halo),
                  pl.BlockSpec((1, SUBLANES, CONV_CH), lambda b, c: (b, 0, 0)),
                  pl.BlockSpec((rows, LANES), rowblk),
                  pl.BlockSpec((CONV_W, CONV_CH), const),
                  pl.BlockSpec((1, LANES), const),
                  pl.BlockSpec((1, LANES), const)],
        out_specs=[pl.BlockSpec((rows, GROUP), rowblk),
                   pl.BlockSpec((rows, GROUP), rowblk),
                   pl.BlockSpec((rows, GROUP), rowblk),
                   pl.BlockSpec((rows, GROUP), rowblk),
                   pl.BlockSpec((N_HEADS, rows, chunk), lambda b, c: (0, b * nblk + c, 0)),
                   pl.BlockSpec((n_sub, 1, LANES), lambda b, c: (b * nblk + c, 0, 0))],
        out_shape=[jax.ShapeDtypeStruct((t, GROUP), F32),
                   jax.ShapeDtypeStruct((t, GROUP), BF16),
                   jax.ShapeDtypeStruct((t, GROUP), BF16),
                   jax.ShapeDtypeStruct((t, GROUP), BF16),
                   jax.ShapeDtypeStruct((N_HEADS, t, chunk), BF16),
                   jax.ShapeDtypeStruct((t // chunk, 1, LANES), F32)],
        compiler_params=_params("parallel", "parallel"),
        name="gdn_prep",
    )(cin, cin, past8, ab, conv_w, lanes(a_log), lanes(dt_bias))


def _gdn_scan_kernel(u_ref, w_ref, qd_ref, kd_ref, qk_ref, gl_ref, z_ref, s0_ref, nw_ref,
                     o_ref, sf_ref, s_ref, *, bsz, chunk):
    c_idx = pl.program_id(0)

    @pl.when(c_idx == 0)
    def _():
        s_ref[...] = s0_ref[...]

    tn = (((0,), (0,)), ((), ()))
    for b in range(bsz):
        glast = jnp.exp(gl_ref[b, 0])
        for h in range(N_HEADS):
            col = slice(h * HEAD, (h + 1) * HEAD)
            s = s_ref[b, h]
            wq = jnp.concatenate([w_ref[b, :, col], qd_ref[b, :, col]], axis=0)
            r = jnp.dot(wq, s.astype(BF16), preferred_element_type=F32)
            v_new = (u_ref[b, :, col] - r[:chunk]).astype(BF16)
            o = r[chunk:] + jnp.dot(qk_ref[h, b], v_new, preferred_element_type=F32)
            s_ref[b, h] = s * glast[:, h:h + 1] + lax.dot_general(kd_ref[b, :, col], v_new, tn,
                                                                 preferred_element_type=F32)
            zh = z_ref[b, :, col]
            ms = jnp.mean(o * o, axis=-1, keepdims=True)
            o = o * lax.rsqrt(ms + GATED_NORM_EPS) * nw_ref[...] * (zh * jax.nn.sigmoid(zh))
            o_ref[b, :, col] = o.astype(o_ref.dtype)

    @pl.when(c_idx == pl.num_programs(0) - 1)
    def _():
        sf_ref[...] = s_ref[...]


def _gdn_scan(u, w, qd, kd, qk, gl, z, s0, norm_w, bsz, seq, chunk):
    nc = seq // chunk
    kern = functools.partial(_gdn_scan_kernel, bsz=bsz, chunk=chunk)
    tok = pl.BlockSpec((bsz, chunk, GROUP), lambda c: (0, c, 0))
    state = pl.BlockSpec((bsz, N_HEADS, HEAD, HEAD), lambda c: (0, 0, 0, 0))
    o, s_final = pl.pallas_call(
        kern,
        grid=(nc,),
        in_specs=[tok, tok, tok, tok,
                  pl.BlockSpec((N_HEADS, bsz, chunk, chunk), lambda c: (0, 0, c, 0)),
                  pl.BlockSpec((bsz, 1, 1, LANES), lambda c: (0, c, 0, 0)),
                  tok, state,
                  pl.BlockSpec((1, HEAD), lambda c: (0, 0))],
        out_specs=[tok, state],
        out_shape=[jax.ShapeDtypeStruct((bsz, seq, GROUP), BF16),
                   jax.ShapeDtypeStruct((bsz, N_HEADS, HEAD, HEAD), F32)],
        scratch_shapes=[pltpu.VMEM((bsz, N_HEADS, HEAD, HEAD), F32)],
        compiler_params=_params("arbitrary"),
        name="gdn_scan",
    )(u.reshape(bsz, seq, GROUP), w.reshape(bsz, seq, GROUP), qd.reshape(bsz, seq, GROUP),
      kd.reshape(bsz, seq, GROUP), qk.reshape(N_HEADS, bsz, seq, chunk), gl.reshape(bsz, nc, 1, LANES),
      z.reshape(bsz, seq, GROUP), s0, norm_w)
    return o.reshape(bsz * seq, GROUP), s_final


def _layernorm(x, g, b):
    mu = jnp.mean(x, axis=-1, keepdims=True)
    xc = x - mu
    var = jnp.mean(xc * xc, axis=-1, keepdims=True)
    return xc * lax.rsqrt(var + LN_EPS) * g + b


def _out_router_kernel(att_ref, o_ref, x_ref, wo_ref, g_ref, b_ref, rw_ref, rb_ref,
                       x1_ref, idx_ref, gate_ref, rank_ref, cnt_ref, carry_ref, *, tm):
    step = pl.program_id(0)

    @pl.when(step == 0)
    def _():
        carry_ref[...] = jnp.zeros_like(carry_ref)

    mix = (jnp.dot(att_ref[...], wo_ref[:GROUP, :], preferred_element_type=F32)
           + jnp.dot(o_ref[...], wo_ref[GROUP:, :], preferred_element_type=F32))
    x1 = _layernorm(DEEPNORM_ALPHA * x_ref[...] + mix, g_ref[...], b_ref[...])
    x1_ref[...] = x1

    logits = jnp.dot(x1, rw_ref[...], preferred_element_type=F32, precision=lax.Precision.HIGHEST) + rb_ref[...]
    lane = lax.broadcasted_iota(jnp.int32, logits.shape, 1)
    work = jnp.where(lane < N_EXPERTS, logits, -jnp.inf)
    vals, idxs = [], []
    for _ in range(TOP_K):
        m = jnp.max(work, axis=-1, keepdims=True)
        am = jnp.min(jnp.where(work == m, lane, LANES), axis=-1, keepdims=True)
        vals.append(m)
        idxs.append(am)
        work = jnp.where(lane == am, -jnp.inf, work)
    exps = [jnp.exp(v - vals[0]) for v in vals]
    denom = exps[0] + exps[1] + exps[2] + exps[3]
    chosen = jnp.zeros(logits.shape, F32)
    gate_out = jnp.zeros(logits.shape, F32)
    idx_out = jnp.zeros(logits.shape, jnp.int32)
    for k in range(TOP_K):
        chosen = jnp.where(lane == idxs[k], 1.0, chosen)
        gate_out = jnp.where(lane == k, exps[k] / denom, gate_out)
        idx_out = jnp.where(lane == k, idxs[k], idx_out)
    ri = lax.broadcasted_iota(jnp.int32, (tm, tm), 0)
    ci = lax.broadcasted_iota(jnp.int32, (tm, tm), 1)
    before = jnp.where(ri > ci, 1.0, 0.0).astype(BF16)
    prefix = jnp.dot(before, chosen.astype(BF16), preferred_element_type=F32) + carry_ref[...]
    rank_out = jnp.zeros(logits.shape, F32)
    for k in range(TOP_K):
        r = jnp.sum(jnp.where(lane == idxs[k], prefix, 0.0), axis=-1, keepdims=True)
        rank_out = jnp.where(lane == k, r, rank_out)
    carry_ref[...] = carry_ref[...] + jnp.sum(chosen, axis=0, keepdims=True)
    idx_ref[...] = idx_out
    gate_ref[...] = gate_out
    rank_ref[...] = rank_out.astype(jnp.int32)
    cnt_ref[...] = carry_ref[...].astype(jnp.int32)


def _out_router(att, o, x2d, w_out_bf, ln_g, ln_b, router_w, router_b, tm):
    t = x2d.shape[0]
    row = lambda i: (i, 0)
    const = lambda i: (0, 0)
    rw = jnp.pad(router_w, ((0, 0), (0, LANES - N_EXPERTS)))
    rb = jnp.pad(router_b.reshape(1, N_EXPERTS), ((0, 0), (0, LANES - N_EXPERTS)))
    kern = functools.partial(_out_router_kernel, tm=tm)
    return pl.pallas_call(
        kern,
        grid=(t // tm,),
        in_specs=[pl.BlockSpec((tm, GROUP), row), pl.BlockSpec((tm, GROUP), row),
                  pl.BlockSpec((tm, D_MODEL), row),
                  pl.BlockSpec((2 * GROUP, D_MODEL), const),
                  pl.BlockSpec((1, D_MODEL), const), pl.BlockSpec((1, D_MODEL), const),
                  pl.BlockSpec((D_MODEL, LANES), const), pl.BlockSpec((1, LANES), const)],
        out_specs=[pl.BlockSpec((tm, D_MODEL), row),
                   pl.BlockSpec((tm, LANES), row), pl.BlockSpec((tm, LANES), row),
                   pl.BlockSpec((tm, LANES), row), pl.BlockSpec((1, LANES), const)],
        out_shape=[jax.ShapeDtypeStruct((t, D_MODEL), F32),
                   jax.ShapeDtypeStruct((t, LANES), jnp.int32), jax.ShapeDtypeStruct((t, LANES), F32),
                   jax.ShapeDtypeStruct((t, LANES), jnp.int32), jax.ShapeDtypeStruct((1, LANES), jnp.int32)],
        scratch_shapes=[pltpu.VMEM((1, LANES), F32)],
        compiler_params=_params("arbitrary"),
        name="out_router",
    )(att, o, x2d, w_out_bf, ln_g.reshape(1, D_MODEL), ln_b.reshape(1, D_MODEL), rw, rb)


def _row_copy(src, src_row, dst, dst_row, sem):
    return pltpu.make_async_copy(src.at[pl.ds(src_row, 1), :], dst.at[pl.ds(dst_row, 1), :], sem)


def _dispatch_kernel(ps_ref, pe_ref, dest_ref, x_ref, xb_hbm, zero_ref, sem, zsem, *, tm, bm):
    i = pl.program_id(0)

    def zero_block(row):
        return pltpu.make_async_copy(zero_ref, xb_hbm.at[pl.ds(pl.multiple_of(row, bm), bm), :], zsem)

    def fill(e):
        return zero_block(pe_ref[e] - bm)

    @pl.when(i == 0)
    def _():
        zero_ref[...] = jnp.zeros_like(zero_ref)
        for e in range(N_EXPERTS):
            @pl.when(pe_ref[e] > ps_ref[e])
            def _():
                fill(e).start()
        first_unused = pe_ref[N_EXPERTS - 1] // bm
        n_blocks = xb_hbm.shape[0] // bm
        lax.fori_loop(first_unused, n_blocks, lambda b, c: (zero_block(b * bm).start(), c)[1], 0)
        for e in range(N_EXPERTS):
            @pl.when(pe_ref[e] > ps_ref[e])
            def _():
                fill(e).wait()
        lax.fori_loop(first_unused, n_blocks, lambda b, c: (zero_block(b * bm).wait(), c)[1], 0)

    def issue(r, carry):
        for k in range(TOP_K):
            _row_copy(x_ref, r, xb_hbm, dest_ref[0, 0, r * TOP_K + k], sem).start()
        return carry

    lax.fori_loop(0, tm, issue, 0)
    pltpu.make_async_copy(xb_hbm.at[pl.ds(0, tm * TOP_K), :], xb_hbm.at[pl.ds(0, tm * TOP_K), :], sem).wait()


def _dispatch(x1, dest, pad_start, pad_end, rows, tm, bm):
    t = x1.shape[0]
    return pl.pallas_call(
        functools.partial(_dispatch_kernel, tm=tm, bm=bm),
        grid_spec=pltpu.PrefetchScalarGridSpec(
            num_scalar_prefetch=2,
            grid=(t // tm,),
            in_specs=[pl.BlockSpec((1, 1, tm * TOP_K), lambda i, ps, pe: (i, 0, 0), memory_space=pltpu.SMEM),
                      pl.BlockSpec((tm, D_MODEL), lambda i, ps, pe: (i, 0))],
            out_specs=pl.BlockSpec(memory_space=pl.ANY),
            scratch_shapes=[pltpu.VMEM((bm, D_MODEL), F32),
                            pltpu.SemaphoreType.DMA(()), pltpu.SemaphoreType.DMA(())]),
        out_shape=jax.ShapeDtypeStruct((rows, D_MODEL), F32),
        compiler_params=_params("arbitrary"),
        name="dispatch",
    )(pad_start, pad_end, dest.reshape(t // tm, 1, tm * TOP_K), x1)


def _expert_kernel(be_ref, nb_ref, x_ref, wgu_ref, bgu_ref, wd_ref, bd_ref, y_ref):
    blk = pl.program_id(0)

    @pl.when(blk < nb_ref[0])
    def _():
        x = x_ref[...].astype(BF16)
        h = jnp.dot(x, wgu_ref[0].astype(BF16), preferred_element_type=F32) + bgu_ref[0]
        gate = jnp.minimum(h[:, :D_FF], SWIGLU_LIMIT)
        up = jnp.clip(h[:, D_FF:], -SWIGLU_LIMIT, SWIGLU_LIMIT)
        act = (up + 1.0) * (gate * jax.nn.sigmoid(SWIGLU_ALPHA * gate))
        y_ref[...] = jnp.dot(act.astype(BF16), wd_ref[0].astype(BF16), preferred_element_type=F32) + bd_ref[0]

    @pl.when(blk >= nb_ref[0])
    def _():
        y_ref[...] = jnp.zeros_like(y_ref)


def _experts(xb, blk_e, n_used, w_gu, b_gu, w_down, b_down, bm):
    rows = xb.shape[0]
    n_blocks = rows // bm
    used = lambda i, be, nb: (jnp.maximum(jnp.minimum(i, nb[0] - 1), 0), 0)
    return pl.pallas_call(
        _expert_kernel,
        grid_spec=pltpu.PrefetchScalarGridSpec(
            num_scalar_prefetch=2,
            grid=(n_blocks,),
            in_specs=[pl.BlockSpec((bm, D_MODEL), used),
                      pl.BlockSpec((1, D_MODEL, 2 * D_FF), lambda i, be, nb: (be[i], 0, 0)),
                      pl.BlockSpec((1, 1, 2 * D_FF), lambda i, be, nb: (be[i], 0, 0)),
                      pl.BlockSpec((1, D_FF, D_MODEL), lambda i, be, nb: (be[i], 0, 0)),
                      pl.BlockSpec((1, 1, D_MODEL), lambda i, be, nb: (be[i], 0, 0))],
            out_specs=pl.BlockSpec((bm, D_MODEL), lambda i, be, nb: (i, 0))),
        out_shape=jax.ShapeDtypeStruct((rows, D_MODEL), F32),
        compiler_params=_params("arbitrary"),
        name="experts",
    )(blk_e, n_used, xb, w_gu, b_gu.reshape(N_EXPERTS, 1, 2 * D_FF), w_down, b_down.reshape(N_EXPERTS, 1, D_MODEL))


def _combine_ln_kernel(dcur_ref, dnext_ref, x_ref, gate_ref, g_ref, b_ref, yb_hbm, op_ref, os_ref, ybuf, sem, *,
                       tm, n_prompt):
    i = pl.program_id(0)
    slot = i % 2

    def issue(dref, s):
        def body(r, carry):
            for k in range(TOP_K):
                _row_copy(yb_hbm, dref[0, 0, r * TOP_K + k], ybuf.at[s, k], r, sem.at[s]).start()
            return carry
        lax.fori_loop(0, tm, body, 0)

    @pl.when(i == 0)
    def _():
        issue(dcur_ref, 0)

    @pl.when(i + 1 < pl.num_programs(0))
    def _():
        issue(dnext_ref, 1 - slot)

    pltpu.make_async_copy(ybuf.at[slot], ybuf.at[slot], sem.at[slot]).wait()
    gates = gate_ref[...]
    y = sum(gates[:, k:k + 1] * ybuf[slot, k] for k in range(TOP_K))
    out = _layernorm(DEEPNORM_ALPHA * x_ref[...] + y, g_ref[...], b_ref[...])

    @pl.when(i < n_prompt)
    def _():
        op_ref[...] = out

    @pl.when(i >= n_prompt)
    def _():
        os_ref[...] = out


def _combine_ln(x1, gates, dest, yb, ln_g, ln_b, tm, t_prompt):
    t = x1.shape[0]
    n = t // tm
    n_prompt = t_prompt // tm
    row = lambda i: (i, 0)
    const = lambda i: (0, 0)
    d2 = dest.reshape(n, 1, tm * TOP_K)
    return pl.pallas_call(
        functools.partial(_combine_ln_kernel, tm=tm, n_prompt=n_prompt),
        grid=(n,),
        in_specs=[pl.BlockSpec((1, 1, tm * TOP_K), lambda i: (i, 0, 0), memory_space=pltpu.SMEM),
                  pl.BlockSpec((1, 1, tm * TOP_K), lambda i: (jnp.minimum(i + 1, n - 1), 0, 0),
                               memory_space=pltpu.SMEM),
                  pl.BlockSpec((tm, D_MODEL), row),
                  pl.BlockSpec((tm, LANES), row),
                  pl.BlockSpec((1, D_MODEL), const), pl.BlockSpec((1, D_MODEL), const),
                  pl.BlockSpec(memory_space=pl.ANY)],
        out_specs=[pl.BlockSpec((tm, D_MODEL), lambda i: (jnp.minimum(i, n_prompt - 1), 0)),
                   pl.BlockSpec((tm, D_MODEL), lambda i: (jnp.maximum(i - n_prompt, 0), 0))],
        out_shape=[jax.ShapeDtypeStruct((t_prompt, D_MODEL), F32),
                   jax.ShapeDtypeStruct((t - t_prompt, D_MODEL), F32)],
        scratch_shapes=[pltpu.VMEM((2, TOP_K, tm, D_MODEL), F32), pltpu.SemaphoreType.DMA((2,))],
        compiler_params=_params("arbitrary"),
        name="combine_ln",
    )(d2, d2, x1, gates, ln_g.reshape(1, D_MODEL), ln_b.reshape(1, D_MODEL), yb)


MOE_BM = 256
ATTN_BLK = 512
GDN_SUB = 4
COMBINE_TM = 128


def _mixers(x2d, bsz, seq, cache_k, cache_v, conv_past, s0, lam, lam_init, w_in_bf, conv_w, a_log, dt_bias,
            delta_norm_w, subln_w):
    prompt = cache_k is None
    tm = ATTN_BLK if prompt else x2d.shape[0]
    q_bf, k_f, v_f, k_bf, v_bf, cin, z, ab = _in_proj(x2d, w_in_bf, tm, prompt)
    if prompt:
        att = _attn_prompt(q_bf, k_bf, v_bf, lam, subln_w, bsz, seq, ATTN_BLK, lam_init)
    else:
        att = _attn_sample(q_bf, k_bf, v_bf, cache_k, cache_v, 0, lam, subln_w.reshape(1, HEAD), bsz, seq, lam_init)
    chunk = CHUNK if seq % CHUNK == 0 else seq
    n_sub = GDN_SUB if (seq // chunk) % GDN_SUB == 0 else 1
    past8 = jnp.pad(conv_past, ((0, 0), (SUBLANES - (CONV_W - 1), 0), (0, 0)))
    u, w, qd, kd, qk, gl = _gdn_prep(cin, ab, past8, conv_w, a_log, dt_bias, bsz, seq, chunk, n_sub)
    o, s_new = _gdn_scan(u, w, qd, kd, qk, gl, z, s0, delta_norm_w.reshape(1, HEAD), bsz, seq, chunk)
    return att, o, k_f, v_f, cin, s_new


def _moe(x1, idx, gates, rank, counts, w_gu, b_gu, w_down, b_down, ln_g, ln_b, bm, tm, t_prompt):
    t = x1.shape[0]
    n = t * TOP_K
    counts = counts[0, :N_EXPERTS]
    padded = (counts + bm - 1) // bm * bm
    pad_end = jnp.cumsum(padded).astype(jnp.int32)
    pad_start = (pad_end - padded).astype(jnp.int32)
    dest = pad_start[idx[:, :TOP_K]] + rank[:, :TOP_K]
    n_blocks = -(-n // bm) + N_EXPERTS
    blk_start = jnp.arange(n_blocks, dtype=jnp.int32) * bm
    blk_e = jnp.minimum(jnp.sum(pad_end[None, :] <= blk_start[:, None], axis=1), N_EXPERTS - 1).astype(jnp.int32)
    n_used = (pad_end[-1] // bm).astype(jnp.int32).reshape(1)
    xb = _dispatch(x1, dest, pad_start, pad_end, n_blocks * bm, tm, bm)
    yb = _experts(xb, blk_e, n_used, w_gu, b_gu, w_down, b_down, bm)
    return _combine_ln(x1, gates, dest, yb, ln_g, ln_b, COMBINE_TM, t_prompt)


def kernel(x_prompt, x_sample, cache_k, cache_v, state_conv, state_delta, w_in, conv_w, a_log, dt_bias,
           delta_norm_w, lambda_q1, lambda_k1, lambda_q2, lambda_k2, subln_w, w_out, ln1_g, ln1_b,
           router_w, router_b, w_gu, b_gu, w_down, b_down, ln2_g, ln2_b):
    bp, lp, _ = x_prompt.shape
    bs, ls, _ = x_sample.shape
    l = 0
    lam_init = 0.8 - 0.6 * math.exp(-0.3 * l)
    lam = (jnp.exp(jnp.sum(lambda_q1[l] * lambda_k1[l])) - jnp.exp(jnp.sum(lambda_q2[l] * lambda_k2[l]))
           + lam_init).reshape(1).astype(F32)
    w_in_bf = jnp.pad(w_in[l], ((0, 0), (0, IN_COLS_PAD - IN_COLS))).astype(BF16)
    shared = (lam, lam_init, w_in_bf, conv_w[l], a_log[l], dt_bias[l], delta_norm_w[l], subln_w[l])

    xp = x_prompt.reshape(bp * lp, D_MODEL)
    xs = x_sample.reshape(bs * ls, D_MODEL)
    zero_conv = jnp.zeros((bp, CONV_W - 1, CONV_CH), F32)
    zero_s = jnp.zeros((bp, N_HEADS, HEAD, HEAD), F32)
    att_p, o_p, k_p, v_p, cin_p, s_p = _mixers(xp, bp, lp, None, None, zero_conv, zero_s, *shared)
    att_s, o_s, k_s, v_s, cin_s, s_s = _mixers(xs, bs, ls, cache_k, cache_v, state_conv[l],
                                               state_delta[l], *shared)

    att = jnp.concatenate([att_p, att_s], axis=0)
    o = jnp.concatenate([o_p, o_s], axis=0)
    x_all = jnp.concatenate([xp, xs], axis=0)
    t = x_all.shape[0]
    tm = next(c for c in (512, 384, 256, 128) if t % c == 0)
    x1, idx, gates, rank, counts = _out_router(att, o, x_all, w_out[l].astype(BF16), ln1_g[l], ln1_b[l],
                                                    router_w[l], router_b[l], tm)
    tp = bp * lp
    y_p, y_s = _moe(x1, idx, gates, rank, counts, w_gu[l], b_gu[l], w_down[l], b_down[l], ln2_g[l], ln2_b[l],
                    MOE_BM, tm, tp)
    conv_tail = lambda cin, b, s: cin.reshape(b, s, CONV_CH)[:, s - (CONV_W - 1):][None]
    return (y_p.reshape(bp, lp, D_MODEL), y_s.reshape(bs, ls, D_MODEL),
            k_p.reshape(1, bp, lp, N_HEADS, HEAD), v_p.reshape(1, bp, lp, N_HEADS, HEAD),
            conv_tail(cin_p, bp, lp), s_p[None].astype(state_delta.dtype),
            k_s.reshape(1, bs, ls, N_HEADS, HEAD), v_s.reshape(1, bs, ls, N_HEADS, HEAD),
            conv_tail(cin_s, bs, ls), s_s[None].astype(state_delta.dtype))
```

```python
import functools
import math

import jax
import jax.numpy as jnp
from jax import lax
from jax.experimental import pallas as pl
from jax.experimental.pallas import tpu as pltpu

F32 = jnp.float32
BF16 = jnp.bfloat16

D_MODEL = 1024
HEAD = 128
N_HEADS = 4
DQK = HEAD // 2
GROUP = N_HEADS * HEAD
CONV_W = 4
CONV_CH = 3 * GROUP
CHUNK = 64
ALIBI_MAX = 8.0
N_EXPERTS = 32
TOP_K = 4
D_FF = D_MODEL
SWIGLU_LIMIT = 7.0
SWIGLU_ALPHA = 1.702
DEPTH = 1
DEEPNORM_ALPHA = (2 * DEPTH) ** 0.25
LN_EPS = 1e-5
SUBLN_EPS = 1e-5
GATED_NORM_EPS = 1e-6
L2_EPS = 1e-6

LANES = 128
SUBLANES = 8
BF16_EXACT_INT = 256
BF16_ROWS = 16
LOG2E = 1.4426950408889634
N_POS = 6
ONES_ROWS = BF16_ROWS
N_DMA_PRIORITIES = 2
VMEM_LIMIT = 56 * 1024 * 1024

COL_Q, COL_K, COL_V, COL_CONV = 0, GROUP, 2 * GROUP, 3 * GROUP
COL_Z = COL_CONV + CONV_CH
COL_AB = COL_Z + GROUP
IN_COLS = COL_AB + 2 * N_HEADS
IN_COLS_PAD = COL_AB + LANES


def _params(*sem):
    return pltpu.CompilerParams(dimension_semantics=sem, vmem_limit_bytes=VMEM_LIMIT)


def _in_proj_kernel(x_ref, w_ref, q_ref, kf_ref, vf_ref, kb_ref, vb_ref, c_ref, z_ref, ab_ref, *, tm, transposed):
    xb = x_ref[...].astype(BF16)

    def section(lo, hi):
        return jnp.dot(xb, w_ref[:, lo:hi], preferred_element_type=F32)

    q = section(COL_Q, COL_K) * (DQK ** -0.5 * (LOG2E if transposed else 1.0))
    k = section(COL_K, COL_V)
    kf_ref[...] = k
    kb_ref[...] = k.astype(BF16)
    v = section(COL_V, COL_CONV)
    vf_ref[...] = v
    if transposed:
        q_ref[0] = q.T.astype(BF16)
        vb_ref[0] = v.T.astype(BF16)
    else:
        q_ref[...] = q.astype(BF16)
        vb_ref[...] = v.astype(BF16)
    c_ref[...] = section(COL_CONV, COL_Z)
    z_ref[...] = section(COL_Z, COL_AB)
    ab_ref[...] = section(COL_AB, IN_COLS_PAD)


def _in_proj(x2d, w_bf, tm, transposed):
    t = x2d.shape[0]
    row = lambda i: (i, 0)
    widths = (GROUP, GROUP, GROUP, GROUP, GROUP, CONV_CH, GROUP, LANES)
    dtypes = (BF16, F32, F32, BF16, BF16, F32, F32, F32)
    out_specs = [pl.BlockSpec((tm, w), row) for w in widths]
    out_shape = [jax.ShapeDtypeStruct((t, w), d) for w, d in zip(widths, dtypes)]
    if transposed:
        for slot in (0, 4):
            out_specs[slot] = pl.BlockSpec((1, GROUP, tm), lambda i: (i, 0, 0))
            out_shape[slot] = jax.ShapeDtypeStruct((t // tm, GROUP, tm), BF16)
    return pl.pallas_call(
        functools.partial(_in_proj_kernel, tm=tm, transposed=transposed),
        grid=(t // tm,),
        in_specs=[pl.BlockSpec((tm, D_MODEL), row),
                  pl.BlockSpec((D_MODEL, IN_COLS_PAD), lambda i: (0, 0))],
        out_specs=out_specs,
        out_shape=out_shape,
        compiler_params=_params("parallel"),
        name="in_proj",
    )(x2d, w_bf)


def _alibi_slopes():
    return [2.0 ** (-ALIBI_MAX * (h + 1) / N_HEADS) for h in range(N_HEADS)]


def _stack_halves(q):
    lane = lax.broadcasted_iota(jnp.int32, q.shape, 1)
    zero = jnp.zeros_like(q)
    return jnp.concatenate([jnp.where(lane < DQK, q, zero), jnp.where(lane < DQK, zero, q)], axis=0)


def _diff_norm(acc, l, lam, w, lam_init, rows):
    o = acc[:rows] / l[:rows] - lam * (acc[rows:] / l[rows:])
    ms = jnp.mean(o * o, axis=-1, keepdims=True)
    return o * lax.rsqrt(ms + SUBLN_EPS) * w * (1.0 - lam_init)


def _head_slope(h):
    s = _alibi_slopes()
    return jnp.where(h == 0, s[0], jnp.where(h == 1, s[1], jnp.where(h == 2, s[2], s[3]))).astype(F32)


def _attn_prompt_kernel(lam_ref, qt_ref, k_ref, vt_ref, w_ref, o_ref,
                        diag_ref, kaug_ref, vaug_ref, qz_ref, s0_ref, s1_ref, p0_ref, p1_ref, m_ref, acc_ref, *, blk, lam_init):
    i = pl.program_id(2)
    slope = _head_slope(pl.program_id(1)) * LOG2E
    rows = 2 * blk
    n_kv = k_ref.shape[0] // blk

    def pieces(x):
        lo = x % BF16_EXACT_INT
        return _split3(slope * lo.astype(F32)) + _split3(slope * (x - lo).astype(F32))

    @pl.when(i == 0)
    def _():
        lane = lax.broadcasted_iota(jnp.int32, (blk, HEAD), 1)
        extra = jnp.where(lane < N_POS, 1.0, 0.0).astype(BF16)
        for n, piece in enumerate(pieces(lax.broadcasted_iota(jnp.int32, (blk, 1), 0))):
            extra = jnp.where(lane == N_POS + n, piece, extra)
        ones = jnp.ones((ONES_ROWS, blk), BF16)
        c = lax.broadcasted_iota(jnp.int32, (blk, rows), 0)
        a = lax.broadcasted_iota(jnp.int32, (blk, rows), 1) % blk
        diag_ref[...] = jnp.where(c // CHUNK <= a // CHUNK, -slope * jnp.abs(a - c).astype(F32), -jnp.inf)

        def fill(j, carry):
            j0 = pl.multiple_of(j * blk, blk)
            kaug_ref[pl.ds(j0, blk), :HEAD] = k_ref[pl.ds(j0, blk), :]
            kaug_ref[pl.ds(j0, blk), HEAD:] = extra
            vaug_ref[j, :HEAD, :] = vt_ref[j]
            vaug_ref[j, HEAD:, :] = ones
            return carry

        lax.fori_loop(0, n_kv, fill, 0)

    qt = qt_ref[0]
    d = lax.broadcasted_iota(jnp.int32, qt.shape, 0)
    zero = jnp.zeros_like(qt)
    qz_ref[:HEAD, :] = jnp.concatenate([jnp.where(d < DQK, qt, zero), jnp.where(d < DQK, zero, qt)], axis=1)
    r = lax.broadcasted_iota(jnp.int32, (HEAD, rows), 0)
    qx = jnp.where((r >= N_POS) & (r < 2 * N_POS), 1.0, 0.0).astype(BF16)
    for n, piece in enumerate(pieces(-(lax.broadcasted_iota(jnp.int32, (1, rows), 1) % blk))):
        qx = jnp.where(r == n, piece, qx)
    qz_ref[HEAD:, :] = qx

    def tile_rows(j):
        return pl.ds(pl.multiple_of(j * blk, blk), blk)

    s = jnp.dot(k_ref[tile_rows(i), :], qz_ref[:HEAD, :], preferred_element_type=F32) + diag_ref[...]
    m = jnp.max(s, axis=0, keepdims=True)
    p1_ref[...] = jnp.exp2(s - m).astype(BF16)
    m_ref[...] = m
    acc_ref[...] = jnp.zeros_like(acc_ref)
    s0_ref[...] = jnp.dot(kaug_ref[tile_rows(0), :], qz_ref[...], preferred_element_type=F32)

    def step(j, s_cur, s_nxt, p_cur, p_nxt):
        s_nxt[...] = jnp.dot(kaug_ref[tile_rows(jnp.minimum(j + 1, i - 1)), :], qz_ref[...],
                             preferred_element_type=F32)
        pv = jnp.dot(vaug_ref[jnp.where(j == 0, i, j - 1)], p_nxt[...], preferred_element_type=F32)
        shift = -slope * ((i - j) * blk).astype(F32)
        m_old = m_ref[...]
        m_new = jnp.maximum(m_old, jnp.max(s_cur[...], axis=0, keepdims=True) + shift)
        p_cur[...] = jnp.exp2(s_cur[...] - (m_new - shift)).astype(BF16)
        acc_ref[...] = (acc_ref[...] + pv) * jnp.exp2(m_old - m_new)
        m_ref[...] = m_new

    def body(jj, carry):
        step(2 * jj, s0_ref, s1_ref, p0_ref, p1_ref)

        @pl.when(2 * jj + 1 < i)
        def _():
            step(2 * jj + 1, s1_ref, s0_ref, p1_ref, p0_ref)

        return carry

    lax.fori_loop(0, (i + 1) // 2, body, 0)
    p_last = jnp.where(i % 2 == 1, p0_ref[...], p1_ref[...])
    acc = acc_ref[...] + jnp.dot(vaug_ref[jnp.where(i > 0, i - 1, i)], p_last, preferred_element_type=F32)
    l = acc[HEAD:HEAD + 1, :]
    num = acc[:HEAD, :]
    ot = num[:, :blk] / l[:, :blk] - lam_ref[0] * (num[:, blk:] / l[:, blk:])
    ms = jnp.mean(ot * ot, axis=0, keepdims=True)
    ot = ot * lax.rsqrt(ms + SUBLN_EPS) * w_ref[...] * (1.0 - lam_init)
    o_ref[...] = ot.T.astype(o_ref.dtype)


def _attn_prompt(qt_bf, k_bf, vt_bf, lam, subln_w, bsz, seq, blk, lam_init):
    nq = seq // blk
    kern = functools.partial(_attn_prompt_kernel, blk=blk, lam_init=lam_init)
    return pl.pallas_call(
        kern,
        grid=(bsz, N_HEADS, nq),
        in_specs=[pl.BlockSpec(memory_space=pltpu.SMEM),
                  pl.BlockSpec((1, HEAD, blk), lambda b, h, i: (b * nq + i, h, 0)),
                  pl.BlockSpec((seq, HEAD), lambda b, h, i: (b, h)),
                  pl.BlockSpec((nq, HEAD, blk), lambda b, h, i: (b, h, 0)),
                  pl.BlockSpec((HEAD, 1), lambda b, h, i: (0, 0))],
        out_specs=pl.BlockSpec((blk, HEAD), lambda b, h, i: (b * nq + i, h)),
        scratch_shapes=[pltpu.VMEM((blk, 2 * blk), F32),
                        pltpu.VMEM((seq, 2 * HEAD), BF16),
                        pltpu.VMEM((nq, HEAD + ONES_ROWS, blk), BF16),
                        pltpu.VMEM((2 * HEAD, 2 * blk), BF16),
                        pltpu.VMEM((blk, 2 * blk), F32), pltpu.VMEM((blk, 2 * blk), F32),
                        pltpu.VMEM((blk, 2 * blk), BF16), pltpu.VMEM((blk, 2 * blk), BF16),
                        pltpu.VMEM((1, 2 * blk), F32),
                        pltpu.VMEM((HEAD + ONES_ROWS, 2 * blk), F32)],
        out_shape=jax.ShapeDtypeStruct((bsz * seq, GROUP), BF16),
        compiler_params=_params("parallel", "parallel", "arbitrary"),
        name="attn_prompt",
    )(lam, qt_bf, k_bf, vt_bf, subln_w.reshape(HEAD, 1))


def _attn_sample_kernel(lam_ref, q_ref, kn_ref, vn_ref, kc_ref, vc_ref, w_ref, o_ref, *, seq, past, lam_init):
    nt = (((1,), (1,)), ((), ()))
    qpos = past + lax.broadcasted_iota(jnp.int32, (2 * seq, 1), 0) % seq
    rel_c = jnp.abs(qpos - lax.broadcasted_iota(jnp.int32, (1, past), 1)).astype(F32)
    rel_n = jnp.abs(qpos - (past + lax.broadcasted_iota(jnp.int32, (1, seq), 1))).astype(F32)
    for h, slope in enumerate(_alibi_slopes()):
        col = slice(h * HEAD, (h + 1) * HEAD)
        qz = _stack_halves(q_ref[:, col])
        s_c = lax.dot_general(qz, kc_ref[0, 0, :, h, :].astype(BF16), nt, preferred_element_type=F32)
        s_c = s_c - slope * rel_c
        s_n = lax.dot_general(qz, kn_ref[:, col], nt, preferred_element_type=F32) - slope * rel_n
        m = jnp.maximum(jnp.max(s_c, axis=-1, keepdims=True), jnp.max(s_n, axis=-1, keepdims=True))
        p_c = jnp.exp(s_c - m)
        p_n = jnp.exp(s_n - m)
        l = jnp.sum(p_c, axis=-1, keepdims=True) + jnp.sum(p_n, axis=-1, keepdims=True)
        acc = (jnp.dot(p_c.astype(BF16), vc_ref[0, 0, :, h, :].astype(BF16), preferred_element_type=F32)
               + jnp.dot(p_n.astype(BF16), vn_ref[:, col], preferred_element_type=F32))
        o_ref[:, col] = _diff_norm(acc, l, lam_ref[0], w_ref[...], lam_init, seq).astype(o_ref.dtype)


def _attn_sample(q_bf, k_bf, v_bf, cache_k, cache_v, layer, lam, subln_w, bsz, seq, lam_init):
    past = cache_k.shape[2]
    kern = functools.partial(_attn_sample_kernel, seq=seq, past=past, lam_init=lam_init)
    new = pl.BlockSpec((seq, GROUP), lambda b: (b, 0))
    cache = pl.BlockSpec((1, 1, past, N_HEADS, HEAD), lambda b: (layer, b, 0, 0, 0))
    return pl.pallas_call(
        kern,
        grid=(bsz,),
        in_specs=[pl.BlockSpec(memory_space=pltpu.SMEM), new, new, new, cache, cache,
                  pl.BlockSpec((1, HEAD), lambda b: (0, 0))],
        out_specs=new,
        out_shape=jax.ShapeDtypeStruct((bsz * seq, GROUP), BF16),
        compiler_params=_params("parallel"),
        name="attn_sample",
    )(lam, q_bf, k_bf, v_bf, cache_k, cache_v, subln_w)


def _split3(x):
    hi = x.astype(BF16)
    r1 = x - hi.astype(F32)
    mid = r1.astype(BF16)
    lo = (r1 - mid.astype(F32)).astype(BF16)
    return hi, mid, lo


def _gdn_prep_kernel(cin_ref, halo_ref, past_ref, ab_ref, cw_ref, alog_ref, dtb_ref,
                     u_ref, w_ref, qd_ref, kd_ref, qk_ref, gl_ref, *, chunk, n_sub):
    c_idx = pl.program_id(1)
    rows = chunk * n_sub
    prev = jnp.where(c_idx == 0, past_ref[0], halo_ref[...])
    xin = jnp.concatenate([prev, cin_ref[...]], axis=0)
    conv = sum(xin[SUBLANES - (CONV_W - 1) + j: SUBLANES - (CONV_W - 1) + j + rows] * cw_ref[j:j + 1, :]
               for j in range(CONV_W))
    conv = conv * jax.nn.sigmoid(conv)

    ab = ab_ref[...]
    lane = lax.broadcasted_iota(jnp.int32, ab.shape, 1)
    pre = ab + dtb_ref[...]
    softplus = jnp.maximum(pre, 0.0) + jnp.log(1.0 + jnp.exp(-jnp.abs(pre)))
    g = jnp.where(lane < N_HEADS, -jnp.exp(alog_ref[...]) * softplus, 0.0)
    beta_all = jax.nn.sigmoid(ab)

    ri = lax.broadcasted_iota(jnp.int32, (rows, rows), 0)
    ci = lax.broadcasted_iota(jnp.int32, (rows, rows), 1)
    same = (ri // chunk) == (ci // chunk)
    incl = same & (ri >= ci)
    strict = same & (ri > ci)
    eye = jnp.where(ri == ci, 1.0, 0.0).astype(F32)
    nt = (((1,), (1,)), ((), ()))
    g_parts = _split3(g)
    ones_incl = jnp.where(incl, 1.0, 0.0).astype(BF16)
    ones_same = jnp.where(same, 1.0, 0.0).astype(BF16)
    gc = sum(jnp.dot(ones_incl, part, preferred_element_type=F32) for part in g_parts)
    g_end = sum(jnp.dot(ones_same, part, preferred_element_type=F32) for part in g_parts)
    gct = gc.T
    for sc in range(n_sub):
        gl_ref[sc] = g_end[sc * chunk:sc * chunk + 1, :]

    t_mats, p_mats, rhs = [], [], []
    for h in range(N_HEADS):
        col = slice(h * HEAD, (h + 1) * HEAD)
        qh = conv[:, h * HEAD:(h + 1) * HEAD]
        kh = conv[:, GROUP + h * HEAD:GROUP + (h + 1) * HEAD]
        vh = conv[:, 2 * GROUP + h * HEAD:2 * GROUP + (h + 1) * HEAD]
        qh = qh * lax.rsqrt(jnp.sum(qh * qh, axis=-1, keepdims=True) + L2_EPS) * (HEAD ** -0.5)
        kh = kh * lax.rsqrt(jnp.sum(kh * kh, axis=-1, keepdims=True) + L2_EPS)
        beta = beta_all[:, N_HEADS + h:N_HEADS + h + 1]
        gcol = gc[:, h:h + 1]
        grow = gct[h:h + 1, :]
        gamma = jnp.exp(jnp.where(incl, gcol - grow, -jnp.inf))
        egc = jnp.exp(gcol)
        kb = kh * beta
        khb = kh.astype(BF16)
        a = jnp.where(strict, lax.dot_general(kb.astype(BF16), khb, nt, preferred_element_type=F32) * gamma, 0.0)
        qk = (lax.dot_general(qh.astype(BF16), khb, nt, preferred_element_type=F32) * gamma).astype(BF16)
        for sc in range(n_sub):
            blk = slice(sc * chunk, (sc + 1) * chunk)
            qk_ref[h, blk, :] = qk[blk, blk]
        qd_ref[:, col] = (qh * egc).astype(BF16)
        kd_ref[:, col] = (kh * jnp.exp(g_end[:, h:h + 1] - gcol)).astype(BF16)
        t_mats.append(eye - a)
        p_mats.append(a)
        rhs.append(((vh * beta).astype(BF16), (kb * egc).astype(BF16)))

    for _ in range(int(math.log2(chunk)) - 1):
        for h in range(N_HEADS):
            pb = p_mats[h].astype(BF16)
            p_mats[h] = jnp.dot(pb, pb, preferred_element_type=F32)
        for h in range(N_HEADS):
            t_mats[h] = t_mats[h] + jnp.dot(t_mats[h].astype(BF16), p_mats[h].astype(BF16),
                                            preferred_element_type=F32)

    for h in range(N_HEADS):
        col = slice(h * HEAD, (h + 1) * HEAD)
        t_inv = t_mats[h].astype(BF16)
        u_ref[:, col] = jnp.dot(t_inv, rhs[h][0], preferred_element_type=F32)
        w_ref[:, col] = jnp.dot(t_inv, rhs[h][1], preferred_element_type=F32).astype(BF16)


def _gdn_prep(cin, ab, past8, conv_w, a_log, dt_bias, bsz, seq, chunk, n_sub):
    rows = chunk * n_sub
    nblk = seq // rows
    t = bsz * seq
    lanes = lambda v: jnp.pad(v.reshape(1, N_HEADS).astype(F32), ((0, 0), (0, LANES - N_HEADS)))
    kern = functools.partial(_gdn_prep_kernel, chunk=chunk, n_sub=n_sub)
    rowblk = lambda b, c: (b * nblk + c, 0)
    halo = lambda b, c: (jnp.maximum((b * nblk + c) * (rows // SUBLANES) - 1, 0), 0)
    const = lambda b, c: (0, 0)
    return pl.pallas_call(
        kern,
        grid=(bsz, nblk),
        in_specs=[pl.BlockSpec((rows, CONV_CH), rowblk),
                  pl.BlockSpec((SUBLANES, CONV_CH), halo),
                  pl.BlockSpec((1, SUBLANES, CONV_CH), lambda b, c: (b, 0, 0)),
                  pl.BlockSpec((rows, LANES), rowblk),
                  pl.BlockSpec((CONV_W, CONV_CH), const),
                  pl.BlockSpec((1, LANES), const),
                  pl.BlockSpec((1, LANES), const)],
        out_specs=[pl.BlockSpec((rows, GROUP), rowblk),
                   pl.BlockSpec((rows, GROUP), rowblk),
                   pl.BlockSpec((rows, GROUP), rowblk),
                   pl.BlockSpec((rows, GROUP), rowblk),
                   pl.BlockSpec((N_HEADS, rows, chunk), lambda b, c: (0, b * nblk + c, 0)),
                   pl.BlockSpec((n_sub, 1, LANES), lambda b, c: (b * nblk + c, 0, 0))],
        out_shape=[jax.ShapeDtypeStruct((t, GROUP), F32),
                   jax.ShapeDtypeStruct((t, GROUP), BF16),
                   jax.ShapeDtypeStruct((t, GROUP), BF16),
                   jax.ShapeDtypeStruct((t, GROUP), BF16),
                   jax.ShapeDtypeStruct((N_HEADS, t, chunk), BF16),
                   jax.ShapeDtypeStruct((t // chunk, 1, LANES), F32)],
        compiler_params=_params("parallel", "parallel"),
        name="gdn_prep",
    )(cin, cin, past8, ab, conv_w, lanes(a_log), lanes(dt_bias))


def _gdn_scan_kernel(u_ref, w_ref, qd_ref, kd_ref, qk_ref, gl_ref, z_ref, s0_ref, nw_ref,
                     o_ref, sf_ref, s_ref, *, bsz, chunk):
    c_idx = pl.program_id(0)

    @pl.when(c_idx == 0)
    def _():
        s_ref[...] = s0_ref[...]

    tn = (((0,), (0,)), ((), ()))
    for b in range(bsz):
        glast = jnp.exp(gl_ref[b, 0])
        for h in range(N_HEADS):
            col = slice(h * HEAD, (h + 1) * HEAD)
            s = s_ref[b, h]
            wq = jnp.concatenate([w_ref[b, :, col], qd_ref[b, :, col]], axis=0)
            r = jnp.dot(wq, s.astype(BF16), preferred_element_type=F32)
            v_new = (u_ref[b, :, col] - r[:chunk]).astype(BF16)
            o = r[chunk:] + jnp.dot(qk_ref[h, b], v_new, preferred_element_type=F32)
            s_ref[b, h] = s * glast[:, h:h + 1] + lax.dot_general(kd_ref[b, :, col], v_new, tn,
                                                                 preferred_element_type=F32)
            zh = z_ref[b, :, col]
            ms = jnp.mean(o * o, axis=-1, keepdims=True)
            o = o * lax.rsqrt(ms + GATED_NORM_EPS) * nw_ref[...] * (zh * jax.nn.sigmoid(zh))
            o_ref[b, :, col] = o.astype(o_ref.dtype)

    @pl.when(c_idx == pl.num_programs(0) - 1)
    def _():
        sf_ref[...] = s_ref[...]


def _gdn_scan(u, w, qd, kd, qk, gl, z, s0, norm_w, bsz, seq, chunk):
    nc = seq // chunk
    kern = functools.partial(_gdn_scan_kernel, bsz=bsz, chunk=chunk)
    tok = pl.BlockSpec((bsz, chunk, GROUP), lambda c: (0, c, 0))
    state = pl.BlockSpec((bsz, N_HEADS, HEAD, HEAD), lambda c: (0, 0, 0, 0))
    o, s_final = pl.pallas_call(
        kern,
        grid=(nc,),
        in_specs=[tok, tok, tok, tok,
                  pl.BlockSpec((N_HEADS, bsz, chunk, chunk), lambda c: (0, 0, c, 0)),
                  pl.BlockSpec((bsz, 1, 1, LANES), lambda c: (0, c, 0, 0)),
                  tok, state,
                  pl.BlockSpec((1, HEAD), lambda c: (0, 0))],
        out_specs=[tok, state],
        out_shape=[jax.ShapeDtypeStruct((bsz, seq, GROUP), BF16),
                   jax.ShapeDtypeStruct((bsz, N_HEADS, HEAD, HEAD), F32)],
        scratch_shapes=[pltpu.VMEM((bsz, N_HEADS, HEAD, HEAD), F32)],
        compiler_params=_params("arbitrary"),
        name="gdn_scan",
    )(u.reshape(bsz, seq, GROUP), w.reshape(bsz, seq, GROUP), qd.reshape(bsz, seq, GROUP),
      kd.reshape(bsz, seq, GROUP), qk.reshape(N_HEADS, bsz, seq, chunk), gl.reshape(bsz, nc, 1, LANES),
      z.reshape(bsz, seq, GROUP), s0, norm_w)
    return o.reshape(bsz * seq, GROUP), s_final


def _layernorm(x, g, b):
    mu = jnp.mean(x, axis=-1, keepdims=True)
    xc = x - mu
    var = jnp.mean(xc * xc, axis=-1, keepdims=True)
    return xc * lax.rsqrt(var + LN_EPS) * g + b


def _out_router_kernel(attp_ref, atts_ref, op_ref, os_ref, xp_ref, xs_ref, wo_ref, g_ref, b_ref, rw_ref, rb_ref,
                       x1_ref, idx_ref, gate_ref, rank_ref, cnt_ref, carry_ref, *, tm, n_prompt):
    step = pl.program_id(0)

    @pl.when(step == 0)
    def _():
        carry_ref[...] = jnp.zeros_like(carry_ref)

    prompt = step < n_prompt
    att = jnp.where(prompt, attp_ref[...], atts_ref[...])
    o = jnp.where(prompt, op_ref[...], os_ref[...])
    x = jnp.where(prompt, xp_ref[...], xs_ref[...])
    mix = (jnp.dot(att, wo_ref[:GROUP, :], preferred_element_type=F32)
           + jnp.dot(o, wo_ref[GROUP:, :], preferred_element_type=F32))
    x1 = _layernorm(DEEPNORM_ALPHA * x + mix, g_ref[...], b_ref[...])
    x1_ref[...] = x1

    logits = jnp.dot(x1, rw_ref[...], preferred_element_type=F32, precision=lax.Precision.HIGHEST) + rb_ref[...]
    lane = lax.broadcasted_iota(jnp.int32, logits.shape, 1)
    work = jnp.where(lane < N_EXPERTS, logits, -jnp.inf)
    vals, idxs = [], []
    for _ in range(TOP_K):
        m = jnp.max(work, axis=-1, keepdims=True)
        am = jnp.min(jnp.where(work == m, lane, LANES), axis=-1, keepdims=True)
        vals.append(m)
        idxs.append(am)
        work = jnp.where(lane == am, -jnp.inf, work)
    exps = [jnp.exp(v - vals[0]) for v in vals]
    denom = exps[0] + exps[1] + exps[2] + exps[3]
    chosen = jnp.zeros(logits.shape, F32)
    gate_out = jnp.zeros(logits.shape, F32)
    idx_out = jnp.zeros(logits.shape, jnp.int32)
    for k in range(TOP_K):
        chosen = jnp.where(lane == idxs[k], 1.0, chosen)
        gate_out = jnp.where(lane == k, exps[k] / denom, gate_out)
        idx_out = jnp.where(lane == k, idxs[k], idx_out)
    ri = lax.broadcasted_iota(jnp.int32, (tm, tm), 0)
    ci = lax.broadcasted_iota(jnp.int32, (tm, tm), 1)
    before = jnp.where(ri > ci, 1.0, 0.0).astype(BF16)
    prefix = jnp.dot(before, chosen.astype(BF16), preferred_element_type=F32) + carry_ref[...]
    rank_out = jnp.zeros(logits.shape, F32)
    for k in range(TOP_K):
        r = jnp.sum(jnp.where(lane == idxs[k], prefix, 0.0), axis=-1, keepdims=True)
        rank_out = jnp.where(lane == k, r, rank_out)
    carry_ref[...] = carry_ref[...] + jnp.sum(chosen, axis=0, keepdims=True)
    idx_ref[0] = idx_out.T[:SUBLANES, :]
    gate_ref[...] = gate_out
    rank_ref[0] = rank_out.astype(jnp.int32).T[:SUBLANES, :]
    cnt_ref[...] = carry_ref[...].astype(jnp.int32)


def _out_router(streams, w_out_bf, ln_g, ln_b, router_w, router_b, tm):
    (att_p, o_p, x_p), (att_s, o_s, x_s) = streams
    n_prompt = x_p.shape[0] // tm
    t = x_p.shape[0] + x_s.shape[0]
    row = lambda i: (i, 0)
    const = lambda i: (0, 0)
    prow = lambda i: (jnp.minimum(i, n_prompt - 1), 0)
    srow = lambda i: (jnp.maximum(i - n_prompt, 0), 0)
    slots = pl.BlockSpec((1, SUBLANES, tm), lambda i: (i, 0, 0))
    rw = jnp.pad(router_w, ((0, 0), (0, LANES - N_EXPERTS)))
    rb = jnp.pad(router_b.reshape(1, N_EXPERTS), ((0, 0), (0, LANES - N_EXPERTS)))
    kern = functools.partial(_out_router_kernel, tm=tm, n_prompt=n_prompt)
    return pl.pallas_call(
        kern,
        grid=(t // tm,),
        in_specs=[pl.BlockSpec((tm, GROUP), prow), pl.BlockSpec((tm, GROUP), srow),
                  pl.BlockSpec((tm, GROUP), prow), pl.BlockSpec((tm, GROUP), srow),
                  pl.BlockSpec((tm, D_MODEL), prow), pl.BlockSpec((tm, D_MODEL), srow),
                  pl.BlockSpec((2 * GROUP, D_MODEL), const),
                  pl.BlockSpec((1, D_MODEL), const), pl.BlockSpec((1, D_MODEL), const),
                  pl.BlockSpec((D_MODEL, LANES), const), pl.BlockSpec((1, LANES), const)],
        out_specs=[pl.BlockSpec((tm, D_MODEL), row), slots, pl.BlockSpec((tm, LANES), row), slots,
                   pl.BlockSpec((1, LANES), const)],
        out_shape=[jax.ShapeDtypeStruct((t, D_MODEL), F32),
                   jax.ShapeDtypeStruct((t // tm, SUBLANES, tm), jnp.int32), jax.ShapeDtypeStruct((t, LANES), F32),
                   jax.ShapeDtypeStruct((t // tm, SUBLANES, tm), jnp.int32),
                   jax.ShapeDtypeStruct((1, LANES), jnp.int32)],
        scratch_shapes=[pltpu.VMEM((1, LANES), F32)],
        compiler_params=_params("arbitrary"),
        name="out_router",
    )(att_p, att_s, o_p, o_s, x_p, x_s, w_out_bf, ln_g.reshape(1, D_MODEL), ln_b.reshape(1, D_MODEL), rw, rb)


def _row_copy(src, src_row, dst, dst_row, sem):
    return pltpu.make_async_copy(src.at[pl.ds(src_row, 1), :], dst.at[pl.ds(dst_row, 1), :], sem)


def _dispatch_kernel(ps_ref, pe_ref, dest_ref, x_ref, xb_hbm, zero_ref, sem, zsem, *, tm, bm):
    i = pl.program_id(0)

    def zero_block(row):
        return pltpu.make_async_copy(zero_ref, xb_hbm.at[pl.ds(pl.multiple_of(row, bm), bm), :], zsem)

    def fill(e):
        return zero_block(pe_ref[e] - bm)

    @pl.when(i == 0)
    def _():
        zero_ref[...] = jnp.zeros_like(zero_ref)
        for e in range(N_EXPERTS):
            @pl.when(pe_ref[e] > ps_ref[e])
            def _():
                fill(e).start()
        first_unused = pe_ref[N_EXPERTS - 1] // bm
        n_blocks = xb_hbm.shape[0] // bm
        lax.fori_loop(first_unused, n_blocks, lambda b, c: (zero_block(b * bm).start(), c)[1], 0)
        for e in range(N_EXPERTS):
            @pl.when(pe_ref[e] > ps_ref[e])
            def _():
                fill(e).wait()
        lax.fori_loop(first_unused, n_blocks, lambda b, c: (zero_block(b * bm).wait(), c)[1], 0)

    def issue(r, carry):
        for k in range(TOP_K):
            _row_copy(x_ref, r, xb_hbm, dest_ref[0, 0, r * TOP_K + k], sem).start(priority=k % N_DMA_PRIORITIES)
        return carry

    lax.fori_loop(0, tm, issue, 0)
    pltpu.make_async_copy(xb_hbm.at[pl.ds(0, tm * TOP_K), :], xb_hbm.at[pl.ds(0, tm * TOP_K), :], sem).wait()


def _dispatch(x1, dest, pad_start, pad_end, rows, tm, bm):
    t = x1.shape[0]
    return pl.pallas_call(
        functools.partial(_dispatch_kernel, tm=tm, bm=bm),
        grid_spec=pltpu.PrefetchScalarGridSpec(
            num_scalar_prefetch=2,
            grid=(t // tm,),
            in_specs=[pl.BlockSpec((1, 1, tm * TOP_K), lambda i, ps, pe: (i, 0, 0), memory_space=pltpu.SMEM),
                      pl.BlockSpec((tm, D_MODEL), lambda i, ps, pe: (i, 0))],
            out_specs=pl.BlockSpec(memory_space=pl.ANY),
            scratch_shapes=[pltpu.VMEM((bm, D_MODEL), F32),
                            pltpu.SemaphoreType.DMA(()), pltpu.SemaphoreType.DMA(())]),
        out_shape=jax.ShapeDtypeStruct((rows, D_MODEL), F32),
        compiler_params=_params("arbitrary"),
        name="dispatch",
    )(pad_start, pad_end, dest.reshape(t // tm, 1, tm * TOP_K), x1)


def _expert_kernel(be_ref, nb_ref, x_ref, wgu_ref, bgu_ref, wd_ref, bd_ref, y_ref):
    blk = pl.program_id(0)

    @pl.when(blk < nb_ref[0])
    def _():
        x = x_ref[...].astype(BF16)
        h = jnp.dot(x, wgu_ref[0].astype(BF16), preferred_element_type=F32) + bgu_ref[0]
        gate = jnp.minimum(h[:, :D_FF], SWIGLU_LIMIT)
        up = jnp.clip(h[:, D_FF:], -SWIGLU_LIMIT, SWIGLU_LIMIT)
        act = (up + 1.0) * (gate * jax.nn.sigmoid(SWIGLU_ALPHA * gate))
        y_ref[...] = jnp.dot(act.astype(BF16), wd_ref[0].astype(BF16), preferred_element_type=F32) + bd_ref[0]

    @pl.when(blk >= nb_ref[0])
    def _():
        y_ref[...] = jnp.zeros_like(y_ref)


def _experts(xb, blk_e, n_used, w_gu, b_gu, w_down, b_down, bm):
    rows = xb.shape[0]
    n_blocks = rows // bm
    used = lambda i, be, nb: (jnp.maximum(jnp.minimum(i, nb[0] - 1), 0), 0)
    return pl.pallas_call(
        _expert_kernel,
        grid_spec=pltpu.PrefetchScalarGridSpec(
            num_scalar_prefetch=2,
            grid=(n_blocks,),
            in_specs=[pl.BlockSpec((bm, D_MODEL), used),
                      pl.BlockSpec((1, D_MODEL, 2 * D_FF), lambda i, be, nb: (be[i], 0, 0)),
                      pl.BlockSpec((1, 1, 2 * D_FF), lambda i, be, nb: (be[i], 0, 0)),
                      pl.BlockSpec((1, D_FF, D_MODEL), lambda i, be, nb: (be[i], 0, 0)),
                      pl.BlockSpec((1, 1, D_MODEL), lambda i, be, nb: (be[i], 0, 0))],
            out_specs=pl.BlockSpec((bm, D_MODEL), lambda i, be, nb: (i, 0))),
        out_shape=jax.ShapeDtypeStruct((rows, D_MODEL), F32),
        compiler_params=_params("arbitrary"),
        name="experts",
    )(blk_e, n_used, xb, w_gu, b_gu.reshape(N_EXPERTS, 1, 2 * D_FF), w_down, b_down.reshape(N_EXPERTS, 1, D_MODEL))


def _combine_ln_kernel(dcur_ref, dnext_ref, x_ref, gate_ref, g_ref, b_ref, yb_hbm, op_ref, os_ref, ybuf, sem, *,
                       tm, n_prompt):
    i = pl.program_id(0)
    slot = i % 2

    def issue(dref, s):
        def body(r, carry):
            for k in range(TOP_K):
                _row_copy(yb_hbm, dref[0, 0, r * TOP_K + k], ybuf.at[s, k], r,
                          sem.at[s]).start(priority=k % N_DMA_PRIORITIES)
            return carry
        lax.fori_loop(0, tm, body, 0)

    @pl.when(i == 0)
    def _():
        issue(dcur_ref, 0)

    for s in range(2):
        @pl.when((i + 1 < pl.num_programs(0)) & (slot != s))
        def _():
            issue(dnext_ref, s)

    pltpu.make_async_copy(ybuf.at[slot], ybuf.at[slot], sem.at[slot]).wait()
    gates = gate_ref[...]
    y = sum(gates[:, k:k + 1] * ybuf[slot, k] for k in range(TOP_K))
    out = _layernorm(DEEPNORM_ALPHA * x_ref[...] + y, g_ref[...], b_ref[...])

    @pl.when(i < n_prompt)
    def _():
        op_ref[...] = out

    @pl.when(i >= n_prompt)
    def _():
        os_ref[...] = out


def _combine_ln(x1, gates, dest, yb, ln_g, ln_b, tm, t_prompt):
    t = x1.shape[0]
    n = t // tm
    n_prompt = t_prompt // tm
    row = lambda i: (i, 0)
    const = lambda i: (0, 0)
    d2 = dest.reshape(n, 1, tm * TOP_K)
    return pl.pallas_call(
        functools.partial(_combine_ln_kernel, tm=tm, n_prompt=n_prompt),
        grid=(n,),
        in_specs=[pl.BlockSpec((1, 1, tm * TOP_K), lambda i: (i, 0, 0), memory_space=pltpu.SMEM),
                  pl.BlockSpec((1, 1, tm * TOP_K), lambda i: (jnp.minimum(i + 1, n - 1), 0, 0),
                               memory_space=pltpu.SMEM),
                  pl.BlockSpec((tm, D_MODEL), row),
                  pl.BlockSpec((tm, LANES), row),
                  pl.BlockSpec((1, D_MODEL), const), pl.BlockSpec((1, D_MODEL), const),
                  pl.BlockSpec(memory_space=pl.ANY)],
        out_specs=[pl.BlockSpec((tm, D_MODEL), lambda i: (jnp.minimum(i, n_prompt - 1), 0)),
                   pl.BlockSpec((tm, D_MODEL), lambda i: (jnp.maximum(i - n_prompt, 0), 0))],
        out_shape=[jax.ShapeDtypeStruct((t_prompt, D_MODEL), F32),
                   jax.ShapeDtypeStruct((t - t_prompt, D_MODEL), F32)],
        scratch_shapes=[pltpu.VMEM((2, TOP_K, tm, D_MODEL), F32), pltpu.SemaphoreType.DMA((2,))],
        compiler_params=_params("arbitrary"),
        name="combine_ln",
    )(d2, d2, x1, gates, ln_g.reshape(1, D_MODEL), ln_b.reshape(1, D_MODEL), yb)


MOE_BM = 256
ATTN_BLK = 512
GDN_SUB = 4
ROUTE_TM = 128


def _mixers(x2d, bsz, seq, cache_k, cache_v, conv_past, s0, lam, lam_init, w_in_bf, conv_w, a_log, dt_bias,
            delta_norm_w, subln_w):
    prompt = cache_k is None
    tm = ATTN_BLK if prompt else x2d.shape[0]
    q_bf, k_f, v_f, k_bf, v_bf, cin, z, ab = _in_proj(x2d, w_in_bf, tm, prompt)
    if prompt:
        att = _attn_prompt(q_bf, k_bf, v_bf, lam, subln_w, bsz, seq, ATTN_BLK, lam_init)
    else:
        att = _attn_sample(q_bf, k_bf, v_bf, cache_k, cache_v, 0, lam, subln_w.reshape(1, HEAD), bsz, seq, lam_init)
    chunk = CHUNK if seq % CHUNK == 0 else seq
    n_sub = GDN_SUB if (seq // chunk) % GDN_SUB == 0 else 1
    past8 = jnp.pad(conv_past, ((0, 0), (SUBLANES - (CONV_W - 1), 0), (0, 0)))
    u, w, qd, kd, qk, gl = _gdn_prep(cin, ab, past8, conv_w, a_log, dt_bias, bsz, seq, chunk, n_sub)
    o, s_new = _gdn_scan(u, w, qd, kd, qk, gl, z, s0, delta_norm_w.reshape(1, HEAD), bsz, seq, chunk)
    return att, o, k_f, v_f, cin, s_new


def _moe(x1, idx, gates, rank, counts, w_gu, b_gu, w_down, b_down, ln_g, ln_b, bm, tm, t_prompt):
    t = x1.shape[0]
    n = t * TOP_K
    counts = counts[0, :N_EXPERTS]
    padded = (counts + bm - 1) // bm * bm
    pad_end = jnp.cumsum(padded).astype(jnp.int32)
    pad_start = (pad_end - padded).astype(jnp.int32)
    dest = pad_start[idx[:, :TOP_K, :]] + rank[:, :TOP_K, :]
    dest = dest.transpose(0, 2, 1).reshape(t, TOP_K)
    n_blocks = -(-n // bm) + N_EXPERTS
    blk_start = jnp.arange(n_blocks, dtype=jnp.int32) * bm
    blk_e = jnp.minimum(jnp.sum(pad_end[None, :] <= blk_start[:, None], axis=1), N_EXPERTS - 1).astype(jnp.int32)
    n_used = (pad_end[-1] // bm).astype(jnp.int32).reshape(1)
    xb = _dispatch(x1, dest, pad_start, pad_end, n_blocks * bm, tm, bm)
    yb = _experts(xb, blk_e, n_used, w_gu, b_gu, w_down, b_down, bm)
    return _combine_ln(x1, gates, dest, yb, ln_g, ln_b, ROUTE_TM, t_prompt)


def kernel(x_prompt, x_sample, cache_k, cache_v, state_conv, state_delta, w_in, conv_w, a_log, dt_bias,
           delta_norm_w, lambda_q1, lambda_k1, lambda_q2, lambda_k2, subln_w, w_out, ln1_g, ln1_b,
           router_w, router_b, w_gu, b_gu, w_down, b_down, ln2_g, ln2_b):
    bp, lp, _ = x_prompt.shape
    bs, ls, _ = x_sample.shape
    l = 0
    lam_init = 0.8 - 0.6 * math.exp(-0.3 * l)
    lam = (jnp.exp(jnp.sum(lambda_q1[l] * lambda_k1[l])) - jnp.exp(jnp.sum(lambda_q2[l] * lambda_k2[l]))
           + lam_init).reshape(1).astype(F32)
    w_in_bf = jnp.pad(w_in[l], ((0, 0), (0, IN_COLS_PAD - IN_COLS))).astype(BF16)
    shared = (lam, lam_init, w_in_bf, conv_w[l], a_log[l], dt_bias[l], delta_norm_w[l], subln_w[l])

    xp = x_prompt.reshape(bp * lp, D_MODEL)
    xs = x_sample.reshape(bs * ls, D_MODEL)
    zero_conv = jnp.zeros((bp, CONV_W - 1, CONV_CH), F32)
    zero_s = jnp.zeros((bp, N_HEADS, HEAD, HEAD), F32)
    att_p, o_p, k_p, v_p, cin_p, s_p = _mixers(xp, bp, lp, None, None, zero_conv, zero_s, *shared)
    att_s, o_s, k_s, v_s, cin_s, s_s = _mixers(xs, bs, ls, cache_k, cache_v, state_conv[l],
                                               state_delta[l], *shared)

    x1, idx, gates, rank, counts = _out_router(((att_p, o_p, xp), (att_s, o_s, xs)), w_out[l].astype(BF16),
                                               ln1_g[l], ln1_b[l], router_w[l], router_b[l], ROUTE_TM)
    tp = bp * lp
    tm = next(c for c in (4, 3, 2, 1) if (x1.shape[0] // ROUTE_TM) % c == 0) * ROUTE_TM
    y_p, y_s = _moe(x1, idx, gates, rank, counts, w_gu[l], b_gu[l], w_down[l], b_down[l], ln2_g[l], ln2_b[l],
                    MOE_BM, tm, tp)
    conv_tail = lambda cin, b, s: cin.reshape(b, s, CONV_CH)[:, s - (CONV_W - 1):][None]
    return (y_p.reshape(bp, lp, D_MODEL), y_s.reshape(bs, ls, D_MODEL),
            k_p.reshape(1, bp, lp, N_HEADS, HEAD), v_p.reshape(1, bp, lp, N_HEADS, HEAD),
            conv_tail(cin_p, bp, lp), s_p[None].astype(state_delta.dtype),
            k_s.reshape(1, bs, ls, N_HEADS, HEAD), v_s.reshape(1, bs, ls, N_HEADS, HEAD),
            conv_tail(cin_s, bs, ls), s_s[None].astype(state_delta.dtype))
```

```python
import functools
import math

import jax
import jax.numpy as jnp
from jax import lax
from jax.experimental import pallas as pl
from jax.experimental.pallas import tpu as pltpu

F32 = jnp.float32
BF16 = jnp.bfloat16

D_MODEL = 1024
HEAD = 128
N_HEADS = 4
DQK = HEAD // 2
GROUP = N_HEADS * HEAD
CONV_W = 4
CONV_CH = 3 * GROUP
CHUNK = 64
ALIBI_MAX = 8.0
N_EXPERTS = 32
TOP_K = 4
D_FF = D_MODEL
SWIGLU_LIMIT = 7.0
SWIGLU_ALPHA = 1.702
DEPTH = 1
DEEPNORM_ALPHA = (2 * DEPTH) ** 0.25
LN_EPS = 1e-5
SUBLN_EPS = 1e-5
GATED_NORM_EPS = 1e-6
L2_EPS = 1e-6

LANES = 128
SUBLANES = 8
BF16_EXACT_INT = 256
BF16_ROWS = 16
LOG2E = 1.4426950408889634
N_POS = 6
ONES_ROWS = BF16_ROWS
N_DMA_PRIORITIES = 2
VMEM_LIMIT = 56 * 1024 * 1024

COL_Q, COL_K, COL_V, COL_CONV = 0, GROUP, 2 * GROUP, 3 * GROUP
COL_Z = COL_CONV + CONV_CH
COL_AB = COL_Z + GROUP
IN_COLS = COL_AB + 2 * N_HEADS
IN_COLS_PAD = COL_AB + LANES


def _params(*sem):
    return pltpu.CompilerParams(dimension_semantics=sem, vmem_limit_bytes=VMEM_LIMIT)


def _in_proj_kernel(x_ref, w_ref, q_ref, kf_ref, vf_ref, kb_ref, vb_ref, c_ref, z_ref, ab_ref, *, tm, transposed):
    xb = x_ref[...].astype(BF16)

    def section(lo, hi):
        return jnp.dot(xb, w_ref[:, lo:hi], preferred_element_type=F32)

    q = section(COL_Q, COL_K) * (DQK ** -0.5 * (LOG2E if transposed else 1.0))
    k = section(COL_K, COL_V)
    kf_ref[...] = k
    kb_ref[...] = k.astype(BF16)
    v = section(COL_V, COL_CONV)
    vf_ref[...] = v
    if transposed:
        q_ref[0] = q.T.astype(BF16)
        vb_ref[0] = v.T.astype(BF16)
    else:
        q_ref[...] = q.astype(BF16)
        vb_ref[...] = v.astype(BF16)
    c_ref[...] = section(COL_CONV, COL_Z)
    z_ref[...] = section(COL_Z, COL_AB)
    ab_ref[...] = section(COL_AB, IN_COLS_PAD)


def _in_proj(x2d, w_bf, tm, transposed):
    t = x2d.shape[0]
    row = lambda i: (i, 0)
    widths = (GROUP, GROUP, GROUP, GROUP, GROUP, CONV_CH, GROUP, LANES)
    dtypes = (BF16, F32, F32, BF16, BF16, F32, F32, F32)
    out_specs = [pl.BlockSpec((tm, w), row) for w in widths]
    out_shape = [jax.ShapeDtypeStruct((t, w), d) for w, d in zip(widths, dtypes)]
    if transposed:
        for slot in (0, 4):
            out_specs[slot] = pl.BlockSpec((1, GROUP, tm), lambda i: (i, 0, 0))
            out_shape[slot] = jax.ShapeDtypeStruct((t // tm, GROUP, tm), BF16)
    return pl.pallas_call(
        functools.partial(_in_proj_kernel, tm=tm, transposed=transposed),
        grid=(t // tm,),
        in_specs=[pl.BlockSpec((tm, D_MODEL), row),
                  pl.BlockSpec((D_MODEL, IN_COLS_PAD), lambda i: (0, 0))],
        out_specs=out_specs,
        out_shape=out_shape,
        compiler_params=_params("parallel"),
        name="in_proj",
    )(x2d, w_bf)


def _alibi_slopes():
    return [2.0 ** (-ALIBI_MAX * (h + 1) / N_HEADS) for h in range(N_HEADS)]


def _stack_halves(q):
    lane = lax.broadcasted_iota(jnp.int32, q.shape, 1)
    zero = jnp.zeros_like(q)
    return jnp.concatenate([jnp.where(lane < DQK, q, zero), jnp.where(lane < DQK, zero, q)], axis=0)


def _diff_norm(acc, l, lam, w, lam_init, rows):
    o = acc[:rows] / l[:rows] - lam * (acc[rows:] / l[rows:])
    ms = jnp.mean(o * o, axis=-1, keepdims=True)
    return o * lax.rsqrt(ms + SUBLN_EPS) * w * (1.0 - lam_init)


def _head_slope(h):
    s = _alibi_slopes()
    return jnp.where(h == 0, s[0], jnp.where(h == 1, s[1], jnp.where(h == 2, s[2], s[3]))).astype(F32)


def _attn_prompt_kernel(lam_ref, qt_ref, k_ref, vt_ref, w_ref, o_ref,
                        diag_ref, kaug_ref, vaug_ref, qz_ref, s0_ref, s1_ref, p0_ref, p1_ref, m_ref, acc_ref, *, blk, lam_init):
    i = pl.program_id(2)
    slope = _head_slope(pl.program_id(1)) * LOG2E
    rows = 2 * blk
    n_kv = k_ref.shape[0] // blk

    def pieces(x):
        lo = x % BF16_EXACT_INT
        return _split3(slope * lo.astype(F32)) + _split3(slope * (x - lo).astype(F32))

    @pl.when(i == 0)
    def _():
        lane = lax.broadcasted_iota(jnp.int32, (blk, HEAD), 1)
        extra = jnp.where(lane < N_POS, 1.0, 0.0).astype(BF16)
        for n, piece in enumerate(pieces(lax.broadcasted_iota(jnp.int32, (blk, 1), 0))):
            extra = jnp.where(lane == N_POS + n, piece, extra)
        ones = jnp.ones((ONES_ROWS, blk), BF16)
        c = lax.broadcasted_iota(jnp.int32, (blk, rows), 0)
        a = lax.broadcasted_iota(jnp.int32, (blk, rows), 1) % blk
        diag_ref[...] = jnp.where(c // CHUNK <= a // CHUNK, -slope * jnp.abs(a - c).astype(F32), -jnp.inf)

        def fill(j, carry):
            j0 = pl.multiple_of(j * blk, blk)
            kaug_ref[pl.ds(j0, blk), :HEAD] = k_ref[pl.ds(j0, blk), :]
            kaug_ref[pl.ds(j0, blk), HEAD:] = extra
            vaug_ref[j, :HEAD, :] = vt_ref[j]
            vaug_ref[j, HEAD:, :] = ones
            return carry

        lax.fori_loop(0, n_kv, fill, 0)

    qt = qt_ref[0]
    d = lax.broadcasted_iota(jnp.int32, qt.shape, 0)
    zero = jnp.zeros_like(qt)
    qz_ref[:HEAD, :] = jnp.concatenate([jnp.where(d < DQK, qt, zero), jnp.where(d < DQK, zero, qt)], axis=1)
    r = lax.broadcasted_iota(jnp.int32, (HEAD, rows), 0)
    qx = jnp.where((r >= N_POS) & (r < 2 * N_POS), 1.0, 0.0).astype(BF16)
    for n, piece in enumerate(pieces(-(lax.broadcasted_iota(jnp.int32, (1, rows), 1) % blk))):
        qx = jnp.where(r == n, piece, qx)
    qz_ref[HEAD:, :] = qx

    def tile_rows(j):
        return pl.ds(pl.multiple_of(j * blk, blk), blk)

    s = jnp.dot(k_ref[tile_rows(i), :], qz_ref[:HEAD, :], preferred_element_type=F32) + diag_ref[...]
    m = jnp.max(s, axis=0, keepdims=True)
    p1_ref[...] = jnp.exp2(s - m).astype(BF16)
    m_ref[...] = m
    acc_ref[...] = jnp.zeros_like(acc_ref)
    s0_ref[...] = jnp.dot(kaug_ref[tile_rows(0), :], qz_ref[...], preferred_element_type=F32)

    def step(j, s_cur, s_nxt, p_cur, p_nxt):
        s_nxt[...] = jnp.dot(kaug_ref[tile_rows(jnp.minimum(j + 1, i - 1)), :], qz_ref[...],
                             preferred_element_type=F32)
        pv = jnp.dot(vaug_ref[jnp.where(j == 0, i, j - 1)], p_nxt[...], preferred_element_type=F32)
        shift = -slope * ((i - j) * blk).astype(F32)
        m_old = m_ref[...]
        m_new = jnp.maximum(m_old, jnp.max(s_cur[...], axis=0, keepdims=True) + shift)
        p_cur[...] = jnp.exp2(s_cur[...] - (m_new - shift)).astype(BF16)
        acc_ref[...] = (acc_ref[...] + pv) * jnp.exp2(m_old - m_new)
        m_ref[...] = m_new

    def body(jj, carry):
        step(2 * jj, s0_ref, s1_ref, p0_ref, p1_ref)

        @pl.when(2 * jj + 1 < i)
        def _():
            step(2 * jj + 1, s1_ref, s0_ref, p1_ref, p0_ref)

        return carry

    lax.fori_loop(0, (i + 1) // 2, body, 0)
    p_last = jnp.where(i % 2 == 1, p0_ref[...], p1_ref[...])
    acc = acc_ref[...] + jnp.dot(vaug_ref[jnp.where(i > 0, i - 1, i)], p_last, preferred_element_type=F32)
    l = acc[HEAD:HEAD + 1, :]
    num = acc[:HEAD, :]
    ot = num[:, :blk] / l[:, :blk] - lam_ref[0] * (num[:, blk:] / l[:, blk:])
    ms = jnp.mean(ot * ot, axis=0, keepdims=True)
    ot = ot * lax.rsqrt(ms + SUBLN_EPS) * w_ref[...] * (1.0 - lam_init)
    o_ref[...] = ot.T.astype(o_ref.dtype)


def _attn_prompt(qt_bf, k_bf, vt_bf, lam, subln_w, bsz, seq, blk, lam_init):
    nq = seq // blk
    kern = functools.partial(_attn_prompt_kernel, blk=blk, lam_init=lam_init)
    return pl.pallas_call(
        kern,
        grid=(bsz, N_HEADS, nq),
        in_specs=[pl.BlockSpec(memory_space=pltpu.SMEM),
                  pl.BlockSpec((1, HEAD, blk), lambda b, h, i: (b * nq + i, h, 0)),
                  pl.BlockSpec((seq, HEAD), lambda b, h, i: (b, h)),
                  pl.BlockSpec((nq, HEAD, blk), lambda b, h, i: (b, h, 0)),
                  pl.BlockSpec((HEAD, 1), lambda b, h, i: (0, 0))],
        out_specs=pl.BlockSpec((blk, HEAD), lambda b, h, i: (b * nq + i, h)),
        scratch_shapes=[pltpu.VMEM((blk, 2 * blk), F32),
                        pltpu.VMEM((seq, 2 * HEAD), BF16),
                        pltpu.VMEM((nq, HEAD + ONES_ROWS, blk), BF16),
                        pltpu.VMEM((2 * HEAD, 2 * blk), BF16),
                        pltpu.VMEM((blk, 2 * blk), F32), pltpu.VMEM((blk, 2 * blk), F32),
                        pltpu.VMEM((blk, 2 * blk), BF16), pltpu.VMEM((blk, 2 * blk), BF16),
                        pltpu.VMEM((1, 2 * blk), F32),
                        pltpu.VMEM((HEAD + ONES_ROWS, 2 * blk), F32)],
        out_shape=jax.ShapeDtypeStruct((bsz * seq, GROUP), BF16),
        compiler_params=_params("parallel", "parallel", "arbitrary"),
        name="attn_prompt",
    )(lam, qt_bf, k_bf, vt_bf, subln_w.reshape(HEAD, 1))


def _attn_sample_kernel(lam_ref, q_ref, kn_ref, vn_ref, kc_ref, vc_ref, w_ref, o_ref, *, seq, past, lam_init):
    nt = (((1,), (1,)), ((), ()))
    qpos = past + lax.broadcasted_iota(jnp.int32, (2 * seq, 1), 0) % seq
    rel_c = jnp.abs(qpos - lax.broadcasted_iota(jnp.int32, (1, past), 1)).astype(F32)
    rel_n = jnp.abs(qpos - (past + lax.broadcasted_iota(jnp.int32, (1, seq), 1))).astype(F32)
    for h, slope in enumerate(_alibi_slopes()):
        col = slice(h * HEAD, (h + 1) * HEAD)
        qz = _stack_halves(q_ref[:, col])
        s_c = lax.dot_general(qz, kc_ref[0, 0, :, h, :].astype(BF16), nt, preferred_element_type=F32)
        s_c = s_c - slope * rel_c
        s_n = lax.dot_general(qz, kn_ref[:, col], nt, preferred_element_type=F32) - slope * rel_n
        m = jnp.maximum(jnp.max(s_c, axis=-1, keepdims=True), jnp.max(s_n, axis=-1, keepdims=True))
        p_c = jnp.exp(s_c - m)
        p_n = jnp.exp(s_n - m)
        l = jnp.sum(p_c, axis=-1, keepdims=True) + jnp.sum(p_n, axis=-1, keepdims=True)
        acc = (jnp.dot(p_c.astype(BF16), vc_ref[0, 0, :, h, :].astype(BF16), preferred_element_type=F32)
               + jnp.dot(p_n.astype(BF16), vn_ref[:, col], preferred_element_type=F32))
        o_ref[:, col] = _diff_norm(acc, l, lam_ref[0], w_ref[...], lam_init, seq).astype(o_ref.dtype)


def _attn_sample(q_bf, k_bf, v_bf, cache_k, cache_v, layer, lam, subln_w, bsz, seq, lam_init):
    past = cache_k.shape[2]
    kern = functools.partial(_attn_sample_kernel, seq=seq, past=past, lam_init=lam_init)
    new = pl.BlockSpec((seq, GROUP), lambda b: (b, 0))
    cache = pl.BlockSpec((1, 1, past, N_HEADS, HEAD), lambda b: (layer, b, 0, 0, 0))
    return pl.pallas_call(
        kern,
        grid=(bsz,),
        in_specs=[pl.BlockSpec(memory_space=pltpu.SMEM), new, new, new, cache, cache,
                  pl.BlockSpec((1, HEAD), lambda b: (0, 0))],
        out_specs=new,
        out_shape=jax.ShapeDtypeStruct((bsz * seq, GROUP), BF16),
        compiler_params=_params("parallel"),
        name="attn_sample",
    )(lam, q_bf, k_bf, v_bf, cache_k, cache_v, subln_w)


def _split3(x):
    hi = x.astype(BF16)
    r1 = x - hi.astype(F32)
    mid = r1.astype(BF16)
    lo = (r1 - mid.astype(F32)).astype(BF16)
    return hi, mid, lo


def _gdn_prep_kernel(cin_ref, halo_ref, past_ref, ab_ref, cw_ref, alog_ref, dtb_ref,
                     u_ref, w_ref, qd_ref, kd_ref, qk_ref, gl_ref, *, chunk, n_sub):
    c_idx = pl.program_id(1)
    rows = chunk * n_sub
    prev = jnp.where(c_idx == 0, past_ref[0], halo_ref[...])
    xin = jnp.concatenate([prev, cin_ref[...]], axis=0)
    conv = sum(xin[SUBLANES - (CONV_W - 1) + j: SUBLANES - (CONV_W - 1) + j + rows] * cw_ref[j:j + 1, :]
               for j in range(CONV_W))
    conv = conv * jax.nn.sigmoid(conv)

    ab = ab_ref[...]
    lane = lax.broadcasted_iota(jnp.int32, ab.shape, 1)
    pre = ab + dtb_ref[...]
    softplus = jnp.maximum(pre, 0.0) + jnp.log(1.0 + jnp.exp(-jnp.abs(pre)))
    g = jnp.where(lane < N_HEADS, -jnp.exp(alog_ref[...]) * softplus, 0.0)
    beta_all = jax.nn.sigmoid(ab)

    ri = lax.broadcasted_iota(jnp.int32, (rows, rows), 0)
    ci = lax.broadcasted_iota(jnp.int32, (rows, rows), 1)
    same = (ri // chunk) == (ci // chunk)
    incl = same & (ri >= ci)
    strict = same & (ri > ci)
    eye = jnp.where(ri == ci, 1.0, 0.0).astype(F32)
    nt = (((1,), (1,)), ((), ()))
    g_parts = _split3(g)
    ones_incl = jnp.where(incl, 1.0, 0.0).astype(BF16)
    ones_same = jnp.where(same, 1.0, 0.0).astype(BF16)
    gc = sum(jnp.dot(ones_incl, part, preferred_element_type=F32) for part in g_parts)
    g_end = sum(jnp.dot(ones_same, part, preferred_element_type=F32) for part in g_parts)
    gct = gc.T
    for sc in range(n_sub):
        gl_ref[sc] = g_end[sc * chunk:sc * chunk + 1, :]

    t_mats, p_mats, rhs = [], [], []
    for h in range(N_HEADS):
        col = slice(h * HEAD, (h + 1) * HEAD)
        qh = conv[:, h * HEAD:(h + 1) * HEAD]
        kh = conv[:, GROUP + h * HEAD:GROUP + (h + 1) * HEAD]
        vh = conv[:, 2 * GROUP + h * HEAD:2 * GROUP + (h + 1) * HEAD]
        qh = qh * lax.rsqrt(jnp.sum(qh * qh, axis=-1, keepdims=True) + L2_EPS) * (HEAD ** -0.5)
        kh = kh * lax.rsqrt(jnp.sum(kh * kh, axis=-1, keepdims=True) + L2_EPS)
        beta = beta_all[:, N_HEADS + h:N_HEADS + h + 1]
        gcol = gc[:, h:h + 1]
        grow = gct[h:h + 1, :]
        gamma = jnp.exp(jnp.where(incl, gcol - grow, -jnp.inf))
        egc = jnp.exp(gcol)
        kb = kh * beta
        khb = kh.astype(BF16)
        a = jnp.where(strict, lax.dot_general(kb.astype(BF16), khb, nt, preferred_element_type=F32) * gamma, 0.0)
        qk = (lax.dot_general(qh.astype(BF16), khb, nt, preferred_element_type=F32) * gamma).astype(BF16)
        for sc in range(n_sub):
            blk = slice(sc * chunk, (sc + 1) * chunk)
            qk_ref[h, blk, :] = qk[blk, blk]
        qd_ref[:, col] = (qh * egc).astype(BF16)
        kd_ref[:, col] = (kh * jnp.exp(g_end[:, h:h + 1] - gcol)).astype(BF16)
        t_mats.append(eye - a)
        p_mats.append(a)
        rhs.append(((vh * beta).astype(BF16), (kb * egc).astype(BF16)))

    for _ in range(int(math.log2(chunk)) - 1):
        for h in range(N_HEADS):
            pb = p_mats[h].astype(BF16)
            p_mats[h] = jnp.dot(pb, pb, preferred_element_type=F32)
        for h in range(N_HEADS):
            t_mats[h] = t_mats[h] + jnp.dot(t_mats[h].astype(BF16), p_mats[h].astype(BF16),
                                            preferred_element_type=F32)

    for h in range(N_HEADS):
        col = slice(h * HEAD, (h + 1) * HEAD)
        t_inv = t_mats[h].astype(BF16)
        u_ref[:, col] = jnp.dot(t_inv, rhs[h][0], preferred_element_type=F32)
        w_ref[:, col] = jnp.dot(t_inv, rhs[h][1], preferred_element_type=F32).astype(BF16)


def _gdn_prep(cin, ab, past8, conv_w, a_log, dt_bias, bsz, seq, chunk, n_sub):
    rows = chunk * n_sub
    nblk = seq // rows
    t = bsz * seq
    lanes = lambda v: jnp.pad(v.reshape(1, N_HEADS).astype(F32), ((0, 0), (0, LANES - N_HEADS)))
    kern = functools.partial(_gdn_prep_kernel, chunk=chunk, n_sub=n_sub)
    rowblk = lambda b, c: (b * nblk + c, 0)
    halo = lambda b, c: (jnp.maximum((b * nblk + c) * (rows // SUBLANES) - 1, 0), 0)
    const = lambda b, c: (0, 0)
    return pl.pallas_call(
        kern,
        grid=(bsz, nblk),
        in_specs=[pl.BlockSpec((rows, CONV_CH), rowblk),
                  pl.BlockSpec((SUBLANES, CONV_CH), halo),
                  pl.BlockSpec((1, SUBLANES, CONV_CH), lambda b, c: (b, 0, 0)),
                  pl.BlockSpec((rows, LANES), rowblk),
                  pl.BlockSpec((CONV_W, CONV_CH), const),
                  pl.BlockSpec((1, LANES), const),
                  pl.BlockSpec((1, LANES), const)],
        out_specs=[pl.BlockSpec((rows, GROUP), rowblk),
                   pl.BlockSpec((rows, GROUP), rowblk),
                   pl.BlockSpec((rows, GROUP), rowblk),
                   pl.BlockSpec((rows, GROUP), rowblk),
                   pl.BlockSpec((N_HEADS, rows, chunk), lambda b, c: (0, b * nblk + c, 0)),
                   pl.BlockSpec((n_sub, 1, LANES), lambda b, c: (b * nblk + c, 0, 0))],
        out_shape=[jax.ShapeDtypeStruct((t, GROUP), F32),
                   jax.ShapeDtypeStruct((t, GROUP), BF16),
                   jax.ShapeDtypeStruct((t, GROUP), BF16),
                   jax.ShapeDtypeStruct((t, GROUP), BF16),
                   jax.ShapeDtypeStruct((N_HEADS, t, chunk), BF16),
                   jax.ShapeDtypeStruct((t // chunk, 1, LANES), F32)],
        compiler_params=_params("parallel", "parallel"),
        name="gdn_prep",
    )(cin, cin, past8, ab, conv_w, lanes(a_log), lanes(dt_bias))


def _gdn_scan_kernel(u_ref, w_ref, qd_ref, kd_ref, qk_ref, gl_ref, z_ref, s0_ref, nw_ref,
                     o_ref, sf_ref, s_ref, *, bsz, chunk):
    c_idx = pl.program_id(0)

    @pl.when(c_idx == 0)
    def _():
        s_ref[...] = s0_ref[...]

    tn = (((0,), (0,)), ((), ()))
    chains = [(b, h, slice(h * HEAD, (h + 1) * HEAD)) for b in range(bsz) for h in range(N_HEADS)]
    states = [s_ref[b, h] for b, h, _ in chains]
    prods = [jnp.dot(jnp.concatenate([w_ref[b, :, col], qd_ref[b, :, col]], axis=0), s.astype(BF16),
                     preferred_element_type=F32) for (b, _, col), s in zip(chains, states)]
    v_news = [(u_ref[b, :, col] - r[:chunk]).astype(BF16) for (b, _, col), r in zip(chains, prods)]
    outs = [r[chunk:] + jnp.dot(qk_ref[h, b], v, preferred_element_type=F32)
            for (b, h, _), r, v in zip(chains, prods, v_news)]
    for (b, h, col), s, v in zip(chains, states, v_news):
        glast = jnp.exp(gl_ref[b, 0])
        s_ref[b, h] = s * glast[:, h:h + 1] + lax.dot_general(kd_ref[b, :, col], v, tn,
                                                             preferred_element_type=F32)
    for (b, h, col), o in zip(chains, outs):
        zh = z_ref[b, :, col]
        ms = jnp.mean(o * o, axis=-1, keepdims=True)
        o = o * lax.rsqrt(ms + GATED_NORM_EPS) * nw_ref[...] * (zh * jax.nn.sigmoid(zh))
        o_ref[b, :, col] = o.astype(o_ref.dtype)

    @pl.when(c_idx == pl.num_programs(0) - 1)
    def _():
        sf_ref[...] = s_ref[...]


def _gdn_scan(u, w, qd, kd, qk, gl, z, s0, norm_w, bsz, seq, chunk):
    nc = seq // chunk
    kern = functools.partial(_gdn_scan_kernel, bsz=bsz, chunk=chunk)
    tok = pl.BlockSpec((bsz, chunk, GROUP), lambda c: (0, c, 0))
    state = pl.BlockSpec((bsz, N_HEADS, HEAD, HEAD), lambda c: (0, 0, 0, 0))
    o, s_final = pl.pallas_call(
        kern,
        grid=(nc,),
        in_specs=[tok, tok, tok, tok,
                  pl.BlockSpec((N_HEADS, bsz, chunk, chunk), lambda c: (0, 0, c, 0)),
                  pl.BlockSpec((bsz, 1, 1, LANES), lambda c: (0, c, 0, 0)),
                  tok, state,
                  pl.BlockSpec((1, HEAD), lambda c: (0, 0))],
        out_specs=[tok, state],
        out_shape=[jax.ShapeDtypeStruct((bsz, seq, GROUP), BF16),
                   jax.ShapeDtypeStruct((bsz, N_HEADS, HEAD, HEAD), F32)],
        scratch_shapes=[pltpu.VMEM((bsz, N_HEADS, HEAD, HEAD), F32)],
        compiler_params=_params("arbitrary"),
        name="gdn_scan",
    )(u.reshape(bsz, seq, GROUP), w.reshape(bsz, seq, GROUP), qd.reshape(bsz, seq, GROUP),
      kd.reshape(bsz, seq, GROUP), qk.reshape(N_HEADS, bsz, seq, chunk), gl.reshape(bsz, nc, 1, LANES),
      z.reshape(bsz, seq, GROUP), s0, norm_w)
    return o.reshape(bsz * seq, GROUP), s_final


def _layernorm(x, g, b):
    mu = jnp.mean(x, axis=-1, keepdims=True)
    xc = x - mu
    var = jnp.mean(xc * xc, axis=-1, keepdims=True)
    return xc * lax.rsqrt(var + LN_EPS) * g + b


def _out_router_kernel(attp_ref, atts_ref, op_ref, os_ref, xp_ref, xs_ref, wo_ref, g_ref, b_ref, rw_ref, rb_ref,
                       x1_ref, idx_ref, gate_ref, rank_ref, cnt_ref, carry_ref, *, tm, n_prompt):
    step = pl.program_id(0)

    @pl.when(step == 0)
    def _():
        carry_ref[...] = jnp.zeros_like(carry_ref)

    prompt = step < n_prompt
    att = jnp.where(prompt, attp_ref[...], atts_ref[...])
    o = jnp.where(prompt, op_ref[...], os_ref[...])
    x = jnp.where(prompt, xp_ref[...], xs_ref[...])
    mix = (jnp.dot(att, wo_ref[:GROUP, :], preferred_element_type=F32)
           + jnp.dot(o, wo_ref[GROUP:, :], preferred_element_type=F32))
    x1 = _layernorm(DEEPNORM_ALPHA * x + mix, g_ref[...], b_ref[...])
    x1_ref[...] = x1

    x_hi = x1.astype(BF16)
    x_lo = (x1 - x_hi.astype(F32)).astype(BF16)
    logits = (jnp.dot(x_hi, rw_ref[0], preferred_element_type=F32)
              + jnp.dot(x_lo, rw_ref[0], preferred_element_type=F32)
              + jnp.dot(x_hi, rw_ref[1], preferred_element_type=F32)) + rb_ref[...]
    lane = lax.broadcasted_iota(jnp.int32, logits.shape, 1)
    work = jnp.where(lane < N_EXPERTS, logits, -jnp.inf)
    vals, idxs = [], []
    for _ in range(TOP_K):
        m = jnp.max(work, axis=-1, keepdims=True)
        am = jnp.min(jnp.where(work == m, lane, LANES), axis=-1, keepdims=True)
        vals.append(m)
        idxs.append(am)
        work = jnp.where(lane == am, -jnp.inf, work)
    exps = [jnp.exp(v - vals[0]) for v in vals]
    denom = exps[0] + exps[1] + exps[2] + exps[3]
    chosen = jnp.zeros(logits.shape, F32)
    gate_out = jnp.zeros(logits.shape, F32)
    idx_out = jnp.zeros(logits.shape, jnp.int32)
    for k in range(TOP_K):
        chosen = jnp.where(lane == idxs[k], 1.0, chosen)
        gate_out = jnp.where(lane == k, exps[k] / denom, gate_out)
        idx_out = jnp.where(lane == k, idxs[k], idx_out)
    ri = lax.broadcasted_iota(jnp.int32, (tm, tm), 0)
    ci = lax.broadcasted_iota(jnp.int32, (tm, tm), 1)
    before = jnp.where(ri > ci, 1.0, 0.0).astype(BF16)
    prefix = jnp.dot(before, chosen.astype(BF16), preferred_element_type=F32) + carry_ref[...]
    rank_out = jnp.zeros(logits.shape, F32)
    for k in range(TOP_K):
        r = jnp.sum(jnp.where(lane == idxs[k], prefix, 0.0), axis=-1, keepdims=True)
        rank_out = jnp.where(lane == k, r, rank_out)
    carry_ref[...] = carry_ref[...] + jnp.sum(chosen, axis=0, keepdims=True)
    idx_ref[0] = idx_out.T[:SUBLANES, :]
    gate_ref[...] = gate_out
    rank_ref[0] = rank_out.astype(jnp.int32).T[:SUBLANES, :]
    cnt_ref[...] = carry_ref[...].astype(jnp.int32)


def _out_router(streams, w_out_bf, ln_g, ln_b, router_w, router_b, tm):
    (att_p, o_p, x_p), (att_s, o_s, x_s) = streams
    n_prompt = x_p.shape[0] // tm
    t = x_p.shape[0] + x_s.shape[0]
    row = lambda i: (i, 0)
    const = lambda i: (0, 0)
    prow = lambda i: (jnp.minimum(i, n_prompt - 1), 0)
    srow = lambda i: (jnp.maximum(i - n_prompt, 0), 0)
    slots = pl.BlockSpec((1, SUBLANES, tm), lambda i: (i, 0, 0))
    rw = jnp.pad(router_w, ((0, 0), (0, LANES - N_EXPERTS)))
    rw_hi = rw.astype(BF16)
    rw = jnp.stack([rw_hi, (rw - rw_hi.astype(F32)).astype(BF16)])
    rb = jnp.pad(router_b.reshape(1, N_EXPERTS), ((0, 0), (0, LANES - N_EXPERTS)))
    kern = functools.partial(_out_router_kernel, tm=tm, n_prompt=n_prompt)
    return pl.pallas_call(
        kern,
        grid=(t // tm,),
        in_specs=[pl.BlockSpec((tm, GROUP), prow), pl.BlockSpec((tm, GROUP), srow),
                  pl.BlockSpec((tm, GROUP), prow), pl.BlockSpec((tm, GROUP), srow),
                  pl.BlockSpec((tm, D_MODEL), prow), pl.BlockSpec((tm, D_MODEL), srow),
                  pl.BlockSpec((2 * GROUP, D_MODEL), const),
                  pl.BlockSpec((1, D_MODEL), const), pl.BlockSpec((1, D_MODEL), const),
                  pl.BlockSpec((2, D_MODEL, LANES), lambda i: (0, 0, 0)), pl.BlockSpec((1, LANES), const)],
        out_specs=[pl.BlockSpec((tm, D_MODEL), row), slots, pl.BlockSpec((tm, LANES), row), slots,
                   pl.BlockSpec((1, LANES), const)],
        out_shape=[jax.ShapeDtypeStruct((t, D_MODEL), F32),
                   jax.ShapeDtypeStruct((t // tm, SUBLANES, tm), jnp.int32), jax.ShapeDtypeStruct((t, LANES), F32),
                   jax.ShapeDtypeStruct((t // tm, SUBLANES, tm), jnp.int32),
                   jax.ShapeDtypeStruct((1, LANES), jnp.int32)],
        scratch_shapes=[pltpu.VMEM((1, LANES), F32)],
        compiler_params=_params("arbitrary"),
        name="out_router",
    )(att_p, att_s, o_p, o_s, x_p, x_s, w_out_bf, ln_g.reshape(1, D_MODEL), ln_b.reshape(1, D_MODEL), rw, rb)


def _row_copy(src, src_row, dst, dst_row, sem):
    return pltpu.make_async_copy(src.at[pl.ds(src_row, 1), :], dst.at[pl.ds(dst_row, 1), :], sem)


def _dispatch_kernel(ps_ref, pe_ref, dest_ref, x_ref, xb_hbm, zero_ref, sem, zsem, *, tm, bm):
    i = pl.program_id(0)

    def zero_block(row):
        return pltpu.make_async_copy(zero_ref, xb_hbm.at[pl.ds(pl.multiple_of(row, bm), bm), :], zsem)

    def fill(e):
        return zero_block(pe_ref[e] - bm)

    @pl.when(i == 0)
    def _():
        zero_ref[...] = jnp.zeros_like(zero_ref)
        for e in range(N_EXPERTS):
            @pl.when(pe_ref[e] > ps_ref[e])
            def _():
                fill(e).start()
        first_unused = pe_ref[N_EXPERTS - 1] // bm
        n_blocks = xb_hbm.shape[0] // bm
        lax.fori_loop(first_unused, n_blocks, lambda b, c: (zero_block(b * bm).start(), c)[1], 0)
        for e in range(N_EXPERTS):
            @pl.when(pe_ref[e] > ps_ref[e])
            def _():
                fill(e).wait()
        lax.fori_loop(first_unused, n_blocks, lambda b, c: (zero_block(b * bm).wait(), c)[1], 0)

    def issue(r, carry):
        for k in range(TOP_K):
            _row_copy(x_ref, r, xb_hbm, dest_ref[0, 0, r * TOP_K + k], sem).start(priority=k % N_DMA_PRIORITIES)
        return carry

    lax.fori_loop(0, tm, issue, 0)
    pltpu.make_async_copy(xb_hbm.at[pl.ds(0, tm * TOP_K), :], xb_hbm.at[pl.ds(0, tm * TOP_K), :], sem).wait()


def _dispatch(x1, dest, pad_start, pad_end, rows, tm, bm):
    t = x1.shape[0]
    return pl.pallas_call(
        functools.partial(_dispatch_kernel, tm=tm, bm=bm),
        grid_spec=pltpu.PrefetchScalarGridSpec(
            num_scalar_prefetch=2,
            grid=(t // tm,),
            in_specs=[pl.BlockSpec((1, 1, tm * TOP_K), lambda i, ps, pe: (i, 0, 0), memory_space=pltpu.SMEM),
                      pl.BlockSpec((tm, D_MODEL), lambda i, ps, pe: (i, 0))],
            out_specs=pl.BlockSpec(memory_space=pl.ANY),
            scratch_shapes=[pltpu.VMEM((bm, D_MODEL), F32),
                            pltpu.SemaphoreType.DMA(()), pltpu.SemaphoreType.DMA(())]),
        out_shape=jax.ShapeDtypeStruct((rows, D_MODEL), F32),
        compiler_params=_params("arbitrary"),
        name="dispatch",
    )(pad_start, pad_end, dest.reshape(t // tm, 1, tm * TOP_K), x1)


def _expert_kernel(be_ref, nb_ref, x_ref, wgu_ref, bgu_ref, wd_ref, bd_ref, y_ref):
    blk = pl.program_id(0)

    @pl.when(blk < nb_ref[0])
    def _():
        x = x_ref[...].astype(BF16)
        h = jnp.dot(x, wgu_ref[0].astype(BF16), preferred_element_type=F32) + bgu_ref[0]
        gate = jnp.minimum(h[:, :D_FF], SWIGLU_LIMIT)
        up = jnp.clip(h[:, D_FF:], -SWIGLU_LIMIT, SWIGLU_LIMIT)
        act = (up + 1.0) * (gate * jax.nn.sigmoid(SWIGLU_ALPHA * gate))
        y_ref[...] = jnp.dot(act.astype(BF16), wd_ref[0].astype(BF16), preferred_element_type=F32) + bd_ref[0]

    @pl.when(blk >= nb_ref[0])
    def _():
        y_ref[...] = jnp.zeros_like(y_ref)


def _experts(xb, blk_e, n_used, w_gu, b_gu, w_down, b_down, bm):
    rows = xb.shape[0]
    n_blocks = rows // bm
    used = lambda i, be, nb: (jnp.maximum(jnp.minimum(i, nb[0] - 1), 0), 0)
    return pl.pallas_call(
        _expert_kernel,
        grid_spec=pltpu.PrefetchScalarGridSpec(
            num_scalar_prefetch=2,
            grid=(n_blocks,),
            in_specs=[pl.BlockSpec((bm, D_MODEL), used),
                      pl.BlockSpec((1, D_MODEL, 2 * D_FF), lambda i, be, nb: (be[i], 0, 0)),
                      pl.BlockSpec((1, 1, 2 * D_FF), lambda i, be, nb: (be[i], 0, 0)),
                      pl.BlockSpec((1, D_FF, D_MODEL), lambda i, be, nb: (be[i], 0, 0)),
                      pl.BlockSpec((1, 1, D_MODEL), lambda i, be, nb: (be[i], 0, 0))],
            out_specs=pl.BlockSpec((bm, D_MODEL), lambda i, be, nb: (i, 0))),
        out_shape=jax.ShapeDtypeStruct((rows, D_MODEL), F32),
        compiler_params=_params("arbitrary"),
        name="experts",
    )(blk_e, n_used, xb, w_gu, b_gu.reshape(N_EXPERTS, 1, 2 * D_FF), w_down, b_down.reshape(N_EXPERTS, 1, D_MODEL))


def _combine_ln_kernel(dcur_ref, dnext_ref, x_ref, gate_ref, g_ref, b_ref, yb_hbm, op_ref, os_ref, ybuf, sem, *,
                       tm, n_prompt):
    i = pl.program_id(0)
    slot = i % 2

    def issue(dref, s):
        def body(r, carry):
            for k in range(TOP_K):
                _row_copy(yb_hbm, dref[0, 0, r * TOP_K + k], ybuf.at[s, k], r,
                          sem.at[s]).start(priority=k % N_DMA_PRIORITIES)
            return carry
        lax.fori_loop(0, tm, body, 0)

    @pl.when(i == 0)
    def _():
        issue(dcur_ref, 0)

    for s in range(2):
        @pl.when((i + 1 < pl.num_programs(0)) & (slot != s))
        def _():
            issue(dnext_ref, s)

    pltpu.make_async_copy(ybuf.at[slot], ybuf.at[slot], sem.at[slot]).wait()
    gates = gate_ref[...]
    y = sum(gates[:, k:k + 1] * ybuf[slot, k] for k in range(TOP_K))
    out = _layernorm(DEEPNORM_ALPHA * x_ref[...] + y, g_ref[...], b_ref[...])

    @pl.when(i < n_prompt)
    def _():
        op_ref[...] = out

    @pl.when(i >= n_prompt)
    def _():
        os_ref[...] = out


def _combine_ln(x1, gates, dest, yb, ln_g, ln_b, tm, t_prompt):
    t = x1.shape[0]
    n = t // tm
    n_prompt = t_prompt // tm
    row = lambda i: (i, 0)
    const = lambda i: (0, 0)
    d2 = dest.reshape(n, 1, tm * TOP_K)
    return pl.pallas_call(
        functools.partial(_combine_ln_kernel, tm=tm, n_prompt=n_prompt),
        grid=(n,),
        in_specs=[pl.BlockSpec((1, 1, tm * TOP_K), lambda i: (i, 0, 0), memory_space=pltpu.SMEM),
                  pl.BlockSpec((1, 1, tm * TOP_K), lambda i: (jnp.minimum(i + 1, n - 1), 0, 0),
                               memory_space=pltpu.SMEM),
                  pl.BlockSpec((tm, D_MODEL), row),
                  pl.BlockSpec((tm, LANES), row),
                  pl.BlockSpec((1, D_MODEL), const), pl.BlockSpec((1, D_MODEL), const),
                  pl.BlockSpec(memory_space=pl.ANY)],
        out_specs=[pl.BlockSpec((tm, D_MODEL), lambda i: (jnp.minimum(i, n_prompt - 1), 0)),
                   pl.BlockSpec((tm, D_MODEL), lambda i: (jnp.maximum(i - n_prompt, 0), 0))],
        out_shape=[jax.ShapeDtypeStruct((t_prompt, D_MODEL), F32),
                   jax.ShapeDtypeStruct((t - t_prompt, D_MODEL), F32)],
        scratch_shapes=[pltpu.VMEM((2, TOP_K, tm, D_MODEL), F32), pltpu.SemaphoreType.DMA((2,))],
        compiler_params=_params("arbitrary"),
        name="combine_ln",
    )(d2, d2, x1, gates, ln_g.reshape(1, D_MODEL), ln_b.reshape(1, D_MODEL), yb)


MOE_BM = 256
ATTN_BLK = 512
GDN_SUB = 4
ROUTE_TM = 128


def _mixers(x2d, bsz, seq, cache_k, cache_v, conv_past, s0, lam, lam_init, w_in_bf, conv_w, a_log, dt_bias,
            delta_norm_w, subln_w):
    prompt = cache_k is None
    tm = ATTN_BLK if prompt else x2d.shape[0]
    q_bf, k_f, v_f, k_bf, v_bf, cin, z, ab = _in_proj(x2d, w_in_bf, tm, prompt)
    if prompt:
        att = _attn_prompt(q_bf, k_bf, v_bf, lam, subln_w, bsz, seq, ATTN_BLK, lam_init)
    else:
        att = _attn_sample(q_bf, k_bf, v_bf, cache_k, cache_v, 0, lam, subln_w.reshape(1, HEAD), bsz, seq, lam_init)
    chunk = CHUNK if seq % CHUNK == 0 else seq
    n_sub = GDN_SUB if (seq // chunk) % GDN_SUB == 0 else 1
    past8 = jnp.pad(conv_past, ((0, 0), (SUBLANES - (CONV_W - 1), 0), (0, 0)))
    u, w, qd, kd, qk, gl = _gdn_prep(cin, ab, past8, conv_w, a_log, dt_bias, bsz, seq, chunk, n_sub)
    o, s_new = _gdn_scan(u, w, qd, kd, qk, gl, z, s0, delta_norm_w.reshape(1, HEAD), bsz, seq, chunk)
    return att, o, k_f, v_f, cin, s_new


def _moe(x1, idx, gates, rank, counts, w_gu, b_gu, w_down, b_down, ln_g, ln_b, bm, tm, t_prompt):
    t = x1.shape[0]
    n = t * TOP_K
    counts = counts[0, :N_EXPERTS]
    padded = (counts + bm - 1) // bm * bm
    pad_end = jnp.cumsum(padded).astype(jnp.int32)
    pad_start = (pad_end - padded).astype(jnp.int32)
    idx = idx[:, :TOP_K, :]
    dest = rank[:, :TOP_K, :] + sum(jnp.where(idx == e, pad_start[e], 0) for e in range(N_EXPERTS))
    dest = dest.transpose(0, 2, 1).reshape(t, TOP_K)
    n_blocks = -(-n // bm) + N_EXPERTS
    blk_start = jnp.arange(n_blocks, dtype=jnp.int32) * bm
    blk_e = jnp.minimum(jnp.sum(pad_end[None, :] <= blk_start[:, None], axis=1), N_EXPERTS - 1).astype(jnp.int32)
    n_used = (pad_end[-1] // bm).astype(jnp.int32).reshape(1)
    xb = _dispatch(x1, dest, pad_start, pad_end, n_blocks * bm, tm, bm)
    yb = _experts(xb, blk_e, n_used, w_gu, b_gu, w_down, b_down, bm)
    return _combine_ln(x1, gates, dest, yb, ln_g, ln_b, ROUTE_TM, t_prompt)


def kernel(x_prompt, x_sample, cache_k, cache_v, state_conv, state_delta, w_in, conv_w, a_log, dt_bias,
           delta_norm_w, lambda_q1, lambda_k1, lambda_q2, lambda_k2, subln_w, w_out, ln1_g, ln1_b,
           router_w, router_b, w_gu, b_gu, w_down, b_down, ln2_g, ln2_b):
    bp, lp, _ = x_prompt.shape
    bs, ls, _ = x_sample.shape
    l = 0
    lam_init = 0.8 - 0.6 * math.exp(-0.3 * l)
    lam = (jnp.exp(jnp.sum(lambda_q1[l] * lambda_k1[l])) - jnp.exp(jnp.sum(lambda_q2[l] * lambda_k2[l]))
           + lam_init).reshape(1).astype(F32)
    w_in_bf = jnp.pad(w_in[l], ((0, 0), (0, IN_COLS_PAD - IN_COLS))).astype(BF16)
    shared = (lam, lam_init, w_in_bf, conv_w[l], a_log[l], dt_bias[l], delta_norm_w[l], subln_w[l])

    xp = x_prompt.reshape(bp * lp, D_MODEL)
    xs = x_sample.reshape(bs * ls, D_MODEL)
    zero_conv = jnp.zeros((bp, CONV_W - 1, CONV_CH), F32)
    zero_s = jnp.zeros((bp, N_HEADS, HEAD, HEAD), F32)
    att_p, o_p, k_p, v_p, cin_p, s_p = _mixers(xp, bp, lp, None, None, zero_conv, zero_s, *shared)
    att_s, o_s, k_s, v_s, cin_s, s_s = _mixers(xs, bs, ls, cache_k, cache_v, state_conv[l],
                                               state_delta[l], *shared)

    x1, idx, gates, rank, counts = _out_router(((att_p, o_p, xp), (att_s, o_s, xs)), w_out[l].astype(BF16),
                                               ln1_g[l], ln1_b[l], router_w[l], router_b[l], ROUTE_TM)
    tp = bp * lp
    tm = next(c for c in (4, 3, 2, 1) if (x1.shape[0] // ROUTE_TM) % c == 0) * ROUTE_TM
    y_p, y_s = _moe(x1, idx, gates, rank, counts, w_gu[l], b_gu[l], w_down[l], b_down[l], ln2_g[l], ln2_b[l],
                    MOE_BM, tm, tp)
    conv_tail = lambda cin, b, s: cin.reshape(b, s, CONV_CH)[:, s - (CONV_W - 1):][None]
    return (y_p.reshape(bp, lp, D_MODEL), y_s.reshape(bs, ls, D_MODEL),
            k_p.reshape(1, bp, lp, N_HEADS, HEAD), v_p.reshape(1, bp, lp, N_HEADS, HEAD),
            conv_tail(cin_p, bp, lp), s_p[None].astype(state_delta.dtype),
            k_s.reshape(1, bs, ls, N_HEADS, HEAD), v_s.reshape(1, bs, ls, N_HEADS, HEAD),
            conv_tail(cin_s, bs, ls), s_s[None].astype(state_delta.dtype))
```

```python
import functools
import math

import jax
import jax.numpy as jnp
from jax import lax
from jax.experimental import pallas as pl
from jax.experimental.pallas import tpu as pltpu

F32 = jnp.float32
BF16 = jnp.bfloat16

D_MODEL = 1024
HEAD = 128
N_HEADS = 4
DQK = HEAD // 2
GROUP = N_HEADS * HEAD
CONV_W = 4
CONV_CH = 3 * GROUP
CHUNK = 64
ALIBI_MAX = 8.0
N_EXPERTS = 32
TOP_K = 4
D_FF = D_MODEL
SWIGLU_LIMIT = 7.0
SWIGLU_ALPHA = 1.702
DEPTH = 1
DEEPNORM_ALPHA = (2 * DEPTH) ** 0.25
LN_EPS = 1e-5
SUBLN_EPS = 1e-5
GATED_NORM_EPS = 1e-6
L2_EPS = 1e-6

LANES = 128
SUBLANES = 8
BF16_EXACT_INT = 256
BF16_ROWS = 16
LOG2E = 1.4426950408889634
N_POS = 6
ONES_ROWS = BF16_ROWS
N_DMA_PRIORITIES = 2
VMEM_LIMIT = 56 * 1024 * 1024

COL_Q, COL_K, COL_V, COL_CONV = 0, GROUP, 2 * GROUP, 3 * GROUP
COL_Z = COL_CONV + CONV_CH
COL_AB = COL_Z + GROUP
IN_COLS = COL_AB + 2 * N_HEADS
IN_COLS_PAD = COL_AB + LANES


def _params(*sem):
    return pltpu.CompilerParams(dimension_semantics=sem, vmem_limit_bytes=VMEM_LIMIT)


def _in_proj_kernel(x_ref, w_ref, q_ref, kf_ref, vf_ref, kb_ref, vb_ref, c_ref, z_ref, ab_ref, *, tm, transposed):
    xb = x_ref[...].astype(BF16)

    def section(lo, hi):
        return jnp.dot(xb, w_ref[:, lo:hi], preferred_element_type=F32)

    q = section(COL_Q, COL_K) * (DQK ** -0.5 * (LOG2E if transposed else 1.0))
    k = section(COL_K, COL_V)
    kf_ref[...] = k
    kb_ref[...] = k.astype(BF16)
    v = section(COL_V, COL_CONV)
    vf_ref[...] = v
    if transposed:
        q_ref[0] = q.T.astype(BF16)
        vb_ref[0] = v.T.astype(BF16)
    else:
        q_ref[...] = q.astype(BF16)
        vb_ref[...] = v.astype(BF16)
    c_ref[...] = section(COL_CONV, COL_Z)
    z_ref[...] = section(COL_Z, COL_AB)
    ab_ref[...] = section(COL_AB, IN_COLS_PAD)


def _in_proj(x2d, w_bf, tm, transposed):
    t = x2d.shape[0]
    row = lambda i: (i, 0)
    widths = (GROUP, GROUP, GROUP, GROUP, GROUP, CONV_CH, GROUP, LANES)
    dtypes = (BF16, F32, F32, BF16, BF16, F32, F32, F32)
    out_specs = [pl.BlockSpec((tm, w), row) for w in widths]
    out_shape = [jax.ShapeDtypeStruct((t, w), d) for w, d in zip(widths, dtypes)]
    if transposed:
        for slot in (0, 4):
            out_specs[slot] = pl.BlockSpec((1, GROUP, tm), lambda i: (i, 0, 0))
            out_shape[slot] = jax.ShapeDtypeStruct((t // tm, GROUP, tm), BF16)
    return pl.pallas_call(
        functools.partial(_in_proj_kernel, tm=tm, transposed=transposed),
        grid=(t // tm,),
        in_specs=[pl.BlockSpec((tm, D_MODEL), row),
                  pl.BlockSpec((D_MODEL, IN_COLS_PAD), lambda i: (0, 0))],
        out_specs=out_specs,
        out_shape=out_shape,
        compiler_params=_params("parallel"),
        name="in_proj",
    )(x2d, w_bf)


def _alibi_slopes():
    return [2.0 ** (-ALIBI_MAX * (h + 1) / N_HEADS) for h in range(N_HEADS)]


def _stack_halves(q):
    lane = lax.broadcasted_iota(jnp.int32, q.shape, 1)
    zero = jnp.zeros_like(q)
    return jnp.concatenate([jnp.where(lane < DQK, q, zero), jnp.where(lane < DQK, zero, q)], axis=0)


def _diff_norm(acc, l, lam, w, lam_init, rows):
    o = acc[:rows] / l[:rows] - lam * (acc[rows:] / l[rows:])
    ms = jnp.mean(o * o, axis=-1, keepdims=True)
    return o * lax.rsqrt(ms + SUBLN_EPS) * w * (1.0 - lam_init)


def _head_slope(h):
    s = _alibi_slopes()
    return jnp.where(h == 0, s[0], jnp.where(h == 1, s[1], jnp.where(h == 2, s[2], s[3]))).astype(F32)


def _attn_prompt_kernel(lam_ref, qt_ref, k_ref, vt_ref, w_ref, o_ref,
                        diag_ref, kaug_ref, vaug_ref, qz_ref, s0_ref, s1_ref, p0_ref, p1_ref, mx0_ref, mx1_ref, m_ref, acc_ref, *, blk, lam_init):
    i = pl.program_id(2)
    slope = _head_slope(pl.program_id(1)) * LOG2E
    rows = 2 * blk
    n_kv = k_ref.shape[0] // blk

    def pieces(x):
        lo = x % BF16_EXACT_INT
        return _split3(slope * lo.astype(F32)) + _split3(slope * (x - lo).astype(F32))

    @pl.when(i == 0)
    def _():
        lane = lax.broadcasted_iota(jnp.int32, (blk, HEAD), 1)
        extra = jnp.where(lane < N_POS, 1.0, 0.0).astype(BF16)
        for n, piece in enumerate(pieces(lax.broadcasted_iota(jnp.int32, (blk, 1), 0))):
            extra = jnp.where(lane == N_POS + n, piece, extra)
        ones = jnp.ones((ONES_ROWS, blk), BF16)
        c = lax.broadcasted_iota(jnp.int32, (blk, rows), 0)
        a = lax.broadcasted_iota(jnp.int32, (blk, rows), 1) % blk
        diag_ref[...] = jnp.where(c // CHUNK <= a // CHUNK, -slope * jnp.abs(a - c).astype(F32), -jnp.inf)

        def fill(j, carry):
            j0 = pl.multiple_of(j * blk, blk)
            kaug_ref[pl.ds(j0, blk), :HEAD] = k_ref[pl.ds(j0, blk), :]
            kaug_ref[pl.ds(j0, blk), HEAD:] = extra
            vaug_ref[j, :HEAD, :] = vt_ref[j]
            vaug_ref[j, HEAD:, :] = ones
            return carry

        lax.fori_loop(0, n_kv, fill, 0)

    qt = qt_ref[0]
    d = lax.broadcasted_iota(jnp.int32, qt.shape, 0)
    zero = jnp.zeros_like(qt)
    qz_ref[:HEAD, :] = jnp.concatenate([jnp.where(d < DQK, qt, zero), jnp.where(d < DQK, zero, qt)], axis=1)
    r = lax.broadcasted_iota(jnp.int32, (HEAD, rows), 0)
    qx = jnp.where((r >= N_POS) & (r < 2 * N_POS), 1.0, 0.0).astype(BF16)
    for n, piece in enumerate(pieces(-(lax.broadcasted_iota(jnp.int32, (1, rows), 1) % blk))):
        qx = jnp.where(r == n, piece, qx)
    qz_ref[HEAD:, :] = qx

    def tile_rows(j):
        return pl.ds(pl.multiple_of(j * blk, blk), blk)

    s = jnp.dot(k_ref[tile_rows(i), :], qz_ref[:HEAD, :], preferred_element_type=F32) + diag_ref[...]
    m = jnp.max(s, axis=0, keepdims=True)
    p1_ref[...] = jnp.exp2(s - m).astype(BF16)
    m_ref[...] = m
    acc_ref[...] = jnp.zeros_like(acc_ref)
    s_first = jnp.dot(kaug_ref[tile_rows(0), :], qz_ref[...], preferred_element_type=F32)
    s0_ref[...] = s_first
    mx0_ref[...] = jnp.max(s_first, axis=0, keepdims=True)

    def step(j, s_cur, s_nxt, p_cur, p_nxt, mx_cur, mx_nxt):
        s_next = jnp.dot(kaug_ref[tile_rows(jnp.minimum(j + 1, i - 1)), :], qz_ref[...],
                         preferred_element_type=F32)
        s_nxt[...] = s_next
        mx_nxt[...] = jnp.max(s_next, axis=0, keepdims=True)
        pv = jnp.dot(vaug_ref[jnp.where(j == 0, i, j - 1)], p_nxt[...], preferred_element_type=F32)
        shift = -slope * ((i - j) * blk).astype(F32)
        m_old = m_ref[...]
        m_new = jnp.maximum(m_old, mx_cur[...] + shift)
        p_cur[...] = jnp.exp2(s_cur[...] - (m_new - shift)).astype(BF16)
        acc_ref[...] = (acc_ref[...] + pv) * jnp.exp2(m_old - m_new)
        m_ref[...] = m_new

    def body(jj, carry):
        step(2 * jj, s0_ref, s1_ref, p0_ref, p1_ref, mx0_ref, mx1_ref)

        @pl.when(2 * jj + 1 < i)
        def _():
            step(2 * jj + 1, s1_ref, s0_ref, p1_ref, p0_ref, mx1_ref, mx0_ref)

        return carry

    lax.fori_loop(0, (i + 1) // 2, body, 0)
    p_last = jnp.where(i % 2 == 1, p0_ref[...], p1_ref[...])
    acc = acc_ref[...] + jnp.dot(vaug_ref[jnp.where(i > 0, i - 1, i)], p_last, preferred_element_type=F32)
    l = acc[HEAD:HEAD + 1, :]
    num = acc[:HEAD, :]
    ot = num[:, :blk] / l[:, :blk] - lam_ref[0] * (num[:, blk:] / l[:, blk:])
    ms = jnp.mean(ot * ot, axis=0, keepdims=True)
    ot = ot * lax.rsqrt(ms + SUBLN_EPS) * w_ref[...] * (1.0 - lam_init)
    o_ref[...] = ot.T.astype(o_ref.dtype)


def _attn_prompt(qt_bf, k_bf, vt_bf, lam, subln_w, bsz, seq, blk, lam_init):
    nq = seq // blk
    kern = functools.partial(_attn_prompt_kernel, blk=blk, lam_init=lam_init)
    return pl.pallas_call(
        kern,
        grid=(bsz, N_HEADS, nq),
        in_specs=[pl.BlockSpec(memory_space=pltpu.SMEM),
                  pl.BlockSpec((1, HEAD, blk), lambda b, h, i: (b * nq + i, h, 0)),
                  pl.BlockSpec((seq, HEAD), lambda b, h, i: (b, h)),
                  pl.BlockSpec((nq, HEAD, blk), lambda b, h, i: (b, h, 0)),
                  pl.BlockSpec((HEAD, 1), lambda b, h, i: (0, 0))],
        out_specs=pl.BlockSpec((blk, HEAD), lambda b, h, i: (b * nq + i, h)),
        scratch_shapes=[pltpu.VMEM((blk, 2 * blk), F32),
                        pltpu.VMEM((seq, 2 * HEAD), BF16),
                        pltpu.VMEM((nq, HEAD + ONES_ROWS, blk), BF16),
                        pltpu.VMEM((2 * HEAD, 2 * blk), BF16),
                        pltpu.VMEM((blk, 2 * blk), F32), pltpu.VMEM((blk, 2 * blk), F32),
                        pltpu.VMEM((blk, 2 * blk), BF16), pltpu.VMEM((blk, 2 * blk), BF16),
                        pltpu.VMEM((1, 2 * blk), F32), pltpu.VMEM((1, 2 * blk), F32),
                        pltpu.VMEM((1, 2 * blk), F32),
                        pltpu.VMEM((HEAD + ONES_ROWS, 2 * blk), F32)],
        out_shape=jax.ShapeDtypeStruct((bsz * seq, GROUP), BF16),
        compiler_params=_params("parallel", "parallel", "arbitrary"),
        name="attn_prompt",
    )(lam, qt_bf, k_bf, vt_bf, subln_w.reshape(HEAD, 1))


def _attn_sample_kernel(lam_ref, q_ref, kn_ref, vn_ref, kc_ref, vc_ref, w_ref, o_ref, *, seq, past, lam_init):
    nt = (((1,), (1,)), ((), ()))
    qpos = past + lax.broadcasted_iota(jnp.int32, (2 * seq, 1), 0) % seq
    rel_c = jnp.abs(qpos - lax.broadcasted_iota(jnp.int32, (1, past), 1)).astype(F32)
    rel_n = jnp.abs(qpos - (past + lax.broadcasted_iota(jnp.int32, (1, seq), 1))).astype(F32)
    for h, slope in enumerate(_alibi_slopes()):
        col = slice(h * HEAD, (h + 1) * HEAD)
        qz = _stack_halves(q_ref[:, col])
        s_c = lax.dot_general(qz, kc_ref[0, 0, :, h, :].astype(BF16), nt, preferred_element_type=F32)
        s_c = s_c - slope * rel_c
        s_n = lax.dot_general(qz, kn_ref[:, col], nt, preferred_element_type=F32) - slope * rel_n
        m = jnp.maximum(jnp.max(s_c, axis=-1, keepdims=True), jnp.max(s_n, axis=-1, keepdims=True))
        p_c = jnp.exp(s_c - m)
        p_n = jnp.exp(s_n - m)
        l = jnp.sum(p_c, axis=-1, keepdims=True) + jnp.sum(p_n, axis=-1, keepdims=True)
        acc = (jnp.dot(p_c.astype(BF16), vc_ref[0, 0, :, h, :].astype(BF16), preferred_element_type=F32)
               + jnp.dot(p_n.astype(BF16), vn_ref[:, col], preferred_element_type=F32))
        o_ref[:, col] = _diff_norm(acc, l, lam_ref[0], w_ref[...], lam_init, seq).astype(o_ref.dtype)


def _attn_sample(q_bf, k_bf, v_bf, cache_k, cache_v, layer, lam, subln_w, bsz, seq, lam_init):
    past = cache_k.shape[2]
    kern = functools.partial(_attn_sample_kernel, seq=seq, past=past, lam_init=lam_init)
    new = pl.BlockSpec((seq, GROUP), lambda b: (b, 0))
    cache = pl.BlockSpec((1, 1, past, N_HEADS, HEAD), lambda b: (layer, b, 0, 0, 0))
    return pl.pallas_call(
        kern,
        grid=(bsz,),
        in_specs=[pl.BlockSpec(memory_space=pltpu.SMEM), new, new, new, cache, cache,
                  pl.BlockSpec((1, HEAD), lambda b: (0, 0))],
        out_specs=new,
        out_shape=jax.ShapeDtypeStruct((bsz * seq, GROUP), BF16),
        compiler_params=_params("parallel"),
        name="attn_sample",
    )(lam, q_bf, k_bf, v_bf, cache_k, cache_v, subln_w)


def _split3(x):
    hi = x.astype(BF16)
    r1 = x - hi.astype(F32)
    mid = r1.astype(BF16)
    lo = (r1 - mid.astype(F32)).astype(BF16)
    return hi, mid, lo


def _gdn_prep_kernel(cin_ref, halo_ref, past_ref, ab_ref, cw_ref, alog_ref, dtb_ref,
                     u_ref, w_ref, qd_ref, kd_ref, qk_ref, gl_ref, *, chunk, n_sub):
    c_idx = pl.program_id(1)
    rows = chunk * n_sub
    prev = jnp.where(c_idx == 0, past_ref[0], halo_ref[...])
    xin = jnp.concatenate([prev, cin_ref[...]], axis=0)
    conv = sum(xin[SUBLANES - (CONV_W - 1) + j: SUBLANES - (CONV_W - 1) + j + rows] * cw_ref[j:j + 1, :]
               for j in range(CONV_W))
    conv = conv * jax.nn.sigmoid(conv)

    ab = ab_ref[...]
    lane = lax.broadcasted_iota(jnp.int32, ab.shape, 1)
    pre = ab + dtb_ref[...]
    softplus = jnp.maximum(pre, 0.0) + jnp.log(1.0 + jnp.exp(-jnp.abs(pre)))
    g = jnp.where(lane < N_HEADS, -jnp.exp(alog_ref[...]) * softplus, 0.0)
    beta_all = jax.nn.sigmoid(ab)

    ri = lax.broadcasted_iota(jnp.int32, (rows, rows), 0)
    ci = lax.broadcasted_iota(jnp.int32, (rows, rows), 1)
    same = (ri // chunk) == (ci // chunk)
    incl = same & (ri >= ci)
    strict = same & (ri > ci)
    eye = jnp.where(ri == ci, 1.0, 0.0).astype(F32)
    nt = (((1,), (1,)), ((), ()))
    g_parts = _split3(g)
    ones_incl = jnp.where(incl, 1.0, 0.0).astype(BF16)
    ones_same = jnp.where(same, 1.0, 0.0).astype(BF16)
    gc = sum(jnp.dot(ones_incl, part, preferred_element_type=F32) for part in g_parts)
    g_end = sum(jnp.dot(ones_same, part, preferred_element_type=F32) for part in g_parts)
    gct = gc.T
    for sc in range(n_sub):
        gl_ref[sc] = g_end[sc * chunk:sc * chunk + 1, :]

    t_mats, p_mats, rhs = [], [], []
    for h in range(N_HEADS):
        col = slice(h * HEAD, (h + 1) * HEAD)
        qh = conv[:, h * HEAD:(h + 1) * HEAD]
        kh = conv[:, GROUP + h * HEAD:GROUP + (h + 1) * HEAD]
        vh = conv[:, 2 * GROUP + h * HEAD:2 * GROUP + (h + 1) * HEAD]
        qh = qh * lax.rsqrt(jnp.sum(qh * qh, axis=-1, keepdims=True) + L2_EPS) * (HEAD ** -0.5)
        kh = kh * lax.rsqrt(jnp.sum(kh * kh, axis=-1, keepdims=True) + L2_EPS)
        beta = beta_all[:, N_HEADS + h:N_HEADS + h + 1]
        gcol = gc[:, h:h + 1]
        grow = gct[h:h + 1, :]
        gamma = jnp.exp(jnp.where(incl, gcol - grow, -jnp.inf))
        egc = jnp.exp(gcol)
        kb = kh * beta
        khb = kh.astype(BF16)
        a = jnp.where(strict, lax.dot_general(kb.astype(BF16), khb, nt, preferred_element_type=F32) * gamma, 0.0)
        qk = (lax.dot_general(qh.astype(BF16), khb, nt, preferred_element_type=F32) * gamma).astype(BF16)
        for sc in range(n_sub):
            blk = slice(sc * chunk, (sc + 1) * chunk)
            qk_ref[h, blk, :] = qk[blk, blk]
        qd_ref[:, col] = (qh * egc).astype(BF16)
        kd_ref[:, col] = (kh * jnp.exp(g_end[:, h:h + 1] - gcol)).astype(BF16)
        t_mats.append(eye - a)
        p_mats.append(a)
        rhs.append(((vh * beta).astype(BF16), (kb * egc).astype(BF16)))

    for _ in range(int(math.log2(chunk)) - 1):
        for h in range(N_HEADS):
            pb = p_mats[h].astype(BF16)
            p_mats[h] = jnp.dot(pb, pb, preferred_element_type=F32)
        for h in range(N_HEADS):
            t_mats[h] = t_mats[h] + jnp.dot(t_mats[h].astype(BF16), p_mats[h].astype(BF16),
                                            preferred_element_type=F32)

    for h in range(N_HEADS):
        col = slice(h * HEAD, (h + 1) * HEAD)
        t_inv = t_mats[h].astype(BF16)
        u_ref[:, col] = jnp.dot(t_inv, rhs[h][0], preferred_element_type=F32)
        w_ref[:, col] = jnp.dot(t_inv, rhs[h][1], preferred_element_type=F32).astype(BF16)


def _gdn_prep(cin, ab, past8, conv_w, a_log, dt_bias, bsz, seq, chunk, n_sub):
    rows = chunk * n_sub
    nblk = seq // rows
    t = bsz * seq
    lanes = lambda v: jnp.pad(v.reshape(1, N_HEADS).astype(F32), ((0, 0), (0, LANES - N_HEADS)))
    kern = functools.partial(_gdn_prep_kernel, chunk=chunk, n_sub=n_sub)
    rowblk = lambda b, c: (b * nblk + c, 0)
    halo = lambda b, c: (jnp.maximum((b * nblk + c) * (rows // SUBLANES) - 1, 0), 0)
    const = lambda b, c: (0, 0)
    return pl.pallas_call(
        kern,
        grid=(bsz, nblk),
        in_specs=[pl.BlockSpec((rows, CONV_CH), rowblk),
                  pl.BlockSpec((SUBLANES, CONV_CH), halo),
                  pl.BlockSpec((1, SUBLANES, CONV_CH), lambda b, c: (b, 0, 0)),
                  pl.BlockSpec((rows, LANES), rowblk),
                  pl.BlockSpec((CONV_W, CONV_CH), const),
                  pl.BlockSpec((1, LANES), const),
                  pl.BlockSpec((1, LANES), const)],
        out_specs=[pl.BlockSpec((rows, GROUP), rowblk),
                   pl.BlockSpec((rows, GROUP), rowblk),
                   pl.BlockSpec((rows, GROUP), rowblk),
                   pl.BlockSpec((rows, GROUP), rowblk),
                   pl.BlockSpec((N_HEADS, rows, chunk), lambda b, c: (0, b * nblk + c, 0)),
                   pl.BlockSpec((n_sub, 1, LANES), lambda b, c: (b * nblk + c, 0, 0))],
        out_shape=[jax.ShapeDtypeStruct((t, GROUP), F32),
                   jax.ShapeDtypeStruct((t, GROUP), BF16),
                   jax.ShapeDtypeStruct((t, GROUP), BF16),
                   jax.ShapeDtypeStruct((t, GROUP), BF16),
                   jax.ShapeDtypeStruct((N_HEADS, t, chunk), BF16),
                   jax.ShapeDtypeStruct((t // chunk, 1, LANES), F32)],
        compiler_params=_params("parallel", "parallel"),
        name="gdn_prep",
    )(cin, cin, past8, ab, conv_w, lanes(a_log), lanes(dt_bias))


def _gdn_scan_kernel(u_ref, w_ref, qd_ref, kd_ref, qk_ref, gl_ref, z_ref, s0_ref, nw_ref,
                     o_ref, sf_ref, s_ref, *, bsz, chunk):
    c_idx = pl.program_id(0)

    @pl.when(c_idx == 0)
    def _():
        s_ref[...] = s0_ref[...]

    tn = (((0,), (0,)), ((), ()))
    chains = [(b, h, slice(h * HEAD, (h + 1) * HEAD)) for b in range(bsz) for h in range(N_HEADS)]
    states = [s_ref[b, h] for b, h, _ in chains]
    prods = [jnp.dot(jnp.concatenate([w_ref[b, :, col], qd_ref[b, :, col]], axis=0), s.astype(BF16),
                     preferred_element_type=F32) for (b, _, col), s in zip(chains, states)]
    v_news = [(u_ref[b, :, col] - r[:chunk]).astype(BF16) for (b, _, col), r in zip(chains, prods)]
    outs = [r[chunk:] + jnp.dot(qk_ref[h, b], v, preferred_element_type=F32)
            for (b, h, _), r, v in zip(chains, prods, v_news)]
    for (b, h, col), s, v in zip(chains, states, v_news):
        glast = jnp.exp(gl_ref[b, 0])
        s_ref[b, h] = s * glast[:, h:h + 1] + lax.dot_general(kd_ref[b, :, col], v, tn,
                                                             preferred_element_type=F32)
    for (b, h, col), o in zip(chains, outs):
        zh = z_ref[b, :, col]
        ms = jnp.mean(o * o, axis=-1, keepdims=True)
        o = o * lax.rsqrt(ms + GATED_NORM_EPS) * nw_ref[...] * (zh * jax.nn.sigmoid(zh))
        o_ref[b, :, col] = o.astype(o_ref.dtype)

    @pl.when(c_idx == pl.num_programs(0) - 1)
    def _():
        sf_ref[...] = s_ref[...]


def _gdn_scan(u, w, qd, kd, qk, gl, z, s0, norm_w, bsz, seq, chunk):
    nc = seq // chunk
    kern = functools.partial(_gdn_scan_kernel, bsz=bsz, chunk=chunk)
    tok = pl.BlockSpec((bsz, chunk, GROUP), lambda c: (0, c, 0))
    state = pl.BlockSpec((bsz, N_HEADS, HEAD, HEAD), lambda c: (0, 0, 0, 0))
    o, s_final = pl.pallas_call(
        kern,
        grid=(nc,),
        in_specs=[tok, tok, tok, tok,
                  pl.BlockSpec((N_HEADS, bsz, chunk, chunk), lambda c: (0, 0, c, 0)),
                  pl.BlockSpec((bsz, 1, 1, LANES), lambda c: (0, c, 0, 0)),
                  tok, state,
                  pl.BlockSpec((1, HEAD), lambda c: (0, 0))],
        out_specs=[tok, state],
        out_shape=[jax.ShapeDtypeStruct((bsz, seq, GROUP), BF16),
                   jax.ShapeDtypeStruct((bsz, N_HEADS, HEAD, HEAD), F32)],
        scratch_shapes=[pltpu.VMEM((bsz, N_HEADS, HEAD, HEAD), F32)],
        compiler_params=_params("arbitrary"),
        name="gdn_scan",
    )(u.reshape(bsz, seq, GROUP), w.reshape(bsz, seq, GROUP), qd.reshape(bsz, seq, GROUP),
      kd.reshape(bsz, seq, GROUP), qk.reshape(N_HEADS, bsz, seq, chunk), gl.reshape(bsz, nc, 1, LANES),
      z.reshape(bsz, seq, GROUP), s0, norm_w)
    return o.reshape(bsz * seq, GROUP), s_final


def _layernorm(x, g, b):
    mu = jnp.mean(x, axis=-1, keepdims=True)
    xc = x - mu
    var = jnp.mean(xc * xc, axis=-1, keepdims=True)
    return xc * lax.rsqrt(var + LN_EPS) * g + b


def _out_router_kernel(attp_ref, atts_ref, op_ref, os_ref, xp_ref, xs_ref, wo_ref, g_ref, b_ref, rw_ref, rb_ref,
                       x1_ref, idx_ref, gate_ref, rank_ref, cnt_ref, carry_ref, *, tm, n_prompt):
    step = pl.program_id(0)

    @pl.when(step == 0)
    def _():
        carry_ref[...] = jnp.zeros_like(carry_ref)

    prompt = step < n_prompt
    att = jnp.where(prompt, attp_ref[...], atts_ref[...])
    o = jnp.where(prompt, op_ref[...], os_ref[...])
    x = jnp.where(prompt, xp_ref[...], xs_ref[...])
    mix = (jnp.dot(att, wo_ref[:GROUP, :], preferred_element_type=F32)
           + jnp.dot(o, wo_ref[GROUP:, :], preferred_element_type=F32))
    x1 = _layernorm(DEEPNORM_ALPHA * x + mix, g_ref[...], b_ref[...])
    x1_ref[...] = x1

    x_hi = x1.astype(BF16)
    x_lo = (x1 - x_hi.astype(F32)).astype(BF16)
    logits = (jnp.dot(x_hi, rw_ref[0], preferred_element_type=F32)
              + jnp.dot(x_lo, rw_ref[0], preferred_element_type=F32)
              + jnp.dot(x_hi, rw_ref[1], preferred_element_type=F32)) + rb_ref[...]
    lane = lax.broadcasted_iota(jnp.int32, logits.shape, 1)
    work = jnp.where(lane < N_EXPERTS, logits, -jnp.inf)
    vals, idxs = [], []
    for _ in range(TOP_K):
        m = jnp.max(work, axis=-1, keepdims=True)
        am = jnp.min(jnp.where(work == m, lane, LANES), axis=-1, keepdims=True)
        vals.append(m)
        idxs.append(am)
        work = jnp.where(lane == am, -jnp.inf, work)
    exps = [jnp.exp(v - vals[0]) for v in vals]
    denom = exps[0] + exps[1] + exps[2] + exps[3]
    chosen = jnp.zeros(logits.shape, F32)
    gate_out = jnp.zeros(logits.shape, F32)
    idx_out = jnp.zeros(logits.shape, jnp.int32)
    for k in range(TOP_K):
        chosen = jnp.where(lane == idxs[k], 1.0, chosen)
        gate_out = jnp.where(lane == k, exps[k] / denom, gate_out)
        idx_out = jnp.where(lane == k, idxs[k], idx_out)
    ri = lax.broadcasted_iota(jnp.int32, (tm, tm), 0)
    ci = lax.broadcasted_iota(jnp.int32, (tm, tm), 1)
    before = jnp.where(ri > ci, 1.0, 0.0).astype(BF16)
    prefix = jnp.dot(before, chosen.astype(BF16), preferred_element_type=F32) + carry_ref[...]
    rank_out = jnp.zeros(logits.shape, F32)
    for k in range(TOP_K):
        r = jnp.sum(jnp.where(lane == idxs[k], prefix, 0.0), axis=-1, keepdims=True)
        rank_out = jnp.where(lane == k, r, rank_out)
    carry_ref[...] = carry_ref[...] + jnp.sum(chosen, axis=0, keepdims=True)
    idx_ref[0] = idx_out.T[:SUBLANES, :]
    gate_ref[...] = gate_out
    rank_ref[0] = rank_out.astype(jnp.int32).T[:SUBLANES, :]
    cnt_ref[...] = carry_ref[...].astype(jnp.int32)


def _out_router(streams, w_out_bf, ln_g, ln_b, router_w, router_b, tm):
    (att_p, o_p, x_p), (att_s, o_s, x_s) = streams
    n_prompt = x_p.shape[0] // tm
    t = x_p.shape[0] + x_s.shape[0]
    row = lambda i: (i, 0)
    const = lambda i: (0, 0)
    prow = lambda i: (jnp.minimum(i, n_prompt - 1), 0)
    srow = lambda i: (jnp.maximum(i - n_prompt, 0), 0)
    slots = pl.BlockSpec((1, SUBLANES, tm), lambda i: (i, 0, 0))
    rw = jnp.pad(router_w, ((0, 0), (0, LANES - N_EXPERTS)))
    rw_hi = rw.astype(BF16)
    rw = jnp.stack([rw_hi, (rw - rw_hi.astype(F32)).astype(BF16)])
    rb = jnp.pad(router_b.reshape(1, N_EXPERTS), ((0, 0), (0, LANES - N_EXPERTS)))
    kern = functools.partial(_out_router_kernel, tm=tm, n_prompt=n_prompt)
    return pl.pallas_call(
        kern,
        grid=(t // tm,),
        in_specs=[pl.BlockSpec((tm, GROUP), prow), pl.BlockSpec((tm, GROUP), srow),
                  pl.BlockSpec((tm, GROUP), prow), pl.BlockSpec((tm, GROUP), srow),
                  pl.BlockSpec((tm, D_MODEL), prow), pl.BlockSpec((tm, D_MODEL), srow),
                  pl.BlockSpec((2 * GROUP, D_MODEL), const),
                  pl.BlockSpec((1, D_MODEL), const), pl.BlockSpec((1, D_MODEL), const),
                  pl.BlockSpec((2, D_MODEL, LANES), lambda i: (0, 0, 0)), pl.BlockSpec((1, LANES), const)],
        out_specs=[pl.BlockSpec((tm, D_MODEL), row), slots, pl.BlockSpec((tm, LANES), row), slots,
                   pl.BlockSpec((1, LANES), const)],
        out_shape=[jax.ShapeDtypeStruct((t, D_MODEL), F32),
                   jax.ShapeDtypeStruct((t // tm, SUBLANES, tm), jnp.int32), jax.ShapeDtypeStruct((t, LANES), F32),
                   jax.ShapeDtypeStruct((t // tm, SUBLANES, tm), jnp.int32),
                   jax.ShapeDtypeStruct((1, LANES), jnp.int32)],
        scratch_shapes=[pltpu.VMEM((1, LANES), F32)],
        compiler_params=_params("arbitrary"),
        name="out_router",
    )(att_p, att_s, o_p, o_s, x_p, x_s, w_out_bf, ln_g.reshape(1, D_MODEL), ln_b.reshape(1, D_MODEL), rw, rb)


def _row_copy(src, src_row, dst, dst_row, sem):
    return pltpu.make_async_copy(src.at[pl.ds(src_row, 1), :], dst.at[pl.ds(dst_row, 1), :], sem)


def _dispatch_kernel(ps_ref, pe_ref, dest_ref, x_ref, xb_hbm, zero_ref, sem, zsem, *, tm, bm):
    i = pl.program_id(0)

    def zero_block(row):
        return pltpu.make_async_copy(zero_ref, xb_hbm.at[pl.ds(pl.multiple_of(row, bm), bm), :], zsem)

    def fill(e):
        return zero_block(pe_ref[e] - bm)

    @pl.when(i == 0)
    def _():
        zero_ref[...] = jnp.zeros_like(zero_ref)
        for e in range(N_EXPERTS):
            @pl.when(pe_ref[e] > ps_ref[e])
            def _():
                fill(e).start()
        first_unused = pe_ref[N_EXPERTS - 1] // bm
        n_blocks = xb_hbm.shape[0] // bm
        lax.fori_loop(first_unused, n_blocks, lambda b, c: (zero_block(b * bm).start(), c)[1], 0)
        for e in range(N_EXPERTS):
            @pl.when(pe_ref[e] > ps_ref[e])
            def _():
                fill(e).wait()
        lax.fori_loop(first_unused, n_blocks, lambda b, c: (zero_block(b * bm).wait(), c)[1], 0)

    def issue(r, carry):
        for k in range(TOP_K):
            _row_copy(x_ref, r, xb_hbm, dest_ref[0, 0, r * TOP_K + k], sem).start(priority=k % N_DMA_PRIORITIES)
        return carry

    lax.fori_loop(0, tm, issue, 0)
    pltpu.make_async_copy(xb_hbm.at[pl.ds(0, tm * TOP_K), :], xb_hbm.at[pl.ds(0, tm * TOP_K), :], sem).wait()


def _dispatch(x1, dest, pad_start, pad_end, rows, tm, bm):
    t = x1.shape[0]
    return pl.pallas_call(
        functools.partial(_dispatch_kernel, tm=tm, bm=bm),
        grid_spec=pltpu.PrefetchScalarGridSpec(
            num_scalar_prefetch=2,
            grid=(t // tm,),
            in_specs=[pl.BlockSpec((1, 1, tm * TOP_K), lambda i, ps, pe: (i, 0, 0), memory_space=pltpu.SMEM),
                      pl.BlockSpec((tm, D_MODEL), lambda i, ps, pe: (i, 0))],
            out_specs=pl.BlockSpec(memory_space=pl.ANY),
            scratch_shapes=[pltpu.VMEM((bm, D_MODEL), F32),
                            pltpu.SemaphoreType.DMA(()), pltpu.SemaphoreType.DMA(())]),
        out_shape=jax.ShapeDtypeStruct((rows, D_MODEL), F32),
        compiler_params=_params("arbitrary"),
        name="dispatch",
    )(pad_start, pad_end, dest.reshape(t // tm, 1, tm * TOP_K), x1)


def _expert_kernel(be_ref, nb_ref, x_ref, wgu_ref, bgu_ref, wd_ref, bd_ref, y_ref):
    blk = pl.program_id(0)

    @pl.when(blk < nb_ref[0])
    def _():
        x = x_ref[...].astype(BF16)
        h = jnp.dot(x, wgu_ref[0].astype(BF16), preferred_element_type=F32) + bgu_ref[0]
        gate = jnp.minimum(h[:, :D_FF], SWIGLU_LIMIT)
        up = jnp.clip(h[:, D_FF:], -SWIGLU_LIMIT, SWIGLU_LIMIT)
        act = (up + 1.0) * (gate * jax.nn.sigmoid(SWIGLU_ALPHA * gate))
        y_ref[...] = jnp.dot(act.astype(BF16), wd_ref[0].astype(BF16), preferred_element_type=F32) + bd_ref[0]

    @pl.when(blk >= nb_ref[0])
    def _():
        y_ref[...] = jnp.zeros_like(y_ref)


def _experts(xb, blk_e, n_used, w_gu, b_gu, w_down, b_down, bm):
    rows = xb.shape[0]
    n_blocks = rows // bm
    used = lambda i, be, nb: (jnp.maximum(jnp.minimum(i, nb[0] - 1), 0), 0)
    return pl.pallas_call(
        _expert_kernel,
        grid_spec=pltpu.PrefetchScalarGridSpec(
            num_scalar_prefetch=2,
            grid=(n_blocks,),
            in_specs=[pl.BlockSpec((bm, D_MODEL), used),
                      pl.BlockSpec((1, D_MODEL, 2 * D_FF), lambda i, be, nb: (be[i], 0, 0)),
                      pl.BlockSpec((1, 1, 2 * D_FF), lambda i, be, nb: (be[i], 0, 0)),
                      pl.BlockSpec((1, D_FF, D_MODEL), lambda i, be, nb: (be[i], 0, 0)),
                      pl.BlockSpec((1, 1, D_MODEL), lambda i, be, nb: (be[i], 0, 0))],
            out_specs=pl.BlockSpec((bm, D_MODEL), lambda i, be, nb: (i, 0))),
        out_shape=jax.ShapeDtypeStruct((rows, D_MODEL), F32),
        compiler_params=_params("arbitrary"),
        name="experts",
    )(blk_e, n_used, xb, w_gu, b_gu.reshape(N_EXPERTS, 1, 2 * D_FF), w_down, b_down.reshape(N_EXPERTS, 1, D_MODEL))


def _combine_ln_kernel(dcur_ref, dnext_ref, x_ref, gate_ref, g_ref, b_ref, yb_hbm, op_ref, os_ref, ybuf, sem, *,
                       tm, n_prompt):
    i = pl.program_id(0)
    slot = i % 2

    def issue(dref, s):
        def body(r, carry):
            for k in range(TOP_K):
                _row_copy(yb_hbm, dref[0, 0, r * TOP_K + k], ybuf.at[s, k], r,
                          sem.at[s]).start(priority=k % N_DMA_PRIORITIES)
            return carry
        lax.fori_loop(0, tm, body, 0)

    @pl.when(i == 0)
    def _():
        issue(dcur_ref, 0)

    for s in range(2):
        @pl.when((i + 1 < pl.num_programs(0)) & (slot != s))
        def _():
            issue(dnext_ref, s)

    pltpu.make_async_copy(ybuf.at[slot], ybuf.at[slot], sem.at[slot]).wait()
    gates = gate_ref[...]
    y = sum(gates[:, k:k + 1] * ybuf[slot, k] for k in range(TOP_K))
    out = _layernorm(DEEPNORM_ALPHA * x_ref[...] + y, g_ref[...], b_ref[...])

    @pl.when(i < n_prompt)
    def _():
        op_ref[...] = out

    @pl.when(i >= n_prompt)
    def _():
        os_ref[...] = out


def _combine_ln(x1, gates, dest, yb, ln_g, ln_b, tm, t_prompt):
    t = x1.shape[0]
    n = t // tm
    n_prompt = t_prompt // tm
    row = lambda i: (i, 0)
    const = lambda i: (0, 0)
    d2 = dest.reshape(n, 1, tm * TOP_K)
    return pl.pallas_call(
        functools.partial(_combine_ln_kernel, tm=tm, n_prompt=n_prompt),
        grid=(n,),
        in_specs=[pl.BlockSpec((1, 1, tm * TOP_K), lambda i: (i, 0, 0), memory_space=pltpu.SMEM),
                  pl.BlockSpec((1, 1, tm * TOP_K), lambda i: (jnp.minimum(i + 1, n - 1), 0, 0),
                               memory_space=pltpu.SMEM),
                  pl.BlockSpec((tm, D_MODEL), row),
                  pl.BlockSpec((tm, LANES), row),
                  pl.BlockSpec((1, D_MODEL), const), pl.BlockSpec((1, D_MODEL), const),
                  pl.BlockSpec(memory_space=pl.ANY)],
        out_specs=[pl.BlockSpec((tm, D_MODEL), lambda i: (jnp.minimum(i, n_prompt - 1), 0)),
                   pl.BlockSpec((tm, D_MODEL), lambda i: (jnp.maximum(i - n_prompt, 0), 0))],
        out_shape=[jax.ShapeDtypeStruct((t_prompt, D_MODEL), F32),
                   jax.ShapeDtypeStruct((t - t_prompt, D_MODEL), F32)],
        scratch_shapes=[pltpu.VMEM((2, TOP_K, tm, D_MODEL), F32), pltpu.SemaphoreType.DMA((2,))],
        compiler_params=_params("arbitrary"),
        name="combine_ln",
    )(d2, d2, x1, gates, ln_g.reshape(1, D_MODEL), ln_b.reshape(1, D_MODEL), yb)


MOE_BM = 512
ATTN_BLK = 512
GDN_SUB = 4
ROUTE_TM = 128


def _mixers(x2d, bsz, seq, cache_k, cache_v, conv_past, s0, lam, lam_init, w_in_bf, conv_w, a_log, dt_bias,
            delta_norm_w, subln_w):
    prompt = cache_k is None
    tm = ATTN_BLK if prompt else x2d.shape[0]
    q_bf, k_f, v_f, k_bf, v_bf, cin, z, ab = _in_proj(x2d, w_in_bf, tm, prompt)
    if prompt:
        att = _attn_prompt(q_bf, k_bf, v_bf, lam, subln_w, bsz, seq, ATTN_BLK, lam_init)
    else:
        att = _attn_sample(q_bf, k_bf, v_bf, cache_k, cache_v, 0, lam, subln_w.reshape(1, HEAD), bsz, seq, lam_init)
    chunk = CHUNK if seq % CHUNK == 0 else seq
    n_sub = GDN_SUB if (seq // chunk) % GDN_SUB == 0 else 1
    past8 = jnp.pad(conv_past, ((0, 0), (SUBLANES - (CONV_W - 1), 0), (0, 0)))
    u, w, qd, kd, qk, gl = _gdn_prep(cin, ab, past8, conv_w, a_log, dt_bias, bsz, seq, chunk, n_sub)
    o, s_new = _gdn_scan(u, w, qd, kd, qk, gl, z, s0, delta_norm_w.reshape(1, HEAD), bsz, seq, chunk)
    return att, o, k_f, v_f, cin, s_new


def _moe(x1, idx, gates, rank, counts, w_gu, b_gu, w_down, b_down, ln_g, ln_b, bm, tm, t_prompt):
    t = x1.shape[0]
    n = t * TOP_K
    counts = counts[0, :N_EXPERTS]
    padded = (counts + bm - 1) // bm * bm
    pad_end = jnp.cumsum(padded).astype(jnp.int32)
    pad_start = (pad_end - padded).astype(jnp.int32)
    idx = idx[:, :TOP_K, :]
    dest = rank[:, :TOP_K, :] + sum(jnp.where(idx == e, pad_start[e], 0) for e in range(N_EXPERTS))
    dest = dest.transpose(0, 2, 1).reshape(t, TOP_K)
    n_blocks = -(-n // bm) + N_EXPERTS
    blk_start = jnp.arange(n_blocks, dtype=jnp.int32) * bm
    blk_e = jnp.minimum(jnp.sum(pad_end[None, :] <= blk_start[:, None], axis=1), N_EXPERTS - 1).astype(jnp.int32)
    n_used = (pad_end[-1] // bm).astype(jnp.int32).reshape(1)
    xb = _dispatch(x1, dest, pad_start, pad_end, n_blocks * bm, tm, bm)
    yb = _experts(xb, blk_e, n_used, w_gu, b_gu, w_down, b_down, bm)
    return _combine_ln(x1, gates, dest, yb, ln_g, ln_b, ROUTE_TM, t_prompt)


def kernel(x_prompt, x_sample, cache_k, cache_v, state_conv, state_delta, w_in, conv_w, a_log, dt_bias,
           delta_norm_w, lambda_q1, lambda_k1, lambda_q2, lambda_k2, subln_w, w_out, ln1_g, ln1_b,
           router_w, router_b, w_gu, b_gu, w_down, b_down, ln2_g, ln2_b):
    bp, lp, _ = x_prompt.shape
    bs, ls, _ = x_sample.shape
    l = 0
    lam_init = 0.8 - 0.6 * math.exp(-0.3 * l)
    lam = (jnp.exp(jnp.sum(lambda_q1[l] * lambda_k1[l])) - jnp.exp(jnp.sum(lambda_q2[l] * lambda_k2[l]))
           + lam_init).reshape(1).astype(F32)
    w_in_bf = jnp.pad(w_in[l], ((0, 0), (0, IN_COLS_PAD - IN_COLS))).astype(BF16)
    shared = (lam, lam_init, w_in_bf, conv_w[l], a_log[l], dt_bias[l], delta_norm_w[l], subln_w[l])

    xp = x_prompt.reshape(bp * lp, D_MODEL)
    xs = x_sample.reshape(bs * ls, D_MODEL)
    zero_conv = jnp.zeros((bp, CONV_W - 1, CONV_CH), F32)
    zero_s = jnp.zeros((bp, N_HEADS, HEAD, HEAD), F32)
    att_p, o_p, k_p, v_p, cin_p, s_p = _mixers(xp, bp, lp, None, None, zero_conv, zero_s, *shared)
    att_s, o_s, k_s, v_s, cin_s, s_s = _mixers(xs, bs, ls, cache_k, cache_v, state_conv[l],
                                               state_delta[l], *shared)

    x1, idx, gates, rank, counts = _out_router(((att_p, o_p, xp), (att_s, o_s, xs)), w_out[l].astype(BF16),
                                               ln1_g[l], ln1_b[l], router_w[l], router_b[l], ROUTE_TM)
    tp = bp * lp
    tm = next(c for c in (4, 3, 2, 1) if (x1.shape[0] // ROUTE_TM) % c == 0) * ROUTE_TM
    y_p, y_s = _moe(x1, idx, gates, rank, counts, w_gu[l], b_gu[l], w_down[l], b_down[l], ln2_g[l], ln2_b[l],
                    MOE_BM, tm, tp)
    conv_tail = lambda cin, b, s: cin.reshape(b, s, CONV_CH)[:, s - (CONV_W - 1):][None]
    return (y_p.reshape(bp, lp, D_MODEL), y_s.reshape(bs, ls, D_MODEL),
            k_p.reshape(1, bp, lp, N_HEADS, HEAD), v_p.reshape(1, bp, lp, N_HEADS, HEAD),
            conv_tail(cin_p, bp, lp), s_p[None].astype(state_delta.dtype),
            k_s.reshape(1, bs, ls, N_HEADS, HEAD), v_s.reshape(1, bs, ls, N_HEADS, HEAD),
            conv_tail(cin_s, bs, ls), s_s[None].astype(state_delta.dtype))
```

```python
import functools
import math

import jax
import jax.numpy as jnp
from jax import lax
from jax.experimental import pallas as pl
from jax.experimental.pallas import tpu as pltpu

F32 = jnp.float32
BF16 = jnp.bfloat16

D_MODEL = 1024
HEAD = 128
N_HEADS = 4
DQK = HEAD // 2
GROUP = N_HEADS * HEAD
CONV_W = 4
CONV_CH = 3 * GROUP
CHUNK = 64
ALIBI_MAX = 8.0
N_EXPERTS = 32
TOP_K = 4
D_FF = D_MODEL
SWIGLU_LIMIT = 7.0
SWIGLU_ALPHA = 1.702
DEPTH = 1
DEEPNORM_ALPHA = (2 * DEPTH) ** 0.25
LN_EPS = 1e-5
SUBLN_EPS = 1e-5
GATED_NORM_EPS = 1e-6
L2_EPS = 1e-6

LANES = 128
SUBLANES = 8
BF16_EXACT_INT = 256
BF16_ROWS = 16
LOG2E = 1.4426950408889634
N_POS = 6
ONES_ROWS = BF16_ROWS
N_DMA_PRIORITIES = 2
VMEM_LIMIT = 56 * 1024 * 1024

COL_Q, COL_K, COL_V, COL_CONV = 0, GROUP, 2 * GROUP, 3 * GROUP
COL_Z = COL_CONV + CONV_CH
COL_AB = COL_Z + GROUP
IN_COLS = COL_AB + 2 * N_HEADS
IN_COLS_PAD = COL_AB + LANES


def _params(*sem):
    return pltpu.CompilerParams(dimension_semantics=sem, vmem_limit_bytes=VMEM_LIMIT)


def _in_proj_kernel(x_ref, w_ref, q_ref, kf_ref, vf_ref, kb_ref, vb_ref, c_ref, z_ref, ab_ref, *, tm, transposed):
    xb = x_ref[...].astype(BF16)

    def section(lo, hi):
        return jnp.dot(xb, w_ref[:, lo:hi], preferred_element_type=F32)

    q = section(COL_Q, COL_K) * (DQK ** -0.5 * (LOG2E if transposed else 1.0))
    k = section(COL_K, COL_V)
    kf_ref[...] = k
    kb_ref[...] = k.astype(BF16)
    v = section(COL_V, COL_CONV)
    vf_ref[...] = v
    if transposed:
        q_ref[0] = q.T.astype(BF16)
        vb_ref[0] = v.T.astype(BF16)
    else:
        q_ref[...] = q.astype(BF16)
        vb_ref[...] = v.astype(BF16)
    c_ref[...] = section(COL_CONV, COL_Z)
    z_ref[...] = section(COL_Z, COL_AB)
    ab_ref[...] = section(COL_AB, IN_COLS_PAD)


def _in_proj(x2d, w_bf, tm, transposed):
    t = x2d.shape[0]
    row = lambda i: (i, 0)
    widths = (GROUP, GROUP, GROUP, GROUP, GROUP, CONV_CH, GROUP, LANES)
    dtypes = (BF16, F32, F32, BF16, BF16, F32, F32, F32)
    out_specs = [pl.BlockSpec((tm, w), row) for w in widths]
    out_shape = [jax.ShapeDtypeStruct((t, w), d) for w, d in zip(widths, dtypes)]
    if transposed:
        for slot in (0, 4):
            out_specs[slot] = pl.BlockSpec((1, GROUP, tm), lambda i: (i, 0, 0))
            out_shape[slot] = jax.ShapeDtypeStruct((t // tm, GROUP, tm), BF16)
    return pl.pallas_call(
        functools.partial(_in_proj_kernel, tm=tm, transposed=transposed),
        grid=(t // tm,),
        in_specs=[pl.BlockSpec((tm, D_MODEL), row),
                  pl.BlockSpec((D_MODEL, IN_COLS_PAD), lambda i: (0, 0))],
        out_specs=out_specs,
        out_shape=out_shape,
        compiler_params=_params("parallel"),
        name="in_proj",
    )(x2d, w_bf)


def _alibi_slopes():
    return [2.0 ** (-ALIBI_MAX * (h + 1) / N_HEADS) for h in range(N_HEADS)]


def _stack_halves(q):
    lane = lax.broadcasted_iota(jnp.int32, q.shape, 1)
    zero = jnp.zeros_like(q)
    return jnp.concatenate([jnp.where(lane < DQK, q, zero), jnp.where(lane < DQK, zero, q)], axis=0)


def _diff_norm(acc, l, lam, w, lam_init, rows):
    o = acc[:rows] / l[:rows] - lam * (acc[rows:] / l[rows:])
    ms = jnp.mean(o * o, axis=-1, keepdims=True)
    return o * lax.rsqrt(ms + SUBLN_EPS) * w * (1.0 - lam_init)


def _head_slope(h):
    s = _alibi_slopes()
    return jnp.where(h == 0, s[0], jnp.where(h == 1, s[1], jnp.where(h == 2, s[2], s[3]))).astype(F32)


def _attn_prompt_kernel(lam_ref, qt_ref, k_ref, vt_ref, w_ref, o_ref,
                        diag_ref, kaug_ref, vaug_ref, qz_ref, s0_ref, s1_ref, p0_ref, p1_ref, mx0_ref, mx1_ref, m_ref, acc_ref, *, blk, lam_init):
    i = pl.program_id(2)
    slope = _head_slope(pl.program_id(1)) * LOG2E
    rows = 2 * blk
    n_kv = k_ref.shape[0] // blk

    def pieces(x):
        lo = x % BF16_EXACT_INT
        return _split3(slope * lo.astype(F32)) + _split3(slope * (x - lo).astype(F32))

    @pl.when(i == 0)
    def _():
        lane = lax.broadcasted_iota(jnp.int32, (blk, HEAD), 1)
        extra = jnp.where(lane < N_POS, 1.0, 0.0).astype(BF16)
        for n, piece in enumerate(pieces(lax.broadcasted_iota(jnp.int32, (blk, 1), 0))):
            extra = jnp.where(lane == N_POS + n, piece, extra)
        ones = jnp.ones((ONES_ROWS, blk), BF16)
        c = lax.broadcasted_iota(jnp.int32, (blk, rows), 0)
        a = lax.broadcasted_iota(jnp.int32, (blk, rows), 1) % blk
        diag_ref[...] = jnp.where(c // CHUNK <= a // CHUNK, -slope * jnp.abs(a - c).astype(F32), -jnp.inf)
        r = lax.broadcasted_iota(jnp.int32, (HEAD, rows), 0)
        qx = jnp.where((r >= N_POS) & (r < 2 * N_POS), 1.0, 0.0).astype(BF16)
        for n, piece in enumerate(pieces(-(lax.broadcasted_iota(jnp.int32, (1, rows), 1) % blk))):
            qx = jnp.where(r == n, piece, qx)
        qz_ref[HEAD:, :] = qx

        def fill(j, carry):
            j0 = pl.multiple_of(j * blk, blk)
            kaug_ref[pl.ds(j0, blk), :HEAD] = k_ref[pl.ds(j0, blk), :]
            kaug_ref[pl.ds(j0, blk), HEAD:] = extra
            vaug_ref[j, :HEAD, :] = vt_ref[j]
            vaug_ref[j, HEAD:, :] = ones
            return carry

        lax.fori_loop(0, n_kv, fill, 0)

    qt = qt_ref[0]
    d = lax.broadcasted_iota(jnp.int32, qt.shape, 0)
    zero = jnp.zeros_like(qt)
    qz_ref[:HEAD, :] = jnp.concatenate([jnp.where(d < DQK, qt, zero), jnp.where(d < DQK, zero, qt)], axis=1)

    def tile_rows(j):
        return pl.ds(pl.multiple_of(j * blk, blk), blk)

    s = jnp.dot(k_ref[tile_rows(i), :], qz_ref[:HEAD, :], preferred_element_type=F32) + diag_ref[...]
    m = jnp.max(s, axis=0, keepdims=True)
    p1_ref[...] = jnp.exp2(s - m).astype(BF16)
    m_ref[...] = m
    acc_ref[...] = jnp.zeros_like(acc_ref)
    s_first = jnp.dot(kaug_ref[tile_rows(0), :], qz_ref[...], preferred_element_type=F32)
    s0_ref[...] = s_first
    mx0_ref[...] = jnp.max(s_first, axis=0, keepdims=True)

    def step(j, s_cur, s_nxt, p_cur, p_nxt, mx_cur, mx_nxt):
        s_next = jnp.dot(kaug_ref[tile_rows(jnp.minimum(j + 1, i - 1)), :], qz_ref[...],
                         preferred_element_type=F32)
        s_nxt[...] = s_next
        mx_nxt[...] = jnp.max(s_next, axis=0, keepdims=True)
        pv = jnp.dot(vaug_ref[jnp.where(j == 0, i, j - 1)], p_nxt[...], preferred_element_type=F32)
        shift = -slope * ((i - j) * blk).astype(F32)
        m_old = m_ref[...]
        m_new = jnp.maximum(m_old, mx_cur[...] + shift)
        p_cur[...] = jnp.exp2(s_cur[...] - (m_new - shift)).astype(BF16)
        acc_ref[...] = (acc_ref[...] + pv) * jnp.exp2(m_old - m_new)
        m_ref[...] = m_new

    def body(jj, carry):
        step(2 * jj, s0_ref, s1_ref, p0_ref, p1_ref, mx0_ref, mx1_ref)

        @pl.when(2 * jj + 1 < i)
        def _():
            step(2 * jj + 1, s1_ref, s0_ref, p1_ref, p0_ref, mx1_ref, mx0_ref)

        return carry

    lax.fori_loop(0, (i + 1) // 2, body, 0)
    p_last = jnp.where(i % 2 == 1, p0_ref[...], p1_ref[...])
    acc = acc_ref[...] + jnp.dot(vaug_ref[jnp.where(i > 0, i - 1, i)], p_last, preferred_element_type=F32)
    l = acc[HEAD:HEAD + 1, :]
    num = acc[:HEAD, :]
    ot = num[:, :blk] / l[:, :blk] - lam_ref[0] * (num[:, blk:] / l[:, blk:])
    ms = jnp.mean(ot * ot, axis=0, keepdims=True)
    ot = ot * lax.rsqrt(ms + SUBLN_EPS) * w_ref[...] * (1.0 - lam_init)
    o_ref[...] = ot.T.astype(o_ref.dtype)


def _attn_prompt(qt_bf, k_bf, vt_bf, lam, subln_w, bsz, seq, blk, lam_init):
    nq = seq // blk
    kern = functools.partial(_attn_prompt_kernel, blk=blk, lam_init=lam_init)
    return pl.pallas_call(
        kern,
        grid=(bsz, N_HEADS, nq),
        in_specs=[pl.BlockSpec(memory_space=pltpu.SMEM),
                  pl.BlockSpec((1, HEAD, blk), lambda b, h, i: (b * nq + i, h, 0)),
                  pl.BlockSpec((seq, HEAD), lambda b, h, i: (b, h)),
                  pl.BlockSpec((nq, HEAD, blk), lambda b, h, i: (b, h, 0)),
                  pl.BlockSpec((HEAD, 1), lambda b, h, i: (0, 0))],
        out_specs=pl.BlockSpec((blk, HEAD), lambda b, h, i: (b * nq + i, h)),
        scratch_shapes=[pltpu.VMEM((blk, 2 * blk), F32),
                        pltpu.VMEM((seq, 2 * HEAD), BF16),
                        pltpu.VMEM((nq, HEAD + ONES_ROWS, blk), BF16),
                        pltpu.VMEM((2 * HEAD, 2 * blk), BF16),
                        pltpu.VMEM((blk, 2 * blk), F32), pltpu.VMEM((blk, 2 * blk), F32),
                        pltpu.VMEM((blk, 2 * blk), BF16), pltpu.VMEM((blk, 2 * blk), BF16),
                        pltpu.VMEM((1, 2 * blk), F32), pltpu.VMEM((1, 2 * blk), F32),
                        pltpu.VMEM((1, 2 * blk), F32),
                        pltpu.VMEM((HEAD + ONES_ROWS, 2 * blk), F32)],
        out_shape=jax.ShapeDtypeStruct((bsz * seq, GROUP), BF16),
        compiler_params=_params("parallel", "parallel", "arbitrary"),
        name="attn_prompt",
    )(lam, qt_bf, k_bf, vt_bf, subln_w.reshape(HEAD, 1))


def _attn_sample_kernel(lam_ref, q_ref, kn_ref, vn_ref, kc_ref, vc_ref, w_ref, o_ref, *, seq, past, lam_init):
    nt = (((1,), (1,)), ((), ()))
    qpos = past + lax.broadcasted_iota(jnp.int32, (2 * seq, 1), 0) % seq
    rel_c = jnp.abs(qpos - lax.broadcasted_iota(jnp.int32, (1, past), 1)).astype(F32)
    rel_n = jnp.abs(qpos - (past + lax.broadcasted_iota(jnp.int32, (1, seq), 1))).astype(F32)
    for h, slope in enumerate(_alibi_slopes()):
        col = slice(h * HEAD, (h + 1) * HEAD)
        qz = _stack_halves(q_ref[:, col])
        s_c = lax.dot_general(qz, kc_ref[0, 0, :, h, :].astype(BF16), nt, preferred_element_type=F32)
        s_c = s_c - slope * rel_c
        s_n = lax.dot_general(qz, kn_ref[:, col], nt, preferred_element_type=F32) - slope * rel_n
        m = jnp.maximum(jnp.max(s_c, axis=-1, keepdims=True), jnp.max(s_n, axis=-1, keepdims=True))
        p_c = jnp.exp(s_c - m)
        p_n = jnp.exp(s_n - m)
        l = jnp.sum(p_c, axis=-1, keepdims=True) + jnp.sum(p_n, axis=-1, keepdims=True)
        acc = (jnp.dot(p_c.astype(BF16), vc_ref[0, 0, :, h, :].astype(BF16), preferred_element_type=F32)
               + jnp.dot(p_n.astype(BF16), vn_ref[:, col], preferred_element_type=F32))
        o_ref[:, col] = _diff_norm(acc, l, lam_ref[0], w_ref[...], lam_init, seq).astype(o_ref.dtype)


def _attn_sample(q_bf, k_bf, v_bf, cache_k, cache_v, layer, lam, subln_w, bsz, seq, lam_init):
    past = cache_k.shape[2]
    kern = functools.partial(_attn_sample_kernel, seq=seq, past=past, lam_init=lam_init)
    new = pl.BlockSpec((seq, GROUP), lambda b: (b, 0))
    cache = pl.BlockSpec((1, 1, past, N_HEADS, HEAD), lambda b: (layer, b, 0, 0, 0))
    return pl.pallas_call(
        kern,
        grid=(bsz,),
        in_specs=[pl.BlockSpec(memory_space=pltpu.SMEM), new, new, new, cache, cache,
                  pl.BlockSpec((1, HEAD), lambda b: (0, 0))],
        out_specs=new,
        out_shape=jax.ShapeDtypeStruct((bsz * seq, GROUP), BF16),
        compiler_params=_params("parallel"),
        name="attn_sample",
    )(lam, q_bf, k_bf, v_bf, cache_k, cache_v, subln_w)


def _split3(x):
    hi = x.astype(BF16)
    r1 = x - hi.astype(F32)
    mid = r1.astype(BF16)
    lo = (r1 - mid.astype(F32)).astype(BF16)
    return hi, mid, lo


def _gdn_prep_kernel(cin_ref, halo_ref, past_ref, ab_ref, cw_ref, alog_ref, dtb_ref,
                     u_ref, w_ref, qd_ref, kd_ref, qk_ref, gl_ref, *, chunk, n_sub):
    c_idx = pl.program_id(1)
    rows = chunk * n_sub
    prev = jnp.where(c_idx == 0, past_ref[0], halo_ref[...])
    xin = jnp.concatenate([prev, cin_ref[...]], axis=0)
    conv = sum(xin[SUBLANES - (CONV_W - 1) + j: SUBLANES - (CONV_W - 1) + j + rows] * cw_ref[j:j + 1, :]
               for j in range(CONV_W))
    conv = conv * jax.nn.sigmoid(conv)

    ab = ab_ref[...]
    lane = lax.broadcasted_iota(jnp.int32, ab.shape, 1)
    pre = ab + dtb_ref[...]
    softplus = jnp.maximum(pre, 0.0) + jnp.log(1.0 + jnp.exp(-jnp.abs(pre)))
    g = jnp.where(lane < N_HEADS, -jnp.exp(alog_ref[...]) * softplus, 0.0)
    beta_all = jax.nn.sigmoid(ab)

    ri = lax.broadcasted_iota(jnp.int32, (rows, rows), 0)
    ci = lax.broadcasted_iota(jnp.int32, (rows, rows), 1)
    same = (ri // chunk) == (ci // chunk)
    incl = same & (ri >= ci)
    strict = same & (ri > ci)
    eye = jnp.where(ri == ci, 1.0, 0.0).astype(F32)
    nt = (((1,), (1,)), ((), ()))
    g_parts = _split3(g)
    ones_incl = jnp.where(incl, 1.0, 0.0).astype(BF16)
    ones_same = jnp.where(same, 1.0, 0.0).astype(BF16)
    gc = sum(jnp.dot(ones_incl, part, preferred_element_type=F32) for part in g_parts)
    g_end = sum(jnp.dot(ones_same, part, preferred_element_type=F32) for part in g_parts)
    gct = gc.T
    for sc in range(n_sub):
        gl_ref[sc] = g_end[sc * chunk:sc * chunk + 1, :]

    t_mats, p_mats, rhs = [], [], []
    for h in range(N_HEADS):
        col = slice(h * HEAD, (h + 1) * HEAD)
        qh = conv[:, h * HEAD:(h + 1) * HEAD]
        kh = conv[:, GROUP + h * HEAD:GROUP + (h + 1) * HEAD]
        vh = conv[:, 2 * GROUP + h * HEAD:2 * GROUP + (h + 1) * HEAD]
        qh = qh * lax.rsqrt(jnp.sum(qh * qh, axis=-1, keepdims=True) + L2_EPS) * (HEAD ** -0.5)
        kh = kh * lax.rsqrt(jnp.sum(kh * kh, axis=-1, keepdims=True) + L2_EPS)
        beta = beta_all[:, N_HEADS + h:N_HEADS + h + 1]
        gcol = gc[:, h:h + 1]
        grow = gct[h:h + 1, :]
        gamma = jnp.exp(jnp.where(incl, gcol - grow, -jnp.inf))
        egc = jnp.exp(gcol)
        kb = kh * beta
        khb = kh.astype(BF16)
        a = jnp.where(strict, lax.dot_general(kb.astype(BF16), khb, nt, preferred_element_type=F32) * gamma, 0.0)
        qk = (lax.dot_general(qh.astype(BF16), khb, nt, preferred_element_type=F32) * gamma).astype(BF16)
        for sc in range(n_sub):
            blk = slice(sc * chunk, (sc + 1) * chunk)
            qk_ref[h, blk, :] = qk[blk, blk]
        qd_ref[:, col] = (qh * egc).astype(BF16)
        kd_ref[:, col] = (kh * jnp.exp(g_end[:, h:h + 1] - gcol)).astype(BF16)
        t_mats.append(eye - a)
        p_mats.append(a)
        rhs.append(((vh * beta).astype(BF16), (kb * egc).astype(BF16)))

    for _ in range(int(math.log2(chunk)) - 1):
        for h in range(N_HEADS):
            pb = p_mats[h].astype(BF16)
            p_mats[h] = jnp.dot(pb, pb, preferred_element_type=F32)
        for h in range(N_HEADS):
            t_mats[h] = t_mats[h] + jnp.dot(t_mats[h].astype(BF16), p_mats[h].astype(BF16),
                                            preferred_element_type=F32)

    for h in range(N_HEADS):
        col = slice(h * HEAD, (h + 1) * HEAD)
        t_inv = t_mats[h].astype(BF16)
        u_ref[:, col] = jnp.dot(t_inv, rhs[h][0], preferred_element_type=F32)
        w_ref[:, col] = jnp.dot(t_inv, rhs[h][1], preferred_element_type=F32).astype(BF16)


def _gdn_prep(cin, ab, past8, conv_w, a_log, dt_bias, bsz, seq, chunk, n_sub):
    rows = chunk * n_sub
    nblk = seq // rows
    t = bsz * seq
    lanes = lambda v: jnp.pad(v.reshape(1, N_HEADS).astype(F32), ((0, 0), (0, LANES - N_HEADS)))
    kern = functools.partial(_gdn_prep_kernel, chunk=chunk, n_sub=n_sub)
    rowblk = lambda b, c: (b * nblk + c, 0)
    halo = lambda b, c: (jnp.maximum((b * nblk + c) * (rows // SUBLANES) - 1, 0), 0)
    const = lambda b, c: (0, 0)
    return pl.pallas_call(
        kern,
        grid=(bsz, nblk),
        in_specs=[pl.BlockSpec((rows, CONV_CH), rowblk),
                  pl.BlockSpec((SUBLANES, CONV_CH), halo),
                  pl.BlockSpec((1, SUBLANES, CONV_CH), lambda b, c: (b, 0, 0)),
                  pl.BlockSpec((rows, LANES), rowblk),
                  pl.BlockSpec((CONV_W, CONV_CH), const),
                  pl.BlockSpec((1, LANES), const),
                  pl.BlockSpec((1, LANES), const)],
        out_specs=[pl.BlockSpec((rows, GROUP), rowblk),
                   pl.BlockSpec((rows, GROUP), rowblk),
                   pl.BlockSpec((rows, GROUP), rowblk),
                   pl.BlockSpec((rows, GROUP), rowblk),
                   pl.BlockSpec((N_HEADS, rows, chunk), lambda b, c: (0, b * nblk + c, 0)),
                   pl.BlockSpec((n_sub, 1, LANES), lambda b, c: (b * nblk + c, 0, 0))],
        out_shape=[jax.ShapeDtypeStruct((t, GROUP), F32),
                   jax.ShapeDtypeStruct((t, GROUP), BF16),
                   jax.ShapeDtypeStruct((t, GROUP), BF16),
                   jax.ShapeDtypeStruct((t, GROUP), BF16),
                   jax.ShapeDtypeStruct((N_HEADS, t, chunk), BF16),
                   jax.ShapeDtypeStruct((t // chunk, 1, LANES), F32)],
        compiler_params=_params("parallel", "parallel"),
        name="gdn_prep",
    )(cin, cin, past8, ab, conv_w, lanes(a_log), lanes(dt_bias))


def _gdn_scan_kernel(u_ref, w_ref, qd_ref, kd_ref, qk_ref, gl_ref, z_ref, s0_ref, nw_ref,
                     o_ref, sf_ref, s_ref, *, bsz, chunk):
    c_idx = pl.program_id(0)

    @pl.when(c_idx == 0)
    def _():
        s_ref[...] = s0_ref[...]

    tn = (((0,), (0,)), ((), ()))
    chains = [(b, h, slice(h * HEAD, (h + 1) * HEAD)) for b in range(bsz) for h in range(N_HEADS)]
    states = [s_ref[b, h] for b, h, _ in chains]
    prods = [jnp.dot(jnp.concatenate([w_ref[b, :, col], qd_ref[b, :, col]], axis=0), s.astype(BF16),
                     preferred_element_type=F32) for (b, _, col), s in zip(chains, states)]
    v_news = [(u_ref[b, :, col] - r[:chunk]).astype(BF16) for (b, _, col), r in zip(chains, prods)]
    outs = [r[chunk:] + jnp.dot(qk_ref[h, b], v, preferred_element_type=F32)
            for (b, h, _), r, v in zip(chains, prods, v_news)]
    for (b, h, col), s, v in zip(chains, states, v_news):
        glast = jnp.exp(gl_ref[b, 0])
        s_ref[b, h] = s * glast[:, h:h + 1] + lax.dot_general(kd_ref[b, :, col], v, tn,
                                                             preferred_element_type=F32)
    for (b, h, col), o in zip(chains, outs):
        zh = z_ref[b, :, col]
        ms = jnp.mean(o * o, axis=-1, keepdims=True)
        o = o * lax.rsqrt(ms + GATED_NORM_EPS) * nw_ref[...] * (zh * jax.nn.sigmoid(zh))
        o_ref[b, :, col] = o.astype(o_ref.dtype)

    @pl.when(c_idx == pl.num_programs(0) - 1)
    def _():
        sf_ref[...] = s_ref[...]


def _gdn_scan(u, w, qd, kd, qk, gl, z, s0, norm_w, bsz, seq, chunk):
    nc = seq // chunk
    kern = functools.partial(_gdn_scan_kernel, bsz=bsz, chunk=chunk)
    tok = pl.BlockSpec((bsz, chunk, GROUP), lambda c: (0, c, 0))
    state = pl.BlockSpec((bsz, N_HEADS, HEAD, HEAD), lambda c: (0, 0, 0, 0))
    o, s_final = pl.pallas_call(
        kern,
        grid=(nc,),
        in_specs=[tok, tok, tok, tok,
                  pl.BlockSpec((N_HEADS, bsz, chunk, chunk), lambda c: (0, 0, c, 0)),
                  pl.BlockSpec((bsz, 1, 1, LANES), lambda c: (0, c, 0, 0)),
                  tok, state,
                  pl.BlockSpec((1, HEAD), lambda c: (0, 0))],
        out_specs=[tok, state],
        out_shape=[jax.ShapeDtypeStruct((bsz, seq, GROUP), BF16),
                   jax.ShapeDtypeStruct((bsz, N_HEADS, HEAD, HEAD), F32)],
        scratch_shapes=[pltpu.VMEM((bsz, N_HEADS, HEAD, HEAD), F32)],
        compiler_params=_params("arbitrary"),
        name="gdn_scan",
    )(u.reshape(bsz, seq, GROUP), w.reshape(bsz, seq, GROUP), qd.reshape(bsz, seq, GROUP),
      kd.reshape(bsz, seq, GROUP), qk.reshape(N_HEADS, bsz, seq, chunk), gl.reshape(bsz, nc, 1, LANES),
      z.reshape(bsz, seq, GROUP), s0, norm_w)
    return o.reshape(bsz * seq, GROUP), s_final


def _layernorm(x, g, b):
    mu = jnp.mean(x, axis=-1, keepdims=True)
    xc = x - mu
    var = jnp.mean(xc * xc, axis=-1, keepdims=True)
    return xc * lax.rsqrt(var + LN_EPS) * g + b


def _out_router_kernel(attp_ref, atts_ref, op_ref, os_ref, xp_ref, xs_ref, wo_ref, g_ref, b_ref, rw_ref, rb_ref,
                       x1_ref, idx_ref, gate_ref, rank_ref, cnt_ref, carry_ref, *, tm, n_prompt):
    step = pl.program_id(0)

    @pl.when(step == 0)
    def _():
        carry_ref[...] = jnp.zeros_like(carry_ref)

    prompt = step < n_prompt
    att = jnp.where(prompt, attp_ref[...], atts_ref[...])
    o = jnp.where(prompt, op_ref[...], os_ref[...])
    x = jnp.where(prompt, xp_ref[...], xs_ref[...])
    mix = (jnp.dot(att, wo_ref[:GROUP, :], preferred_element_type=F32)
           + jnp.dot(o, wo_ref[GROUP:, :], preferred_element_type=F32))
    x1 = _layernorm(DEEPNORM_ALPHA * x + mix, g_ref[...], b_ref[...])
    x1_ref[...] = x1

    x_hi = x1.astype(BF16)
    x_lo = (x1 - x_hi.astype(F32)).astype(BF16)
    logits = (jnp.dot(x_hi, rw_ref[0], preferred_element_type=F32)
              + jnp.dot(x_lo, rw_ref[0], preferred_element_type=F32)
              + jnp.dot(x_hi, rw_ref[1], preferred_element_type=F32)) + rb_ref[...]
    lane = lax.broadcasted_iota(jnp.int32, logits.shape, 1)
    work = jnp.where(lane < N_EXPERTS, logits, -jnp.inf)
    vals, idxs = [], []
    for _ in range(TOP_K):
        m = jnp.max(work, axis=-1, keepdims=True)
        am = jnp.min(jnp.where(work == m, lane, LANES), axis=-1, keepdims=True)
        vals.append(m)
        idxs.append(am)
        work = jnp.where(lane == am, -jnp.inf, work)
    exps = [jnp.exp(v - vals[0]) for v in vals]
    denom = exps[0] + exps[1] + exps[2] + exps[3]
    chosen = jnp.zeros(logits.shape, F32)
    gate_out = jnp.zeros(logits.shape, F32)
    idx_out = jnp.zeros(logits.shape, jnp.int32)
    for k in range(TOP_K):
        chosen = jnp.where(lane == idxs[k], 1.0, chosen)
        gate_out = jnp.where(lane == k, exps[k] / denom, gate_out)
        idx_out = jnp.where(lane == k, idxs[k], idx_out)
    ri = lax.broadcasted_iota(jnp.int32, (tm, tm), 0)
    ci = lax.broadcasted_iota(jnp.int32, (tm, tm), 1)
    before = jnp.where(ri > ci, 1.0, 0.0).astype(BF16)
    prefix = jnp.dot(before, chosen.astype(BF16), preferred_element_type=F32) + carry_ref[...]
    rank_out = jnp.zeros(logits.shape, F32)
    for k in range(TOP_K):
        r = jnp.sum(jnp.where(lane == idxs[k], prefix, 0.0), axis=-1, keepdims=True)
        rank_out = jnp.where(lane == k, r, rank_out)
    carry_ref[...] = carry_ref[...] + jnp.sum(chosen, axis=0, keepdims=True)
    idx_ref[0] = idx_out.T[:SUBLANES, :]
    gate_ref[...] = gate_out
    rank_ref[0] = rank_out.astype(jnp.int32).T[:SUBLANES, :]
    cnt_ref[...] = carry_ref[...].astype(jnp.int32)


def _out_router(streams, w_out_bf, ln_g, ln_b, router_w, router_b, tm):
    (att_p, o_p, x_p), (att_s, o_s, x_s) = streams
    n_prompt = x_p.shape[0] // tm
    t = x_p.shape[0] + x_s.shape[0]
    row = lambda i: (i, 0)
    const = lambda i: (0, 0)
    prow = lambda i: (jnp.minimum(i, n_prompt - 1), 0)
    srow = lambda i: (jnp.maximum(i - n_prompt, 0), 0)
    slots = pl.BlockSpec((1, SUBLANES, tm), lambda i: (i, 0, 0))
    rw = jnp.pad(router_w, ((0, 0), (0, LANES - N_EXPERTS)))
    rw_hi = rw.astype(BF16)
    rw = jnp.stack([rw_hi, (rw - rw_hi.astype(F32)).astype(BF16)])
    rb = jnp.pad(router_b.reshape(1, N_EXPERTS), ((0, 0), (0, LANES - N_EXPERTS)))
    kern = functools.partial(_out_router_kernel, tm=tm, n_prompt=n_prompt)
    return pl.pallas_call(
        kern,
        grid=(t // tm,),
        in_specs=[pl.BlockSpec((tm, GROUP), prow), pl.BlockSpec((tm, GROUP), srow),
                  pl.BlockSpec((tm, GROUP), prow), pl.BlockSpec((tm, GROUP), srow),
                  pl.BlockSpec((tm, D_MODEL), prow), pl.BlockSpec((tm, D_MODEL), srow),
                  pl.BlockSpec((2 * GROUP, D_MODEL), const),
                  pl.BlockSpec((1, D_MODEL), const), pl.BlockSpec((1, D_MODEL), const),
                  pl.BlockSpec((2, D_MODEL, LANES), lambda i: (0, 0, 0)), pl.BlockSpec((1, LANES), const)],
        out_specs=[pl.BlockSpec((tm, D_MODEL), row), slots, pl.BlockSpec((tm, LANES), row), slots,
                   pl.BlockSpec((1, LANES), const)],
        out_shape=[jax.ShapeDtypeStruct((t, D_MODEL), F32),
                   jax.ShapeDtypeStruct((t // tm, SUBLANES, tm), jnp.int32), jax.ShapeDtypeStruct((t, LANES), F32),
                   jax.ShapeDtypeStruct((t // tm, SUBLANES, tm), jnp.int32),
                   jax.ShapeDtypeStruct((1, LANES), jnp.int32)],
        scratch_shapes=[pltpu.VMEM((1, LANES), F32)],
        compiler_params=_params("arbitrary"),
        name="out_router",
    )(att_p, att_s, o_p, o_s, x_p, x_s, w_out_bf, ln_g.reshape(1, D_MODEL), ln_b.reshape(1, D_MODEL), rw, rb)


def _row_copy(src, src_row, dst, dst_row, sem):
    return pltpu.make_async_copy(src.at[pl.ds(src_row, 1), :], dst.at[pl.ds(dst_row, 1), :], sem)


def _dispatch_kernel(ps_ref, pe_ref, dest_ref, x_ref, xb_hbm, zero_ref, sem, zsem, *, tm, bm):
    i = pl.program_id(0)

    def zero_block(row):
        return pltpu.make_async_copy(zero_ref, xb_hbm.at[pl.ds(pl.multiple_of(row, bm), bm), :], zsem)

    def fill(e):
        return zero_block(pe_ref[e] - bm)

    @pl.when(i == 0)
    def _():
        zero_ref[...] = jnp.zeros_like(zero_ref)
        for e in range(N_EXPERTS):
            @pl.when(pe_ref[e] > ps_ref[e])
            def _():
                fill(e).start()
        first_unused = pe_ref[N_EXPERTS - 1] // bm
        n_blocks = xb_hbm.shape[0] // bm
        lax.fori_loop(first_unused, n_blocks, lambda b, c: (zero_block(b * bm).start(), c)[1], 0)
        for e in range(N_EXPERTS):
            @pl.when(pe_ref[e] > ps_ref[e])
            def _():
                fill(e).wait()
        lax.fori_loop(first_unused, n_blocks, lambda b, c: (zero_block(b * bm).wait(), c)[1], 0)

    def issue(r, carry):
        for k in range(TOP_K):
            _row_copy(x_ref, r, xb_hbm, dest_ref[0, 0, r * TOP_K + k], sem).start(priority=k % N_DMA_PRIORITIES)
        return carry

    lax.fori_loop(0, tm, issue, 0)
    pltpu.make_async_copy(xb_hbm.at[pl.ds(0, tm * TOP_K), :], xb_hbm.at[pl.ds(0, tm * TOP_K), :], sem).wait()


def _dispatch(x1, dest, pad_start, pad_end, rows, tm, bm):
    t = x1.shape[0]
    return pl.pallas_call(
        functools.partial(_dispatch_kernel, tm=tm, bm=bm),
        grid_spec=pltpu.PrefetchScalarGridSpec(
            num_scalar_prefetch=2,
            grid=(t // tm,),
            in_specs=[pl.BlockSpec((1, 1, tm * TOP_K), lambda i, ps, pe: (i, 0, 0), memory_space=pltpu.SMEM),
                      pl.BlockSpec((tm, D_MODEL), lambda i, ps, pe: (i, 0))],
            out_specs=pl.BlockSpec(memory_space=pl.ANY),
            scratch_shapes=[pltpu.VMEM((bm, D_MODEL), F32),
                            pltpu.SemaphoreType.DMA(()), pltpu.SemaphoreType.DMA(())]),
        out_shape=jax.ShapeDtypeStruct((rows, D_MODEL), F32),
        compiler_params=_params("arbitrary"),
        name="dispatch",
    )(pad_start, pad_end, dest.reshape(t // tm, 1, tm * TOP_K), x1)


def _expert_kernel(be_ref, first_ref, slot_ref, nxt_ref, nb_ref, x_ref, wgu_hbm, bgu_ref, wd_hbm, bd_ref, y_ref,
                   wgu_buf, wd_buf, sem):
    blk = pl.program_id(0)

    def fetch(e, s):
        return (pltpu.make_async_copy(wgu_hbm.at[e], wgu_buf.at[s], sem.at[s]),
                pltpu.make_async_copy(wd_hbm.at[e], wd_buf.at[s], sem.at[s]))

    @pl.when(blk < nb_ref[0])
    def _():
        e, s = be_ref[blk], slot_ref[blk]

        @pl.when(blk == 0)
        def _():
            for copy in fetch(e, s):
                copy.start()

        @pl.when(first_ref[blk] == 1)
        def _():
            for copy in fetch(e, s):
                copy.wait()

            @pl.when(nxt_ref[blk] >= 0)
            def _():
                for copy in fetch(nxt_ref[blk], 1 - s):
                    copy.start()

        x = x_ref[...].astype(BF16)
        h = jnp.dot(x, wgu_buf[s].astype(BF16), preferred_element_type=F32) + bgu_ref[0]
        gate = jnp.minimum(h[:, :D_FF], SWIGLU_LIMIT)
        up = jnp.clip(h[:, D_FF:], -SWIGLU_LIMIT, SWIGLU_LIMIT)
        act = (up + 1.0) * (gate * jax.nn.sigmoid(SWIGLU_ALPHA * gate))
        y_ref[...] = jnp.dot(act.astype(BF16), wd_buf[s].astype(BF16), preferred_element_type=F32) + bd_ref[0]

    @pl.when(blk >= nb_ref[0])
    def _():
        y_ref[...] = jnp.zeros_like(y_ref)


def _experts(xb, blk_e, blk_first, blk_slot, blk_next, n_used, w_gu, b_gu, w_down, b_down, bm):
    rows = xb.shape[0]
    n_blocks = rows // bm
    used = lambda i, be, *_: (jnp.maximum(jnp.minimum(i, _[-1][0] - 1), 0), 0)
    bias = lambda i, be, *_: (be[i], 0, 0)
    return pl.pallas_call(
        _expert_kernel,
        grid_spec=pltpu.PrefetchScalarGridSpec(
            num_scalar_prefetch=5,
            grid=(n_blocks,),
            in_specs=[pl.BlockSpec((bm, D_MODEL), used),
                      pl.BlockSpec(memory_space=pl.ANY),
                      pl.BlockSpec((1, 1, 2 * D_FF), bias),
                      pl.BlockSpec(memory_space=pl.ANY),
                      pl.BlockSpec((1, 1, D_MODEL), bias)],
            out_specs=pl.BlockSpec((bm, D_MODEL), lambda i, *_: (i, 0)),
            scratch_shapes=[pltpu.VMEM((2, D_MODEL, 2 * D_FF), F32), pltpu.VMEM((2, D_FF, D_MODEL), F32),
                            pltpu.SemaphoreType.DMA((2,))]),
        out_shape=jax.ShapeDtypeStruct((rows, D_MODEL), F32),
        compiler_params=_params("arbitrary"),
        name="experts",
    )(blk_e, blk_first, blk_slot, blk_next, n_used, xb, w_gu, b_gu.reshape(N_EXPERTS, 1, 2 * D_FF),
      w_down, b_down.reshape(N_EXPERTS, 1, D_MODEL))


def _combine_ln_kernel(dcur_ref, dnext_ref, x_ref, gate_ref, g_ref, b_ref, yb_hbm, op_ref, os_ref, ybuf, sem, *,
                       tm, n_prompt):
    i = pl.program_id(0)
    slot = i % 2

    def issue(dref, s):
        def body(r, carry):
            for k in range(TOP_K):
                _row_copy(yb_hbm, dref[0, 0, r * TOP_K + k], ybuf.at[s, k], r,
                          sem.at[s]).start(priority=k % N_DMA_PRIORITIES)
            return carry
        lax.fori_loop(0, tm, body, 0)

    @pl.when(i == 0)
    def _():
        issue(dcur_ref, 0)

    for s in range(2):
        @pl.when((i + 1 < pl.num_programs(0)) & (slot != s))
        def _():
            issue(dnext_ref, s)

    pltpu.make_async_copy(ybuf.at[slot], ybuf.at[slot], sem.at[slot]).wait()
    gates = gate_ref[...]
    y = sum(gates[:, k:k + 1] * ybuf[slot, k] for k in range(TOP_K))
    out = _layernorm(DEEPNORM_ALPHA * x_ref[...] + y, g_ref[...], b_ref[...])

    @pl.when(i < n_prompt)
    def _():
        op_ref[...] = out

    @pl.when(i >= n_prompt)
    def _():
        os_ref[...] = out


def _combine_ln(x1, gates, dest, yb, ln_g, ln_b, tm, t_prompt):
    t = x1.shape[0]
    n = t // tm
    n_prompt = t_prompt // tm
    row = lambda i: (i, 0)
    const = lambda i: (0, 0)
    d2 = dest.reshape(n, 1, tm * TOP_K)
    return pl.pallas_call(
        functools.partial(_combine_ln_kernel, tm=tm, n_prompt=n_prompt),
        grid=(n,),
        in_specs=[pl.BlockSpec((1, 1, tm * TOP_K), lambda i: (i, 0, 0), memory_space=pltpu.SMEM),
                  pl.BlockSpec((1, 1, tm * TOP_K), lambda i: (jnp.minimum(i + 1, n - 1), 0, 0),
                               memory_space=pltpu.SMEM),
                  pl.BlockSpec((tm, D_MODEL), row),
                  pl.BlockSpec((tm, LANES), row),
                  pl.BlockSpec((1, D_MODEL), const), pl.BlockSpec((1, D_MODEL), const),
                  pl.BlockSpec(memory_space=pl.ANY)],
        out_specs=[pl.BlockSpec((tm, D_MODEL), lambda i: (jnp.minimum(i, n_prompt - 1), 0)),
                   pl.BlockSpec((tm, D_MODEL), lambda i: (jnp.maximum(i - n_prompt, 0), 0))],
        out_shape=[jax.ShapeDtypeStruct((t_prompt, D_MODEL), F32),
                   jax.ShapeDtypeStruct((t - t_prompt, D_MODEL), F32)],
        scratch_shapes=[pltpu.VMEM((2, TOP_K, tm, D_MODEL), F32), pltpu.SemaphoreType.DMA((2,))],
        compiler_params=_params("arbitrary"),
        name="combine_ln",
    )(d2, d2, x1, gates, ln_g.reshape(1, D_MODEL), ln_b.reshape(1, D_MODEL), yb)


MOE_BM = 512
ATTN_BLK = 512
GDN_SUB = 4
ROUTE_TM = 128


def _mixers(x2d, bsz, seq, cache_k, cache_v, conv_past, s0, lam, lam_init, w_in_bf, conv_w, a_log, dt_bias,
            delta_norm_w, subln_w):
    prompt = cache_k is None
    tm = ATTN_BLK if prompt else x2d.shape[0]
    q_bf, k_f, v_f, k_bf, v_bf, cin, z, ab = _in_proj(x2d, w_in_bf, tm, prompt)
    if prompt:
        att = _attn_prompt(q_bf, k_bf, v_bf, lam, subln_w, bsz, seq, ATTN_BLK, lam_init)
    else:
        att = _attn_sample(q_bf, k_bf, v_bf, cache_k, cache_v, 0, lam, subln_w.reshape(1, HEAD), bsz, seq, lam_init)
    chunk = CHUNK if seq % CHUNK == 0 else seq
    n_sub = GDN_SUB if (seq // chunk) % GDN_SUB == 0 else 1
    past8 = jnp.pad(conv_past, ((0, 0), (SUBLANES - (CONV_W - 1), 0), (0, 0)))
    u, w, qd, kd, qk, gl = _gdn_prep(cin, ab, past8, conv_w, a_log, dt_bias, bsz, seq, chunk, n_sub)
    o, s_new = _gdn_scan(u, w, qd, kd, qk, gl, z, s0, delta_norm_w.reshape(1, HEAD), bsz, seq, chunk)
    return att, o, k_f, v_f, cin, s_new


def _moe(x1, idx, gates, rank, counts, w_gu, b_gu, w_down, b_down, ln_g, ln_b, bm, tm, t_prompt):
    t = x1.shape[0]
    n = t * TOP_K
    counts = counts[0, :N_EXPERTS]
    padded = (counts + bm - 1) // bm * bm
    pad_end = jnp.cumsum(padded).astype(jnp.int32)
    pad_start = (pad_end - padded).astype(jnp.int32)
    idx = idx[:, :TOP_K, :]
    dest = rank[:, :TOP_K, :] + sum(jnp.where(idx == e, pad_start[e], 0) for e in range(N_EXPERTS))
    dest = dest.transpose(0, 2, 1).reshape(t, TOP_K)
    n_blocks = -(-n // bm) + N_EXPERTS
    blk_start = jnp.arange(n_blocks, dtype=jnp.int32) * bm
    blk_e = jnp.minimum(jnp.sum(pad_end[None, :] <= blk_start[:, None], axis=1), N_EXPERTS - 1).astype(jnp.int32)
    n_used = (pad_end[-1] // bm).astype(jnp.int32).reshape(1)
    has_rows = padded > 0
    experts = jnp.arange(N_EXPERTS, dtype=jnp.int32)
    slot_e = (jnp.cumsum(has_rows) - 1) % 2
    later = jnp.where(has_rows[None, :] & (experts[None, :] > experts[:, None]), experts[None, :], N_EXPERTS)
    next_e = jnp.min(later, axis=1)
    next_e = jnp.where(next_e < N_EXPERTS, next_e, -1)
    blk_first = (blk_start == pad_start[blk_e]).astype(jnp.int32)
    blk_slot = slot_e[blk_e].astype(jnp.int32)
    blk_next = next_e[blk_e].astype(jnp.int32)
    xb = _dispatch(x1, dest, pad_start, pad_end, n_blocks * bm, tm, bm)
    yb = _experts(xb, blk_e, blk_first, blk_slot, blk_next, n_used, w_gu, b_gu, w_down, b_down, bm)
    return _combine_ln(x1, gates, dest, yb, ln_g, ln_b, ROUTE_TM, t_prompt)


def kernel(x_prompt, x_sample, cache_k, cache_v, state_conv, state_delta, w_in, conv_w, a_log, dt_bias,
           delta_norm_w, lambda_q1, lambda_k1, lambda_q2, lambda_k2, subln_w, w_out, ln1_g, ln1_b,
           router_w, router_b, w_gu, b_gu, w_down, b_down, ln2_g, ln2_b):
    bp, lp, _ = x_prompt.shape
    bs, ls, _ = x_sample.shape
    l = 0
    lam_init = 0.8 - 0.6 * math.exp(-0.3 * l)
    lam = (jnp.exp(jnp.sum(lambda_q1[l] * lambda_k1[l])) - jnp.exp(jnp.sum(lambda_q2[l] * lambda_k2[l]))
           + lam_init).reshape(1).astype(F32)
    w_in_bf = jnp.pad(w_in[l], ((0, 0), (0, IN_COLS_PAD - IN_COLS))).astype(BF16)
    shared = (lam, lam_init, w_in_bf, conv_w[l], a_log[l], dt_bias[l], delta_norm_w[l], subln_w[l])

    xp = x_prompt.reshape(bp * lp, D_MODEL)
    xs = x_sample.reshape(bs * ls, D_MODEL)
    zero_conv = jnp.zeros((bp, CONV_W - 1, CONV_CH), F32)
    zero_s = jnp.zeros((bp, N_HEADS, HEAD, HEAD), F32)
    att_p, o_p, k_p, v_p, cin_p, s_p = _mixers(xp, bp, lp, None, None, zero_conv, zero_s, *shared)
    att_s, o_s, k_s, v_s, cin_s, s_s = _mixers(xs, bs, ls, cache_k, cache_v, state_conv[l],
                                               state_delta[l], *shared)

    x1, idx, gates, rank, counts = _out_router(((att_p, o_p, xp), (att_s, o_s, xs)), w_out[l].astype(BF16),
                                               ln1_g[l], ln1_b[l], router_w[l], router_b[l], ROUTE_TM)
    tp = bp * lp
    tm = next(c for c in (4, 3, 2, 1) if (x1.shape[0] // ROUTE_TM) % c == 0) * ROUTE_TM
    y_p, y_s = _moe(x1, idx, gates, rank, counts, w_gu[l], b_gu[l], w_down[l], b_down[l], ln2_g[l], ln2_b[l],
                    MOE_BM, tm, tp)
    conv_tail = lambda cin, b, s: cin.reshape(b, s, CONV_CH)[:, s - (CONV_W - 1):][None]
    return (y_p.reshape(bp, lp, D_MODEL), y_s.reshape(bs, ls, D_MODEL),
            k_p.reshape(1, bp, lp, N_HEADS, HEAD), v_p.reshape(1, bp, lp, N_HEADS, HEAD),
            conv_tail(cin_p, bp, lp), s_p[None].astype(state_delta.dtype),
            k_s.reshape(1, bs, ls, N_HEADS, HEAD), v_s.reshape(1, bs, ls, N_HEADS, HEAD),
            conv_tail(cin_s, bs, ls), s_s[None].astype(state_delta.dtype))
```

```python
import functools
import math

import jax
import jax.numpy as jnp
from jax import lax
from jax.experimental import pallas as pl
from jax.experimental.pallas import tpu as pltpu

F32 = jnp.float32
BF16 = jnp.bfloat16

D_MODEL = 1024
HEAD = 128
N_HEADS = 4
DQK = HEAD // 2
GROUP = N_HEADS * HEAD
CONV_W = 4
CONV_CH = 3 * GROUP
CHUNK = 64
ALIBI_MAX = 8.0
N_EXPERTS = 32
TOP_K = 4
D_FF = D_MODEL
SWIGLU_LIMIT = 7.0
SWIGLU_ALPHA = 1.702
DEPTH = 1
DEEPNORM_ALPHA = (2 * DEPTH) ** 0.25
LN_EPS = 1e-5
SUBLN_EPS = 1e-5
GATED_NORM_EPS = 1e-6
L2_EPS = 1e-6

LANES = 128
SUBLANES = 8
BF16_EXACT_INT = 256
BF16_ROWS = 16
LOG2E = 1.4426950408889634
N_POS = 6
ONES_ROWS = BF16_ROWS
N_DMA_PRIORITIES = 2
VMEM_LIMIT = 56 * 1024 * 1024

COL_Q, COL_K, COL_V, COL_CONV = 0, GROUP, 2 * GROUP, 3 * GROUP
COL_Z = COL_CONV + CONV_CH
COL_AB = COL_Z + GROUP
IN_COLS = COL_AB + 2 * N_HEADS
IN_COLS_PAD = COL_AB + LANES


def _params(*sem):
    return pltpu.CompilerParams(dimension_semantics=sem, vmem_limit_bytes=VMEM_LIMIT)


def _in_proj_kernel(x_ref, w_ref, q_ref, kf_ref, vf_ref, kb_ref, vb_ref, c_ref, z_ref, ab_ref, *, tm, transposed):
    xb = x_ref[...].astype(BF16)

    def section(lo, hi):
        return jnp.dot(xb, w_ref[:, lo:hi], preferred_element_type=F32)

    q = section(COL_Q, COL_K) * (DQK ** -0.5 * (LOG2E if transposed else 1.0))
    k = section(COL_K, COL_V)
    kf_ref[...] = k
    kb_ref[...] = k.astype(BF16)
    v = section(COL_V, COL_CONV)
    vf_ref[...] = v
    if transposed:
        q_ref[0] = q.T.astype(BF16)
        vb_ref[0] = v.T.astype(BF16)
    else:
        q_ref[...] = q.astype(BF16)
        vb_ref[...] = v.astype(BF16)
    c_ref[...] = section(COL_CONV, COL_Z)
    z_ref[...] = section(COL_Z, COL_AB)
    ab_ref[...] = section(COL_AB, IN_COLS_PAD)


def _in_proj(x2d, w_bf, tm, transposed):
    t = x2d.shape[0]
    row = lambda i: (i, 0)
    widths = (GROUP, GROUP, GROUP, GROUP, GROUP, CONV_CH, GROUP, LANES)
    dtypes = (BF16, F32, F32, BF16, BF16, F32, F32, F32)
    out_specs = [pl.BlockSpec((tm, w), row) for w in widths]
    out_shape = [jax.ShapeDtypeStruct((t, w), d) for w, d in zip(widths, dtypes)]
    if transposed:
        for slot in (0, 4):
            out_specs[slot] = pl.BlockSpec((1, GROUP, tm), lambda i: (i, 0, 0))
            out_shape[slot] = jax.ShapeDtypeStruct((t // tm, GROUP, tm), BF16)
    return pl.pallas_call(
        functools.partial(_in_proj_kernel, tm=tm, transposed=transposed),
        grid=(t // tm,),
        in_specs=[pl.BlockSpec((tm, D_MODEL), row),
                  pl.BlockSpec((D_MODEL, IN_COLS_PAD), lambda i: (0, 0))],
        out_specs=out_specs,
        out_shape=out_shape,
        compiler_params=_params("parallel"),
        name="in_proj",
    )(x2d, w_bf)


def _alibi_slopes():
    return [2.0 ** (-ALIBI_MAX * (h + 1) / N_HEADS) for h in range(N_HEADS)]


def _stack_halves(q):
    lane = lax.broadcasted_iota(jnp.int32, q.shape, 1)
    zero = jnp.zeros_like(q)
    return jnp.concatenate([jnp.where(lane < DQK, q, zero), jnp.where(lane < DQK, zero, q)], axis=0)


def _diff_norm(acc, l, lam, w, lam_init, rows):
    o = acc[:rows] / l[:rows] - lam * (acc[rows:] / l[rows:])
    ms = jnp.mean(o * o, axis=-1, keepdims=True)
    return o * lax.rsqrt(ms + SUBLN_EPS) * w * (1.0 - lam_init)


def _head_slope(h):
    s = _alibi_slopes()
    return jnp.where(h == 0, s[0], jnp.where(h == 1, s[1], jnp.where(h == 2, s[2], s[3]))).astype(F32)


def _attn_prompt_kernel(lam_ref, qt_ref, k_ref, vt_ref, w_ref, o_ref,
                        diag_ref, kaug_ref, vaug_ref, qz_ref, s0_ref, s1_ref, p0_ref, p1_ref, mx0_ref, mx1_ref, m_ref, acc_ref, *, blk, lam_init):
    i = pl.program_id(2)
    slope = _head_slope(pl.program_id(1)) * LOG2E
    rows = 2 * blk
    n_kv = k_ref.shape[0] // blk

    def pieces(x):
        lo = x % BF16_EXACT_INT
        return _split3(slope * lo.astype(F32)) + _split3(slope * (x - lo).astype(F32))

    @pl.when(i == 0)
    def _():
        lane = lax.broadcasted_iota(jnp.int32, (blk, HEAD), 1)
        extra = jnp.where(lane < N_POS, 1.0, 0.0).astype(BF16)
        for n, piece in enumerate(pieces(lax.broadcasted_iota(jnp.int32, (blk, 1), 0))):
            extra = jnp.where(lane == N_POS + n, piece, extra)
        ones = jnp.ones((ONES_ROWS, blk), BF16)
        c = lax.broadcasted_iota(jnp.int32, (blk, rows), 0)
        a = lax.broadcasted_iota(jnp.int32, (blk, rows), 1) % blk
        diag_ref[...] = jnp.where(c // CHUNK <= a // CHUNK, -slope * jnp.abs(a - c).astype(F32), -jnp.inf)
        r = lax.broadcasted_iota(jnp.int32, (HEAD, rows), 0)
        qx = jnp.where((r >= N_POS) & (r < 2 * N_POS), 1.0, 0.0).astype(BF16)
        for n, piece in enumerate(pieces(-(lax.broadcasted_iota(jnp.int32, (1, rows), 1) % blk))):
            qx = jnp.where(r == n, piece, qx)
        qz_ref[HEAD:, :] = qx

        def fill(j, carry):
            j0 = pl.multiple_of(j * blk, blk)
            kaug_ref[pl.ds(j0, blk), :HEAD] = k_ref[pl.ds(j0, blk), :]
            kaug_ref[pl.ds(j0, blk), HEAD:] = extra
            vaug_ref[j, :HEAD, :] = vt_ref[j]
            vaug_ref[j, HEAD:, :] = ones
            return carry

        lax.fori_loop(0, n_kv, fill, 0)

    qt = qt_ref[0]
    d = lax.broadcasted_iota(jnp.int32, qt.shape, 0)
    zero = jnp.zeros_like(qt)
    qz_ref[:HEAD, :] = jnp.concatenate([jnp.where(d < DQK, qt, zero), jnp.where(d < DQK, zero, qt)], axis=1)

    def tile_rows(j):
        return pl.ds(pl.multiple_of(j * blk, blk), blk)

    s = jnp.dot(k_ref[tile_rows(i), :], qz_ref[:HEAD, :], preferred_element_type=F32) + diag_ref[...]
    m = jnp.max(s, axis=0, keepdims=True)
    p1_ref[...] = jnp.exp2(s - m).astype(BF16)
    m_ref[...] = m
    acc_ref[...] = jnp.zeros_like(acc_ref)
    s_first = jnp.dot(kaug_ref[tile_rows(0), :], qz_ref[...], preferred_element_type=F32)
    s0_ref[...] = s_first
    mx0_ref[...] = jnp.max(s_first, axis=0, keepdims=True)

    def step(j, s_cur, s_nxt, p_cur, p_nxt, mx_cur, mx_nxt):
        s_next = jnp.dot(kaug_ref[tile_rows(jnp.minimum(j + 1, i - 1)), :], qz_ref[...],
                         preferred_element_type=F32)
        s_nxt[...] = s_next
        mx_nxt[...] = jnp.max(s_next, axis=0, keepdims=True)
        pv = jnp.dot(vaug_ref[jnp.where(j == 0, i, j - 1)], p_nxt[...], preferred_element_type=F32)
        shift = -slope * ((i - j) * blk).astype(F32)
        m_old = m_ref[...]
        m_new = jnp.maximum(m_old, mx_cur[...] + shift)
        p_cur[...] = jnp.exp2(s_cur[...] - (m_new - shift)).astype(BF16)
        acc_ref[...] = (acc_ref[...] + pv) * jnp.exp2(m_old - m_new)
        m_ref[...] = m_new

    def body(jj, carry):
        step(2 * jj, s0_ref, s1_ref, p0_ref, p1_ref, mx0_ref, mx1_ref)

        @pl.when(2 * jj + 1 < i)
        def _():
            step(2 * jj + 1, s1_ref, s0_ref, p1_ref, p0_ref, mx1_ref, mx0_ref)

        return carry

    lax.fori_loop(0, (i + 1) // 2, body, 0)
    p_last = jnp.where(i % 2 == 1, p0_ref[...], p1_ref[...])
    acc = acc_ref[...] + jnp.dot(vaug_ref[jnp.where(i > 0, i - 1, i)], p_last, preferred_element_type=F32)
    l = acc[HEAD:HEAD + 1, :]
    num = acc[:HEAD, :]
    ot = num[:, :blk] / l[:, :blk] - lam_ref[0] * (num[:, blk:] / l[:, blk:])
    ms = jnp.mean(ot * ot, axis=0, keepdims=True)
    ot = ot * lax.rsqrt(ms + SUBLN_EPS) * w_ref[...] * (1.0 - lam_init)
    o_ref[...] = ot.T.astype(o_ref.dtype)


def _attn_prompt(qt_bf, k_bf, vt_bf, lam, subln_w, bsz, seq, blk, lam_init):
    nq = seq // blk
    kern = functools.partial(_attn_prompt_kernel, blk=blk, lam_init=lam_init)
    return pl.pallas_call(
        kern,
        grid=(bsz, N_HEADS, nq),
        in_specs=[pl.BlockSpec(memory_space=pltpu.SMEM),
                  pl.BlockSpec((1, HEAD, blk), lambda b, h, i: (b * nq + i, h, 0)),
                  pl.BlockSpec((seq, HEAD), lambda b, h, i: (b, h)),
                  pl.BlockSpec((nq, HEAD, blk), lambda b, h, i: (b, h, 0)),
                  pl.BlockSpec((HEAD, 1), lambda b, h, i: (0, 0))],
        out_specs=pl.BlockSpec((blk, HEAD), lambda b, h, i: (b * nq + i, h)),
        scratch_shapes=[pltpu.VMEM((blk, 2 * blk), F32),
                        pltpu.VMEM((seq, 2 * HEAD), BF16),
                        pltpu.VMEM((nq, HEAD + ONES_ROWS, blk), BF16),
                        pltpu.VMEM((2 * HEAD, 2 * blk), BF16),
                        pltpu.VMEM((blk, 2 * blk), F32), pltpu.VMEM((blk, 2 * blk), F32),
                        pltpu.VMEM((blk, 2 * blk), BF16), pltpu.VMEM((blk, 2 * blk), BF16),
                        pltpu.VMEM((1, 2 * blk), F32), pltpu.VMEM((1, 2 * blk), F32),
                        pltpu.VMEM((1, 2 * blk), F32),
                        pltpu.VMEM((HEAD + ONES_ROWS, 2 * blk), F32)],
        out_shape=jax.ShapeDtypeStruct((bsz * seq, GROUP), BF16),
        compiler_params=_params("parallel", "parallel", "arbitrary"),
        name="attn_prompt",
    )(lam, qt_bf, k_bf, vt_bf, subln_w.reshape(HEAD, 1))


def _attn_sample_kernel(lam_ref, q_ref, kn_ref, vn_ref, kc_ref, vc_ref, w_ref, o_ref, *, seq, past, lam_init):
    nt = (((1,), (1,)), ((), ()))
    qpos = past + lax.broadcasted_iota(jnp.int32, (2 * seq, 1), 0) % seq
    rel_c = jnp.abs(qpos - lax.broadcasted_iota(jnp.int32, (1, past), 1)).astype(F32)
    rel_n = jnp.abs(qpos - (past + lax.broadcasted_iota(jnp.int32, (1, seq), 1))).astype(F32)
    for h, slope in enumerate(_alibi_slopes()):
        col = slice(h * HEAD, (h + 1) * HEAD)
        qz = _stack_halves(q_ref[:, col])
        s_c = lax.dot_general(qz, kc_ref[0, 0, :, h, :].astype(BF16), nt, preferred_element_type=F32)
        s_c = s_c - slope * rel_c
        s_n = lax.dot_general(qz, kn_ref[:, col], nt, preferred_element_type=F32) - slope * rel_n
        m = jnp.maximum(jnp.max(s_c, axis=-1, keepdims=True), jnp.max(s_n, axis=-1, keepdims=True))
        p_c = jnp.exp(s_c - m)
        p_n = jnp.exp(s_n - m)
        l = jnp.sum(p_c, axis=-1, keepdims=True) + jnp.sum(p_n, axis=-1, keepdims=True)
        acc = (jnp.dot(p_c.astype(BF16), vc_ref[0, 0, :, h, :].astype(BF16), preferred_element_type=F32)
               + jnp.dot(p_n.astype(BF16), vn_ref[:, col], preferred_element_type=F32))
        o_ref[:, col] = _diff_norm(acc, l, lam_ref[0], w_ref[...], lam_init, seq).astype(o_ref.dtype)


def _attn_sample(q_bf, k_bf, v_bf, cache_k, cache_v, layer, lam, subln_w, bsz, seq, lam_init):
    past = cache_k.shape[2]
    kern = functools.partial(_attn_sample_kernel, seq=seq, past=past, lam_init=lam_init)
    new = pl.BlockSpec((seq, GROUP), lambda b: (b, 0))
    cache = pl.BlockSpec((1, 1, past, N_HEADS, HEAD), lambda b: (layer, b, 0, 0, 0))
    return pl.pallas_call(
        kern,
        grid=(bsz,),
        in_specs=[pl.BlockSpec(memory_space=pltpu.SMEM), new, new, new, cache, cache,
                  pl.BlockSpec((1, HEAD), lambda b: (0, 0))],
        out_specs=new,
        out_shape=jax.ShapeDtypeStruct((bsz * seq, GROUP), BF16),
        compiler_params=_params("parallel"),
        name="attn_sample",
    )(lam, q_bf, k_bf, v_bf, cache_k, cache_v, subln_w)


def _split3(x):
    hi = x.astype(BF16)
    r1 = x - hi.astype(F32)
    mid = r1.astype(BF16)
    lo = (r1 - mid.astype(F32)).astype(BF16)
    return hi, mid, lo


def _gdn_prep_kernel(cin_ref, halo_ref, past_ref, ab_ref, cw_ref, alog_ref, dtb_ref,
                     u_ref, w_ref, qd_ref, kd_ref, qk_ref, gl_ref, *, chunk, n_sub):
    c_idx = pl.program_id(1)
    rows = chunk * n_sub
    prev = jnp.where(c_idx == 0, past_ref[0], halo_ref[...])
    xin = jnp.concatenate([prev, cin_ref[...]], axis=0)
    conv = sum(xin[SUBLANES - (CONV_W - 1) + j: SUBLANES - (CONV_W - 1) + j + rows] * cw_ref[j:j + 1, :]
               for j in range(CONV_W))
    conv = conv * jax.nn.sigmoid(conv)

    ab = ab_ref[...]
    lane = lax.broadcasted_iota(jnp.int32, ab.shape, 1)
    pre = ab + dtb_ref[...]
    softplus = jnp.maximum(pre, 0.0) + jnp.log(1.0 + jnp.exp(-jnp.abs(pre)))
    g = jnp.where(lane < N_HEADS, -jnp.exp(alog_ref[...]) * softplus, 0.0)
    beta_all = jax.nn.sigmoid(ab)

    ri = lax.broadcasted_iota(jnp.int32, (rows, rows), 0)
    ci = lax.broadcasted_iota(jnp.int32, (rows, rows), 1)
    same = (ri // chunk) == (ci // chunk)
    incl = same & (ri >= ci)
    strict = same & (ri > ci)
    eye = jnp.where(ri == ci, 1.0, 0.0).astype(F32)
    nt = (((1,), (1,)), ((), ()))
    g_parts = _split3(g)
    ones_incl = jnp.where(incl, 1.0, 0.0).astype(BF16)
    ones_same = jnp.where(same, 1.0, 0.0).astype(BF16)
    gc = sum(jnp.dot(ones_incl, part, preferred_element_type=F32) for part in g_parts)
    g_end = sum(jnp.dot(ones_same, part, preferred_element_type=F32) for part in g_parts)
    gct = gc.T
    for sc in range(n_sub):
        gl_ref[sc] = g_end[sc * chunk:sc * chunk + 1, :]

    t_mats, p_mats, rhs = [], [], []
    for h in range(N_HEADS):
        col = slice(h * HEAD, (h + 1) * HEAD)
        qh = conv[:, h * HEAD:(h + 1) * HEAD]
        kh = conv[:, GROUP + h * HEAD:GROUP + (h + 1) * HEAD]
        vh = conv[:, 2 * GROUP + h * HEAD:2 * GROUP + (h + 1) * HEAD]
        qh = qh * lax.rsqrt(jnp.sum(qh * qh, axis=-1, keepdims=True) + L2_EPS) * (HEAD ** -0.5)
        kh = kh * lax.rsqrt(jnp.sum(kh * kh, axis=-1, keepdims=True) + L2_EPS)
        beta = beta_all[:, N_HEADS + h:N_HEADS + h + 1]
        gcol = gc[:, h:h + 1]
        grow = gct[h:h + 1, :]
        gamma = jnp.exp(jnp.where(incl, gcol - grow, -jnp.inf))
        egc = jnp.exp(gcol)
        kb = kh * beta
        khb = kh.astype(BF16)
        a = jnp.where(strict, lax.dot_general(kb.astype(BF16), khb, nt, preferred_element_type=F32) * gamma, 0.0)
        qk = (lax.dot_general(qh.astype(BF16), khb, nt, preferred_element_type=F32) * gamma).astype(BF16)
        for sc in range(n_sub):
            blk = slice(sc * chunk, (sc + 1) * chunk)
            qk_ref[h, blk, :] = qk[blk, blk]
        qd_ref[:, col] = (qh * egc).astype(BF16)
        kd_ref[:, col] = (kh * jnp.exp(g_end[:, h:h + 1] - gcol)).astype(BF16)
        t_mats.append(eye - a)
        p_mats.append(a)
        rhs.append(((vh * beta).astype(BF16), (kb * egc).astype(BF16)))

    for _ in range(int(math.log2(chunk)) - 1):
        for h in range(N_HEADS):
            pb = p_mats[h].astype(BF16)
            p_mats[h] = jnp.dot(pb, pb, preferred_element_type=F32)
        for h in range(N_HEADS):
            t_mats[h] = t_mats[h] + jnp.dot(t_mats[h].astype(BF16), p_mats[h].astype(BF16),
                                            preferred_element_type=F32)

    for h in range(N_HEADS):
        col = slice(h * HEAD, (h + 1) * HEAD)
        t_inv = t_mats[h].astype(BF16)
        u_ref[:, col] = jnp.dot(t_inv, rhs[h][0], preferred_element_type=F32)
        w_ref[:, col] = jnp.dot(t_inv, rhs[h][1], preferred_element_type=F32).astype(BF16)


def _gdn_prep(cin, ab, past8, conv_w, a_log, dt_bias, bsz, seq, chunk, n_sub):
    rows = chunk * n_sub
    nblk = seq // rows
    t = bsz * seq
    lanes = lambda v: jnp.pad(v.reshape(1, N_HEADS).astype(F32), ((0, 0), (0, LANES - N_HEADS)))
    kern = functools.partial(_gdn_prep_kernel, chunk=chunk, n_sub=n_sub)
    rowblk = lambda b, c: (b * nblk + c, 0)
    halo = lambda b, c: (jnp.maximum((b * nblk + c) * (rows // SUBLANES) - 1, 0), 0)
    const = lambda b, c: (0, 0)
    return pl.pallas_call(
        kern,
        grid=(bsz, nblk),
        in_specs=[pl.BlockSpec((rows, CONV_CH), rowblk),
                  pl.BlockSpec((SUBLANES, CONV_CH), halo),
                  pl.BlockSpec((1, SUBLANES, CONV_CH), lambda b, c: (b, 0, 0)),
                  pl.BlockSpec((rows, LANES), rowblk),
                  pl.BlockSpec((CONV_W, CONV_CH), const),
                  pl.BlockSpec((1, LANES), const),
                  pl.BlockSpec((1, LANES), const)],
        out_specs=[pl.BlockSpec((rows, GROUP), rowblk),
                   pl.BlockSpec((rows, GROUP), rowblk),
                   pl.BlockSpec((rows, GROUP), rowblk),
                   pl.BlockSpec((rows, GROUP), rowblk),
                   pl.BlockSpec((N_HEADS, rows, chunk), lambda b, c: (0, b * nblk + c, 0)),
                   pl.BlockSpec((n_sub, 1, LANES), lambda b, c: (b * nblk + c, 0, 0))],
        out_shape=[jax.ShapeDtypeStruct((t, GROUP), F32),
                   jax.ShapeDtypeStruct((t, GROUP), BF16),
                   jax.ShapeDtypeStruct((t, GROUP), BF16),
                   jax.ShapeDtypeStruct((t, GROUP), BF16),
                   jax.ShapeDtypeStruct((N_HEADS, t, chunk), BF16),
                   jax.ShapeDtypeStruct((t // chunk, 1, LANES), F32)],
        compiler_params=_params("parallel", "parallel"),
        name="gdn_prep",
    )(cin, cin, past8, ab, conv_w, lanes(a_log), lanes(dt_bias))


def _gdn_scan_kernel(u_ref, w_ref, qd_ref, kd_ref, qk_ref, gl_ref, z_ref, s0_ref, nw_ref,
                     o_ref, sf_ref, s_ref, *, bsz, chunk):
    c_idx = pl.program_id(0)

    @pl.when(c_idx == 0)
    def _():
        s_ref[...] = s0_ref[...]

    tn = (((0,), (0,)), ((), ()))
    chains = [(b, h, slice(h * HEAD, (h + 1) * HEAD)) for b in range(bsz) for h in range(N_HEADS)]
    states = [s_ref[b, h] for b, h, _ in chains]
    prods = [jnp.dot(jnp.concatenate([w_ref[b, :, col], qd_ref[b, :, col]], axis=0), s.astype(BF16),
                     preferred_element_type=F32) for (b, _, col), s in zip(chains, states)]
    v_news = [(u_ref[b, :, col] - r[:chunk]).astype(BF16) for (b, _, col), r in zip(chains, prods)]
    outs = [r[chunk:] + jnp.dot(qk_ref[h, b], v, preferred_element_type=F32)
            for (b, h, _), r, v in zip(chains, prods, v_news)]
    for (b, h, col), s, v in zip(chains, states, v_news):
        glast = jnp.exp(gl_ref[b, 0])
        s_ref[b, h] = s * glast[:, h:h + 1] + lax.dot_general(kd_ref[b, :, col], v, tn,
                                                             preferred_element_type=F32)
    for (b, h, col), o in zip(chains, outs):
        zh = z_ref[b, :, col]
        ms = jnp.mean(o * o, axis=-1, keepdims=True)
        o = o * lax.rsqrt(ms + GATED_NORM_EPS) * nw_ref[...] * (zh * jax.nn.sigmoid(zh))
        o_ref[b, :, col] = o.astype(o_ref.dtype)

    @pl.when(c_idx == pl.num_programs(0) - 1)
    def _():
        sf_ref[...] = s_ref[...]


def _gdn_scan(u, w, qd, kd, qk, gl, z, s0, norm_w, bsz, seq, chunk):
    nc = seq // chunk
    kern = functools.partial(_gdn_scan_kernel, bsz=bsz, chunk=chunk)
    tok = pl.BlockSpec((bsz, chunk, GROUP), lambda c: (0, c, 0))
    state = pl.BlockSpec((bsz, N_HEADS, HEAD, HEAD), lambda c: (0, 0, 0, 0))
    o, s_final = pl.pallas_call(
        kern,
        grid=(nc,),
        in_specs=[tok, tok, tok, tok,
                  pl.BlockSpec((N_HEADS, bsz, chunk, chunk), lambda c: (0, 0, c, 0)),
                  pl.BlockSpec((bsz, 1, 1, LANES), lambda c: (0, c, 0, 0)),
                  tok, state,
                  pl.BlockSpec((1, HEAD), lambda c: (0, 0))],
        out_specs=[tok, state],
        out_shape=[jax.ShapeDtypeStruct((bsz, seq, GROUP), BF16),
                   jax.ShapeDtypeStruct((bsz, N_HEADS, HEAD, HEAD), F32)],
        scratch_shapes=[pltpu.VMEM((bsz, N_HEADS, HEAD, HEAD), F32)],
        compiler_params=_params("arbitrary"),
        name="gdn_scan",
    )(u.reshape(bsz, seq, GROUP), w.reshape(bsz, seq, GROUP), qd.reshape(bsz, seq, GROUP),
      kd.reshape(bsz, seq, GROUP), qk.reshape(N_HEADS, bsz, seq, chunk), gl.reshape(bsz, nc, 1, LANES),
      z.reshape(bsz, seq, GROUP), s0, norm_w)
    return o.reshape(bsz * seq, GROUP), s_final


def _layernorm(x, g, b):
    mu = jnp.mean(x, axis=-1, keepdims=True)
    xc = x - mu
    var = jnp.mean(xc * xc, axis=-1, keepdims=True)
    return xc * lax.rsqrt(var + LN_EPS) * g + b


def _out_router_kernel(*refs, tm, n_sub, n_prompt):
    tiles = [refs[6 * u:6 * u + 6] for u in range(n_sub)]
    wo_ref, g_ref, b_ref, rw_ref, rb_ref, x1_ref, idx_ref, gate_ref, rank_ref, cnt_ref, carry_ref = refs[6 * n_sub:]
    step = pl.program_id(0)
    units = range(n_sub)

    @pl.when(step == 0)
    def _():
        carry_ref[...] = jnp.zeros_like(carry_ref)

    x1s = []
    for u in units:
        attp_ref, atts_ref, op_ref, os_ref, xp_ref, xs_ref = tiles[u]
        prompt = step * n_sub + u < n_prompt
        att = jnp.where(prompt, attp_ref[...], atts_ref[...])
        o = jnp.where(prompt, op_ref[...], os_ref[...])
        x = jnp.where(prompt, xp_ref[...], xs_ref[...])
        mix = (jnp.dot(att, wo_ref[:GROUP, :], preferred_element_type=F32)
               + jnp.dot(o, wo_ref[GROUP:, :], preferred_element_type=F32))
        x1 = _layernorm(DEEPNORM_ALPHA * x + mix, g_ref[...], b_ref[...])
        x1_ref[u * tm:(u + 1) * tm, :] = x1
        x1s.append(x1)

    works = []
    lane = lax.broadcasted_iota(jnp.int32, (tm, LANES), 1)
    for x1 in x1s:
        x_hi = x1.astype(BF16)
        x_lo = (x1 - x_hi.astype(F32)).astype(BF16)
        logits = (jnp.dot(x_hi, rw_ref[0], preferred_element_type=F32)
                  + jnp.dot(x_lo, rw_ref[0], preferred_element_type=F32)
                  + jnp.dot(x_hi, rw_ref[1], preferred_element_type=F32)) + rb_ref[...]
        works.append(jnp.where(lane < N_EXPERTS, logits, -jnp.inf))

    vals, idxs = [[] for _ in units], [[] for _ in units]
    for _ in range(TOP_K):
        for u in units:
            m = jnp.max(works[u], axis=-1, keepdims=True)
            am = jnp.min(jnp.where(works[u] == m, lane, LANES), axis=-1, keepdims=True)
            vals[u].append(m)
            idxs[u].append(am)
            works[u] = jnp.where(lane == am, -jnp.inf, works[u])

    ri = lax.broadcasted_iota(jnp.int32, (tm, tm), 0)
    ci = lax.broadcasted_iota(jnp.int32, (tm, tm), 1)
    before = jnp.where(ri > ci, 1.0, 0.0).astype(BF16)
    base = carry_ref[...]
    for u in units:
        exps = [jnp.exp(v - vals[u][0]) for v in vals[u]]
        denom = exps[0] + exps[1] + exps[2] + exps[3]
        chosen = jnp.zeros((tm, LANES), F32)
        gate_out = jnp.zeros((tm, LANES), F32)
        idx_out = jnp.zeros((tm, LANES), jnp.int32)
        for k in range(TOP_K):
            chosen = jnp.where(lane == idxs[u][k], 1.0, chosen)
            gate_out = jnp.where(lane == k, exps[k] / denom, gate_out)
            idx_out = jnp.where(lane == k, idxs[u][k], idx_out)
        prefix = jnp.dot(before, chosen.astype(BF16), preferred_element_type=F32) + base
        base = base + jnp.sum(chosen, axis=0, keepdims=True)
        rank_out = jnp.zeros((tm, LANES), F32)
        for k in range(TOP_K):
            r = jnp.sum(jnp.where(lane == idxs[u][k], prefix, 0.0), axis=-1, keepdims=True)
            rank_out = jnp.where(lane == k, r, rank_out)
        idx_ref[0, :, u * tm:(u + 1) * tm] = idx_out.T[:SUBLANES, :]
        gate_ref[u * tm:(u + 1) * tm, :] = gate_out
        rank_ref[0, :, u * tm:(u + 1) * tm] = rank_out.astype(jnp.int32).T[:SUBLANES, :]
    carry_ref[...] = base
    cnt_ref[...] = base.astype(jnp.int32)


def _out_router(streams, w_out_bf, ln_g, ln_b, router_w, router_b, tm, n_sub):
    (att_p, o_p, x_p), (att_s, o_s, x_s) = streams
    n_prompt = x_p.shape[0] // tm
    n_sample = x_s.shape[0] // tm
    t = x_p.shape[0] + x_s.shape[0]
    step_rows = tm * n_sub
    row = lambda i: (i, 0)
    const = lambda i: (0, 0)
    slots = pl.BlockSpec((1, SUBLANES, step_rows), lambda i: (i, 0, 0))
    tile_specs, tile_args = [], []
    for u in range(n_sub):
        prow = lambda i, u=u: (jnp.minimum(i * n_sub + u, n_prompt - 1), 0)
        srow = lambda i, u=u: (jnp.clip(i * n_sub + u - n_prompt, 0, n_sample - 1), 0)
        tile_specs += [pl.BlockSpec((tm, GROUP), prow), pl.BlockSpec((tm, GROUP), srow),
                       pl.BlockSpec((tm, GROUP), prow), pl.BlockSpec((tm, GROUP), srow),
                       pl.BlockSpec((tm, D_MODEL), prow), pl.BlockSpec((tm, D_MODEL), srow)]
        tile_args += [att_p, att_s, o_p, o_s, x_p, x_s]
    rw = jnp.pad(router_w, ((0, 0), (0, LANES - N_EXPERTS)))
    rw_hi = rw.astype(BF16)
    rw = jnp.stack([rw_hi, (rw - rw_hi.astype(F32)).astype(BF16)])
    rb = jnp.pad(router_b.reshape(1, N_EXPERTS), ((0, 0), (0, LANES - N_EXPERTS)))
    kern = functools.partial(_out_router_kernel, tm=tm, n_sub=n_sub, n_prompt=n_prompt)
    return pl.pallas_call(
        kern,
        grid=(t // step_rows,),
        in_specs=tile_specs + [pl.BlockSpec((2 * GROUP, D_MODEL), const),
                               pl.BlockSpec((1, D_MODEL), const), pl.BlockSpec((1, D_MODEL), const),
                               pl.BlockSpec((2, D_MODEL, LANES), lambda i: (0, 0, 0)),
                               pl.BlockSpec((1, LANES), const)],
        out_specs=[pl.BlockSpec((step_rows, D_MODEL), row), slots, pl.BlockSpec((step_rows, LANES), row), slots,
                   pl.BlockSpec((1, LANES), const)],
        out_shape=[jax.ShapeDtypeStruct((t, D_MODEL), F32),
                   jax.ShapeDtypeStruct((t // step_rows, SUBLANES, step_rows), jnp.int32),
                   jax.ShapeDtypeStruct((t, LANES), F32),
                   jax.ShapeDtypeStruct((t // step_rows, SUBLANES, step_rows), jnp.int32),
                   jax.ShapeDtypeStruct((1, LANES), jnp.int32)],
        scratch_shapes=[pltpu.VMEM((1, LANES), F32)],
        compiler_params=_params("arbitrary"),
        name="out_router",
    )(*tile_args, w_out_bf, ln_g.reshape(1, D_MODEL), ln_b.reshape(1, D_MODEL), rw, rb)


def _row_copy(src, src_row, dst, dst_row, sem):
    return pltpu.make_async_copy(src.at[pl.ds(src_row, 1), :], dst.at[pl.ds(dst_row, 1), :], sem)


def _dispatch_kernel(ps_ref, pe_ref, dest_ref, x_ref, xb_hbm, zero_ref, sem, zsem, *, tm, bm):
    i = pl.program_id(0)

    def zero_block(row):
        return pltpu.make_async_copy(zero_ref, xb_hbm.at[pl.ds(pl.multiple_of(row, bm), bm), :], zsem)

    def fill(e):
        return zero_block(pe_ref[e] - bm)

    @pl.when(i == 0)
    def _():
        zero_ref[...] = jnp.zeros_like(zero_ref)
        for e in range(N_EXPERTS):
            @pl.when(pe_ref[e] > ps_ref[e])
            def _():
                fill(e).start()
        first_unused = pe_ref[N_EXPERTS - 1] // bm
        n_blocks = xb_hbm.shape[0] // bm
        lax.fori_loop(first_unused, n_blocks, lambda b, c: (zero_block(b * bm).start(), c)[1], 0)
        for e in range(N_EXPERTS):
            @pl.when(pe_ref[e] > ps_ref[e])
            def _():
                fill(e).wait()
        lax.fori_loop(first_unused, n_blocks, lambda b, c: (zero_block(b * bm).wait(), c)[1], 0)

    def issue(r, carry):
        for k in range(TOP_K):
            _row_copy(x_ref, r, xb_hbm, dest_ref[0, 0, r * TOP_K + k], sem).start(priority=k % N_DMA_PRIORITIES)
        return carry

    lax.fori_loop(0, tm, issue, 0)
    pltpu.make_async_copy(xb_hbm.at[pl.ds(0, tm * TOP_K), :], xb_hbm.at[pl.ds(0, tm * TOP_K), :], sem).wait()


def _dispatch(x1, dest, pad_start, pad_end, rows, tm, bm):
    t = x1.shape[0]
    return pl.pallas_call(
        functools.partial(_dispatch_kernel, tm=tm, bm=bm),
        grid_spec=pltpu.PrefetchScalarGridSpec(
            num_scalar_prefetch=2,
            grid=(t // tm,),
            in_specs=[pl.BlockSpec((1, 1, tm * TOP_K), lambda i, ps, pe: (i, 0, 0), memory_space=pltpu.SMEM),
                      pl.BlockSpec((tm, D_MODEL), lambda i, ps, pe: (i, 0))],
            out_specs=pl.BlockSpec(memory_space=pl.ANY),
            scratch_shapes=[pltpu.VMEM((bm, D_MODEL), F32),
                            pltpu.SemaphoreType.DMA(()), pltpu.SemaphoreType.DMA(())]),
        out_shape=jax.ShapeDtypeStruct((rows, D_MODEL), F32),
        compiler_params=_params("arbitrary"),
        name="dispatch",
    )(pad_start, pad_end, dest.reshape(t // tm, 1, tm * TOP_K), x1)


def _expert_kernel(ps_ref, pe_ref, x_ref, wgu_hbm, bgu_ref, wd_hbm, bd_ref, y_ref,
                   wgu_buf, wd_buf, sem, state_ref, *, bm):
    blk = pl.program_id(0)
    row0 = blk * bm

    def fetch(e, s):
        return (pltpu.make_async_copy(wgu_hbm.at[e], wgu_buf.at[s], sem.at[s]),
                pltpu.make_async_copy(wd_hbm.at[e], wd_buf.at[s], sem.at[s]))

    def next_with_rows(e):
        return lax.while_loop(lambda n: (n < N_EXPERTS) & (pe_ref[jnp.minimum(n, N_EXPERTS - 1)] <= row0),
                              lambda n: n + 1, e)

    @pl.when(row0 < pe_ref[N_EXPERTS - 1])
    def _():
        @pl.when(blk == 0)
        def _():
            first = next_with_rows(0)
            state_ref[0] = first
            state_ref[1] = 0
            for copy in fetch(first, 0):
                copy.start()

        @pl.when((blk > 0) & (row0 >= pe_ref[state_ref[0]]))
        def _():
            state_ref[0] = next_with_rows(state_ref[0])
            state_ref[1] = 1 - state_ref[1]

        e, s = state_ref[0], state_ref[1]

        @pl.when(row0 == ps_ref[e])
        def _():
            for copy in fetch(e, s):
                copy.wait()
            nxt = lax.while_loop(lambda n: (n < N_EXPERTS) & (pe_ref[jnp.minimum(n, N_EXPERTS - 1)] <= pe_ref[e]),
                                 lambda n: n + 1, e + 1)

            @pl.when(nxt < N_EXPERTS)
            def _():
                for copy in fetch(nxt, 1 - s):
                    copy.start()

        x = x_ref[...].astype(BF16)
        h = jnp.dot(x, wgu_buf[s].astype(BF16), preferred_element_type=F32) + bgu_ref[pl.ds(e, 1), :]
        gate = jnp.minimum(h[:, :D_FF], SWIGLU_LIMIT)
        up = jnp.clip(h[:, D_FF:], -SWIGLU_LIMIT, SWIGLU_LIMIT)
        act = (up + 1.0) * (gate * jax.nn.sigmoid(SWIGLU_ALPHA * gate))
        y_ref[...] = (jnp.dot(act.astype(BF16), wd_buf[s].astype(BF16), preferred_element_type=F32)
                      + bd_ref[pl.ds(e, 1), :])

    @pl.when(row0 >= pe_ref[N_EXPERTS - 1])
    def _():
        y_ref[...] = jnp.zeros_like(y_ref)


def _experts(xb, pad_start, pad_end, w_gu, b_gu, w_down, b_down, bm):
    rows = xb.shape[0]
    n_blocks = rows // bm
    used = lambda i, ps, pe: (jnp.maximum(jnp.minimum(i, pe[N_EXPERTS - 1] // bm - 1), 0), 0)
    whole = lambda i, ps, pe: (0, 0)
    return pl.pallas_call(
        functools.partial(_expert_kernel, bm=bm),
        grid_spec=pltpu.PrefetchScalarGridSpec(
            num_scalar_prefetch=2,
            grid=(n_blocks,),
            in_specs=[pl.BlockSpec((bm, D_MODEL), used),
                      pl.BlockSpec(memory_space=pl.ANY),
                      pl.BlockSpec((N_EXPERTS, 2 * D_FF), whole),
                      pl.BlockSpec(memory_space=pl.ANY),
                      pl.BlockSpec((N_EXPERTS, D_MODEL), whole)],
            out_specs=pl.BlockSpec((bm, D_MODEL), lambda i, ps, pe: (i, 0)),
            scratch_shapes=[pltpu.VMEM((2, D_MODEL, 2 * D_FF), F32), pltpu.VMEM((2, D_FF, D_MODEL), F32),
                            pltpu.SemaphoreType.DMA((2,)), pltpu.SMEM((2,), jnp.int32)]),
        out_shape=jax.ShapeDtypeStruct((rows, D_MODEL), F32),
        compiler_params=_params("arbitrary"),
        name="experts",
    )(pad_start, pad_end, xb, w_gu, b_gu, w_down, b_down)


def _combine_ln_kernel(dcur_ref, dnext_ref, x_ref, gate_ref, g_ref, b_ref, yb_hbm, op_ref, os_ref, ybuf, sem, *,
                       tm, n_prompt):
    i = pl.program_id(0)
    slot = i % 2

    def issue(dref, s):
        def body(r, carry):
            for k in range(TOP_K):
                _row_copy(yb_hbm, dref[0, 0, r * TOP_K + k], ybuf.at[s, k], r,
                          sem.at[s]).start(priority=k % N_DMA_PRIORITIES)
            return carry
        lax.fori_loop(0, tm, body, 0)

    @pl.when(i == 0)
    def _():
        issue(dcur_ref, 0)

    for s in range(2):
        @pl.when((i + 1 < pl.num_programs(0)) & (slot != s))
        def _():
            issue(dnext_ref, s)

    pltpu.make_async_copy(ybuf.at[slot], ybuf.at[slot], sem.at[slot]).wait()
    gates = gate_ref[...]
    y = sum(gates[:, k:k + 1] * ybuf[slot, k] for k in range(TOP_K))
    out = _layernorm(DEEPNORM_ALPHA * x_ref[...] + y, g_ref[...], b_ref[...])

    @pl.when(i < n_prompt)
    def _():
        op_ref[...] = out

    @pl.when(i >= n_prompt)
    def _():
        os_ref[...] = out


def _combine_ln(x1, gates, dest, yb, ln_g, ln_b, tm, t_prompt):
    t = x1.shape[0]
    n = t // tm
    n_prompt = t_prompt // tm
    row = lambda i: (i, 0)
    const = lambda i: (0, 0)
    d2 = dest.reshape(n, 1, tm * TOP_K)
    return pl.pallas_call(
        functools.partial(_combine_ln_kernel, tm=tm, n_prompt=n_prompt),
        grid=(n,),
        in_specs=[pl.BlockSpec((1, 1, tm * TOP_K), lambda i: (i, 0, 0), memory_space=pltpu.SMEM),
                  pl.BlockSpec((1, 1, tm * TOP_K), lambda i: (jnp.minimum(i + 1, n - 1), 0, 0),
                               memory_space=pltpu.SMEM),
                  pl.BlockSpec((tm, D_MODEL), row),
                  pl.BlockSpec((tm, LANES), row),
                  pl.BlockSpec((1, D_MODEL), const), pl.BlockSpec((1, D_MODEL), const),
                  pl.BlockSpec(memory_space=pl.ANY)],
        out_specs=[pl.BlockSpec((tm, D_MODEL), lambda i: (jnp.minimum(i, n_prompt - 1), 0)),
                   pl.BlockSpec((tm, D_MODEL), lambda i: (jnp.maximum(i - n_prompt, 0), 0))],
        out_shape=[jax.ShapeDtypeStruct((t_prompt, D_MODEL), F32),
                   jax.ShapeDtypeStruct((t - t_prompt, D_MODEL), F32)],
        scratch_shapes=[pltpu.VMEM((2, TOP_K, tm, D_MODEL), F32), pltpu.SemaphoreType.DMA((2,))],
        compiler_params=_params("arbitrary"),
        name="combine_ln",
    )(d2, d2, x1, gates, ln_g.reshape(1, D_MODEL), ln_b.reshape(1, D_MODEL), yb)


MOE_BM = 512
ATTN_BLK = 512
GDN_SUB = 4
ROUTE_TM = 128


def _mixers(x2d, bsz, seq, cache_k, cache_v, conv_past, s0, lam, lam_init, w_in_bf, conv_w, a_log, dt_bias,
            delta_norm_w, subln_w):
    prompt = cache_k is None
    tm = ATTN_BLK if prompt else x2d.shape[0]
    q_bf, k_f, v_f, k_bf, v_bf, cin, z, ab = _in_proj(x2d, w_in_bf, tm, prompt)
    if prompt:
        att = _attn_prompt(q_bf, k_bf, v_bf, lam, subln_w, bsz, seq, ATTN_BLK, lam_init)
    else:
        att = _attn_sample(q_bf, k_bf, v_bf, cache_k, cache_v, 0, lam, subln_w.reshape(1, HEAD), bsz, seq, lam_init)
    chunk = CHUNK if seq % CHUNK == 0 else seq
    n_sub = GDN_SUB if (seq // chunk) % GDN_SUB == 0 else 1
    past8 = jnp.pad(conv_past, ((0, 0), (SUBLANES - (CONV_W - 1), 0), (0, 0)))
    u, w, qd, kd, qk, gl = _gdn_prep(cin, ab, past8, conv_w, a_log, dt_bias, bsz, seq, chunk, n_sub)
    o, s_new = _gdn_scan(u, w, qd, kd, qk, gl, z, s0, delta_norm_w.reshape(1, HEAD), bsz, seq, chunk)
    return att, o, k_f, v_f, cin, s_new


def _moe(x1, idx, gates, rank, counts, w_gu, b_gu, w_down, b_down, ln_g, ln_b, bm, tm, t_prompt):
    t = x1.shape[0]
    n = t * TOP_K
    counts = counts[0, :N_EXPERTS]
    padded = (counts + bm - 1) // bm * bm
    pad_end = jnp.cumsum(padded).astype(jnp.int32)
    pad_start = (pad_end - padded).astype(jnp.int32)
    idx = idx[:, :TOP_K, :]
    dest = rank[:, :TOP_K, :] + sum(jnp.where(idx == e, pad_start[e], 0) for e in range(N_EXPERTS))
    dest = dest.transpose(0, 2, 1).reshape(t, TOP_K)
    n_blocks = -(-n // bm) + N_EXPERTS
    xb = _dispatch(x1, dest, pad_start, pad_end, n_blocks * bm, tm, bm)
    yb = _experts(xb, pad_start, pad_end, w_gu, b_gu, w_down, b_down, bm)
    return _combine_ln(x1, gates, dest, yb, ln_g, ln_b, ROUTE_TM, t_prompt)


def kernel(x_prompt, x_sample, cache_k, cache_v, state_conv, state_delta, w_in, conv_w, a_log, dt_bias,
           delta_norm_w, lambda_q1, lambda_k1, lambda_q2, lambda_k2, subln_w, w_out, ln1_g, ln1_b,
           router_w, router_b, w_gu, b_gu, w_down, b_down, ln2_g, ln2_b):
    bp, lp, _ = x_prompt.shape
    bs, ls, _ = x_sample.shape
    l = 0
    lam_init = 0.8 - 0.6 * math.exp(-0.3 * l)
    lam = (jnp.exp(jnp.sum(lambda_q1[l] * lambda_k1[l])) - jnp.exp(jnp.sum(lambda_q2[l] * lambda_k2[l]))
           + lam_init).reshape(1).astype(F32)
    w_in_bf = jnp.pad(w_in[l], ((0, 0), (0, IN_COLS_PAD - IN_COLS))).astype(BF16)
    shared = (lam, lam_init, w_in_bf, conv_w[l], a_log[l], dt_bias[l], delta_norm_w[l], subln_w[l])

    xp = x_prompt.reshape(bp * lp, D_MODEL)
    xs = x_sample.reshape(bs * ls, D_MODEL)
    zero_conv = jnp.zeros((bp, CONV_W - 1, CONV_CH), F32)
    zero_s = jnp.zeros((bp, N_HEADS, HEAD, HEAD), F32)
    att_p, o_p, k_p, v_p, cin_p, s_p = _mixers(xp, bp, lp, None, None, zero_conv, zero_s, *shared)
    att_s, o_s, k_s, v_s, cin_s, s_s = _mixers(xs, bs, ls, cache_k, cache_v, state_conv[l],
                                               state_delta[l], *shared)

    n_sub = next(c for c in (4, 3, 2, 1) if ((bp * lp + bs * ls) // ROUTE_TM) % c == 0)
    x1, idx, gates, rank, counts = _out_router(((att_p, o_p, xp), (att_s, o_s, xs)), w_out[l].astype(BF16),
                                               ln1_g[l], ln1_b[l], router_w[l], router_b[l], ROUTE_TM, n_sub)
    tp = bp * lp
    tm = n_sub * ROUTE_TM
    y_p, y_s = _moe(x1, idx, gates, rank, counts, w_gu[l], b_gu[l], w_down[l], b_down[l], ln2_g[l], ln2_b[l],
                    MOE_BM, tm, tp)
    conv_tail = lambda cin, b, s: cin.reshape(b, s, CONV_CH)[:, s - (CONV_W - 1):][None]
    return (y_p.reshape(bp, lp, D_MODEL), y_s.reshape(bs, ls, D_MODEL),
            k_p.reshape(1, bp, lp, N_HEADS, HEAD), v_p.reshape(1, bp, lp, N_HEADS, HEAD),
            conv_tail(cin_p, bp, lp), s_p[None].astype(state_delta.dtype),
            k_s.reshape(1, bs, ls, N_HEADS, HEAD), v_s.reshape(1, bs, ls, N_HEADS, HEAD),
            conv_tail(cin_s, bs, ls), s_s[None].astype(state_delta.dtype))
```

```python
import functools
import math

import jax
import jax.numpy as jnp
from jax import lax
from jax.experimental import pallas as pl
from jax.experimental.pallas import tpu as pltpu

F32 = jnp.float32
BF16 = jnp.bfloat16

D_MODEL = 1024
HEAD = 128
N_HEADS = 4
DQK = HEAD // 2
GROUP = N_HEADS * HEAD
CONV_W = 4
CONV_CH = 3 * GROUP
CHUNK = 64
ALIBI_MAX = 8.0
N_EXPERTS = 32
TOP_K = 4
D_FF = D_MODEL
SWIGLU_LIMIT = 7.0
SWIGLU_ALPHA = 1.702
DEPTH = 1
DEEPNORM_ALPHA = (2 * DEPTH) ** 0.25
LN_EPS = 1e-5
SUBLN_EPS = 1e-5
GATED_NORM_EPS = 1e-6
L2_EPS = 1e-6

LANES = 128
SUBLANES = 8
BF16_EXACT_INT = 256
BF16_ROWS = 16
LOG2E = 1.4426950408889634
N_POS = 6
ONES_ROWS = BF16_ROWS
N_DMA_PRIORITIES = 2
VMEM_LIMIT = 56 * 1024 * 1024

COL_Q, COL_K, COL_V, COL_CONV = 0, GROUP, 2 * GROUP, 3 * GROUP
COL_Z = COL_CONV + CONV_CH
COL_AB = COL_Z + GROUP
IN_COLS = COL_AB + 2 * N_HEADS
IN_COLS_PAD = COL_AB + LANES


def _params(*sem):
    return pltpu.CompilerParams(dimension_semantics=sem, vmem_limit_bytes=VMEM_LIMIT)


def _in_proj_kernel(x_ref, w_ref, q_ref, kf_ref, vf_ref, kb_ref, vb_ref, c_ref, z_ref, ab_ref, *, tm, transposed):
    xb = x_ref[...].astype(BF16)

    def section(lo, hi):
        return jnp.dot(xb, w_ref[:, lo:hi], preferred_element_type=F32)

    q = section(COL_Q, COL_K) * (DQK ** -0.5 * (LOG2E if transposed else 1.0))
    k = section(COL_K, COL_V)
    kf_ref[0] = k.reshape(tm, N_HEADS, HEAD)
    kb_ref[...] = k.astype(BF16)
    v = section(COL_V, COL_CONV)
    vf_ref[0] = v.reshape(tm, N_HEADS, HEAD)
    if transposed:
        q_ref[0] = q.T.astype(BF16)
        vb_ref[0] = v.T.astype(BF16)
    else:
        q_ref[...] = q.astype(BF16)
        vb_ref[...] = v.astype(BF16)
    c_ref[...] = section(COL_CONV, COL_Z)
    z_ref[...] = section(COL_Z, COL_AB)
    ab_ref[...] = section(COL_AB, IN_COLS_PAD)


def _in_proj(x2d, w_bf, tm, transposed):
    t = x2d.shape[0]
    row = lambda i: (i, 0)
    widths = (GROUP, GROUP, GROUP, GROUP, GROUP, CONV_CH, GROUP, LANES)
    dtypes = (BF16, F32, F32, BF16, BF16, F32, F32, F32)
    out_specs = [pl.BlockSpec((tm, w), row) for w in widths]
    out_shape = [jax.ShapeDtypeStruct((t, w), d) for w, d in zip(widths, dtypes)]
    for slot in (1, 2):
        out_specs[slot] = pl.BlockSpec((1, tm, N_HEADS, HEAD), lambda i: (i, 0, 0, 0))
        out_shape[slot] = jax.ShapeDtypeStruct((t // tm, tm, N_HEADS, HEAD), F32)
    if transposed:
        for slot in (0, 4):
            out_specs[slot] = pl.BlockSpec((1, GROUP, tm), lambda i: (i, 0, 0))
            out_shape[slot] = jax.ShapeDtypeStruct((t // tm, GROUP, tm), BF16)
    return pl.pallas_call(
        functools.partial(_in_proj_kernel, tm=tm, transposed=transposed),
        grid=(t // tm,),
        in_specs=[pl.BlockSpec((tm, D_MODEL), row),
                  pl.BlockSpec((D_MODEL, IN_COLS_PAD), lambda i: (0, 0))],
        out_specs=out_specs,
        out_shape=out_shape,
        compiler_params=_params("parallel"),
        name="in_proj",
    )(x2d, w_bf)


def _alibi_slopes():
    return [2.0 ** (-ALIBI_MAX * (h + 1) / N_HEADS) for h in range(N_HEADS)]


def _stack_halves(q):
    lane = lax.broadcasted_iota(jnp.int32, q.shape, 1)
    zero = jnp.zeros_like(q)
    return jnp.concatenate([jnp.where(lane < DQK, q, zero), jnp.where(lane < DQK, zero, q)], axis=0)


def _diff_norm(acc, l, lam, w, lam_init, rows):
    o = acc[:rows] / l[:rows] - lam * (acc[rows:] / l[rows:])
    ms = jnp.mean(o * o, axis=-1, keepdims=True)
    return o * lax.rsqrt(ms + SUBLN_EPS) * w * (1.0 - lam_init)


def _head_slope(h):
    s = _alibi_slopes()
    return jnp.where(h == 0, s[0], jnp.where(h == 1, s[1], jnp.where(h == 2, s[2], s[3]))).astype(F32)


def _attn_prompt_kernel(lam_ref, qt_ref, k_ref, vt_ref, w_ref, o_ref,
                        diag_ref, kaug_ref, vaug_ref, qz_ref, s0_ref, s1_ref, p0_ref, p1_ref, mx0_ref, mx1_ref, m_ref, acc_ref, *, blk, lam_init):
    i = pl.program_id(2)
    slope = _head_slope(pl.program_id(1)) * LOG2E
    rows = 2 * blk
    n_kv = k_ref.shape[0] // blk

    def pieces(x):
        lo = x % BF16_EXACT_INT
        return _split3(slope * lo.astype(F32)) + _split3(slope * (x - lo).astype(F32))

    @pl.when(i == 0)
    def _():
        lane = lax.broadcasted_iota(jnp.int32, (blk, HEAD), 1)
        extra = jnp.where(lane < N_POS, 1.0, 0.0).astype(BF16)
        for n, piece in enumerate(pieces(lax.broadcasted_iota(jnp.int32, (blk, 1), 0))):
            extra = jnp.where(lane == N_POS + n, piece, extra)
        ones = jnp.ones((ONES_ROWS, blk), BF16)
        c = lax.broadcasted_iota(jnp.int32, (blk, rows), 0)
        a = lax.broadcasted_iota(jnp.int32, (blk, rows), 1) % blk
        diag_ref[...] = jnp.where(c // CHUNK <= a // CHUNK, -slope * jnp.abs(a - c).astype(F32), -jnp.inf)
        r = lax.broadcasted_iota(jnp.int32, (HEAD, rows), 0)
        qx = jnp.where((r >= N_POS) & (r < 2 * N_POS), 1.0, 0.0).astype(BF16)
        for n, piece in enumerate(pieces(-(lax.broadcasted_iota(jnp.int32, (1, rows), 1) % blk))):
            qx = jnp.where(r == n, piece, qx)
        qz_ref[HEAD:, :] = qx

        def fill(j, carry):
            j0 = pl.multiple_of(j * blk, blk)
            kaug_ref[pl.ds(j0, blk), :HEAD] = k_ref[pl.ds(j0, blk), :]
            kaug_ref[pl.ds(j0, blk), HEAD:] = extra
            vaug_ref[j, :HEAD, :] = vt_ref[j]
            vaug_ref[j, HEAD:, :] = ones
            return carry

        lax.fori_loop(0, n_kv, fill, 0)

    qt = qt_ref[0]
    d = lax.broadcasted_iota(jnp.int32, qt.shape, 0)
    zero = jnp.zeros_like(qt)
    qz_ref[:HEAD, :] = jnp.concatenate([jnp.where(d < DQK, qt, zero), jnp.where(d < DQK, zero, qt)], axis=1)

    def tile_rows(j):
        return pl.ds(pl.multiple_of(j * blk, blk), blk)

    s = jnp.dot(k_ref[tile_rows(i), :], qz_ref[:HEAD, :], preferred_element_type=F32) + diag_ref[...]
    m = jnp.max(s, axis=0, keepdims=True)
    p1_ref[...] = jnp.exp2(s - m).astype(BF16)
    m_ref[...] = m
    acc_ref[...] = jnp.zeros_like(acc_ref)
    s_first = jnp.dot(kaug_ref[tile_rows(0), :], qz_ref[...], preferred_element_type=F32)
    s0_ref[...] = s_first
    mx0_ref[...] = jnp.max(s_first, axis=0, keepdims=True)

    def step(j, s_cur, s_nxt, p_cur, p_nxt, mx_cur, mx_nxt):
        s_next = jnp.dot(kaug_ref[tile_rows(jnp.minimum(j + 1, i - 1)), :], qz_ref[...],
                         preferred_element_type=F32)
        s_nxt[...] = s_next
        mx_nxt[...] = jnp.max(s_next, axis=0, keepdims=True)
        pv = jnp.dot(vaug_ref[jnp.where(j == 0, i, j - 1)], p_nxt[...], preferred_element_type=F32)
        shift = -slope * ((i - j) * blk).astype(F32)
        m_old = m_ref[...]
        m_new = jnp.maximum(m_old, mx_cur[...] + shift)
        p_cur[...] = jnp.exp2(s_cur[...] - (m_new - shift)).astype(BF16)
        acc_ref[...] = (acc_ref[...] + pv) * jnp.exp2(m_old - m_new)
        m_ref[...] = m_new

    def body(jj, carry):
        step(2 * jj, s0_ref, s1_ref, p0_ref, p1_ref, mx0_ref, mx1_ref)

        @pl.when(2 * jj + 1 < i)
        def _():
            step(2 * jj + 1, s1_ref, s0_ref, p1_ref, p0_ref, mx1_ref, mx0_ref)

        return carry

    lax.fori_loop(0, (i + 1) // 2, body, 0)
    p_last = jnp.where(i % 2 == 1, p0_ref[...], p1_ref[...])
    acc = acc_ref[...] + jnp.dot(vaug_ref[jnp.where(i > 0, i - 1, i)], p_last, preferred_element_type=F32)
    l = acc[HEAD:HEAD + 1, :]
    num = acc[:HEAD, :]
    ot = num[:, :blk] / l[:, :blk] - lam_ref[0] * (num[:, blk:] / l[:, blk:])
    ms = jnp.mean(ot * ot, axis=0, keepdims=True)
    ot = ot * lax.rsqrt(ms + SUBLN_EPS) * w_ref[...] * (1.0 - lam_init)
    o_ref[...] = ot.T.astype(o_ref.dtype)


def _attn_prompt(qt_bf, k_bf, vt_bf, lam, subln_w, bsz, seq, blk, lam_init):
    nq = seq // blk
    kern = functools.partial(_attn_prompt_kernel, blk=blk, lam_init=lam_init)
    return pl.pallas_call(
        kern,
        grid=(bsz, N_HEADS, nq),
        in_specs=[pl.BlockSpec(memory_space=pltpu.SMEM),
                  pl.BlockSpec((1, HEAD, blk), lambda b, h, i: (b * nq + i, h, 0)),
                  pl.BlockSpec((seq, HEAD), lambda b, h, i: (b, h)),
                  pl.BlockSpec((nq, HEAD, blk), lambda b, h, i: (b, h, 0)),
                  pl.BlockSpec((HEAD, 1), lambda b, h, i: (0, 0))],
        out_specs=pl.BlockSpec((blk, HEAD), lambda b, h, i: (b * nq + i, h)),
        scratch_shapes=[pltpu.VMEM((blk, 2 * blk), F32),
                        pltpu.VMEM((seq, 2 * HEAD), BF16),
                        pltpu.VMEM((nq, HEAD + ONES_ROWS, blk), BF16),
                        pltpu.VMEM((2 * HEAD, 2 * blk), BF16),
                        pltpu.VMEM((blk, 2 * blk), F32), pltpu.VMEM((blk, 2 * blk), F32),
                        pltpu.VMEM((blk, 2 * blk), BF16), pltpu.VMEM((blk, 2 * blk), BF16),
                        pltpu.VMEM((1, 2 * blk), F32), pltpu.VMEM((1, 2 * blk), F32),
                        pltpu.VMEM((1, 2 * blk), F32),
                        pltpu.VMEM((HEAD + ONES_ROWS, 2 * blk), F32)],
        out_shape=jax.ShapeDtypeStruct((bsz * seq, GROUP), BF16),
        compiler_params=_params("parallel", "parallel", "arbitrary"),
        name="attn_prompt",
    )(lam, qt_bf, k_bf, vt_bf, subln_w.reshape(HEAD, 1))


def _attn_sample_kernel(lam_ref, q_ref, kn_ref, vn_ref, kc_ref, vc_ref, w_ref, o_ref, *, seq, past, lam_init):
    nt = (((1,), (1,)), ((), ()))
    qpos = past + lax.broadcasted_iota(jnp.int32, (2 * seq, 1), 0) % seq
    rel_c = jnp.abs(qpos - lax.broadcasted_iota(jnp.int32, (1, past), 1)).astype(F32)
    rel_n = jnp.abs(qpos - (past + lax.broadcasted_iota(jnp.int32, (1, seq), 1))).astype(F32)
    for h, slope in enumerate(_alibi_slopes()):
        col = slice(h * HEAD, (h + 1) * HEAD)
        qz = _stack_halves(q_ref[:, col])
        s_c = lax.dot_general(qz, kc_ref[0, 0, :, h, :].astype(BF16), nt, preferred_element_type=F32)
        s_c = s_c - slope * rel_c
        s_n = lax.dot_general(qz, kn_ref[:, col], nt, preferred_element_type=F32) - slope * rel_n
        m = jnp.maximum(jnp.max(s_c, axis=-1, keepdims=True), jnp.max(s_n, axis=-1, keepdims=True))
        p_c = jnp.exp(s_c - m)
        p_n = jnp.exp(s_n - m)
        l = jnp.sum(p_c, axis=-1, keepdims=True) + jnp.sum(p_n, axis=-1, keepdims=True)
        acc = (jnp.dot(p_c.astype(BF16), vc_ref[0, 0, :, h, :].astype(BF16), preferred_element_type=F32)
               + jnp.dot(p_n.astype(BF16), vn_ref[:, col], preferred_element_type=F32))
        o_ref[:, col] = _diff_norm(acc, l, lam_ref[0], w_ref[...], lam_init, seq).astype(o_ref.dtype)


def _attn_sample(q_bf, k_bf, v_bf, cache_k, cache_v, layer, lam, subln_w, bsz, seq, lam_init):
    past = cache_k.shape[2]
    kern = functools.partial(_attn_sample_kernel, seq=seq, past=past, lam_init=lam_init)
    new = pl.BlockSpec((seq, GROUP), lambda b: (b, 0))
    cache = pl.BlockSpec((1, 1, past, N_HEADS, HEAD), lambda b: (layer, b, 0, 0, 0))
    return pl.pallas_call(
        kern,
        grid=(bsz,),
        in_specs=[pl.BlockSpec(memory_space=pltpu.SMEM), new, new, new, cache, cache,
                  pl.BlockSpec((1, HEAD), lambda b: (0, 0))],
        out_specs=new,
        out_shape=jax.ShapeDtypeStruct((bsz * seq, GROUP), BF16),
        compiler_params=_params("parallel"),
        name="attn_sample",
    )(lam, q_bf, k_bf, v_bf, cache_k, cache_v, subln_w)


def _split3(x):
    hi = x.astype(BF16)
    r1 = x - hi.astype(F32)
    mid = r1.astype(BF16)
    lo = (r1 - mid.astype(F32)).astype(BF16)
    return hi, mid, lo


def _gdn_prep_kernel(cin_ref, halo_ref, past_ref, ab_ref, cw_ref, alog_ref, dtb_ref,
                     u_ref, w_ref, qd_ref, kd_ref, qk_ref, gl_ref, *, chunk, n_sub):
    c_idx = pl.program_id(1)
    rows = chunk * n_sub
    prev = jnp.where(c_idx == 0, past_ref[0], halo_ref[...])
    xin = jnp.concatenate([prev, cin_ref[...]], axis=0)
    conv = sum(xin[SUBLANES - (CONV_W - 1) + j: SUBLANES - (CONV_W - 1) + j + rows] * cw_ref[j:j + 1, :]
               for j in range(CONV_W))
    conv = conv * jax.nn.sigmoid(conv)

    ab = ab_ref[...]
    lane = lax.broadcasted_iota(jnp.int32, ab.shape, 1)
    pre = ab + dtb_ref[...]
    softplus = jnp.maximum(pre, 0.0) + jnp.log(1.0 + jnp.exp(-jnp.abs(pre)))
    g = jnp.where(lane < N_HEADS, -jnp.exp(alog_ref[...]) * softplus, 0.0)
    beta_all = jax.nn.sigmoid(ab)

    ri = lax.broadcasted_iota(jnp.int32, (rows, rows), 0)
    ci = lax.broadcasted_iota(jnp.int32, (rows, rows), 1)
    same = (ri // chunk) == (ci // chunk)
    incl = same & (ri >= ci)
    strict = same & (ri > ci)
    eye = jnp.where(ri == ci, 1.0, 0.0).astype(F32)
    nt = (((1,), (1,)), ((), ()))
    g_parts = _split3(g)
    ones_incl = jnp.where(incl, 1.0, 0.0).astype(BF16)
    ones_same = jnp.where(same, 1.0, 0.0).astype(BF16)
    gc = sum(jnp.dot(ones_incl, part, preferred_element_type=F32) for part in g_parts)
    g_end = sum(jnp.dot(ones_same, part, preferred_element_type=F32) for part in g_parts)
    gct = gc.T
    for sc in range(n_sub):
        gl_ref[sc] = g_end[sc * chunk:sc * chunk + 1, :]

    t_mats, p_mats, rhs = [], [], []
    for h in range(N_HEADS):
        col = slice(h * HEAD, (h + 1) * HEAD)
        qh = conv[:, h * HEAD:(h + 1) * HEAD]
        kh = conv[:, GROUP + h * HEAD:GROUP + (h + 1) * HEAD]
        vh = conv[:, 2 * GROUP + h * HEAD:2 * GROUP + (h + 1) * HEAD]
        qh = qh * lax.rsqrt(jnp.sum(qh * qh, axis=-1, keepdims=True) + L2_EPS) * (HEAD ** -0.5)
        kh = kh * lax.rsqrt(jnp.sum(kh * kh, axis=-1, keepdims=True) + L2_EPS)
        beta = beta_all[:, N_HEADS + h:N_HEADS + h + 1]
        gcol = gc[:, h:h + 1]
        grow = gct[h:h + 1, :]
        gamma = jnp.exp(jnp.where(incl, gcol - grow, -jnp.inf))
        egc = jnp.exp(gcol)
        kb = kh * beta
        khb = kh.astype(BF16)
        a = jnp.where(strict, lax.dot_general(kb.astype(BF16), khb, nt, preferred_element_type=F32) * gamma, 0.0)
        qk = (lax.dot_general(qh.astype(BF16), khb, nt, preferred_element_type=F32) * gamma).astype(BF16)
        for sc in range(n_sub):
            blk = slice(sc * chunk, (sc + 1) * chunk)
            qk_ref[h, blk, :] = qk[blk, blk]
        qd_ref[:, col] = (qh * egc).astype(BF16)
        kd_ref[:, col] = (kh * jnp.exp(g_end[:, h:h + 1] - gcol)).astype(BF16)
        t_mats.append(eye - a)
        p_mats.append(a)
        rhs.append(((vh * beta).astype(BF16), (kb * egc).astype(BF16)))

    for _ in range(int(math.log2(chunk)) - 1):
        for h in range(N_HEADS):
            pb = p_mats[h].astype(BF16)
            p_mats[h] = jnp.dot(pb, pb, preferred_element_type=F32)
        for h in range(N_HEADS):
            t_mats[h] = t_mats[h] + jnp.dot(t_mats[h].astype(BF16), p_mats[h].astype(BF16),
                                            preferred_element_type=F32)

    for h in range(N_HEADS):
        col = slice(h * HEAD, (h + 1) * HEAD)
        t_inv = t_mats[h].astype(BF16)
        u_ref[:, col] = jnp.dot(t_inv, rhs[h][0], preferred_element_type=F32)
        w_ref[:, col] = jnp.dot(t_inv, rhs[h][1], preferred_element_type=F32).astype(BF16)


def _gdn_prep(cin, ab, past8, conv_w, a_log, dt_bias, bsz, seq, chunk, n_sub):
    rows = chunk * n_sub
    nblk = seq // rows
    t = bsz * seq
    lanes = lambda v: jnp.pad(v.reshape(1, N_HEADS).astype(F32), ((0, 0), (0, LANES - N_HEADS)))
    kern = functools.partial(_gdn_prep_kernel, chunk=chunk, n_sub=n_sub)
    rowblk = lambda b, c: (b * nblk + c, 0)
    halo = lambda b, c: (jnp.maximum((b * nblk + c) * (rows // SUBLANES) - 1, 0), 0)
    const = lambda b, c: (0, 0)
    return pl.pallas_call(
        kern,
        grid=(bsz, nblk),
        in_specs=[pl.BlockSpec((rows, CONV_CH), rowblk),
                  pl.BlockSpec((SUBLANES, CONV_CH), halo),
                  pl.BlockSpec((1, SUBLANES, CONV_CH), lambda b, c: (b, 0, 0)),
                  pl.BlockSpec((rows, LANES), rowblk),
                  pl.BlockSpec((CONV_W, CONV_CH), const),
                  pl.BlockSpec((1, LANES), const),
                  pl.BlockSpec((1, LANES), const)],
        out_specs=[pl.BlockSpec((rows, GROUP), rowblk),
                   pl.BlockSpec((rows, GROUP), rowblk),
                   pl.BlockSpec((rows, GROUP), rowblk),
                   pl.BlockSpec((rows, GROUP), rowblk),
                   pl.BlockSpec((N_HEADS, rows, chunk), lambda b, c: (0, b * nblk + c, 0)),
                   pl.BlockSpec((n_sub, 1, LANES), lambda b, c: (b * nblk + c, 0, 0))],
        out_shape=[jax.ShapeDtypeStruct((t, GROUP), F32),
                   jax.ShapeDtypeStruct((t, GROUP), BF16),
                   jax.ShapeDtypeStruct((t, GROUP), BF16),
                   jax.ShapeDtypeStruct((t, GROUP), BF16),
                   jax.ShapeDtypeStruct((N_HEADS, t, chunk), BF16),
                   jax.ShapeDtypeStruct((t // chunk, 1, LANES), F32)],
        compiler_params=_params("parallel", "parallel"),
        name="gdn_prep",
    )(cin, cin, past8, ab, conv_w, lanes(a_log), lanes(dt_bias))


def _gdn_scan_kernel(u_ref, w_ref, qd_ref, kd_ref, qk_ref, gl_ref, z_ref, s0_ref, nw_ref,
                     o_ref, sf_ref, s_ref, *, bsz, chunk):
    c_idx = pl.program_id(0)

    @pl.when(c_idx == 0)
    def _():
        s_ref[...] = s0_ref[...]

    tn = (((0,), (0,)), ((), ()))
    chains = [(b, h, slice(h * HEAD, (h + 1) * HEAD)) for b in range(bsz) for h in range(N_HEADS)]
    states = [s_ref[b, h] for b, h, _ in chains]
    prods = [jnp.dot(jnp.concatenate([w_ref[b, :, col], qd_ref[b, :, col]], axis=0), s.astype(BF16),
                     preferred_element_type=F32) for (b, _, col), s in zip(chains, states)]
    v_news = [(u_ref[b, :, col] - r[:chunk]).astype(BF16) for (b, _, col), r in zip(chains, prods)]
    outs = [r[chunk:] + jnp.dot(qk_ref[h, b], v, preferred_element_type=F32)
            for (b, h, _), r, v in zip(chains, prods, v_news)]
    for (b, h, col), s, v in zip(chains, states, v_news):
        glast = jnp.exp(gl_ref[b, 0])
        s_ref[b, h] = s * glast[:, h:h + 1] + lax.dot_general(kd_ref[b, :, col], v, tn,
                                                             preferred_element_type=F32)
    for (b, h, col), o in zip(chains, outs):
        zh = z_ref[b, :, col]
        ms = jnp.mean(o * o, axis=-1, keepdims=True)
        o = o * lax.rsqrt(ms + GATED_NORM_EPS) * nw_ref[...] * (zh * jax.nn.sigmoid(zh))
        o_ref[b, :, col] = o.astype(o_ref.dtype)

    @pl.when(c_idx == pl.num_programs(0) - 1)
    def _():
        sf_ref[...] = s_ref[...]


def _gdn_scan(u, w, qd, kd, qk, gl, z, s0, norm_w, bsz, seq, chunk):
    nc = seq // chunk
    kern = functools.partial(_gdn_scan_kernel, bsz=bsz, chunk=chunk)
    tok = pl.BlockSpec((bsz, chunk, GROUP), lambda c: (0, c, 0))
    state = pl.BlockSpec((bsz, N_HEADS, HEAD, HEAD), lambda c: (0, 0, 0, 0))
    o, s_final = pl.pallas_call(
        kern,
        grid=(nc,),
        in_specs=[tok, tok, tok, tok,
                  pl.BlockSpec((N_HEADS, bsz, chunk, chunk), lambda c: (0, 0, c, 0)),
                  pl.BlockSpec((bsz, 1, 1, LANES), lambda c: (0, c, 0, 0)),
                  tok, state,
                  pl.BlockSpec((1, HEAD), lambda c: (0, 0))],
        out_specs=[tok, state],
        out_shape=[jax.ShapeDtypeStruct((bsz, seq, GROUP), BF16),
                   jax.ShapeDtypeStruct((bsz, N_HEADS, HEAD, HEAD), F32)],
        scratch_shapes=[pltpu.VMEM((bsz, N_HEADS, HEAD, HEAD), F32)],
        compiler_params=_params("arbitrary"),
        name="gdn_scan",
    )(u.reshape(bsz, seq, GROUP), w.reshape(bsz, seq, GROUP), qd.reshape(bsz, seq, GROUP),
      kd.reshape(bsz, seq, GROUP), qk.reshape(N_HEADS, bsz, seq, chunk), gl.reshape(bsz, nc, 1, LANES),
      z.reshape(bsz, seq, GROUP), s0, norm_w)
    return o.reshape(bsz * seq, GROUP), s_final


def _layernorm(x, g, b):
    mu = jnp.mean(x, axis=-1, keepdims=True)
    xc = x - mu
    var = jnp.mean(xc * xc, axis=-1, keepdims=True)
    return xc * lax.rsqrt(var + LN_EPS) * g + b


def _out_router_kernel(*refs, tm, n_sub, n_prompt):
    tiles = [refs[6 * u:6 * u + 6] for u in range(n_sub)]
    wo_ref, g_ref, b_ref, rw_ref, rb_ref, x1_ref, idx_ref, gate_ref, rank_ref, cnt_ref, carry_ref = refs[6 * n_sub:]
    step = pl.program_id(0)
    units = range(n_sub)

    @pl.when(step == 0)
    def _():
        carry_ref[...] = jnp.zeros_like(carry_ref)

    x1s = []
    for u in units:
        attp_ref, atts_ref, op_ref, os_ref, xp_ref, xs_ref = tiles[u]
        prompt = step * n_sub + u < n_prompt
        att = jnp.where(prompt, attp_ref[...], atts_ref[...])
        o = jnp.where(prompt, op_ref[...], os_ref[...])
        x = jnp.where(prompt, xp_ref[...], xs_ref[...])
        mix = (jnp.dot(att, wo_ref[:GROUP, :], preferred_element_type=F32)
               + jnp.dot(o, wo_ref[GROUP:, :], preferred_element_type=F32))
        x1 = _layernorm(DEEPNORM_ALPHA * x + mix, g_ref[...], b_ref[...])
        x1_ref[u * tm:(u + 1) * tm, :] = x1
        x1s.append(x1)

    works = []
    lane = lax.broadcasted_iota(jnp.int32, (tm, LANES), 1)
    for x1 in x1s:
        x_hi = x1.astype(BF16)
        x_lo = (x1 - x_hi.astype(F32)).astype(BF16)
        logits = (jnp.dot(x_hi, rw_ref[0], preferred_element_type=F32)
                  + jnp.dot(x_lo, rw_ref[0], preferred_element_type=F32)
                  + jnp.dot(x_hi, rw_ref[1], preferred_element_type=F32)) + rb_ref[...]
        works.append(jnp.where(lane < N_EXPERTS, logits, -jnp.inf))

    vals, idxs = [[] for _ in units], [[] for _ in units]
    for _ in range(TOP_K):
        for u in units:
            m = jnp.max(works[u], axis=-1, keepdims=True)
            am = jnp.min(jnp.where(works[u] == m, lane, LANES), axis=-1, keepdims=True)
            vals[u].append(m)
            idxs[u].append(am)
            works[u] = jnp.where(lane == am, -jnp.inf, works[u])

    ri = lax.broadcasted_iota(jnp.int32, (tm, tm), 0)
    ci = lax.broadcasted_iota(jnp.int32, (tm, tm), 1)
    before = jnp.where(ri > ci, 1.0, 0.0).astype(BF16)
    base = carry_ref[...]
    for u in units:
        exps = [jnp.exp(v - vals[u][0]) for v in vals[u]]
        denom = exps[0] + exps[1] + exps[2] + exps[3]
        chosen = jnp.zeros((tm, LANES), F32)
        gate_out = jnp.zeros((tm, LANES), F32)
        idx_out = jnp.zeros((tm, LANES), jnp.int32)
        for k in range(TOP_K):
            chosen = jnp.where(lane == idxs[u][k], 1.0, chosen)
            gate_out = jnp.where(lane == k, exps[k] / denom, gate_out)
            idx_out = jnp.where(lane == k, idxs[u][k], idx_out)
        prefix = jnp.dot(before, chosen.astype(BF16), preferred_element_type=F32) + base
        base = base + jnp.sum(chosen, axis=0, keepdims=True)
        rank_out = jnp.zeros((tm, LANES), F32)
        for k in range(TOP_K):
            r = jnp.sum(jnp.where(lane == idxs[u][k], prefix, 0.0), axis=-1, keepdims=True)
            rank_out = jnp.where(lane == k, r, rank_out)
        idx_ref[0, :, u * tm:(u + 1) * tm] = idx_out.T[:SUBLANES, :]
        gate_ref[u * tm:(u + 1) * tm, :] = gate_out
        rank_ref[0, :, u * tm:(u + 1) * tm] = rank_out.astype(jnp.int32).T[:SUBLANES, :]
    carry_ref[...] = base
    cnt_ref[...] = base.astype(jnp.int32)


def _out_router(streams, w_out_bf, ln_g, ln_b, router_w, router_b, tm, n_sub):
    (att_p, o_p, x_p), (att_s, o_s, x_s) = streams
    n_prompt = x_p.shape[0] // tm
    n_sample = x_s.shape[0] // tm
    t = x_p.shape[0] + x_s.shape[0]
    step_rows = tm * n_sub
    row = lambda i: (i, 0)
    const = lambda i: (0, 0)
    slots = pl.BlockSpec((1, SUBLANES, step_rows), lambda i: (i, 0, 0))
    tile_specs, tile_args = [], []
    for u in range(n_sub):
        prow = lambda i, u=u: (jnp.minimum(i * n_sub + u, n_prompt - 1), 0)
        srow = lambda i, u=u: (jnp.clip(i * n_sub + u - n_prompt, 0, n_sample - 1), 0)
        tile_specs += [pl.BlockSpec((tm, GROUP), prow), pl.BlockSpec((tm, GROUP), srow),
                       pl.BlockSpec((tm, GROUP), prow), pl.BlockSpec((tm, GROUP), srow),
                       pl.BlockSpec((tm, D_MODEL), prow), pl.BlockSpec((tm, D_MODEL), srow)]
        tile_args += [att_p, att_s, o_p, o_s, x_p, x_s]
    rw = jnp.pad(router_w, ((0, 0), (0, LANES - N_EXPERTS)))
    rw_hi = rw.astype(BF16)
    rw = jnp.stack([rw_hi, (rw - rw_hi.astype(F32)).astype(BF16)])
    rb = jnp.pad(router_b.reshape(1, N_EXPERTS), ((0, 0), (0, LANES - N_EXPERTS)))
    kern = functools.partial(_out_router_kernel, tm=tm, n_sub=n_sub, n_prompt=n_prompt)
    return pl.pallas_call(
        kern,
        grid=(t // step_rows,),
        in_specs=tile_specs + [pl.BlockSpec((2 * GROUP, D_MODEL), const),
                               pl.BlockSpec((1, D_MODEL), const), pl.BlockSpec((1, D_MODEL), const),
                               pl.BlockSpec((2, D_MODEL, LANES), lambda i: (0, 0, 0)),
                               pl.BlockSpec((1, LANES), const)],
        out_specs=[pl.BlockSpec((step_rows, D_MODEL), row), slots, pl.BlockSpec((step_rows, LANES), row), slots,
                   pl.BlockSpec((1, LANES), const)],
        out_shape=[jax.ShapeDtypeStruct((t, D_MODEL), F32),
                   jax.ShapeDtypeStruct((t // step_rows, SUBLANES, step_rows), jnp.int32),
                   jax.ShapeDtypeStruct((t, LANES), F32),
                   jax.ShapeDtypeStruct((t // step_rows, SUBLANES, step_rows), jnp.int32),
                   jax.ShapeDtypeStruct((1, LANES), jnp.int32)],
        scratch_shapes=[pltpu.VMEM((1, LANES), F32)],
        compiler_params=_params("arbitrary"),
        name="out_router",
    )(*tile_args, w_out_bf, ln_g.reshape(1, D_MODEL), ln_b.reshape(1, D_MODEL), rw, rb)


def _row_copy(src, src_row, dst, dst_row, sem):
    return pltpu.make_async_copy(src.at[pl.ds(src_row, 1), :], dst.at[pl.ds(dst_row, 1), :], sem)


def _dispatch_kernel(ps_ref, pe_ref, dest_ref, x_ref, xb_hbm, zero_ref, sem, zsem, *, tm, bm):
    i = pl.program_id(0)

    def zero_block(row):
        return pltpu.make_async_copy(zero_ref, xb_hbm.at[pl.ds(pl.multiple_of(row, bm), bm), :], zsem)

    def fill(e):
        return zero_block(pe_ref[e] - bm)

    @pl.when(i == 0)
    def _():
        zero_ref[...] = jnp.zeros_like(zero_ref)
        for e in range(N_EXPERTS):
            @pl.when(pe_ref[e] > ps_ref[e])
            def _():
                fill(e).start()
        first_unused = pe_ref[N_EXPERTS - 1] // bm
        n_blocks = xb_hbm.shape[0] // bm
        lax.fori_loop(first_unused, n_blocks, lambda b, c: (zero_block(b * bm).start(), c)[1], 0)
        for e in range(N_EXPERTS):
            @pl.when(pe_ref[e] > ps_ref[e])
            def _():
                fill(e).wait()
        lax.fori_loop(first_unused, n_blocks, lambda b, c: (zero_block(b * bm).wait(), c)[1], 0)

    def issue(r, carry):
        for k in range(TOP_K):
            _row_copy(x_ref, r, xb_hbm, dest_ref[0, 0, r * TOP_K + k], sem).start(priority=k % N_DMA_PRIORITIES)
        return carry

    lax.fori_loop(0, tm, issue, 0)
    pltpu.make_async_copy(xb_hbm.at[pl.ds(0, tm * TOP_K), :], xb_hbm.at[pl.ds(0, tm * TOP_K), :], sem).wait()


def _dispatch(x1, dest, pad_start, pad_end, rows, tm, bm):
    t = x1.shape[0]
    return pl.pallas_call(
        functools.partial(_dispatch_kernel, tm=tm, bm=bm),
        grid_spec=pltpu.PrefetchScalarGridSpec(
            num_scalar_prefetch=2,
            grid=(t // tm,),
            in_specs=[pl.BlockSpec((1, 1, tm * TOP_K), lambda i, ps, pe: (i, 0, 0), memory_space=pltpu.SMEM),
                      pl.BlockSpec((tm, D_MODEL), lambda i, ps, pe: (i, 0))],
            out_specs=pl.BlockSpec(memory_space=pl.ANY),
            scratch_shapes=[pltpu.VMEM((bm, D_MODEL), F32),
                            pltpu.SemaphoreType.DMA(()), pltpu.SemaphoreType.DMA(())]),
        out_shape=jax.ShapeDtypeStruct((rows, D_MODEL), F32),
        compiler_params=_params("arbitrary"),
        name="dispatch",
    )(pad_start, pad_end, dest.reshape(t // tm, 1, tm * TOP_K), x1)


def _expert_kernel(ps_ref, pe_ref, x_ref, wgu_hbm, bgu_ref, wd_hbm, bd_ref, y_ref,
                   wgu_buf, wd_buf, sem, state_ref, *, bm):
    blk = pl.program_id(0)
    row0 = blk * bm

    def fetch(e, s):
        return (pltpu.make_async_copy(wgu_hbm.at[e], wgu_buf.at[s], sem.at[s]),
                pltpu.make_async_copy(wd_hbm.at[e], wd_buf.at[s], sem.at[s]))

    def next_with_rows(e):
        return lax.while_loop(lambda n: (n < N_EXPERTS) & (pe_ref[jnp.minimum(n, N_EXPERTS - 1)] <= row0),
                              lambda n: n + 1, e)

    @pl.when(row0 < pe_ref[N_EXPERTS - 1])
    def _():
        @pl.when(blk == 0)
        def _():
            first = next_with_rows(0)
            state_ref[0] = first
            state_ref[1] = 0
            for copy in fetch(first, 0):
                copy.start()

        @pl.when((blk > 0) & (row0 >= pe_ref[state_ref[0]]))
        def _():
            state_ref[0] = next_with_rows(state_ref[0])
            state_ref[1] = 1 - state_ref[1]

        e, s = state_ref[0], state_ref[1]

        @pl.when(row0 == ps_ref[e])
        def _():
            for copy in fetch(e, s):
                copy.wait()
            nxt = lax.while_loop(lambda n: (n < N_EXPERTS) & (pe_ref[jnp.minimum(n, N_EXPERTS - 1)] <= pe_ref[e]),
                                 lambda n: n + 1, e + 1)

            @pl.when(nxt < N_EXPERTS)
            def _():
                for copy in fetch(nxt, 1 - s):
                    copy.start()

        x = x_ref[...].astype(BF16)
        h = jnp.dot(x, wgu_buf[s].astype(BF16), preferred_element_type=F32) + bgu_ref[pl.ds(e, 1), :]
        gate = jnp.minimum(h[:, :D_FF], SWIGLU_LIMIT)
        up = jnp.clip(h[:, D_FF:], -SWIGLU_LIMIT, SWIGLU_LIMIT)
        act = (up + 1.0) * (gate * jax.nn.sigmoid(SWIGLU_ALPHA * gate))
        y_ref[...] = (jnp.dot(act.astype(BF16), wd_buf[s].astype(BF16), preferred_element_type=F32)
                      + bd_ref[pl.ds(e, 1), :])

    @pl.when(row0 >= pe_ref[N_EXPERTS - 1])
    def _():
        y_ref[...] = jnp.zeros_like(y_ref)


def _experts(xb, pad_start, pad_end, w_gu, b_gu, w_down, b_down, bm):
    rows = xb.shape[0]
    n_blocks = rows // bm
    used = lambda i, ps, pe: (jnp.maximum(jnp.minimum(i, pe[N_EXPERTS - 1] // bm - 1), 0), 0)
    whole = lambda i, ps, pe: (0, 0)
    return pl.pallas_call(
        functools.partial(_expert_kernel, bm=bm),
        grid_spec=pltpu.PrefetchScalarGridSpec(
            num_scalar_prefetch=2,
            grid=(n_blocks,),
            in_specs=[pl.BlockSpec((bm, D_MODEL), used),
                      pl.BlockSpec(memory_space=pl.ANY),
                      pl.BlockSpec((N_EXPERTS, 2 * D_FF), whole),
                      pl.BlockSpec(memory_space=pl.ANY),
                      pl.BlockSpec((N_EXPERTS, D_MODEL), whole)],
            out_specs=pl.BlockSpec((bm, D_MODEL), lambda i, ps, pe: (i, 0)),
            scratch_shapes=[pltpu.VMEM((2, D_MODEL, 2 * D_FF), F32), pltpu.VMEM((2, D_FF, D_MODEL), F32),
                            pltpu.SemaphoreType.DMA((2,)), pltpu.SMEM((2,), jnp.int32)]),
        out_shape=jax.ShapeDtypeStruct((rows, D_MODEL), F32),
        compiler_params=_params("arbitrary"),
        name="experts",
    )(pad_start, pad_end, xb, w_gu, b_gu, w_down, b_down)


def _combine_ln_kernel(dcur_ref, dnext_ref, x_ref, gate_ref, g_ref, b_ref, yb_hbm, op_ref, os_ref, ybuf, sem, *,
                       tm, n_prompt):
    i = pl.program_id(0)
    slot = i % 2

    def issue(dref, s):
        def body(r, carry):
            for k in range(TOP_K):
                _row_copy(yb_hbm, dref[0, 0, r * TOP_K + k], ybuf.at[s, k], r,
                          sem.at[s]).start(priority=k % N_DMA_PRIORITIES)
            return carry
        lax.fori_loop(0, tm, body, 0)

    @pl.when(i == 0)
    def _():
        issue(dcur_ref, 0)

    for s in range(2):
        @pl.when((i + 1 < pl.num_programs(0)) & (slot != s))
        def _():
            issue(dnext_ref, s)

    pltpu.make_async_copy(ybuf.at[slot], ybuf.at[slot], sem.at[slot]).wait()
    gates = gate_ref[...]
    y = sum(gates[:, k:k + 1] * ybuf[slot, k] for k in range(TOP_K))
    out = _layernorm(DEEPNORM_ALPHA * x_ref[...] + y, g_ref[...], b_ref[...])

    @pl.when(i < n_prompt)
    def _():
        op_ref[...] = out

    @pl.when(i >= n_prompt)
    def _():
        os_ref[...] = out


def _combine_ln(x1, gates, dest, yb, ln_g, ln_b, tm, t_prompt):
    t = x1.shape[0]
    n = t // tm
    n_prompt = t_prompt // tm
    row = lambda i: (i, 0)
    const = lambda i: (0, 0)
    d2 = dest.reshape(n, 1, tm * TOP_K)
    return pl.pallas_call(
        functools.partial(_combine_ln_kernel, tm=tm, n_prompt=n_prompt),
        grid=(n,),
        in_specs=[pl.BlockSpec((1, 1, tm * TOP_K), lambda i: (i, 0, 0), memory_space=pltpu.SMEM),
                  pl.BlockSpec((1, 1, tm * TOP_K), lambda i: (jnp.minimum(i + 1, n - 1), 0, 0),
                               memory_space=pltpu.SMEM),
                  pl.BlockSpec((tm, D_MODEL), row),
                  pl.BlockSpec((tm, LANES), row),
                  pl.BlockSpec((1, D_MODEL), const), pl.BlockSpec((1, D_MODEL), const),
                  pl.BlockSpec(memory_space=pl.ANY)],
        out_specs=[pl.BlockSpec((tm, D_MODEL), lambda i: (jnp.minimum(i, n_prompt - 1), 0)),
                   pl.BlockSpec((tm, D_MODEL), lambda i: (jnp.maximum(i - n_prompt, 0), 0))],
        out_shape=[jax.ShapeDtypeStruct((t_prompt, D_MODEL), F32),
                   jax.ShapeDtypeStruct((t - t_prompt, D_MODEL), F32)],
        scratch_shapes=[pltpu.VMEM((2, TOP_K, tm, D_MODEL), F32), pltpu.SemaphoreType.DMA((2,))],
        compiler_params=_params("arbitrary"),
        name="combine_ln",
    )(d2, d2, x1, gates, ln_g.reshape(1, D_MODEL), ln_b.reshape(1, D_MODEL), yb)


MOE_BM = 512
ATTN_BLK = 512
GDN_SUB = 4
ROUTE_TM = 128


def _mixers(x2d, bsz, seq, cache_k, cache_v, conv_past, s0, lam, lam_init, w_in_bf, conv_w, a_log, dt_bias,
            delta_norm_w, subln_w):
    prompt = cache_k is None
    tm = ATTN_BLK if prompt else x2d.shape[0]
    q_bf, k_f, v_f, k_bf, v_bf, cin, z, ab = _in_proj(x2d, w_in_bf, tm, prompt)
    if prompt:
        att = _attn_prompt(q_bf, k_bf, v_bf, lam, subln_w, bsz, seq, ATTN_BLK, lam_init)
    else:
        att = _attn_sample(q_bf, k_bf, v_bf, cache_k, cache_v, 0, lam, subln_w.reshape(1, HEAD), bsz, seq, lam_init)
    chunk = CHUNK if seq % CHUNK == 0 else seq
    n_sub = GDN_SUB if (seq // chunk) % GDN_SUB == 0 else 1
    past8 = jnp.pad(conv_past, ((0, 0), (SUBLANES - (CONV_W - 1), 0), (0, 0)))
    u, w, qd, kd, qk, gl = _gdn_prep(cin, ab, past8, conv_w, a_log, dt_bias, bsz, seq, chunk, n_sub)
    o, s_new = _gdn_scan(u, w, qd, kd, qk, gl, z, s0, delta_norm_w.reshape(1, HEAD), bsz, seq, chunk)
    return att, o, k_f, v_f, cin, s_new


def _moe(x1, idx, gates, rank, counts, w_gu, b_gu, w_down, b_down, ln_g, ln_b, bm, tm, t_prompt):
    t = x1.shape[0]
    n = t * TOP_K
    counts = counts[0, :N_EXPERTS]
    padded = (counts + bm - 1) // bm * bm
    pad_end = jnp.cumsum(padded).astype(jnp.int32)
    pad_start = (pad_end - padded).astype(jnp.int32)
    idx = idx[:, :TOP_K, :]
    dest = rank[:, :TOP_K, :] + sum(jnp.where(idx == e, pad_start[e], 0) for e in range(N_EXPERTS))
    dest = dest.transpose(0, 2, 1).reshape(t, TOP_K)
    n_blocks = -(-n // bm) + N_EXPERTS
    xb = _dispatch(x1, dest, pad_start, pad_end, n_blocks * bm, tm, bm)
    yb = _experts(xb, pad_start, pad_end, w_gu, b_gu, w_down, b_down, bm)
    return _combine_ln(x1, gates, dest, yb, ln_g, ln_b, ROUTE_TM, t_prompt)


def kernel(x_prompt, x_sample, cache_k, cache_v, state_conv, state_delta, w_in, conv_w, a_log, dt_bias,
           delta_norm_w, lambda_q1, lambda_k1, lambda_q2, lambda_k2, subln_w, w_out, ln1_g, ln1_b,
           router_w, router_b, w_gu, b_gu, w_down, b_down, ln2_g, ln2_b):
    bp, lp, _ = x_prompt.shape
    bs, ls, _ = x_sample.shape
    l = 0
    lam_init = 0.8 - 0.6 * math.exp(-0.3 * l)
    lam = (jnp.exp(jnp.sum(lambda_q1[l] * lambda_k1[l])) - jnp.exp(jnp.sum(lambda_q2[l] * lambda_k2[l]))
           + lam_init).reshape(1).astype(F32)
    w_in_bf = jnp.pad(w_in[l], ((0, 0), (0, IN_COLS_PAD - IN_COLS))).astype(BF16)
    shared = (lam, lam_init, w_in_bf, conv_w[l], a_log[l], dt_bias[l], delta_norm_w[l], subln_w[l])

    xp = x_prompt.reshape(bp * lp, D_MODEL)
    xs = x_sample.reshape(bs * ls, D_MODEL)
    zero_conv = jnp.zeros((bp, CONV_W - 1, CONV_CH), F32)
    zero_s = jnp.zeros((bp, N_HEADS, HEAD, HEAD), F32)
    att_p, o_p, k_p, v_p, cin_p, s_p = _mixers(xp, bp, lp, None, None, zero_conv, zero_s, *shared)
    att_s, o_s, k_s, v_s, cin_s, s_s = _mixers(xs, bs, ls, cache_k, cache_v, state_conv[l],
                                               state_delta[l], *shared)

    n_sub = next(c for c in (4, 3, 2, 1) if ((bp * lp + bs * ls) // ROUTE_TM) % c == 0)
    x1, idx, gates, rank, counts = _out_router(((att_p, o_p, xp), (att_s, o_s, xs)), w_out[l].astype(BF16),
                                               ln1_g[l], ln1_b[l], router_w[l], router_b[l], ROUTE_TM, n_sub)
    tp = bp * lp
    tm = n_sub * ROUTE_TM
    y_p, y_s = _moe(x1, idx, gates, rank, counts, w_gu[l], b_gu[l], w_down[l], b_down[l], ln2_g[l], ln2_b[l],
                    MOE_BM, tm, tp)
    conv_tail = lambda cin, b, s: cin.reshape(b, s, CONV_CH)[:, s - (CONV_W - 1):][None]
    return (y_p.reshape(bp, lp, D_MODEL), y_s.reshape(bs, ls, D_MODEL),
            k_p.reshape(1, bp, lp, N_HEADS, HEAD), v_p.reshape(1, bp, lp, N_HEADS, HEAD),
            conv_tail(cin_p, bp, lp), s_p[None].astype(state_delta.dtype),
            k_s.reshape(1, bs, ls, N_HEADS, HEAD), v_s.reshape(1, bs, ls, N_HEADS, HEAD),
            conv_tail(cin_s, bs, ls), s_s[None].astype(state_delta.dtype))
```

```python
import functools
import math

import jax
import jax.numpy as jnp
from jax import lax
from jax.experimental import pallas as pl
from jax.experimental.pallas import tpu as pltpu

F32 = jnp.float32
BF16 = jnp.bfloat16

D_MODEL = 1024
HEAD = 128
N_HEADS = 4
DQK = HEAD // 2
GROUP = N_HEADS * HEAD
CONV_W = 4
CONV_CH = 3 * GROUP
CHUNK = 64
ALIBI_MAX = 8.0
N_EXPERTS = 32
TOP_K = 4
D_FF = D_MODEL
SWIGLU_LIMIT = 7.0
SWIGLU_ALPHA = 1.702
DEPTH = 1
DEEPNORM_ALPHA = (2 * DEPTH) ** 0.25
LN_EPS = 1e-5
SUBLN_EPS = 1e-5
GATED_NORM_EPS = 1e-6
L2_EPS = 1e-6

LANES = 128
SUBLANES = 8
BF16_EXACT_INT = 256
BF16_ROWS = 16
LOG2E = 1.4426950408889634
N_POS = 6
ONES_ROWS = BF16_ROWS
N_DMA_PRIORITIES = 2
VMEM_LIMIT = 56 * 1024 * 1024

COL_Q, COL_K, COL_V, COL_CONV = 0, GROUP, 2 * GROUP, 3 * GROUP
COL_Z = COL_CONV + CONV_CH
COL_AB = COL_Z + GROUP
IN_COLS = COL_AB + 2 * N_HEADS
IN_COLS_PAD = COL_AB + LANES


def _params(*sem):
    return pltpu.CompilerParams(dimension_semantics=sem, vmem_limit_bytes=VMEM_LIMIT)


def _conv_activation(prev, raw, cw_ref):
    rows = raw.shape[0]
    xin = jnp.concatenate([prev, raw], axis=0)
    conv = sum(xin[SUBLANES - (CONV_W - 1) + j: SUBLANES - (CONV_W - 1) + j + rows] * cw_ref[j:j + 1, :]
               for j in range(CONV_W))
    conv = conv * jax.nn.sigmoid(conv)
    parts = []
    for h in range(2 * N_HEADS):
        x = conv[:, h * HEAD:(h + 1) * HEAD]
        scale = HEAD ** -0.5 if h < N_HEADS else 1.0
        parts.append(x * (lax.rsqrt(jnp.sum(x * x, axis=-1, keepdims=True) + L2_EPS) * scale))
    return jnp.concatenate(parts + [conv[:, 2 * GROUP:]], axis=1)


def _in_proj_kernel(x_ref, w_ref, past_ref, cw_ref, q_ref, kf_ref, vf_ref, kb_ref, vb_ref, c_ref, z_ref, ab_ref,
                    *conv_refs, tm, tiles_per_stream, prompt):
    xb = x_ref[...].astype(BF16)

    def section(lo, hi):
        return jnp.dot(xb, w_ref[:, lo:hi], preferred_element_type=F32)

    q = section(COL_Q, COL_K) * (DQK ** -0.5 * (LOG2E if prompt else 1.0))
    k = section(COL_K, COL_V)
    kf_ref[0] = k.reshape(tm, N_HEADS, HEAD)
    kb_ref[...] = k.astype(BF16)
    v = section(COL_V, COL_CONV)
    vf_ref[0] = v.reshape(tm, N_HEADS, HEAD)
    raw = section(COL_CONV, COL_Z)
    if prompt:
        ctail_ref, tail_ref = conv_refs
        q_ref[0] = q.T.astype(BF16)
        vb_ref[0] = v.T.astype(BF16)
        prev = jnp.where(pl.program_id(0) % tiles_per_stream == 0, past_ref[0], tail_ref[...])
        c_ref[...] = _conv_activation(prev, raw, cw_ref)
        tail_ref[...] = raw[tm - SUBLANES:, :]
        ctail_ref[0] = raw[tm - SUBLANES:, :]
    else:
        q_ref[...] = q.astype(BF16)
        vb_ref[...] = v.astype(BF16)
        c_ref[...] = raw
    z_ref[...] = section(COL_Z, COL_AB)
    ab_ref[...] = section(COL_AB, IN_COLS_PAD)


def _in_proj(x2d, w_bf, past8, conv_w, tm, tiles_per_stream, prompt):
    t = x2d.shape[0]
    row = lambda i: (i, 0)
    stream = lambda i: (i // tiles_per_stream, 0, 0)
    widths = (GROUP, GROUP, GROUP, GROUP, GROUP, CONV_CH, GROUP, LANES)
    dtypes = (BF16, F32, F32, BF16, BF16, F32, F32, F32)
    out_specs = [pl.BlockSpec((tm, w), row) for w in widths]
    out_shape = [jax.ShapeDtypeStruct((t, w), d) for w, d in zip(widths, dtypes)]
    for slot in (1, 2):
        out_specs[slot] = pl.BlockSpec((1, tm, N_HEADS, HEAD), lambda i: (i, 0, 0, 0))
        out_shape[slot] = jax.ShapeDtypeStruct((t // tm, tm, N_HEADS, HEAD), F32)
    if prompt:
        for slot in (0, 4):
            out_specs[slot] = pl.BlockSpec((1, GROUP, tm), lambda i: (i, 0, 0))
            out_shape[slot] = jax.ShapeDtypeStruct((t // tm, GROUP, tm), BF16)
        out_specs.append(pl.BlockSpec((1, SUBLANES, CONV_CH), stream))
        out_shape.append(jax.ShapeDtypeStruct((past8.shape[0], SUBLANES, CONV_CH), F32))
    return pl.pallas_call(
        functools.partial(_in_proj_kernel, tm=tm, tiles_per_stream=tiles_per_stream, prompt=prompt),
        grid=(t // tm,),
        in_specs=[pl.BlockSpec((tm, D_MODEL), row),
                  pl.BlockSpec((D_MODEL, IN_COLS_PAD), lambda i: (0, 0)),
                  pl.BlockSpec((1, SUBLANES, CONV_CH), stream),
                  pl.BlockSpec((CONV_W, CONV_CH), lambda i: (0, 0))],
        out_specs=out_specs,
        out_shape=out_shape,
        scratch_shapes=[pltpu.VMEM((SUBLANES, CONV_CH), F32)] if prompt else [],
        compiler_params=_params("arbitrary"),
        name="in_proj",
    )(x2d, w_bf, past8, conv_w)


def _alibi_slopes():
    return [2.0 ** (-ALIBI_MAX * (h + 1) / N_HEADS) for h in range(N_HEADS)]


def _stack_halves(q):
    lane = lax.broadcasted_iota(jnp.int32, q.shape, 1)
    zero = jnp.zeros_like(q)
    return jnp.concatenate([jnp.where(lane < DQK, q, zero), jnp.where(lane < DQK, zero, q)], axis=0)


def _diff_norm(acc, l, lam, w, lam_init, rows):
    o = acc[:rows] / l[:rows] - lam * (acc[rows:] / l[rows:])
    ms = jnp.mean(o * o, axis=-1, keepdims=True)
    return o * lax.rsqrt(ms + SUBLN_EPS) * w * (1.0 - lam_init)


def _head_slope(h):
    s = _alibi_slopes()
    return jnp.where(h == 0, s[0], jnp.where(h == 1, s[1], jnp.where(h == 2, s[2], s[3]))).astype(F32)


def _attn_prompt_kernel(lam_ref, qt_ref, k_ref, vt_ref, w_ref, o_ref,
                        diag_ref, kaug_ref, vaug_ref, qz_ref, s0_ref, s1_ref, p0_ref, p1_ref, mx0_ref, mx1_ref, m_ref, acc_ref, *, blk, lam_init):
    i = pl.program_id(2)
    slope = _head_slope(pl.program_id(1)) * LOG2E
    rows = 2 * blk
    n_kv = k_ref.shape[0] // blk

    def pieces(x):
        lo = x % BF16_EXACT_INT
        return _split3(slope * lo.astype(F32)) + _split3(slope * (x - lo).astype(F32))

    @pl.when(i == 0)
    def _():
        lane = lax.broadcasted_iota(jnp.int32, (blk, HEAD), 1)
        extra = jnp.where(lane < N_POS, 1.0, 0.0).astype(BF16)
        for n, piece in enumerate(pieces(lax.broadcasted_iota(jnp.int32, (blk, 1), 0))):
            extra = jnp.where(lane == N_POS + n, piece, extra)
        ones = jnp.ones((ONES_ROWS, blk), BF16)
        c = lax.broadcasted_iota(jnp.int32, (blk, rows), 0)
        a = lax.broadcasted_iota(jnp.int32, (blk, rows), 1) % blk
        diag_ref[...] = jnp.where(c // CHUNK <= a // CHUNK, -slope * jnp.abs(a - c).astype(F32), -jnp.inf)
        r = lax.broadcasted_iota(jnp.int32, (HEAD, rows), 0)
        qx = jnp.where((r >= N_POS) & (r < 2 * N_POS), 1.0, 0.0).astype(BF16)
        for n, piece in enumerate(pieces(-(lax.broadcasted_iota(jnp.int32, (1, rows), 1) % blk))):
            qx = jnp.where(r == n, piece, qx)
        qz_ref[HEAD:, :] = qx

        def fill(j, carry):
            j0 = pl.multiple_of(j * blk, blk)
            kaug_ref[pl.ds(j0, blk), :HEAD] = k_ref[pl.ds(j0, blk), :]
            kaug_ref[pl.ds(j0, blk), HEAD:] = extra
            vaug_ref[j, :HEAD, :] = vt_ref[j]
            vaug_ref[j, HEAD:, :] = ones
            return carry

        lax.fori_loop(0, n_kv, fill, 0)

    qt = qt_ref[0]
    d = lax.broadcasted_iota(jnp.int32, qt.shape, 0)
    zero = jnp.zeros_like(qt)
    qz_ref[:HEAD, :] = jnp.concatenate([jnp.where(d < DQK, qt, zero), jnp.where(d < DQK, zero, qt)], axis=1)

    def tile_rows(j):
        return pl.ds(pl.multiple_of(j * blk, blk), blk)

    s = jnp.dot(k_ref[tile_rows(i), :], qz_ref[:HEAD, :], preferred_element_type=F32) + diag_ref[...]
    m = jnp.max(s, axis=0, keepdims=True)
    p1_ref[...] = jnp.exp2(s - m).astype(BF16)
    m_ref[...] = m
    acc_ref[...] = jnp.zeros_like(acc_ref)
    s_first = jnp.dot(kaug_ref[tile_rows(0), :], qz_ref[...], preferred_element_type=F32)
    s0_ref[...] = s_first
    mx0_ref[...] = jnp.max(s_first, axis=0, keepdims=True)

    def step(j, s_cur, s_nxt, p_cur, p_nxt, mx_cur, mx_nxt):
        s_next = jnp.dot(kaug_ref[tile_rows(jnp.minimum(j + 1, i - 1)), :], qz_ref[...],
                         preferred_element_type=F32)
        s_nxt[...] = s_next
        mx_nxt[...] = jnp.max(s_next, axis=0, keepdims=True)
        pv = jnp.dot(vaug_ref[jnp.where(j == 0, i, j - 1)], p_nxt[...], preferred_element_type=F32)
        shift = -slope * ((i - j) * blk).astype(F32)
        m_old = m_ref[...]
        m_new = jnp.maximum(m_old, mx_cur[...] + shift)
        p_cur[...] = jnp.exp2(s_cur[...] - (m_new - shift)).astype(BF16)
        acc_ref[...] = (acc_ref[...] + pv) * jnp.exp2(m_old - m_new)
        m_ref[...] = m_new

    def body(jj, carry):
        step(2 * jj, s0_ref, s1_ref, p0_ref, p1_ref, mx0_ref, mx1_ref)

        @pl.when(2 * jj + 1 < i)
        def _():
            step(2 * jj + 1, s1_ref, s0_ref, p1_ref, p0_ref, mx1_ref, mx0_ref)

        return carry

    lax.fori_loop(0, (i + 1) // 2, body, 0)
    p_last = jnp.where(i % 2 == 1, p0_ref[...], p1_ref[...])
    acc = acc_ref[...] + jnp.dot(vaug_ref[jnp.where(i > 0, i - 1, i)], p_last, preferred_element_type=F32)
    l = acc[HEAD:HEAD + 1, :]
    num = acc[:HEAD, :]
    ot = num[:, :blk] / l[:, :blk] - lam_ref[0] * (num[:, blk:] / l[:, blk:])
    ms = jnp.mean(ot * ot, axis=0, keepdims=True)
    ot = ot * lax.rsqrt(ms + SUBLN_EPS) * w_ref[...] * (1.0 - lam_init)
    o_ref[...] = ot.T.astype(o_ref.dtype)


def _attn_prompt(qt_bf, k_bf, vt_bf, lam, subln_w, bsz, seq, blk, lam_init):
    nq = seq // blk
    kern = functools.partial(_attn_prompt_kernel, blk=blk, lam_init=lam_init)
    return pl.pallas_call(
        kern,
        grid=(bsz, N_HEADS, nq),
        in_specs=[pl.BlockSpec(memory_space=pltpu.SMEM),
                  pl.BlockSpec((1, HEAD, blk), lambda b, h, i: (b * nq + i, h, 0)),
                  pl.BlockSpec((seq, HEAD), lambda b, h, i: (b, h)),
                  pl.BlockSpec((nq, HEAD, blk), lambda b, h, i: (b, h, 0)),
                  pl.BlockSpec((HEAD, 1), lambda b, h, i: (0, 0))],
        out_specs=pl.BlockSpec((blk, HEAD), lambda b, h, i: (b * nq + i, h)),
        scratch_shapes=[pltpu.VMEM((blk, 2 * blk), F32),
                        pltpu.VMEM((seq, 2 * HEAD), BF16),
                        pltpu.VMEM((nq, HEAD + ONES_ROWS, blk), BF16),
                        pltpu.VMEM((2 * HEAD, 2 * blk), BF16),
                        pltpu.VMEM((blk, 2 * blk), F32), pltpu.VMEM((blk, 2 * blk), F32),
                        pltpu.VMEM((blk, 2 * blk), BF16), pltpu.VMEM((blk, 2 * blk), BF16),
                        pltpu.VMEM((1, 2 * blk), F32), pltpu.VMEM((1, 2 * blk), F32),
                        pltpu.VMEM((1, 2 * blk), F32),
                        pltpu.VMEM((HEAD + ONES_ROWS, 2 * blk), F32)],
        out_shape=jax.ShapeDtypeStruct((bsz * seq, GROUP), BF16),
        compiler_params=_params("parallel", "parallel", "arbitrary"),
        name="attn_prompt",
    )(lam, qt_bf, k_bf, vt_bf, subln_w.reshape(HEAD, 1))


def _attn_sample_kernel(lam_ref, q_ref, kn_ref, vn_ref, kc_ref, vc_ref, w_ref, o_ref, *, seq, past, lam_init):
    nt = (((1,), (1,)), ((), ()))
    qpos = past + lax.broadcasted_iota(jnp.int32, (2 * seq, 1), 0) % seq
    rel_c = jnp.abs(qpos - lax.broadcasted_iota(jnp.int32, (1, past), 1)).astype(F32)
    rel_n = jnp.abs(qpos - (past + lax.broadcasted_iota(jnp.int32, (1, seq), 1))).astype(F32)
    for h, slope in enumerate(_alibi_slopes()):
        col = slice(h * HEAD, (h + 1) * HEAD)
        qz = _stack_halves(q_ref[:, col])
        s_c = lax.dot_general(qz, kc_ref[0, 0, :, h, :].astype(BF16), nt, preferred_element_type=F32)
        s_c = s_c - slope * rel_c
        s_n = lax.dot_general(qz, kn_ref[:, col], nt, preferred_element_type=F32) - slope * rel_n
        m = jnp.maximum(jnp.max(s_c, axis=-1, keepdims=True), jnp.max(s_n, axis=-1, keepdims=True))
        p_c = jnp.exp(s_c - m)
        p_n = jnp.exp(s_n - m)
        l = jnp.sum(p_c, axis=-1, keepdims=True) + jnp.sum(p_n, axis=-1, keepdims=True)
        acc = (jnp.dot(p_c.astype(BF16), vc_ref[0, 0, :, h, :].astype(BF16), preferred_element_type=F32)
               + jnp.dot(p_n.astype(BF16), vn_ref[:, col], preferred_element_type=F32))
        o_ref[:, col] = _diff_norm(acc, l, lam_ref[0], w_ref[...], lam_init, seq).astype(o_ref.dtype)


def _attn_sample(q_bf, k_bf, v_bf, cache_k, cache_v, layer, lam, subln_w, bsz, seq, lam_init):
    past = cache_k.shape[2]
    kern = functools.partial(_attn_sample_kernel, seq=seq, past=past, lam_init=lam_init)
    new = pl.BlockSpec((seq, GROUP), lambda b: (b, 0))
    cache = pl.BlockSpec((1, 1, past, N_HEADS, HEAD), lambda b: (layer, b, 0, 0, 0))
    return pl.pallas_call(
        kern,
        grid=(bsz,),
        in_specs=[pl.BlockSpec(memory_space=pltpu.SMEM), new, new, new, cache, cache,
                  pl.BlockSpec((1, HEAD), lambda b: (0, 0))],
        out_specs=new,
        out_shape=jax.ShapeDtypeStruct((bsz * seq, GROUP), BF16),
        compiler_params=_params("parallel"),
        name="attn_sample",
    )(lam, q_bf, k_bf, v_bf, cache_k, cache_v, subln_w)


def _split3(x):
    hi = x.astype(BF16)
    r1 = x - hi.astype(F32)
    mid = r1.astype(BF16)
    lo = (r1 - mid.astype(F32)).astype(BF16)
    return hi, mid, lo


def _gdn_prep_kernel(cin_ref, halo_ref, past_ref, ab_ref, cw_ref, alog_ref, dtb_ref,
                     u_ref, w_ref, qd_ref, kd_ref, qk_ref, gl_ref, *, chunk, n_sub, activated):
    c_idx = pl.program_id(1)
    rows = chunk * n_sub
    if activated:
        conv = cin_ref[...]
    else:
        prev = jnp.where(c_idx == 0, past_ref[0], halo_ref[...])
        conv = _conv_activation(prev, cin_ref[...], cw_ref)

    ab = ab_ref[...]
    lane = lax.broadcasted_iota(jnp.int32, ab.shape, 1)
    pre = ab + dtb_ref[...]
    softplus = jnp.maximum(pre, 0.0) + jnp.log(1.0 + jnp.exp(-jnp.abs(pre)))
    g = jnp.where(lane < N_HEADS, -jnp.exp(alog_ref[...]) * softplus, 0.0)
    beta_all = jax.nn.sigmoid(ab)

    ri = lax.broadcasted_iota(jnp.int32, (rows, rows), 0)
    ci = lax.broadcasted_iota(jnp.int32, (rows, rows), 1)
    same = (ri // chunk) == (ci // chunk)
    incl = same & (ri >= ci)
    strict = same & (ri > ci)
    eye = jnp.where(ri == ci, 1.0, 0.0).astype(F32)
    nt = (((1,), (1,)), ((), ()))
    g_parts = _split3(g)
    ones_incl = jnp.where(incl, 1.0, 0.0).astype(BF16)
    ones_same = jnp.where(same, 1.0, 0.0).astype(BF16)
    gc = sum(jnp.dot(ones_incl, part, preferred_element_type=F32) for part in g_parts)
    g_end = sum(jnp.dot(ones_same, part, preferred_element_type=F32) for part in g_parts)
    gct = gc.T
    for sc in range(n_sub):
        gl_ref[sc] = g_end[sc * chunk:sc * chunk + 1, :]

    t_mats, p_mats, rhs = [], [], []
    for h in range(N_HEADS):
        col = slice(h * HEAD, (h + 1) * HEAD)
        qh = conv[:, h * HEAD:(h + 1) * HEAD]
        kh = conv[:, GROUP + h * HEAD:GROUP + (h + 1) * HEAD]
        vh = conv[:, 2 * GROUP + h * HEAD:2 * GROUP + (h + 1) * HEAD]
        beta = beta_all[:, N_HEADS + h:N_HEADS + h + 1]
        gcol = gc[:, h:h + 1]
        grow = gct[h:h + 1, :]
        gamma = jnp.exp(jnp.where(incl, gcol - grow, -jnp.inf))
        egc = jnp.exp(gcol)
        kb = kh * beta
        khb = kh.astype(BF16)
        a = jnp.where(strict, lax.dot_general(kb.astype(BF16), khb, nt, preferred_element_type=F32) * gamma, 0.0)
        qk = (lax.dot_general(qh.astype(BF16), khb, nt, preferred_element_type=F32) * gamma).astype(BF16)
        for sc in range(n_sub):
            blk = slice(sc * chunk, (sc + 1) * chunk)
            qk_ref[h, blk, :] = qk[blk, blk]
        qd_ref[:, col] = (qh * egc).astype(BF16)
        kd_ref[:, col] = (kh * jnp.exp(g_end[:, h:h + 1] - gcol)).astype(BF16)
        t_mats.append(eye - a)
        p_mats.append(a)
        rhs.append(((vh * beta).astype(BF16), (kb * egc).astype(BF16)))

    for _ in range(int(math.log2(chunk)) - 1):
        for h in range(N_HEADS):
            pb = p_mats[h].astype(BF16)
            p_mats[h] = jnp.dot(pb, pb, preferred_element_type=F32)
        for h in range(N_HEADS):
            t_mats[h] = t_mats[h] + jnp.dot(t_mats[h].astype(BF16), p_mats[h].astype(BF16),
                                            preferred_element_type=F32)

    for h in range(N_HEADS):
        col = slice(h * HEAD, (h + 1) * HEAD)
        t_inv = t_mats[h].astype(BF16)
        u_ref[:, col] = jnp.dot(t_inv, rhs[h][0], preferred_element_type=F32)
        w_ref[:, col] = jnp.dot(t_inv, rhs[h][1], preferred_element_type=F32).astype(BF16)


def _gdn_prep(cin, ab, past8, conv_w, a_log, dt_bias, bsz, seq, chunk, n_sub, activated):
    rows = chunk * n_sub
    nblk = seq // rows
    t = bsz * seq
    lanes = lambda v: jnp.pad(v.reshape(1, N_HEADS).astype(F32), ((0, 0), (0, LANES - N_HEADS)))
    kern = functools.partial(_gdn_prep_kernel, chunk=chunk, n_sub=n_sub, activated=activated)
    rowblk = lambda b, c: (b * nblk + c, 0)
    halo = lambda b, c: (jnp.maximum((b * nblk + c) * (rows // SUBLANES) - 1, 0), 0)
    const = lambda b, c: (0, 0)
    return pl.pallas_call(
        kern,
        grid=(bsz, nblk),
        in_specs=[pl.BlockSpec((rows, CONV_CH), rowblk),
                  pl.BlockSpec((SUBLANES, CONV_CH), halo),
                  pl.BlockSpec((1, SUBLANES, CONV_CH), lambda b, c: (b, 0, 0)),
                  pl.BlockSpec((rows, LANES), rowblk),
                  pl.BlockSpec((CONV_W, CONV_CH), const),
                  pl.BlockSpec((1, LANES), const),
                  pl.BlockSpec((1, LANES), const)],
        out_specs=[pl.BlockSpec((rows, GROUP), rowblk),
                   pl.BlockSpec((rows, GROUP), rowblk),
                   pl.BlockSpec((rows, GROUP), rowblk),
                   pl.BlockSpec((rows, GROUP), rowblk),
                   pl.BlockSpec((N_HEADS, rows, chunk), lambda b, c: (0, b * nblk + c, 0)),
                   pl.BlockSpec((n_sub, 1, LANES), lambda b, c: (b * nblk + c, 0, 0))],
        out_shape=[jax.ShapeDtypeStruct((t, GROUP), F32),
                   jax.ShapeDtypeStruct((t, GROUP), BF16),
                   jax.ShapeDtypeStruct((t, GROUP), BF16),
                   jax.ShapeDtypeStruct((t, GROUP), BF16),
                   jax.ShapeDtypeStruct((N_HEADS, t, chunk), BF16),
                   jax.ShapeDtypeStruct((t // chunk, 1, LANES), F32)],
        compiler_params=_params("parallel", "parallel"),
        name="gdn_prep",
    )(cin, cin, past8, ab, conv_w, lanes(a_log), lanes(dt_bias))


def _gdn_scan_kernel(u_ref, w_ref, qd_ref, kd_ref, qk_ref, gl_ref, z_ref, s0_ref, nw_ref,
                     o_ref, sf_ref, s_ref, *, bsz, chunk):
    c_idx = pl.program_id(0)

    @pl.when(c_idx == 0)
    def _():
        s_ref[...] = s0_ref[...]

    tn = (((0,), (0,)), ((), ()))
    chains = [(b, h, slice(h * HEAD, (h + 1) * HEAD)) for b in range(bsz) for h in range(N_HEADS)]
    states = [s_ref[b, h] for b, h, _ in chains]
    prods = [jnp.dot(jnp.concatenate([w_ref[b, :, col], qd_ref[b, :, col]], axis=0), s.astype(BF16),
                     preferred_element_type=F32) for (b, _, col), s in zip(chains, states)]
    v_news = [(u_ref[b, :, col] - r[:chunk]).astype(BF16) for (b, _, col), r in zip(chains, prods)]
    outs = [r[chunk:] + jnp.dot(qk_ref[h, b], v, preferred_element_type=F32)
            for (b, h, _), r, v in zip(chains, prods, v_news)]
    for (b, h, col), s, v in zip(chains, states, v_news):
        glast = jnp.exp(gl_ref[b, 0])
        s_ref[b, h] = s * glast[:, h:h + 1] + lax.dot_general(kd_ref[b, :, col], v, tn,
                                                             preferred_element_type=F32)
    for (b, h, col), o in zip(chains, outs):
        zh = z_ref[b, :, col]
        ms = jnp.mean(o * o, axis=-1, keepdims=True)
        o = o * lax.rsqrt(ms + GATED_NORM_EPS) * nw_ref[...] * (zh * jax.nn.sigmoid(zh))
        o_ref[b, :, col] = o.astype(o_ref.dtype)

    @pl.when(c_idx == pl.num_programs(0) - 1)
    def _():
        sf_ref[...] = s_ref[...]


def _gdn_scan(u, w, qd, kd, qk, gl, z, s0, norm_w, bsz, seq, chunk):
    nc = seq // chunk
    kern = functools.partial(_gdn_scan_kernel, bsz=bsz, chunk=chunk)
    tok = pl.BlockSpec((bsz, chunk, GROUP), lambda c: (0, c, 0))
    state = pl.BlockSpec((bsz, N_HEADS, HEAD, HEAD), lambda c: (0, 0, 0, 0))
    o, s_final = pl.pallas_call(
        kern,
        grid=(nc,),
        in_specs=[tok, tok, tok, tok,
                  pl.BlockSpec((N_HEADS, bsz, chunk, chunk), lambda c: (0, 0, c, 0)),
                  pl.BlockSpec((bsz, 1, 1, LANES), lambda c: (0, c, 0, 0)),
                  tok, state,
                  pl.BlockSpec((1, HEAD), lambda c: (0, 0))],
        out_specs=[tok, state],
        out_shape=[jax.ShapeDtypeStruct((bsz, seq, GROUP), BF16),
                   jax.ShapeDtypeStruct((bsz, N_HEADS, HEAD, HEAD), F32)],
        scratch_shapes=[pltpu.VMEM((bsz, N_HEADS, HEAD, HEAD), F32)],
        compiler_params=_params("arbitrary"),
        name="gdn_scan",
    )(u.reshape(bsz, seq, GROUP), w.reshape(bsz, seq, GROUP), qd.reshape(bsz, seq, GROUP),
      kd.reshape(bsz, seq, GROUP), qk.reshape(N_HEADS, bsz, seq, chunk), gl.reshape(bsz, nc, 1, LANES),
      z.reshape(bsz, seq, GROUP), s0, norm_w)
    return o.reshape(bsz * seq, GROUP), s_final


def _layernorm(x, g, b):
    mu = jnp.mean(x, axis=-1, keepdims=True)
    xc = x - mu
    var = jnp.mean(xc * xc, axis=-1, keepdims=True)
    return xc * lax.rsqrt(var + LN_EPS) * g + b


def _out_router_kernel(*refs, tm, n_sub, n_prompt):
    tiles = [refs[6 * u:6 * u + 6] for u in range(n_sub)]
    wo_ref, g_ref, b_ref, rw_ref, rb_ref, x1_ref, idx_ref, gate_ref, rank_ref, cnt_ref, carry_ref = refs[6 * n_sub:]
    step = pl.program_id(0)
    units = range(n_sub)

    @pl.when(step == 0)
    def _():
        carry_ref[...] = jnp.zeros_like(carry_ref)

    x1s = []
    for u in units:
        attp_ref, atts_ref, op_ref, os_ref, xp_ref, xs_ref = tiles[u]
        prompt = step * n_sub + u < n_prompt
        att = jnp.where(prompt, attp_ref[...], atts_ref[...])
        o = jnp.where(prompt, op_ref[...], os_ref[...])
        x = jnp.where(prompt, xp_ref[...], xs_ref[...])
        mix = (jnp.dot(att, wo_ref[:GROUP, :], preferred_element_type=F32)
               + jnp.dot(o, wo_ref[GROUP:, :], preferred_element_type=F32))
        x1 = _layernorm(DEEPNORM_ALPHA * x + mix, g_ref[...], b_ref[...])
        x1_ref[u * tm:(u + 1) * tm, :] = x1
        x1s.append(x1)

    works = []
    lane = lax.broadcasted_iota(jnp.int32, (tm, LANES), 1)
    for x1 in x1s:
        x_hi = x1.astype(BF16)
        x_lo = (x1 - x_hi.astype(F32)).astype(BF16)
        logits = (jnp.dot(x_hi, rw_ref[0], preferred_element_type=F32)
                  + jnp.dot(x_lo, rw_ref[0], preferred_element_type=F32)
                  + jnp.dot(x_hi, rw_ref[1], preferred_element_type=F32)) + rb_ref[...]
        works.append(jnp.where(lane < N_EXPERTS, logits, -jnp.inf))

    vals, idxs = [[] for _ in units], [[] for _ in units]
    for _ in range(TOP_K):
        for u in units:
            m = jnp.max(works[u], axis=-1, keepdims=True)
            am = jnp.min(jnp.where(works[u] == m, lane, LANES), axis=-1, keepdims=True)
            vals[u].append(m)
            idxs[u].append(am)
            works[u] = jnp.where(lane == am, -jnp.inf, works[u])

    ri = lax.broadcasted_iota(jnp.int32, (tm, tm), 0)
    ci = lax.broadcasted_iota(jnp.int32, (tm, tm), 1)
    before = jnp.where(ri > ci, 1.0, 0.0).astype(BF16)
    base = carry_ref[...]
    for u in units:
        exps = [jnp.exp(v - vals[u][0]) for v in vals[u]]
        denom = exps[0] + exps[1] + exps[2] + exps[3]
        chosen = jnp.zeros((tm, LANES), F32)
        gate_out = jnp.zeros((tm, LANES), F32)
        idx_out = jnp.zeros((tm, LANES), jnp.int32)
        for k in range(TOP_K):
            chosen = jnp.where(lane == idxs[u][k], 1.0, chosen)
            gate_out = jnp.where(lane == k, exps[k] / denom, gate_out)
            idx_out = jnp.where(lane == k, idxs[u][k], idx_out)
        prefix = jnp.dot(before, chosen.astype(BF16), preferred_element_type=F32) + base
        base = base + jnp.sum(chosen, axis=0, keepdims=True)
        rank_out = jnp.zeros((tm, LANES), F32)
        for k in range(TOP_K):
            r = jnp.sum(jnp.where(lane == idxs[u][k], prefix, 0.0), axis=-1, keepdims=True)
            rank_out = jnp.where(lane == k, r, rank_out)
        idx_ref[0, :, u * tm:(u + 1) * tm] = idx_out.T[:SUBLANES, :]
        gate_ref[u * tm:(u + 1) * tm, :] = gate_out
        rank_ref[0, :, u * tm:(u + 1) * tm] = rank_out.astype(jnp.int32).T[:SUBLANES, :]
    carry_ref[...] = base
    cnt_ref[...] = base.astype(jnp.int32)


def _out_router(streams, w_out_bf, ln_g, ln_b, router_w, router_b, tm, n_sub):
    (att_p, o_p, x_p), (att_s, o_s, x_s) = streams
    n_prompt = x_p.shape[0] // tm
    n_sample = x_s.shape[0] // tm
    t = x_p.shape[0] + x_s.shape[0]
    step_rows = tm * n_sub
    row = lambda i: (i, 0)
    const = lambda i: (0, 0)
    slots = pl.BlockSpec((1, SUBLANES, step_rows), lambda i: (i, 0, 0))
    tile_specs, tile_args = [], []
    for u in range(n_sub):
        prow = lambda i, u=u: (jnp.minimum(i * n_sub + u, n_prompt - 1), 0)
        srow = lambda i, u=u: (jnp.clip(i * n_sub + u - n_prompt, 0, n_sample - 1), 0)
        tile_specs += [pl.BlockSpec((tm, GROUP), prow), pl.BlockSpec((tm, GROUP), srow),
                       pl.BlockSpec((tm, GROUP), prow), pl.BlockSpec((tm, GROUP), srow),
                       pl.BlockSpec((tm, D_MODEL), prow), pl.BlockSpec((tm, D_MODEL), srow)]
        tile_args += [att_p, att_s, o_p, o_s, x_p, x_s]
    rw = jnp.pad(router_w, ((0, 0), (0, LANES - N_EXPERTS)))
    rw_hi = rw.astype(BF16)
    rw = jnp.stack([rw_hi, (rw - rw_hi.astype(F32)).astype(BF16)])
    rb = jnp.pad(router_b.reshape(1, N_EXPERTS), ((0, 0), (0, LANES - N_EXPERTS)))
    kern = functools.partial(_out_router_kernel, tm=tm, n_sub=n_sub, n_prompt=n_prompt)
    return pl.pallas_call(
        kern,
        grid=(t // step_rows,),
        in_specs=tile_specs + [pl.BlockSpec((2 * GROUP, D_MODEL), const),
                               pl.BlockSpec((1, D_MODEL), const), pl.BlockSpec((1, D_MODEL), const),
                               pl.BlockSpec((2, D_MODEL, LANES), lambda i: (0, 0, 0)),
                               pl.BlockSpec((1, LANES), const)],
        out_specs=[pl.BlockSpec((step_rows, D_MODEL), row), slots, pl.BlockSpec((step_rows, LANES), row), slots,
                   pl.BlockSpec((1, LANES), const)],
        out_shape=[jax.ShapeDtypeStruct((t, D_MODEL), F32),
                   jax.ShapeDtypeStruct((t // step_rows, SUBLANES, step_rows), jnp.int32),
                   jax.ShapeDtypeStruct((t, LANES), F32),
                   jax.ShapeDtypeStruct((t // step_rows, SUBLANES, step_rows), jnp.int32),
                   jax.ShapeDtypeStruct((1, LANES), jnp.int32)],
        scratch_shapes=[pltpu.VMEM((1, LANES), F32)],
        compiler_params=_params("arbitrary"),
        name="out_router",
    )(*tile_args, w_out_bf, ln_g.reshape(1, D_MODEL), ln_b.reshape(1, D_MODEL), rw, rb)


def _row_copy(src, src_row, dst, dst_row, sem):
    return pltpu.make_async_copy(src.at[pl.ds(src_row, 1), :], dst.at[pl.ds(dst_row, 1), :], sem)


def _dispatch_kernel(ps_ref, pe_ref, dest_ref, x_ref, xb_hbm, zero_ref, sem, zsem, *, tm, bm):
    i = pl.program_id(0)

    def zero_block(row):
        return pltpu.make_async_copy(zero_ref, xb_hbm.at[pl.ds(pl.multiple_of(row, bm), bm), :], zsem)

    def fill(e):
        return zero_block(pe_ref[e] - bm)

    @pl.when(i == 0)
    def _():
        zero_ref[...] = jnp.zeros_like(zero_ref)
        for e in range(N_EXPERTS):
            @pl.when(pe_ref[e] > ps_ref[e])
            def _():
                fill(e).start()
        first_unused = pe_ref[N_EXPERTS - 1] // bm
        n_blocks = xb_hbm.shape[0] // bm
        lax.fori_loop(first_unused, n_blocks, lambda b, c: (zero_block(b * bm).start(), c)[1], 0)
        for e in range(N_EXPERTS):
            @pl.when(pe_ref[e] > ps_ref[e])
            def _():
                fill(e).wait()
        lax.fori_loop(first_unused, n_blocks, lambda b, c: (zero_block(b * bm).wait(), c)[1], 0)

    def issue(r, carry):
        for k in range(TOP_K):
            _row_copy(x_ref, r, xb_hbm, dest_ref[0, 0, r * TOP_K + k], sem).start(priority=k % N_DMA_PRIORITIES)
        return carry

    lax.fori_loop(0, tm, issue, 0)
    pltpu.make_async_copy(xb_hbm.at[pl.ds(0, tm * TOP_K), :], xb_hbm.at[pl.ds(0, tm * TOP_K), :], sem).wait()


def _dispatch(x1, dest, pad_start, pad_end, rows, tm, bm):
    t = x1.shape[0]
    return pl.pallas_call(
        functools.partial(_dispatch_kernel, tm=tm, bm=bm),
        grid_spec=pltpu.PrefetchScalarGridSpec(
            num_scalar_prefetch=2,
            grid=(t // tm,),
            in_specs=[pl.BlockSpec((1, 1, tm * TOP_K), lambda i, ps, pe: (i, 0, 0), memory_space=pltpu.SMEM),
                      pl.BlockSpec((tm, D_MODEL), lambda i, ps, pe: (i, 0))],
            out_specs=pl.BlockSpec(memory_space=pl.ANY),
            scratch_shapes=[pltpu.VMEM((bm, D_MODEL), F32),
                            pltpu.SemaphoreType.DMA(()), pltpu.SemaphoreType.DMA(())]),
        out_shape=jax.ShapeDtypeStruct((rows, D_MODEL), F32),
        compiler_params=_params("arbitrary"),
        name="dispatch",
    )(pad_start, pad_end, dest.reshape(t // tm, 1, tm * TOP_K), x1)


def _expert_kernel(ps_ref, pe_ref, x_ref, wgu_hbm, bgu_ref, wd_hbm, bd_ref, y_ref,
                   wgu_buf, wd_buf, sem, state_ref, *, bm):
    blk = pl.program_id(0)
    row0 = blk * bm

    def fetch(e, s):
        return (pltpu.make_async_copy(wgu_hbm.at[e], wgu_buf.at[s], sem.at[s]),
                pltpu.make_async_copy(wd_hbm.at[e], wd_buf.at[s], sem.at[s]))

    def next_with_rows(e):
        return lax.while_loop(lambda n: (n < N_EXPERTS) & (pe_ref[jnp.minimum(n, N_EXPERTS - 1)] <= row0),
                              lambda n: n + 1, e)

    @pl.when(row0 < pe_ref[N_EXPERTS - 1])
    def _():
        @pl.when(blk == 0)
        def _():
            first = next_with_rows(0)
            state_ref[0] = first
            state_ref[1] = 0
            for copy in fetch(first, 0):
                copy.start()

        @pl.when((blk > 0) & (row0 >= pe_ref[state_ref[0]]))
        def _():
            state_ref[0] = next_with_rows(state_ref[0])
            state_ref[1] = 1 - state_ref[1]

        e, s = state_ref[0], state_ref[1]

        @pl.when(row0 == ps_ref[e])
        def _():
            for copy in fetch(e, s):
                copy.wait()
            nxt = lax.while_loop(lambda n: (n < N_EXPERTS) & (pe_ref[jnp.minimum(n, N_EXPERTS - 1)] <= pe_ref[e]),
                                 lambda n: n + 1, e + 1)

            @pl.when(nxt < N_EXPERTS)
            def _():
                for copy in fetch(nxt, 1 - s):
                    copy.start()

        x = x_ref[...].astype(BF16)
        h = jnp.dot(x, wgu_buf[s].astype(BF16), preferred_element_type=F32) + bgu_ref[pl.ds(e, 1), :]
        gate = jnp.minimum(h[:, :D_FF], SWIGLU_LIMIT)
        up = jnp.clip(h[:, D_FF:], -SWIGLU_LIMIT, SWIGLU_LIMIT)
        act = (up + 1.0) * (gate * jax.nn.sigmoid(SWIGLU_ALPHA * gate))
        y_ref[...] = (jnp.dot(act.astype(BF16), wd_buf[s].astype(BF16), preferred_element_type=F32)
                      + bd_ref[pl.ds(e, 1), :])

    @pl.when(row0 >= pe_ref[N_EXPERTS - 1])
    def _():
        y_ref[...] = jnp.zeros_like(y_ref)


def _experts(xb, pad_start, pad_end, w_gu, b_gu, w_down, b_down, bm):
    rows = xb.shape[0]
    n_blocks = rows // bm
    used = lambda i, ps, pe: (jnp.maximum(jnp.minimum(i, pe[N_EXPERTS - 1] // bm - 1), 0), 0)
    whole = lambda i, ps, pe: (0, 0)
    return pl.pallas_call(
        functools.partial(_expert_kernel, bm=bm),
        grid_spec=pltpu.PrefetchScalarGridSpec(
            num_scalar_prefetch=2,
            grid=(n_blocks,),
            in_specs=[pl.BlockSpec((bm, D_MODEL), used),
                      pl.BlockSpec(memory_space=pl.ANY),
                      pl.BlockSpec((N_EXPERTS, 2 * D_FF), whole),
                      pl.BlockSpec(memory_space=pl.ANY),
                      pl.BlockSpec((N_EXPERTS, D_MODEL), whole)],
            out_specs=pl.BlockSpec((bm, D_MODEL), lambda i, ps, pe: (i, 0)),
            scratch_shapes=[pltpu.VMEM((2, D_MODEL, 2 * D_FF), F32), pltpu.VMEM((2, D_FF, D_MODEL), F32),
                            pltpu.SemaphoreType.DMA((2,)), pltpu.SMEM((2,), jnp.int32)]),
        out_shape=jax.ShapeDtypeStruct((rows, D_MODEL), F32),
        compiler_params=_params("arbitrary"),
        name="experts",
    )(pad_start, pad_end, xb, w_gu, b_gu, w_down, b_down)


def _combine_ln_kernel(dcur_ref, dnext_ref, x_ref, gate_ref, g_ref, b_ref, yb_hbm, op_ref, os_ref, ybuf, sem, *,
                       tm, n_prompt):
    i = pl.program_id(0)
    slot = i % 2

    def issue(dref, s):
        def body(r, carry):
            for k in range(TOP_K):
                _row_copy(yb_hbm, dref[0, 0, r * TOP_K + k], ybuf.at[s, k], r,
                          sem.at[s]).start(priority=k % N_DMA_PRIORITIES)
            return carry
        lax.fori_loop(0, tm, body, 0)

    @pl.when(i == 0)
    def _():
        issue(dcur_ref, 0)

    for s in range(2):
        @pl.when((i + 1 < pl.num_programs(0)) & (slot != s))
        def _():
            issue(dnext_ref, s)

    pltpu.make_async_copy(ybuf.at[slot], ybuf.at[slot], sem.at[slot]).wait()
    gates = gate_ref[...]
    y = sum(gates[:, k:k + 1] * ybuf[slot, k] for k in range(TOP_K))
    out = _layernorm(DEEPNORM_ALPHA * x_ref[...] + y, g_ref[...], b_ref[...])

    @pl.when(i < n_prompt)
    def _():
        op_ref[...] = out

    @pl.when(i >= n_prompt)
    def _():
        os_ref[...] = out


def _combine_ln(x1, gates, dest, yb, ln_g, ln_b, tm, t_prompt):
    t = x1.shape[0]
    n = t // tm
    n_prompt = t_prompt // tm
    row = lambda i: (i, 0)
    const = lambda i: (0, 0)
    d2 = dest.reshape(n, 1, tm * TOP_K)
    return pl.pallas_call(
        functools.partial(_combine_ln_kernel, tm=tm, n_prompt=n_prompt),
        grid=(n,),
        in_specs=[pl.BlockSpec((1, 1, tm * TOP_K), lambda i: (i, 0, 0), memory_space=pltpu.SMEM),
                  pl.BlockSpec((1, 1, tm * TOP_K), lambda i: (jnp.minimum(i + 1, n - 1), 0, 0),
                               memory_space=pltpu.SMEM),
                  pl.BlockSpec((tm, D_MODEL), row),
                  pl.BlockSpec((tm, LANES), row),
                  pl.BlockSpec((1, D_MODEL), const), pl.BlockSpec((1, D_MODEL), const),
                  pl.BlockSpec(memory_space=pl.ANY)],
        out_specs=[pl.BlockSpec((tm, D_MODEL), lambda i: (jnp.minimum(i, n_prompt - 1), 0)),
                   pl.BlockSpec((tm, D_MODEL), lambda i: (jnp.maximum(i - n_prompt, 0), 0))],
        out_shape=[jax.ShapeDtypeStruct((t_prompt, D_MODEL), F32),
                   jax.ShapeDtypeStruct((t - t_prompt, D_MODEL), F32)],
        scratch_shapes=[pltpu.VMEM((2, TOP_K, tm, D_MODEL), F32), pltpu.SemaphoreType.DMA((2,))],
        compiler_params=_params("arbitrary"),
        name="combine_ln",
    )(d2, d2, x1, gates, ln_g.reshape(1, D_MODEL), ln_b.reshape(1, D_MODEL), yb)


MOE_BM = 512
ATTN_BLK = 512
GDN_SUB = 4
ROUTE_TM = 128


def _mixers(x2d, bsz, seq, cache_k, cache_v, conv_past, s0, lam, lam_init, w_in_bf, conv_w, a_log, dt_bias,
            delta_norm_w, subln_w):
    prompt = cache_k is None
    tm = ATTN_BLK if prompt else x2d.shape[0]
    past8 = jnp.pad(conv_past, ((0, 0), (SUBLANES - (CONV_W - 1), 0), (0, 0)))
    q_bf, k_f, v_f, k_bf, v_bf, cin, z, ab, *ctail = _in_proj(x2d, w_in_bf, past8, conv_w, tm,
                                                              seq // tm if prompt else 1, prompt)
    if prompt:
        att = _attn_prompt(q_bf, k_bf, v_bf, lam, subln_w, bsz, seq, ATTN_BLK, lam_init)
        conv_new = ctail[0][:, SUBLANES - (CONV_W - 1):]
    else:
        att = _attn_sample(q_bf, k_bf, v_bf, cache_k, cache_v, 0, lam, subln_w.reshape(1, HEAD), bsz, seq, lam_init)
        conv_new = cin.reshape(bsz, seq, CONV_CH)[:, seq - (CONV_W - 1):]
    chunk = CHUNK if seq % CHUNK == 0 else seq
    n_sub = GDN_SUB if (seq // chunk) % GDN_SUB == 0 else 1
    u, w, qd, kd, qk, gl = _gdn_prep(cin, ab, past8, conv_w, a_log, dt_bias, bsz, seq, chunk, n_sub, prompt)
    o, s_new = _gdn_scan(u, w, qd, kd, qk, gl, z, s0, delta_norm_w.reshape(1, HEAD), bsz, seq, chunk)
    return att, o, k_f, v_f, conv_new, s_new


def _moe(x1, idx, gates, rank, counts, w_gu, b_gu, w_down, b_down, ln_g, ln_b, bm, tm, t_prompt):
    t = x1.shape[0]
    n = t * TOP_K
    counts = counts[0, :N_EXPERTS]
    padded = (counts + bm - 1) // bm * bm
    pad_end = jnp.cumsum(padded).astype(jnp.int32)
    pad_start = (pad_end - padded).astype(jnp.int32)
    idx = idx[:, :TOP_K, :]
    dest = rank[:, :TOP_K, :] + sum(jnp.where(idx == e, pad_start[e], 0) for e in range(N_EXPERTS))
    dest = dest.transpose(0, 2, 1).reshape(t, TOP_K)
    n_blocks = -(-n // bm) + N_EXPERTS
    xb = _dispatch(x1, dest, pad_start, pad_end, n_blocks * bm, tm, bm)
    yb = _experts(xb, pad_start, pad_end, w_gu, b_gu, w_down, b_down, bm)
    return _combine_ln(x1, gates, dest, yb, ln_g, ln_b, ROUTE_TM, t_prompt)


def kernel(x_prompt, x_sample, cache_k, cache_v, state_conv, state_delta, w_in, conv_w, a_log, dt_bias,
           delta_norm_w, lambda_q1, lambda_k1, lambda_q2, lambda_k2, subln_w, w_out, ln1_g, ln1_b,
           router_w, router_b, w_gu, b_gu, w_down, b_down, ln2_g, ln2_b):
    bp, lp, _ = x_prompt.shape
    bs, ls, _ = x_sample.shape
    l = 0
    lam_init = 0.8 - 0.6 * math.exp(-0.3 * l)
    lam = (jnp.exp(jnp.sum(lambda_q1[l] * lambda_k1[l])) - jnp.exp(jnp.sum(lambda_q2[l] * lambda_k2[l]))
           + lam_init).reshape(1).astype(F32)
    w_in_bf = jnp.pad(w_in[l], ((0, 0), (0, IN_COLS_PAD - IN_COLS))).astype(BF16)
    shared = (lam, lam_init, w_in_bf, conv_w[l], a_log[l], dt_bias[l], delta_norm_w[l], subln_w[l])

    xp = x_prompt.reshape(bp * lp, D_MODEL)
    xs = x_sample.reshape(bs * ls, D_MODEL)
    zero_conv = jnp.zeros((bp, CONV_W - 1, CONV_CH), F32)
    zero_s = jnp.zeros((bp, N_HEADS, HEAD, HEAD), F32)
    att_p, o_p, k_p, v_p, cin_p, s_p = _mixers(xp, bp, lp, None, None, zero_conv, zero_s, *shared)
    att_s, o_s, k_s, v_s, cin_s, s_s = _mixers(xs, bs, ls, cache_k, cache_v, state_conv[l],
                                               state_delta[l], *shared)

    n_sub = next(c for c in (4, 3, 2, 1) if ((bp * lp + bs * ls) // ROUTE_TM) % c == 0)
    x1, idx, gates, rank, counts = _out_router(((att_p, o_p, xp), (att_s, o_s, xs)), w_out[l].astype(BF16),
                                               ln1_g[l], ln1_b[l], router_w[l], router_b[l], ROUTE_TM, n_sub)
    tp = bp * lp
    tm = n_sub * ROUTE_TM
    y_p, y_s = _moe(x1, idx, gates, rank, counts, w_gu[l], b_gu[l], w_down[l], b_down[l], ln2_g[l], ln2_b[l],
                    MOE_BM, tm, tp)
    return (y_p.reshape(bp, lp, D_MODEL), y_s.reshape(bs, ls, D_MODEL),
            k_p.reshape(1, bp, lp, N_HEADS, HEAD), v_p.reshape(1, bp, lp, N_HEADS, HEAD),
            cin_p[None], s_p[None].astype(state_delta.dtype),
            k_s.reshape(1, bs, ls, N_HEADS, HEAD), v_s.reshape(1, bs, ls, N_HEADS, HEAD),
            cin_s[None], s_s[None].astype(state_delta.dtype))
```

```python
import functools
import math

import jax
import jax.numpy as jnp
from jax import lax
from jax.experimental import pallas as pl
from jax.experimental.pallas import tpu as pltpu

F32 = jnp.float32
BF16 = jnp.bfloat16

D_MODEL = 1024
HEAD = 128
N_HEADS = 4
DQK = HEAD // 2
GROUP = N_HEADS * HEAD
CONV_W = 4
CONV_CH = 3 * GROUP
CHUNK = 64
ALIBI_MAX = 8.0
N_EXPERTS = 32
TOP_K = 4
D_FF = D_MODEL
SWIGLU_LIMIT = 7.0
SWIGLU_ALPHA = 1.702
DEPTH = 1
DEEPNORM_ALPHA = (2 * DEPTH) ** 0.25
LN_EPS = 1e-5
SUBLN_EPS = 1e-5
GATED_NORM_EPS = 1e-6
L2_EPS = 1e-6

LANES = 128
SUBLANES = 8
BF16_EXACT_INT = 256
BF16_ROWS = 16
LOG2E = 1.4426950408889634
N_POS = 6
ONES_ROWS = BF16_ROWS
N_DMA_PRIORITIES = 2
VMEM_LIMIT = 56 * 1024 * 1024

COL_Q, COL_K, COL_V, COL_CONV = 0, GROUP, 2 * GROUP, 3 * GROUP
COL_Z = COL_CONV + CONV_CH
COL_AB = COL_Z + GROUP
IN_COLS = COL_AB + 2 * N_HEADS
IN_COLS_PAD = COL_AB + LANES


def _params(*sem):
    return pltpu.CompilerParams(dimension_semantics=sem, vmem_limit_bytes=VMEM_LIMIT)


def _conv_activation(prev, raw, cw_ref):
    rows = raw.shape[0]
    xin = jnp.concatenate([prev, raw], axis=0)
    conv = sum(xin[SUBLANES - (CONV_W - 1) + j: SUBLANES - (CONV_W - 1) + j + rows] * cw_ref[j:j + 1, :]
               for j in range(CONV_W))
    conv = conv * jax.nn.sigmoid(conv)
    parts = []
    for h in range(2 * N_HEADS):
        x = conv[:, h * HEAD:(h + 1) * HEAD]
        scale = HEAD ** -0.5 if h < N_HEADS else 1.0
        parts.append(x * (lax.rsqrt(jnp.sum(x * x, axis=-1, keepdims=True) + L2_EPS) * scale))
    return jnp.concatenate(parts + [conv[:, 2 * GROUP:]], axis=1)


def _in_proj_kernel(x_ref, w_ref, past_ref, cw_ref, q_ref, kf_ref, vf_ref, kb_ref, vb_ref, c_ref, z_ref, ab_ref,
                    *conv_refs, tm, tiles_per_stream, prompt):
    xb = x_ref[...].astype(BF16)

    def section(lo, hi):
        return jnp.dot(xb, w_ref[:, lo:hi], preferred_element_type=F32)

    q = section(COL_Q, COL_K) * (DQK ** -0.5 * (LOG2E if prompt else 1.0))
    k = section(COL_K, COL_V)
    kf_ref[0] = k.reshape(tm, N_HEADS, HEAD)
    kb_ref[...] = k.astype(BF16)
    v = section(COL_V, COL_CONV)
    vf_ref[0] = v.reshape(tm, N_HEADS, HEAD)
    raw = section(COL_CONV, COL_Z)
    if prompt:
        ctail_ref, tail_ref = conv_refs
        q_ref[0] = q.T.astype(BF16)
        vb_ref[0] = v.T.astype(BF16)
        prev = jnp.where(pl.program_id(0) % tiles_per_stream == 0, past_ref[0], tail_ref[...])
        c_ref[...] = _conv_activation(prev, raw, cw_ref)
        tail_ref[...] = raw[tm - SUBLANES:, :]
        ctail_ref[0] = raw[tm - SUBLANES:, :]
    else:
        q_ref[...] = q.astype(BF16)
        vb_ref[...] = v.astype(BF16)
        c_ref[...] = raw
    z_ref[...] = section(COL_Z, COL_AB)
    ab_ref[...] = section(COL_AB, IN_COLS_PAD)


def _in_proj(x2d, w_bf, past8, conv_w, tm, tiles_per_stream, prompt):
    t = x2d.shape[0]
    row = lambda i: (i, 0)
    stream = lambda i: (i // tiles_per_stream, 0, 0)
    widths = (GROUP, GROUP, GROUP, GROUP, GROUP, CONV_CH, GROUP, LANES)
    dtypes = (BF16, F32, F32, BF16, BF16, F32, F32, F32)
    out_specs = [pl.BlockSpec((tm, w), row) for w in widths]
    out_shape = [jax.ShapeDtypeStruct((t, w), d) for w, d in zip(widths, dtypes)]
    for slot in (1, 2):
        out_specs[slot] = pl.BlockSpec((1, tm, N_HEADS, HEAD), lambda i: (i, 0, 0, 0))
        out_shape[slot] = jax.ShapeDtypeStruct((t // tm, tm, N_HEADS, HEAD), F32)
    if prompt:
        for slot in (0, 4):
            out_specs[slot] = pl.BlockSpec((1, GROUP, tm), lambda i: (i, 0, 0))
            out_shape[slot] = jax.ShapeDtypeStruct((t // tm, GROUP, tm), BF16)
        out_specs.append(pl.BlockSpec((1, SUBLANES, CONV_CH), stream))
        out_shape.append(jax.ShapeDtypeStruct((past8.shape[0], SUBLANES, CONV_CH), F32))
    return pl.pallas_call(
        functools.partial(_in_proj_kernel, tm=tm, tiles_per_stream=tiles_per_stream, prompt=prompt),
        grid=(t // tm,),
        in_specs=[pl.BlockSpec((tm, D_MODEL), row),
                  pl.BlockSpec((D_MODEL, IN_COLS_PAD), lambda i: (0, 0)),
                  pl.BlockSpec((1, SUBLANES, CONV_CH), stream),
                  pl.BlockSpec((CONV_W, CONV_CH), lambda i: (0, 0))],
        out_specs=out_specs,
        out_shape=out_shape,
        scratch_shapes=[pltpu.VMEM((SUBLANES, CONV_CH), F32)] if prompt else [],
        compiler_params=_params("arbitrary"),
        name="in_proj",
    )(x2d, w_bf, past8, conv_w)


def _alibi_slopes():
    return [2.0 ** (-ALIBI_MAX * (h + 1) / N_HEADS) for h in range(N_HEADS)]


def _stack_halves(q):
    lane = lax.broadcasted_iota(jnp.int32, q.shape, 1)
    zero = jnp.zeros_like(q)
    return jnp.concatenate([jnp.where(lane < DQK, q, zero), jnp.where(lane < DQK, zero, q)], axis=0)


def _diff_norm(acc, l, lam, w, lam_init, rows):
    o = acc[:rows] / l[:rows] - lam * (acc[rows:] / l[rows:])
    ms = jnp.mean(o * o, axis=-1, keepdims=True)
    return o * lax.rsqrt(ms + SUBLN_EPS) * w * (1.0 - lam_init)


def _head_slope(h):
    s = _alibi_slopes()
    return jnp.where(h == 0, s[0], jnp.where(h == 1, s[1], jnp.where(h == 2, s[2], s[3]))).astype(F32)


def _attn_prompt_kernel(lam_ref, qt_ref, k_ref, vt_ref, w_ref, o_ref,
                        diag_ref, kaug_ref, vaug_ref, qz_ref, s0_ref, s1_ref, p0_ref, p1_ref, mx0_ref, mx1_ref, m_ref, acc_ref, *, blk, lam_init):
    i = pl.program_id(2)
    slope = _head_slope(pl.program_id(1)) * LOG2E
    rows = 2 * blk
    n_kv = k_ref.shape[0] // blk

    def pieces(x):
        lo = x % BF16_EXACT_INT
        return _split3(slope * lo.astype(F32)) + _split3(slope * (x - lo).astype(F32))

    @pl.when(i == 0)
    def _():
        lane = lax.broadcasted_iota(jnp.int32, (blk, HEAD), 1)
        extra = jnp.where(lane < N_POS, 1.0, 0.0).astype(BF16)
        for n, piece in enumerate(pieces(lax.broadcasted_iota(jnp.int32, (blk, 1), 0))):
            extra = jnp.where(lane == N_POS + n, piece, extra)
        ones = jnp.ones((ONES_ROWS, blk), BF16)
        c = lax.broadcasted_iota(jnp.int32, (blk, rows), 0)
        a = lax.broadcasted_iota(jnp.int32, (blk, rows), 1) % blk
        diag_ref[...] = jnp.where(c // CHUNK <= a // CHUNK, -slope * jnp.abs(a - c).astype(F32), -jnp.inf)
        r = lax.broadcasted_iota(jnp.int32, (HEAD, rows), 0)
        qx = jnp.where((r >= N_POS) & (r < 2 * N_POS), 1.0, 0.0).astype(BF16)
        for n, piece in enumerate(pieces(-(lax.broadcasted_iota(jnp.int32, (1, rows), 1) % blk))):
            qx = jnp.where(r == n, piece, qx)
        qz_ref[HEAD:, :] = qx

        def fill(j, carry):
            j0 = pl.multiple_of(j * blk, blk)
            kaug_ref[pl.ds(j0, blk), :HEAD] = k_ref[pl.ds(j0, blk), :]
            kaug_ref[pl.ds(j0, blk), HEAD:] = extra
            vaug_ref[j, :HEAD, :] = vt_ref[j]
            vaug_ref[j, HEAD:, :] = ones
            return carry

        lax.fori_loop(0, n_kv, fill, 0)

    qt = qt_ref[0]
    d = lax.broadcasted_iota(jnp.int32, qt.shape, 0)
    zero = jnp.zeros_like(qt)
    qz_ref[:HEAD, :] = jnp.concatenate([jnp.where(d < DQK, qt, zero), jnp.where(d < DQK, zero, qt)], axis=1)

    def tile_rows(j):
        return pl.ds(pl.multiple_of(j * blk, blk), blk)

    s = jnp.dot(k_ref[tile_rows(i), :], qz_ref[:HEAD, :], preferred_element_type=F32) + diag_ref[...]
    m = jnp.max(s, axis=0, keepdims=True)
    p1_ref[...] = jnp.exp2(s - m).astype(BF16)
    m_ref[...] = m
    acc_ref[...] = jnp.zeros_like(acc_ref)
    s_first = jnp.dot(kaug_ref[tile_rows(0), :], qz_ref[...], preferred_element_type=F32)
    s0_ref[...] = s_first
    mx0_ref[...] = jnp.max(s_first, axis=0, keepdims=True)

    def step(j, s_cur, s_nxt, p_cur, p_nxt, mx_cur, mx_nxt):
        s_next = jnp.dot(kaug_ref[tile_rows(jnp.minimum(j + 1, i - 1)), :], qz_ref[...],
                         preferred_element_type=F32)
        s_nxt[...] = s_next
        mx_nxt[...] = jnp.max(s_next, axis=0, keepdims=True)
        pv = jnp.dot(vaug_ref[jnp.where(j == 0, i, j - 1)], p_nxt[...], preferred_element_type=F32)
        shift = -slope * ((i - j) * blk).astype(F32)
        m_old = m_ref[...]
        m_new = jnp.maximum(m_old, mx_cur[...] + shift)
        p_cur[...] = jnp.exp2(s_cur[...] - (m_new - shift)).astype(BF16)
        acc_ref[...] = (acc_ref[...] + pv) * jnp.exp2(m_old - m_new)
        m_ref[...] = m_new

    def body(jj, carry):
        step(2 * jj, s0_ref, s1_ref, p0_ref, p1_ref, mx0_ref, mx1_ref)

        @pl.when(2 * jj + 1 < i)
        def _():
            step(2 * jj + 1, s1_ref, s0_ref, p1_ref, p0_ref, mx1_ref, mx0_ref)

        return carry

    lax.fori_loop(0, (i + 1) // 2, body, 0)
    p_last = jnp.where(i % 2 == 1, p0_ref[...], p1_ref[...])
    acc = acc_ref[...] + jnp.dot(vaug_ref[jnp.where(i > 0, i - 1, i)], p_last, preferred_element_type=F32)
    l = acc[HEAD:HEAD + 1, :]
    num = acc[:HEAD, :]
    ot = num[:, :blk] / l[:, :blk] - lam_ref[0] * (num[:, blk:] / l[:, blk:])
    ms = jnp.mean(ot * ot, axis=0, keepdims=True)
    ot = ot * lax.rsqrt(ms + SUBLN_EPS) * w_ref[...] * (1.0 - lam_init)
    o_ref[...] = ot.T.astype(o_ref.dtype)


def _attn_prompt(qt_bf, k_bf, vt_bf, lam, subln_w, bsz, seq, blk, lam_init):
    nq = seq // blk
    kern = functools.partial(_attn_prompt_kernel, blk=blk, lam_init=lam_init)
    return pl.pallas_call(
        kern,
        grid=(bsz, N_HEADS, nq),
        in_specs=[pl.BlockSpec(memory_space=pltpu.SMEM),
                  pl.BlockSpec((1, HEAD, blk), lambda b, h, i: (b * nq + i, h, 0)),
                  pl.BlockSpec((seq, HEAD), lambda b, h, i: (b, h)),
                  pl.BlockSpec((nq, HEAD, blk), lambda b, h, i: (b, h, 0)),
                  pl.BlockSpec((HEAD, 1), lambda b, h, i: (0, 0))],
        out_specs=pl.BlockSpec((blk, HEAD), lambda b, h, i: (b * nq + i, h)),
        scratch_shapes=[pltpu.VMEM((blk, 2 * blk), F32),
                        pltpu.VMEM((seq, 2 * HEAD), BF16),
                        pltpu.VMEM((nq, HEAD + ONES_ROWS, blk), BF16),
                        pltpu.VMEM((2 * HEAD, 2 * blk), BF16),
                        pltpu.VMEM((blk, 2 * blk), F32), pltpu.VMEM((blk, 2 * blk), F32),
                        pltpu.VMEM((blk, 2 * blk), BF16), pltpu.VMEM((blk, 2 * blk), BF16),
                        pltpu.VMEM((1, 2 * blk), F32), pltpu.VMEM((1, 2 * blk), F32),
                        pltpu.VMEM((1, 2 * blk), F32),
                        pltpu.VMEM((HEAD + ONES_ROWS, 2 * blk), F32)],
        out_shape=jax.ShapeDtypeStruct((bsz * seq, GROUP), BF16),
        compiler_params=_params("parallel", "parallel", "arbitrary"),
        name="attn_prompt",
    )(lam, qt_bf, k_bf, vt_bf, subln_w.reshape(HEAD, 1))


def _attn_sample_kernel(lam_ref, q_ref, kn_ref, vn_ref, kc_ref, vc_ref, w_ref, o_ref, *, seq, past, lam_init):
    nt = (((1,), (1,)), ((), ()))
    per_head = 2 * seq
    qz = jnp.concatenate([_stack_halves(q_ref[:, h * HEAD:(h + 1) * HEAD]) for h in range(N_HEADS)], axis=0)
    row = lax.broadcasted_iota(jnp.int32, (N_HEADS * per_head, 1), 0)
    q_head = row // per_head
    q_pos = past + row % seq
    slopes = _alibi_slopes()
    slope = jnp.where(q_head == 0, slopes[0], jnp.where(q_head == 1, slopes[1],
                                                        jnp.where(q_head == 2, slopes[2], slopes[3]))).astype(F32)

    def scores(keys, first_pos):
        n = lax.broadcasted_iota(jnp.int32, (1, keys.shape[0]), 1)
        s = lax.dot_general(qz, keys, nt, preferred_element_type=F32)
        bias = -slope * jnp.abs(q_pos - (first_pos + n // N_HEADS)).astype(F32)
        return jnp.where(n % N_HEADS == q_head, s + bias, -jnp.inf)

    s_c = scores(kc_ref[0, 0].reshape(past * N_HEADS, HEAD).astype(BF16), 0)
    s_n = scores(kn_ref[...].reshape(seq * N_HEADS, HEAD), past)
    m = jnp.maximum(jnp.max(s_c, axis=-1, keepdims=True), jnp.max(s_n, axis=-1, keepdims=True))
    p_c = jnp.exp(s_c - m)
    p_n = jnp.exp(s_n - m)
    l = jnp.sum(p_c, axis=-1, keepdims=True) + jnp.sum(p_n, axis=-1, keepdims=True)
    acc = (jnp.dot(p_c.astype(BF16), vc_ref[0, 0].reshape(past * N_HEADS, HEAD).astype(BF16),
                   preferred_element_type=F32)
           + jnp.dot(p_n.astype(BF16), vn_ref[...].reshape(seq * N_HEADS, HEAD), preferred_element_type=F32))
    for h in range(N_HEADS):
        rows = slice(h * per_head, (h + 1) * per_head)
        o_ref[:, h * HEAD:(h + 1) * HEAD] = _diff_norm(acc[rows], l[rows], lam_ref[0], w_ref[...], lam_init,
                                                       seq).astype(o_ref.dtype)


def _attn_sample(q_bf, k_bf, v_bf, cache_k, cache_v, layer, lam, subln_w, bsz, seq, lam_init):
    past = cache_k.shape[2]
    kern = functools.partial(_attn_sample_kernel, seq=seq, past=past, lam_init=lam_init)
    new = pl.BlockSpec((seq, GROUP), lambda b: (b, 0))
    cache = pl.BlockSpec((1, 1, past, N_HEADS, HEAD), lambda b: (layer, b, 0, 0, 0))
    return pl.pallas_call(
        kern,
        grid=(bsz,),
        in_specs=[pl.BlockSpec(memory_space=pltpu.SMEM), new, new, new, cache, cache,
                  pl.BlockSpec((1, HEAD), lambda b: (0, 0))],
        out_specs=new,
        out_shape=jax.ShapeDtypeStruct((bsz * seq, GROUP), BF16),
        compiler_params=_params("parallel"),
        name="attn_sample",
    )(lam, q_bf, k_bf, v_bf, cache_k, cache_v, subln_w)


def _split3(x):
    hi = x.astype(BF16)
    r1 = x - hi.astype(F32)
    mid = r1.astype(BF16)
    lo = (r1 - mid.astype(F32)).astype(BF16)
    return hi, mid, lo


def _gdn_prep_kernel(cin_ref, halo_ref, past_ref, ab_ref, cw_ref, alog_ref, dtb_ref,
                     u_ref, w_ref, qd_ref, kd_ref, qk_ref, gl_ref, *, chunk, n_sub, activated):
    c_idx = pl.program_id(1)
    rows = chunk * n_sub
    if activated:
        conv = cin_ref[...]
    else:
        prev = jnp.where(c_idx == 0, past_ref[0], halo_ref[...])
        conv = _conv_activation(prev, cin_ref[...], cw_ref)

    ab = ab_ref[...]
    lane = lax.broadcasted_iota(jnp.int32, ab.shape, 1)
    pre = ab + dtb_ref[...]
    softplus = jnp.maximum(pre, 0.0) + jnp.log(1.0 + jnp.exp(-jnp.abs(pre)))
    g = jnp.where(lane < N_HEADS, -jnp.exp(alog_ref[...]) * softplus, 0.0)
    beta_all = jax.nn.sigmoid(ab)

    ri = lax.broadcasted_iota(jnp.int32, (rows, rows), 0)
    ci = lax.broadcasted_iota(jnp.int32, (rows, rows), 1)
    same = (ri // chunk) == (ci // chunk)
    incl = same & (ri >= ci)
    strict = same & (ri > ci)
    eye = jnp.where(ri == ci, 1.0, 0.0).astype(F32)
    nt = (((1,), (1,)), ((), ()))
    g_parts = _split3(g)
    ones_incl = jnp.where(incl, 1.0, 0.0).astype(BF16)
    ones_same = jnp.where(same, 1.0, 0.0).astype(BF16)
    gc = sum(jnp.dot(ones_incl, part, preferred_element_type=F32) for part in g_parts)
    g_end = sum(jnp.dot(ones_same, part, preferred_element_type=F32) for part in g_parts)
    gct = gc.T
    for sc in range(n_sub):
        gl_ref[sc] = g_end[sc * chunk:sc * chunk + 1, :]

    t_mats, p_mats, rhs = [], [], []
    for h in range(N_HEADS):
        col = slice(h * HEAD, (h + 1) * HEAD)
        qh = conv[:, h * HEAD:(h + 1) * HEAD]
        kh = conv[:, GROUP + h * HEAD:GROUP + (h + 1) * HEAD]
        vh = conv[:, 2 * GROUP + h * HEAD:2 * GROUP + (h + 1) * HEAD]
        beta = beta_all[:, N_HEADS + h:N_HEADS + h + 1]
        gcol = gc[:, h:h + 1]
        grow = gct[h:h + 1, :]
        gamma = jnp.exp(jnp.where(incl, gcol - grow, -jnp.inf))
        egc = jnp.exp(gcol)
        kb = kh * beta
        khb = kh.astype(BF16)
        a = jnp.where(strict, lax.dot_general(kb.astype(BF16), khb, nt, preferred_element_type=F32) * gamma, 0.0)
        qk = (lax.dot_general(qh.astype(BF16), khb, nt, preferred_element_type=F32) * gamma).astype(BF16)
        for sc in range(n_sub):
            blk = slice(sc * chunk, (sc + 1) * chunk)
            qk_ref[h, blk, :] = qk[blk, blk]
        qd_ref[:, col] = (qh * egc).astype(BF16)
        kd_ref[:, col] = (kh * jnp.exp(g_end[:, h:h + 1] - gcol)).astype(BF16)
        t_mats.append(eye - a)
        p_mats.append(a)
        rhs.append(((vh * beta).astype(BF16), (kb * egc).astype(BF16)))

    for _ in range(int(math.log2(chunk)) - 1):
        for h in range(N_HEADS):
            pb = p_mats[h].astype(BF16)
            p_mats[h] = jnp.dot(pb, pb, preferred_element_type=F32)
        for h in range(N_HEADS):
            t_mats[h] = t_mats[h] + jnp.dot(t_mats[h].astype(BF16), p_mats[h].astype(BF16),
                                            preferred_element_type=F32)

    for h in range(N_HEADS):
        col = slice(h * HEAD, (h + 1) * HEAD)
        t_inv = t_mats[h].astype(BF16)
        u_ref[:, col] = jnp.dot(t_inv, rhs[h][0], preferred_element_type=F32)
        w_ref[:, col] = jnp.dot(t_inv, rhs[h][1], preferred_element_type=F32).astype(BF16)


def _gdn_prep(cin, ab, past8, conv_w, a_log, dt_bias, bsz, seq, chunk, n_sub, activated):
    rows = chunk * n_sub
    nblk = seq // rows
    t = bsz * seq
    lanes = lambda v: jnp.pad(v.reshape(1, N_HEADS).astype(F32), ((0, 0), (0, LANES - N_HEADS)))
    kern = functools.partial(_gdn_prep_kernel, chunk=chunk, n_sub=n_sub, activated=activated)
    rowblk = lambda b, c: (b * nblk + c, 0)
    halo = lambda b, c: (jnp.maximum((b * nblk + c) * (rows // SUBLANES) - 1, 0), 0)
    const = lambda b, c: (0, 0)
    return pl.pallas_call(
        kern,
        grid=(bsz, nblk),
        in_specs=[pl.BlockSpec((rows, CONV_CH), rowblk),
                  pl.BlockSpec((SUBLANES, CONV_CH), halo),
                  pl.BlockSpec((1, SUBLANES, CONV_CH), lambda b, c: (b, 0, 0)),
                  pl.BlockSpec((rows, LANES), rowblk),
                  pl.BlockSpec((CONV_W, CONV_CH), const),
                  pl.BlockSpec((1, LANES), const),
                  pl.BlockSpec((1, LANES), const)],
        out_specs=[pl.BlockSpec((rows, GROUP), rowblk),
                   pl.BlockSpec((rows, GROUP), rowblk),
                   pl.BlockSpec((rows, GROUP), rowblk),
                   pl.BlockSpec((rows, GROUP), rowblk),
                   pl.BlockSpec((N_HEADS, rows, chunk), lambda b, c: (0, b * nblk + c, 0)),
                   pl.BlockSpec((n_sub, 1, LANES), lambda b, c: (b * nblk + c, 0, 0))],
        out_shape=[jax.ShapeDtypeStruct((t, GROUP), F32),
                   jax.ShapeDtypeStruct((t, GROUP), BF16),
                   jax.ShapeDtypeStruct((t, GROUP), BF16),
                   jax.ShapeDtypeStruct((t, GROUP), BF16),
                   jax.ShapeDtypeStruct((N_HEADS, t, chunk), BF16),
                   jax.ShapeDtypeStruct((t // chunk, 1, LANES), F32)],
        compiler_params=_params("parallel", "parallel"),
        name="gdn_prep",
    )(cin, cin, past8, ab, conv_w, lanes(a_log), lanes(dt_bias))


def _gdn_scan_kernel(u_ref, w_ref, qd_ref, kd_ref, qk_ref, gl_ref, z_ref, s0_ref, nw_ref,
                     o_ref, sf_ref, s_ref, *, bsz, chunk):
    c_idx = pl.program_id(0)

    @pl.when(c_idx == 0)
    def _():
        s_ref[...] = s0_ref[...]

    tn = (((0,), (0,)), ((), ()))
    chains = [(b, h, slice(h * HEAD, (h + 1) * HEAD)) for b in range(bsz) for h in range(N_HEADS)]
    states = [s_ref[b, h] for b, h, _ in chains]
    prods = [jnp.dot(jnp.concatenate([w_ref[b, :, col], qd_ref[b, :, col]], axis=0), s.astype(BF16),
                     preferred_element_type=F32) for (b, _, col), s in zip(chains, states)]
    v_news = [(u_ref[b, :, col] - r[:chunk]).astype(BF16) for (b, _, col), r in zip(chains, prods)]
    outs = [r[chunk:] + jnp.dot(qk_ref[h, b], v, preferred_element_type=F32)
            for (b, h, _), r, v in zip(chains, prods, v_news)]
    for (b, h, col), s, v in zip(chains, states, v_news):
        glast = jnp.exp(gl_ref[b, 0])
        s_ref[b, h] = s * glast[:, h:h + 1] + lax.dot_general(kd_ref[b, :, col], v, tn,
                                                             preferred_element_type=F32)
    for (b, h, col), o in zip(chains, outs):
        zh = z_ref[b, :, col]
        ms = jnp.mean(o * o, axis=-1, keepdims=True)
        o = o * lax.rsqrt(ms + GATED_NORM_EPS) * nw_ref[...] * (zh * jax.nn.sigmoid(zh))
        o_ref[b, :, col] = o.astype(o_ref.dtype)

    @pl.when(c_idx == pl.num_programs(0) - 1)
    def _():
        sf_ref[...] = s_ref[...]


def _gdn_scan(u, w, qd, kd, qk, gl, z, s0, norm_w, bsz, seq, chunk):
    nc = seq // chunk
    kern = functools.partial(_gdn_scan_kernel, bsz=bsz, chunk=chunk)
    tok = pl.BlockSpec((bsz, chunk, GROUP), lambda c: (0, c, 0))
    state = pl.BlockSpec((bsz, N_HEADS, HEAD, HEAD), lambda c: (0, 0, 0, 0))
    o, s_final = pl.pallas_call(
        kern,
        grid=(nc,),
        in_specs=[tok, tok, tok, tok,
                  pl.BlockSpec((N_HEADS, bsz, chunk, chunk), lambda c: (0, 0, c, 0)),
                  pl.BlockSpec((bsz, 1, 1, LANES), lambda c: (0, c, 0, 0)),
                  tok, state,
                  pl.BlockSpec((1, HEAD), lambda c: (0, 0))],
        out_specs=[tok, state],
        out_shape=[jax.ShapeDtypeStruct((bsz, seq, GROUP), BF16),
                   jax.ShapeDtypeStruct((bsz, N_HEADS, HEAD, HEAD), F32)],
        scratch_shapes=[pltpu.VMEM((bsz, N_HEADS, HEAD, HEAD), F32)],
        compiler_params=_params("arbitrary"),
        name="gdn_scan",
    )(u.reshape(bsz, seq, GROUP), w.reshape(bsz, seq, GROUP), qd.reshape(bsz, seq, GROUP),
      kd.reshape(bsz, seq, GROUP), qk.reshape(N_HEADS, bsz, seq, chunk), gl.reshape(bsz, nc, 1, LANES),
      z.reshape(bsz, seq, GROUP), s0, norm_w)
    return o.reshape(bsz * seq, GROUP), s_final


def _layernorm(x, g, b):
    mu = jnp.mean(x, axis=-1, keepdims=True)
    xc = x - mu
    var = jnp.mean(xc * xc, axis=-1, keepdims=True)
    return xc * lax.rsqrt(var + LN_EPS) * g + b


def _out_router_kernel(*refs, tm, n_sub, n_prompt):
    tiles = [refs[6 * u:6 * u + 6] for u in range(n_sub)]
    wo_ref, g_ref, b_ref, rw_ref, rb_ref, x1_ref, idx_ref, gate_ref, rank_ref, cnt_ref, carry_ref = refs[6 * n_sub:]
    step = pl.program_id(0)
    units = range(n_sub)

    @pl.when(step == 0)
    def _():
        carry_ref[...] = jnp.zeros_like(carry_ref)

    x1s = []
    for u in units:
        attp_ref, atts_ref, op_ref, os_ref, xp_ref, xs_ref = tiles[u]
        prompt = step * n_sub + u < n_prompt
        att = jnp.where(prompt, attp_ref[...], atts_ref[...])
        o = jnp.where(prompt, op_ref[...], os_ref[...])
        x = jnp.where(prompt, xp_ref[...], xs_ref[...])
        mix = (jnp.dot(att, wo_ref[:GROUP, :], preferred_element_type=F32)
               + jnp.dot(o, wo_ref[GROUP:, :], preferred_element_type=F32))
        x1 = _layernorm(DEEPNORM_ALPHA * x + mix, g_ref[...], b_ref[...])
        x1_ref[u * tm:(u + 1) * tm, :] = x1
        x1s.append(x1)

    works = []
    lane = lax.broadcasted_iota(jnp.int32, (tm, LANES), 1)
    for x1 in x1s:
        x_hi = x1.astype(BF16)
        x_lo = (x1 - x_hi.astype(F32)).astype(BF16)
        logits = (jnp.dot(x_hi, rw_ref[0], preferred_element_type=F32)
                  + jnp.dot(x_lo, rw_ref[0], preferred_element_type=F32)
                  + jnp.dot(x_hi, rw_ref[1], preferred_element_type=F32)) + rb_ref[...]
        works.append(jnp.where(lane < N_EXPERTS, logits, -jnp.inf))

    vals, idxs = [[] for _ in units], [[] for _ in units]
    for _ in range(TOP_K):
        for u in units:
            m = jnp.max(works[u], axis=-1, keepdims=True)
            am = jnp.min(jnp.where(works[u] == m, lane, LANES), axis=-1, keepdims=True)
            vals[u].append(m)
            idxs[u].append(am)
            works[u] = jnp.where(lane == am, -jnp.inf, works[u])

    ri = lax.broadcasted_iota(jnp.int32, (tm, tm), 0)
    ci = lax.broadcasted_iota(jnp.int32, (tm, tm), 1)
    before = jnp.where(ri > ci, 1.0, 0.0).astype(BF16)
    base = carry_ref[...]
    for u in units:
        exps = [jnp.exp(v - vals[u][0]) for v in vals[u]]
        denom = exps[0] + exps[1] + exps[2] + exps[3]
        chosen = jnp.zeros((tm, LANES), F32)
        gate_out = jnp.zeros((tm, LANES), F32)
        idx_out = jnp.zeros((tm, LANES), jnp.int32)
        for k in range(TOP_K):
            chosen = jnp.where(lane == idxs[u][k], 1.0, chosen)
            gate_out = jnp.where(lane == k, exps[k] / denom, gate_out)
            idx_out = jnp.where(lane == k, idxs[u][k], idx_out)
        prefix = jnp.dot(before, chosen.astype(BF16), preferred_element_type=F32) + base
        base = base + jnp.sum(chosen, axis=0, keepdims=True)
        rank_out = jnp.zeros((tm, LANES), F32)
        for k in range(TOP_K):
            r = jnp.sum(jnp.where(lane == idxs[u][k], prefix, 0.0), axis=-1, keepdims=True)
            rank_out = jnp.where(lane == k, r, rank_out)
        idx_ref[0, :, u * tm:(u + 1) * tm] = idx_out.T[:SUBLANES, :]
        gate_ref[u * tm:(u + 1) * tm, :] = gate_out
        rank_ref[0, :, u * tm:(u + 1) * tm] = rank_out.astype(jnp.int32).T[:SUBLANES, :]
    carry_ref[...] = base
    cnt_ref[...] = base.astype(jnp.int32)


def _out_router(streams, w_out_bf, ln_g, ln_b, router_w, router_b, tm, n_sub):
    (att_p, o_p, x_p), (att_s, o_s, x_s) = streams
    n_prompt = x_p.shape[0] // tm
    n_sample = x_s.shape[0] // tm
    t = x_p.shape[0] + x_s.shape[0]
    step_rows = tm * n_sub
    row = lambda i: (i, 0)
    const = lambda i: (0, 0)
    slots = pl.BlockSpec((1, SUBLANES, step_rows), lambda i: (i, 0, 0))
    tile_specs, tile_args = [], []
    for u in range(n_sub):
        prow = lambda i, u=u: (jnp.minimum(i * n_sub + u, n_prompt - 1), 0)
        srow = lambda i, u=u: (jnp.clip(i * n_sub + u - n_prompt, 0, n_sample - 1), 0)
        tile_specs += [pl.BlockSpec((tm, GROUP), prow), pl.BlockSpec((tm, GROUP), srow),
                       pl.BlockSpec((tm, GROUP), prow), pl.BlockSpec((tm, GROUP), srow),
                       pl.BlockSpec((tm, D_MODEL), prow), pl.BlockSpec((tm, D_MODEL), srow)]
        tile_args += [att_p, att_s, o_p, o_s, x_p, x_s]
    rw = jnp.pad(router_w, ((0, 0), (0, LANES - N_EXPERTS)))
    rw_hi = rw.astype(BF16)
    rw = jnp.stack([rw_hi, (rw - rw_hi.astype(F32)).astype(BF16)])
    rb = jnp.pad(router_b.reshape(1, N_EXPERTS), ((0, 0), (0, LANES - N_EXPERTS)))
    kern = functools.partial(_out_router_kernel, tm=tm, n_sub=n_sub, n_prompt=n_prompt)
    return pl.pallas_call(
        kern,
        grid=(t // step_rows,),
        in_specs=tile_specs + [pl.BlockSpec((2 * GROUP, D_MODEL), const),
                               pl.BlockSpec((1, D_MODEL), const), pl.BlockSpec((1, D_MODEL), const),
                               pl.BlockSpec((2, D_MODEL, LANES), lambda i: (0, 0, 0)),
                               pl.BlockSpec((1, LANES), const)],
        out_specs=[pl.BlockSpec((step_rows, D_MODEL), row), slots, pl.BlockSpec((step_rows, LANES), row), slots,
                   pl.BlockSpec((1, LANES), const)],
        out_shape=[jax.ShapeDtypeStruct((t, D_MODEL), F32),
                   jax.ShapeDtypeStruct((t // step_rows, SUBLANES, step_rows), jnp.int32),
                   jax.ShapeDtypeStruct((t, LANES), F32),
                   jax.ShapeDtypeStruct((t // step_rows, SUBLANES, step_rows), jnp.int32),
                   jax.ShapeDtypeStruct((1, LANES), jnp.int32)],
        scratch_shapes=[pltpu.VMEM((1, LANES), F32)],
        compiler_params=_params("arbitrary"),
        name="out_router",
    )(*tile_args, w_out_bf, ln_g.reshape(1, D_MODEL), ln_b.reshape(1, D_MODEL), rw, rb)


def _row_copy(src, src_row, dst, dst_row, sem):
    return pltpu.make_async_copy(src.at[pl.ds(src_row, 1), :], dst.at[pl.ds(dst_row, 1), :], sem)


def _dispatch_kernel(ps_ref, pe_ref, dest_ref, x_ref, xb_hbm, zero_ref, sem, zsem, *, tm, bm):
    i = pl.program_id(0)

    def zero_block(row):
        return pltpu.make_async_copy(zero_ref, xb_hbm.at[pl.ds(pl.multiple_of(row, bm), bm), :], zsem)

    def fill(e):
        return zero_block(pe_ref[e] - bm)

    @pl.when(i == 0)
    def _():
        zero_ref[...] = jnp.zeros_like(zero_ref)
        for e in range(N_EXPERTS):
            @pl.when(pe_ref[e] > ps_ref[e])
            def _():
                fill(e).start()
        first_unused = pe_ref[N_EXPERTS - 1] // bm
        n_blocks = xb_hbm.shape[0] // bm
        lax.fori_loop(first_unused, n_blocks, lambda b, c: (zero_block(b * bm).start(), c)[1], 0)
        for e in range(N_EXPERTS):
            @pl.when(pe_ref[e] > ps_ref[e])
            def _():
                fill(e).wait()
        lax.fori_loop(first_unused, n_blocks, lambda b, c: (zero_block(b * bm).wait(), c)[1], 0)

    def issue(r, carry):
        for k in range(TOP_K):
            _row_copy(x_ref, r, xb_hbm, dest_ref[0, 0, r * TOP_K + k], sem).start(priority=k % N_DMA_PRIORITIES)
        return carry

    lax.fori_loop(0, tm, issue, 0)
    pltpu.make_async_copy(xb_hbm.at[pl.ds(0, tm * TOP_K), :], xb_hbm.at[pl.ds(0, tm * TOP_K), :], sem).wait()


def _dispatch(x1, dest, pad_start, pad_end, rows, tm, bm):
    t = x1.shape[0]
    return pl.pallas_call(
        functools.partial(_dispatch_kernel, tm=tm, bm=bm),
        grid_spec=pltpu.PrefetchScalarGridSpec(
            num_scalar_prefetch=2,
            grid=(t // tm,),
            in_specs=[pl.BlockSpec((1, 1, tm * TOP_K), lambda i, ps, pe: (i, 0, 0), memory_space=pltpu.SMEM),
                      pl.BlockSpec((tm, D_MODEL), lambda i, ps, pe: (i, 0))],
            out_specs=pl.BlockSpec(memory_space=pl.ANY),
            scratch_shapes=[pltpu.VMEM((bm, D_MODEL), F32),
                            pltpu.SemaphoreType.DMA(()), pltpu.SemaphoreType.DMA(())]),
        out_shape=jax.ShapeDtypeStruct((rows, D_MODEL), F32),
        compiler_params=_params("arbitrary"),
        name="dispatch",
    )(pad_start, pad_end, dest.reshape(t // tm, 1, tm * TOP_K), x1)


def _expert_kernel(ps_ref, pe_ref, x_ref, wgu_hbm, bgu_ref, wd_hbm, bd_ref, y_ref,
                   wgu_buf, wd_buf, sem, state_ref, *, bm):
    blk = pl.program_id(0)
    row0 = blk * bm

    def fetch(e, s):
        return (pltpu.make_async_copy(wgu_hbm.at[e], wgu_buf.at[s], sem.at[s]),
                pltpu.make_async_copy(wd_hbm.at[e], wd_buf.at[s], sem.at[s]))

    def next_with_rows(e):
        return lax.while_loop(lambda n: (n < N_EXPERTS) & (pe_ref[jnp.minimum(n, N_EXPERTS - 1)] <= row0),
                              lambda n: n + 1, e)

    @pl.when(row0 < pe_ref[N_EXPERTS - 1])
    def _():
        @pl.when(blk == 0)
        def _():
            first = next_with_rows(0)
            state_ref[0] = first
            state_ref[1] = 0
            for copy in fetch(first, 0):
                copy.start()

        @pl.when((blk > 0) & (row0 >= pe_ref[state_ref[0]]))
        def _():
            state_ref[0] = next_with_rows(state_ref[0])
            state_ref[1] = 1 - state_ref[1]

        e, s = state_ref[0], state_ref[1]

        @pl.when(row0 == ps_ref[e])
        def _():
            for copy in fetch(e, s):
                copy.wait()
            nxt = lax.while_loop(lambda n: (n < N_EXPERTS) & (pe_ref[jnp.minimum(n, N_EXPERTS - 1)] <= pe_ref[e]),
                                 lambda n: n + 1, e + 1)

            @pl.when(nxt < N_EXPERTS)
            def _():
                for copy in fetch(nxt, 1 - s):
                    copy.start()

        x = x_ref[...].astype(BF16)
        h = jnp.dot(x, wgu_buf[s].astype(BF16), preferred_element_type=F32) + bgu_ref[pl.ds(e, 1), :]
        gate = jnp.minimum(h[:, :D_FF], SWIGLU_LIMIT)
        up = jnp.clip(h[:, D_FF:], -SWIGLU_LIMIT, SWIGLU_LIMIT)
        act = (up + 1.0) * (gate * jax.nn.sigmoid(SWIGLU_ALPHA * gate))
        y_ref[...] = (jnp.dot(act.astype(BF16), wd_buf[s].astype(BF16), preferred_element_type=F32)
                      + bd_ref[pl.ds(e, 1), :])

    @pl.when(row0 >= pe_ref[N_EXPERTS - 1])
    def _():
        y_ref[...] = jnp.zeros_like(y_ref)


def _experts(xb, pad_start, pad_end, w_gu, b_gu, w_down, b_down, bm):
    rows = xb.shape[0]
    n_blocks = rows // bm
    used = lambda i, ps, pe: (jnp.maximum(jnp.minimum(i, pe[N_EXPERTS - 1] // bm - 1), 0), 0)
    whole = lambda i, ps, pe: (0, 0)
    return pl.pallas_call(
        functools.partial(_expert_kernel, bm=bm),
        grid_spec=pltpu.PrefetchScalarGridSpec(
            num_scalar_prefetch=2,
            grid=(n_blocks,),
            in_specs=[pl.BlockSpec((bm, D_MODEL), used),
                      pl.BlockSpec(memory_space=pl.ANY),
                      pl.BlockSpec((N_EXPERTS, 2 * D_FF), whole),
                      pl.BlockSpec(memory_space=pl.ANY),
                      pl.BlockSpec((N_EXPERTS, D_MODEL), whole)],
            out_specs=pl.BlockSpec((bm, D_MODEL), lambda i, ps, pe: (i, 0)),
            scratch_shapes=[pltpu.VMEM((2, D_MODEL, 2 * D_FF), F32), pltpu.VMEM((2, D_FF, D_MODEL), F32),
                            pltpu.SemaphoreType.DMA((2,)), pltpu.SMEM((2,), jnp.int32)]),
        out_shape=jax.ShapeDtypeStruct((rows, D_MODEL), F32),
        compiler_params=_params("arbitrary"),
        name="experts",
    )(pad_start, pad_end, xb, w_gu, b_gu, w_down, b_down)


def _combine_ln_kernel(dcur_ref, dnext_ref, x_ref, gate_ref, g_ref, b_ref, yb_hbm, op_ref, os_ref, ybuf, sem, *,
                       tm, n_prompt):
    i = pl.program_id(0)
    slot = i % 2

    def issue(dref, s):
        def body(r, carry):
            for k in range(TOP_K):
                _row_copy(yb_hbm, dref[0, 0, r * TOP_K + k], ybuf.at[s, k], r,
                          sem.at[s]).start(priority=k % N_DMA_PRIORITIES)
            return carry
        lax.fori_loop(0, tm, body, 0)

    @pl.when(i == 0)
    def _():
        issue(dcur_ref, 0)

    for s in range(2):
        @pl.when((i + 1 < pl.num_programs(0)) & (slot != s))
        def _():
            issue(dnext_ref, s)

    pltpu.make_async_copy(ybuf.at[slot], ybuf.at[slot], sem.at[slot]).wait()
    gates = gate_ref[...]
    y = sum(gates[:, k:k + 1] * ybuf[slot, k] for k in range(TOP_K))
    out = _layernorm(DEEPNORM_ALPHA * x_ref[...] + y, g_ref[...], b_ref[...])

    @pl.when(i < n_prompt)
    def _():
        op_ref[...] = out

    @pl.when(i >= n_prompt)
    def _():
        os_ref[...] = out


def _combine_ln(x1, gates, dest, yb, ln_g, ln_b, tm, t_prompt):
    t = x1.shape[0]
    n = t // tm
    n_prompt = t_prompt // tm
    row = lambda i: (i, 0)
    const = lambda i: (0, 0)
    d2 = dest.reshape(n, 1, tm * TOP_K)
    return pl.pallas_call(
        functools.partial(_combine_ln_kernel, tm=tm, n_prompt=n_prompt),
        grid=(n,),
        in_specs=[pl.BlockSpec((1, 1, tm * TOP_K), lambda i: (i, 0, 0), memory_space=pltpu.SMEM),
                  pl.BlockSpec((1, 1, tm * TOP_K), lambda i: (jnp.minimum(i + 1, n - 1), 0, 0),
                               memory_space=pltpu.SMEM),
                  pl.BlockSpec((tm, D_MODEL), row),
                  pl.BlockSpec((tm, LANES), row),
                  pl.BlockSpec((1, D_MODEL), const), pl.BlockSpec((1, D_MODEL), const),
                  pl.BlockSpec(memory_space=pl.ANY)],
        out_specs=[pl.BlockSpec((tm, D_MODEL), lambda i: (jnp.minimum(i, n_prompt - 1), 0)),
                   pl.BlockSpec((tm, D_MODEL), lambda i: (jnp.maximum(i - n_prompt, 0), 0))],
        out_shape=[jax.ShapeDtypeStruct((t_prompt, D_MODEL), F32),
                   jax.ShapeDtypeStruct((t - t_prompt, D_MODEL), F32)],
        scratch_shapes=[pltpu.VMEM((2, TOP_K, tm, D_MODEL), F32), pltpu.SemaphoreType.DMA((2,))],
        compiler_params=_params("arbitrary"),
        name="combine_ln",
    )(d2, d2, x1, gates, ln_g.reshape(1, D_MODEL), ln_b.reshape(1, D_MODEL), yb)


MOE_BM = 384
ATTN_BLK = 512
GDN_SUB = 4
ROUTE_TM = 128


def _mixers(x2d, bsz, seq, cache_k, cache_v, conv_past, s0, lam, lam_init, w_in_bf, conv_w, a_log, dt_bias,
            delta_norm_w, subln_w):
    prompt = cache_k is None
    tm = ATTN_BLK if prompt else x2d.shape[0]
    past8 = jnp.pad(conv_past, ((0, 0), (SUBLANES - (CONV_W - 1), 0), (0, 0)))
    q_bf, k_f, v_f, k_bf, v_bf, cin, z, ab, *ctail = _in_proj(x2d, w_in_bf, past8, conv_w, tm,
                                                              seq // tm if prompt else 1, prompt)
    if prompt:
        att = _attn_prompt(q_bf, k_bf, v_bf, lam, subln_w, bsz, seq, ATTN_BLK, lam_init)
        conv_new = ctail[0][:, SUBLANES - (CONV_W - 1):]
    else:
        att = _attn_sample(q_bf, k_bf, v_bf, cache_k, cache_v, 0, lam, subln_w.reshape(1, HEAD), bsz, seq, lam_init)
        conv_new = cin.reshape(bsz, seq, CONV_CH)[:, seq - (CONV_W - 1):]
    chunk = CHUNK if seq % CHUNK == 0 else seq
    n_sub = GDN_SUB if (seq // chunk) % GDN_SUB == 0 else 1
    u, w, qd, kd, qk, gl = _gdn_prep(cin, ab, past8, conv_w, a_log, dt_bias, bsz, seq, chunk, n_sub, prompt)
    o, s_new = _gdn_scan(u, w, qd, kd, qk, gl, z, s0, delta_norm_w.reshape(1, HEAD), bsz, seq, chunk)
    return att, o, k_f, v_f, conv_new, s_new


def _moe(x1, idx, gates, rank, counts, w_gu, b_gu, w_down, b_down, ln_g, ln_b, bm, tm, t_prompt):
    t = x1.shape[0]
    n = t * TOP_K
    counts = counts[0, :N_EXPERTS]
    padded = (counts + bm - 1) // bm * bm
    pad_end = jnp.cumsum(padded).astype(jnp.int32)
    pad_start = (pad_end - padded).astype(jnp.int32)
    idx = idx[:, :TOP_K, :]
    dest = rank[:, :TOP_K, :] + sum(jnp.where(idx == e, pad_start[e], 0) for e in range(N_EXPERTS))
    dest = dest.transpose(0, 2, 1).reshape(t, TOP_K)
    n_blocks = -(-n // bm) + N_EXPERTS
    xb = _dispatch(x1, dest, pad_start, pad_end, n_blocks * bm, tm, bm)
    yb = _experts(xb, pad_start, pad_end, w_gu, b_gu, w_down, b_down, bm)
    return _combine_ln(x1, gates, dest, yb, ln_g, ln_b, ROUTE_TM, t_prompt)


def kernel(x_prompt, x_sample, cache_k, cache_v, state_conv, state_delta, w_in, conv_w, a_log, dt_bias,
           delta_norm_w, lambda_q1, lambda_k1, lambda_q2, lambda_k2, subln_w, w_out, ln1_g, ln1_b,
           router_w, router_b, w_gu, b_gu, w_down, b_down, ln2_g, ln2_b):
    bp, lp, _ = x_prompt.shape
    bs, ls, _ = x_sample.shape
    l = 0
    lam_init = 0.8 - 0.6 * math.exp(-0.3 * l)
    lam = (jnp.exp(jnp.sum(lambda_q1[l] * lambda_k1[l])) - jnp.exp(jnp.sum(lambda_q2[l] * lambda_k2[l]))
           + lam_init).reshape(1).astype(F32)
    w_in_bf = jnp.pad(w_in[l], ((0, 0), (0, IN_COLS_PAD - IN_COLS))).astype(BF16)
    shared = (lam, lam_init, w_in_bf, conv_w[l], a_log[l], dt_bias[l], delta_norm_w[l], subln_w[l])

    xp = x_prompt.reshape(bp * lp, D_MODEL)
    xs = x_sample.reshape(bs * ls, D_MODEL)
    zero_conv = jnp.zeros((bp, CONV_W - 1, CONV_CH), F32)
    zero_s = jnp.zeros((bp, N_HEADS, HEAD, HEAD), F32)
    att_p, o_p, k_p, v_p, cin_p, s_p = _mixers(xp, bp, lp, None, None, zero_conv, zero_s, *shared)
    att_s, o_s, k_s, v_s, cin_s, s_s = _mixers(xs, bs, ls, cache_k, cache_v, state_conv[l],
                                               state_delta[l], *shared)

    n_sub = next(c for c in (4, 3, 2, 1) if ((bp * lp + bs * ls) // ROUTE_TM) % c == 0)
    x1, idx, gates, rank, counts = _out_router(((att_p, o_p, xp), (att_s, o_s, xs)), w_out[l].astype(BF16),
                                               ln1_g[l], ln1_b[l], router_w[l], router_b[l], ROUTE_TM, n_sub)
    tp = bp * lp
    tm = n_sub * ROUTE_TM
    y_p, y_s = _moe(x1, idx, gates, rank, counts, w_gu[l], b_gu[l], w_down[l], b_down[l], ln2_g[l], ln2_b[l],
                    MOE_BM, tm, tp)
    return (y_p.reshape(bp, lp, D_MODEL), y_s.reshape(bs, ls, D_MODEL),
            k_p.reshape(1, bp, lp, N_HEADS, HEAD), v_p.reshape(1, bp, lp, N_HEADS, HEAD),
            cin_p[None], s_p[None].astype(state_delta.dtype),
            k_s.reshape(1, bs, ls, N_HEADS, HEAD), v_s.reshape(1, bs, ls, N_HEADS, HEAD),
            cin_s[None], s_s[None].astype(state_delta.dtype))
```

```python
import functools
import math

import jax
import jax.numpy as jnp
from jax import lax
from jax.experimental import pallas as pl
from jax.experimental.pallas import tpu as pltpu

F32 = jnp.float32
BF16 = jnp.bfloat16

D_MODEL = 1024
HEAD = 128
N_HEADS = 4
DQK = HEAD // 2
GROUP = N_HEADS * HEAD
CONV_W = 4
CONV_CH = 3 * GROUP
CHUNK = 64
ALIBI_MAX = 8.0
N_EXPERTS = 32
TOP_K = 4
D_FF = D_MODEL
SWIGLU_LIMIT = 7.0
SWIGLU_ALPHA = 1.702
DEPTH = 1
DEEPNORM_ALPHA = (2 * DEPTH) ** 0.25
LN_EPS = 1e-5
SUBLN_EPS = 1e-5
GATED_NORM_EPS = 1e-6
L2_EPS = 1e-6

LANES = 128
SUBLANES = 8
BF16_EXACT_INT = 256
BF16_ROWS = 16
LOG2E = 1.4426950408889634
N_POS = 6
ONES_ROWS = BF16_ROWS
N_DMA_PRIORITIES = 2
VMEM_LIMIT = 56 * 1024 * 1024

COL_Q, COL_K, COL_V, COL_CONV = 0, GROUP, 2 * GROUP, 3 * GROUP
COL_Z = COL_CONV + CONV_CH
COL_AB = COL_Z + GROUP
IN_COLS = COL_AB + 2 * N_HEADS
IN_COLS_PAD = COL_AB + LANES


def _params(*sem):
    return pltpu.CompilerParams(dimension_semantics=sem, vmem_limit_bytes=VMEM_LIMIT)


def _conv_activation(prev, raw, cw_ref):
    rows = raw.shape[0]
    xin = jnp.concatenate([prev, raw], axis=0)
    conv = sum(xin[SUBLANES - (CONV_W - 1) + j: SUBLANES - (CONV_W - 1) + j + rows] * cw_ref[j:j + 1, :]
               for j in range(CONV_W))
    conv = conv * jax.nn.sigmoid(conv)
    parts = []
    for h in range(2 * N_HEADS):
        x = conv[:, h * HEAD:(h + 1) * HEAD]
        scale = HEAD ** -0.5 if h < N_HEADS else 1.0
        parts.append(x * (lax.rsqrt(jnp.sum(x * x, axis=-1, keepdims=True) + L2_EPS) * scale))
    return jnp.concatenate(parts + [conv[:, 2 * GROUP:]], axis=1)


def _in_proj_kernel(x_ref, w_ref, past_ref, cw_ref, q_ref, kf_ref, vf_ref, kb_ref, vb_ref, c_ref, z_ref, ab_ref,
                    *conv_refs, tm, tiles_per_stream, prompt):
    xb = x_ref[...].astype(BF16)

    def section(lo, hi):
        return jnp.dot(xb, w_ref[:, lo:hi], preferred_element_type=F32)

    q = section(COL_Q, COL_K) * (DQK ** -0.5 * (LOG2E if prompt else 1.0))
    k = section(COL_K, COL_V)
    kf_ref[0] = k.reshape(tm, N_HEADS, HEAD)
    kb_ref[...] = k.astype(BF16)
    v = section(COL_V, COL_CONV)
    vf_ref[0] = v.reshape(tm, N_HEADS, HEAD)
    raw = section(COL_CONV, COL_Z)
    if prompt:
        ctail_ref, tail_ref = conv_refs
        q_ref[0] = q.T.astype(BF16)
        vb_ref[0] = v.T.astype(BF16)
        prev = jnp.where(pl.program_id(0) % tiles_per_stream == 0, past_ref[0], tail_ref[...])
        c_ref[...] = _conv_activation(prev, raw, cw_ref)
        tail_ref[...] = raw[tm - SUBLANES:, :]
        ctail_ref[0] = raw[tm - SUBLANES:, :]
    else:
        q_ref[...] = q.astype(BF16)
        vb_ref[...] = v.astype(BF16)
        c_ref[...] = raw
    z_ref[...] = section(COL_Z, COL_AB)
    ab_ref[...] = section(COL_AB, IN_COLS_PAD)


def _in_proj(x2d, w_bf, past8, conv_w, tm, tiles_per_stream, prompt):
    t = x2d.shape[0]
    row = lambda i: (i, 0)
    stream = lambda i: (i // tiles_per_stream, 0, 0)
    widths = (GROUP, GROUP, GROUP, GROUP, GROUP, CONV_CH, GROUP, LANES)
    dtypes = (BF16, F32, F32, BF16, BF16, F32, F32, F32)
    out_specs = [pl.BlockSpec((tm, w), row) for w in widths]
    out_shape = [jax.ShapeDtypeStruct((t, w), d) for w, d in zip(widths, dtypes)]
    for slot in (1, 2):
        out_specs[slot] = pl.BlockSpec((1, tm, N_HEADS, HEAD), lambda i: (i, 0, 0, 0))
        out_shape[slot] = jax.ShapeDtypeStruct((t // tm, tm, N_HEADS, HEAD), F32)
    if prompt:
        for slot in (0, 4):
            out_specs[slot] = pl.BlockSpec((1, GROUP, tm), lambda i: (i, 0, 0))
            out_shape[slot] = jax.ShapeDtypeStruct((t // tm, GROUP, tm), BF16)
        out_specs.append(pl.BlockSpec((1, SUBLANES, CONV_CH), stream))
        out_shape.append(jax.ShapeDtypeStruct((past8.shape[0], SUBLANES, CONV_CH), F32))
    return pl.pallas_call(
        functools.partial(_in_proj_kernel, tm=tm, tiles_per_stream=tiles_per_stream, prompt=prompt),
        grid=(t // tm,),
        in_specs=[pl.BlockSpec((tm, D_MODEL), row),
                  pl.BlockSpec((D_MODEL, IN_COLS_PAD), lambda i: (0, 0)),
                  pl.BlockSpec((1, SUBLANES, CONV_CH), stream),
                  pl.BlockSpec((CONV_W, CONV_CH), lambda i: (0, 0))],
        out_specs=out_specs,
        out_shape=out_shape,
        scratch_shapes=[pltpu.VMEM((SUBLANES, CONV_CH), F32)] if prompt else [],
        compiler_params=_params("arbitrary"),
        name="in_proj",
    )(x2d, w_bf, past8, conv_w)


def _alibi_slopes():
    return [2.0 ** (-ALIBI_MAX * (h + 1) / N_HEADS) for h in range(N_HEADS)]


def _stack_halves(q):
    lane = lax.broadcasted_iota(jnp.int32, q.shape, 1)
    zero = jnp.zeros_like(q)
    return jnp.concatenate([jnp.where(lane < DQK, q, zero), jnp.where(lane < DQK, zero, q)], axis=0)


def _diff_norm(acc, l, lam, w, lam_init, rows):
    o = acc[:rows] / l[:rows] - lam * (acc[rows:] / l[rows:])
    ms = jnp.mean(o * o, axis=-1, keepdims=True)
    return o * lax.rsqrt(ms + SUBLN_EPS) * w * (1.0 - lam_init)


def _head_slope(h):
    s = _alibi_slopes()
    return jnp.where(h == 0, s[0], jnp.where(h == 1, s[1], jnp.where(h == 2, s[2], s[3]))).astype(F32)


def _attn_prompt_kernel(lam_ref, qt_ref, k_ref, vt_ref, w_ref, o_ref,
                        diag_ref, kaug_ref, vaug_ref, qz_ref, s0_ref, s1_ref, p0_ref, p1_ref, mx0_ref, mx1_ref, m_ref, acc_ref, *, blk, lam_init):
    i = pl.program_id(2)
    slope = _head_slope(pl.program_id(1)) * LOG2E
    rows = 2 * blk
    n_kv = k_ref.shape[0] // blk

    def pieces(x):
        lo = x % BF16_EXACT_INT
        return _split3(slope * lo.astype(F32)) + _split3(slope * (x - lo).astype(F32))

    @pl.when(i == 0)
    def _():
        lane = lax.broadcasted_iota(jnp.int32, (blk, HEAD), 1)
        extra = jnp.where(lane < N_POS, 1.0, 0.0).astype(BF16)
        for n, piece in enumerate(pieces(lax.broadcasted_iota(jnp.int32, (blk, 1), 0))):
            extra = jnp.where(lane == N_POS + n, piece, extra)
        ones = jnp.ones((ONES_ROWS, blk), BF16)
        c = lax.broadcasted_iota(jnp.int32, (blk, rows), 0)
        a = lax.broadcasted_iota(jnp.int32, (blk, rows), 1) % blk
        diag_ref[...] = jnp.where(c // CHUNK <= a // CHUNK, -slope * jnp.abs(a - c).astype(F32), -jnp.inf)
        r = lax.broadcasted_iota(jnp.int32, (HEAD, rows), 0)
        qx = jnp.where((r >= N_POS) & (r < 2 * N_POS), 1.0, 0.0).astype(BF16)
        for n, piece in enumerate(pieces(-(lax.broadcasted_iota(jnp.int32, (1, rows), 1) % blk))):
            qx = jnp.where(r == n, piece, qx)
        qz_ref[HEAD:, :] = qx

        def fill(j, carry):
            j0 = pl.multiple_of(j * blk, blk)
            kaug_ref[pl.ds(j0, blk), :HEAD] = k_ref[pl.ds(j0, blk), :]
            kaug_ref[pl.ds(j0, blk), HEAD:] = extra
            vaug_ref[j, :HEAD, :] = vt_ref[j]
            vaug_ref[j, HEAD:, :] = ones
            return carry

        lax.fori_loop(0, n_kv, fill, 0)

    qt = qt_ref[0]
    d = lax.broadcasted_iota(jnp.int32, qt.shape, 0)
    zero = jnp.zeros_like(qt)
    qz_ref[:HEAD, :] = jnp.concatenate([jnp.where(d < DQK, qt, zero), jnp.where(d < DQK, zero, qt)], axis=1)

    def tile_rows(j):
        return pl.ds(pl.multiple_of(j * blk, blk), blk)

    acc_ref[...] = jnp.zeros_like(acc_ref)
    for part in range(rows // QUERY_PART):
        cols = slice(part * QUERY_PART, (part + 1) * QUERY_PART)
        s = jnp.dot(k_ref[tile_rows(i), :], qz_ref[:HEAD, cols], preferred_element_type=F32) + diag_ref[:, cols]
        m = jnp.max(s, axis=0, keepdims=True)
        p1_ref[:, cols] = jnp.exp2(s - m).astype(BF16)
        m_ref[:, cols] = m
        s_first = jnp.dot(kaug_ref[tile_rows(0), :], qz_ref[:, cols], preferred_element_type=F32)
        s0_ref[:, cols] = s_first
        mx0_ref[:, cols] = jnp.max(s_first, axis=0, keepdims=True)

    def step(j, s_cur, s_nxt, p_cur, p_nxt, mx_cur, mx_nxt):
        shift = -slope * ((i - j) * blk).astype(F32)
        k_next = kaug_ref[tile_rows(jnp.minimum(j + 1, i - 1)), :]
        v_prev = vaug_ref[jnp.where(j == 0, i, j - 1)]
        for part in range(rows // QUERY_PART):
            cols = slice(part * QUERY_PART, (part + 1) * QUERY_PART)
            s_next = jnp.dot(k_next, qz_ref[:, cols], preferred_element_type=F32)
            s_nxt[:, cols] = s_next
            mx_nxt[:, cols] = jnp.max(s_next, axis=0, keepdims=True)
            pv = jnp.dot(v_prev, p_nxt[:, cols], preferred_element_type=F32)
            m_old = m_ref[:, cols]
            m_new = jnp.maximum(m_old, mx_cur[:, cols] + shift)
            p_cur[:, cols] = jnp.exp2(s_cur[:, cols] - (m_new - shift)).astype(BF16)
            acc_ref[:, cols] = (acc_ref[:, cols] + pv) * jnp.exp2(m_old - m_new)
            m_ref[:, cols] = m_new

    def body(jj, carry):
        step(2 * jj, s0_ref, s1_ref, p0_ref, p1_ref, mx0_ref, mx1_ref)

        @pl.when(2 * jj + 1 < i)
        def _():
            step(2 * jj + 1, s1_ref, s0_ref, p1_ref, p0_ref, mx1_ref, mx0_ref)

        return carry

    lax.fori_loop(0, (i + 1) // 2, body, 0)
    p_last = jnp.where(i % 2 == 1, p0_ref[...], p1_ref[...])
    acc = acc_ref[...] + jnp.dot(vaug_ref[jnp.where(i > 0, i - 1, i)], p_last, preferred_element_type=F32)
    l = acc[HEAD:HEAD + 1, :]
    num = acc[:HEAD, :]
    ot = num[:, :blk] / l[:, :blk] - lam_ref[0] * (num[:, blk:] / l[:, blk:])
    ms = jnp.mean(ot * ot, axis=0, keepdims=True)
    ot = ot * lax.rsqrt(ms + SUBLN_EPS) * w_ref[...] * (1.0 - lam_init)
    o_ref[...] = ot.T.astype(o_ref.dtype)


def _attn_prompt(qt_bf, k_bf, vt_bf, lam, subln_w, bsz, seq, blk, lam_init):
    nq = seq // blk
    kern = functools.partial(_attn_prompt_kernel, blk=blk, lam_init=lam_init)
    return pl.pallas_call(
        kern,
        grid=(bsz, N_HEADS, nq),
        in_specs=[pl.BlockSpec(memory_space=pltpu.SMEM),
                  pl.BlockSpec((1, HEAD, blk), lambda b, h, i: (b * nq + i, h, 0)),
                  pl.BlockSpec((seq, HEAD), lambda b, h, i: (b, h)),
                  pl.BlockSpec((nq, HEAD, blk), lambda b, h, i: (b, h, 0)),
                  pl.BlockSpec((HEAD, 1), lambda b, h, i: (0, 0))],
        out_specs=pl.BlockSpec((blk, HEAD), lambda b, h, i: (b * nq + i, h)),
        scratch_shapes=[pltpu.VMEM((blk, 2 * blk), F32),
                        pltpu.VMEM((seq, 2 * HEAD), BF16),
                        pltpu.VMEM((nq, HEAD + ONES_ROWS, blk), BF16),
                        pltpu.VMEM((2 * HEAD, 2 * blk), BF16),
                        pltpu.VMEM((blk, 2 * blk), F32), pltpu.VMEM((blk, 2 * blk), F32),
                        pltpu.VMEM((blk, 2 * blk), BF16), pltpu.VMEM((blk, 2 * blk), BF16),
                        pltpu.VMEM((1, 2 * blk), F32), pltpu.VMEM((1, 2 * blk), F32),
                        pltpu.VMEM((1, 2 * blk), F32),
                        pltpu.VMEM((HEAD + ONES_ROWS, 2 * blk), F32)],
        out_shape=jax.ShapeDtypeStruct((bsz * seq, GROUP), BF16),
        compiler_params=_params("parallel", "parallel", "arbitrary"),
        name="attn_prompt",
    )(lam, qt_bf, k_bf, vt_bf, subln_w.reshape(HEAD, 1))


def _attn_sample_kernel(lam_ref, q_ref, kn_ref, vn_ref, kc_ref, vc_ref, w_ref, o_ref, *, seq, past, lam_init):
    nt = (((1,), (1,)), ((), ()))
    per_head = 2 * seq
    qz = jnp.concatenate([_stack_halves(q_ref[:, h * HEAD:(h + 1) * HEAD]) for h in range(N_HEADS)], axis=0)
    row = lax.broadcasted_iota(jnp.int32, (N_HEADS * per_head, 1), 0)
    q_head = row // per_head
    q_pos = past + row % seq
    slopes = _alibi_slopes()
    slope = jnp.where(q_head == 0, slopes[0], jnp.where(q_head == 1, slopes[1],
                                                        jnp.where(q_head == 2, slopes[2], slopes[3]))).astype(F32)

    def scores(keys, first_pos):
        n = lax.broadcasted_iota(jnp.int32, (1, keys.shape[0]), 1)
        s = lax.dot_general(qz, keys, nt, preferred_element_type=F32)
        bias = -slope * jnp.abs(q_pos - (first_pos + n // N_HEADS)).astype(F32)
        return jnp.where(n % N_HEADS == q_head, s + bias, -jnp.inf)

    s_c = scores(kc_ref[0, 0].reshape(past * N_HEADS, HEAD).astype(BF16), 0)
    s_n = scores(kn_ref[...].reshape(seq * N_HEADS, HEAD), past)
    m = jnp.maximum(jnp.max(s_c, axis=-1, keepdims=True), jnp.max(s_n, axis=-1, keepdims=True))
    p_c = jnp.exp(s_c - m)
    p_n = jnp.exp(s_n - m)
    l = jnp.sum(p_c, axis=-1, keepdims=True) + jnp.sum(p_n, axis=-1, keepdims=True)
    acc = (jnp.dot(p_c.astype(BF16), vc_ref[0, 0].reshape(past * N_HEADS, HEAD).astype(BF16),
                   preferred_element_type=F32)
           + jnp.dot(p_n.astype(BF16), vn_ref[...].reshape(seq * N_HEADS, HEAD), preferred_element_type=F32))
    for h in range(N_HEADS):
        rows = slice(h * per_head, (h + 1) * per_head)
        o_ref[:, h * HEAD:(h + 1) * HEAD] = _diff_norm(acc[rows], l[rows], lam_ref[0], w_ref[...], lam_init,
                                                       seq).astype(o_ref.dtype)


def _attn_sample(q_bf, k_bf, v_bf, cache_k, cache_v, layer, lam, subln_w, bsz, seq, lam_init):
    past = cache_k.shape[2]
    kern = functools.partial(_attn_sample_kernel, seq=seq, past=past, lam_init=lam_init)
    new = pl.BlockSpec((seq, GROUP), lambda b: (b, 0))
    cache = pl.BlockSpec((1, 1, past, N_HEADS, HEAD), lambda b: (layer, b, 0, 0, 0))
    return pl.pallas_call(
        kern,
        grid=(bsz,),
        in_specs=[pl.BlockSpec(memory_space=pltpu.SMEM), new, new, new, cache, cache,
                  pl.BlockSpec((1, HEAD), lambda b: (0, 0))],
        out_specs=new,
        out_shape=jax.ShapeDtypeStruct((bsz * seq, GROUP), BF16),
        compiler_params=_params("parallel"),
        name="attn_sample",
    )(lam, q_bf, k_bf, v_bf, cache_k, cache_v, subln_w)


def _split3(x):
    hi = x.astype(BF16)
    r1 = x - hi.astype(F32)
    mid = r1.astype(BF16)
    lo = (r1 - mid.astype(F32)).astype(BF16)
    return hi, mid, lo


def _gdn_prep_kernel(cin_ref, halo_ref, past_ref, ab_ref, cw_ref, alog_ref, dtb_ref,
                     u_ref, w_ref, qd_ref, kd_ref, qk_ref, gl_ref, *, chunk, n_sub, activated):
    c_idx = pl.program_id(1)
    rows = chunk * n_sub
    if activated:
        conv = cin_ref[...]
    else:
        prev = jnp.where(c_idx == 0, past_ref[0], halo_ref[...])
        conv = _conv_activation(prev, cin_ref[...], cw_ref)

    ab = ab_ref[...]
    lane = lax.broadcasted_iota(jnp.int32, ab.shape, 1)
    pre = ab + dtb_ref[...]
    softplus = jnp.maximum(pre, 0.0) + jnp.log(1.0 + jnp.exp(-jnp.abs(pre)))
    g = jnp.where(lane < N_HEADS, -jnp.exp(alog_ref[...]) * softplus, 0.0)
    beta_all = jax.nn.sigmoid(ab)

    ri = lax.broadcasted_iota(jnp.int32, (rows, rows), 0)
    ci = lax.broadcasted_iota(jnp.int32, (rows, rows), 1)
    same = (ri // chunk) == (ci // chunk)
    incl = same & (ri >= ci)
    strict = same & (ri > ci)
    eye = jnp.where(ri == ci, 1.0, 0.0).astype(F32)
    nt = (((1,), (1,)), ((), ()))
    g_parts = _split3(g)
    ones_incl = jnp.where(incl, 1.0, 0.0).astype(BF16)
    ones_same = jnp.where(same, 1.0, 0.0).astype(BF16)
    gc = sum(jnp.dot(ones_incl, part, preferred_element_type=F32) for part in g_parts)
    g_end = sum(jnp.dot(ones_same, part, preferred_element_type=F32) for part in g_parts)
    gct = gc.T
    for sc in range(n_sub):
        gl_ref[sc] = g_end[sc * chunk:sc * chunk + 1, :]

    t_mats, p_mats, rhs = [], [], []
    for h in range(N_HEADS):
        col = slice(h * HEAD, (h + 1) * HEAD)
        qh = conv[:, h * HEAD:(h + 1) * HEAD]
        kh = conv[:, GROUP + h * HEAD:GROUP + (h + 1) * HEAD]
        vh = conv[:, 2 * GROUP + h * HEAD:2 * GROUP + (h + 1) * HEAD]
        beta = beta_all[:, N_HEADS + h:N_HEADS + h + 1]
        gcol = gc[:, h:h + 1]
        grow = gct[h:h + 1, :]
        gamma = jnp.exp(jnp.where(incl, gcol - grow, -jnp.inf))
        egc = jnp.exp(gcol)
        kb = kh * beta
        khb = kh.astype(BF16)
        a = jnp.where(strict, lax.dot_general(kb.astype(BF16), khb, nt, preferred_element_type=F32) * gamma, 0.0)
        qk = (lax.dot_general(qh.astype(BF16), khb, nt, preferred_element_type=F32) * gamma).astype(BF16)
        for sc in range(n_sub):
            blk = slice(sc * chunk, (sc + 1) * chunk)
            qk_ref[h, blk, :] = qk[blk, blk]
        qd_ref[:, col] = (qh * egc).astype(BF16)
        kd_ref[:, col] = (kh * jnp.exp(g_end[:, h:h + 1] - gcol)).astype(BF16)
        t_mats.append(eye - a)
        p_mats.append(a)
        rhs.append(((vh * beta).astype(BF16), (kb * egc).astype(BF16)))

    for _ in range(int(math.log2(chunk)) - 1):
        for h in range(N_HEADS):
            pb = p_mats[h].astype(BF16)
            p_mats[h] = jnp.dot(pb, pb, preferred_element_type=F32)
        for h in range(N_HEADS):
            t_mats[h] = t_mats[h] + jnp.dot(t_mats[h].astype(BF16), p_mats[h].astype(BF16),
                                            preferred_element_type=F32)

    for h in range(N_HEADS):
        col = slice(h * HEAD, (h + 1) * HEAD)
        t_inv = t_mats[h].astype(BF16)
        u_ref[:, col] = jnp.dot(t_inv, rhs[h][0], preferred_element_type=F32)
        w_ref[:, col] = jnp.dot(t_inv, rhs[h][1], preferred_element_type=F32).astype(BF16)


def _gdn_prep(cin, ab, past8, conv_w, a_log, dt_bias, bsz, seq, chunk, n_sub, activated):
    rows = chunk * n_sub
    nblk = seq // rows
    t = bsz * seq
    lanes = lambda v: jnp.pad(v.reshape(1, N_HEADS).astype(F32), ((0, 0), (0, LANES - N_HEADS)))
    kern = functools.partial(_gdn_prep_kernel, chunk=chunk, n_sub=n_sub, activated=activated)
    rowblk = lambda b, c: (b * nblk + c, 0)
    halo = lambda b, c: (jnp.maximum((b * nblk + c) * (rows // SUBLANES) - 1, 0), 0)
    const = lambda b, c: (0, 0)
    return pl.pallas_call(
        kern,
        grid=(bsz, nblk),
        in_specs=[pl.BlockSpec((rows, CONV_CH), rowblk),
                  pl.BlockSpec((SUBLANES, CONV_CH), halo),
                  pl.BlockSpec((1, SUBLANES, CONV_CH), lambda b, c: (b, 0, 0)),
                  pl.BlockSpec((rows, LANES), rowblk),
                  pl.BlockSpec((CONV_W, CONV_CH), const),
                  pl.BlockSpec((1, LANES), const),
                  pl.BlockSpec((1, LANES), const)],
        out_specs=[pl.BlockSpec((rows, GROUP), rowblk),
                   pl.BlockSpec((rows, GROUP), rowblk),
                   pl.BlockSpec((rows, GROUP), rowblk),
                   pl.BlockSpec((rows, GROUP), rowblk),
                   pl.BlockSpec((N_HEADS, rows, chunk), lambda b, c: (0, b * nblk + c, 0)),
                   pl.BlockSpec((n_sub, 1, LANES), lambda b, c: (b * nblk + c, 0, 0))],
        out_shape=[jax.ShapeDtypeStruct((t, GROUP), F32),
                   jax.ShapeDtypeStruct((t, GROUP), BF16),
                   jax.ShapeDtypeStruct((t, GROUP), BF16),
                   jax.ShapeDtypeStruct((t, GROUP), BF16),
                   jax.ShapeDtypeStruct((N_HEADS, t, chunk), BF16),
                   jax.ShapeDtypeStruct((t // chunk, 1, LANES), F32)],
        compiler_params=_params("parallel", "parallel"),
        name="gdn_prep",
    )(cin, cin, past8, ab, conv_w, lanes(a_log), lanes(dt_bias))


def _gdn_scan_kernel(u_ref, w_ref, qd_ref, kd_ref, qk_ref, gl_ref, z_ref, s0_ref, nw_ref,
                     o_ref, sf_ref, s_ref, *, bsz, chunk):
    c_idx = pl.program_id(0)

    @pl.when(c_idx == 0)
    def _():
        s_ref[...] = s0_ref[...]

    tn = (((0,), (0,)), ((), ()))
    chains = [(b, h, slice(h * HEAD, (h + 1) * HEAD)) for b in range(bsz) for h in range(N_HEADS)]
    states = [s_ref[b, h] for b, h, _ in chains]
    prods = [jnp.dot(jnp.concatenate([w_ref[b, :, col], qd_ref[b, :, col]], axis=0), s.astype(BF16),
                     preferred_element_type=F32) for (b, _, col), s in zip(chains, states)]
    v_news = [(u_ref[b, :, col] - r[:chunk]).astype(BF16) for (b, _, col), r in zip(chains, prods)]
    outs = [r[chunk:] + jnp.dot(qk_ref[h, b], v, preferred_element_type=F32)
            for (b, h, _), r, v in zip(chains, prods, v_news)]
    for (b, h, col), s, v in zip(chains, states, v_news):
        glast = jnp.exp(gl_ref[b, 0])
        s_ref[b, h] = s * glast[:, h:h + 1] + lax.dot_general(kd_ref[b, :, col], v, tn,
                                                             preferred_element_type=F32)
    for (b, h, col), o in zip(chains, outs):
        zh = z_ref[b, :, col]
        ms = jnp.mean(o * o, axis=-1, keepdims=True)
        o = o * lax.rsqrt(ms + GATED_NORM_EPS) * nw_ref[...] * (zh * jax.nn.sigmoid(zh))
        o_ref[b, :, col] = o.astype(o_ref.dtype)

    @pl.when(c_idx == pl.num_programs(0) - 1)
    def _():
        sf_ref[...] = s_ref[...]


def _gdn_scan(u, w, qd, kd, qk, gl, z, s0, norm_w, bsz, seq, chunk):
    nc = seq // chunk
    kern = functools.partial(_gdn_scan_kernel, bsz=bsz, chunk=chunk)
    tok = pl.BlockSpec((bsz, chunk, GROUP), lambda c: (0, c, 0))
    state = pl.BlockSpec((bsz, N_HEADS, HEAD, HEAD), lambda c: (0, 0, 0, 0))
    o, s_final = pl.pallas_call(
        kern,
        grid=(nc,),
        in_specs=[tok, tok, tok, tok,
                  pl.BlockSpec((N_HEADS, bsz, chunk, chunk), lambda c: (0, 0, c, 0)),
                  pl.BlockSpec((bsz, 1, 1, LANES), lambda c: (0, c, 0, 0)),
                  tok, state,
                  pl.BlockSpec((1, HEAD), lambda c: (0, 0))],
        out_specs=[tok, state],
        out_shape=[jax.ShapeDtypeStruct((bsz, seq, GROUP), BF16),
                   jax.ShapeDtypeStruct((bsz, N_HEADS, HEAD, HEAD), F32)],
        scratch_shapes=[pltpu.VMEM((bsz, N_HEADS, HEAD, HEAD), F32)],
        compiler_params=_params("arbitrary"),
        name="gdn_scan",
    )(u.reshape(bsz, seq, GROUP), w.reshape(bsz, seq, GROUP), qd.reshape(bsz, seq, GROUP),
      kd.reshape(bsz, seq, GROUP), qk.reshape(N_HEADS, bsz, seq, chunk), gl.reshape(bsz, nc, 1, LANES),
      z.reshape(bsz, seq, GROUP), s0, norm_w)
    return o.reshape(bsz * seq, GROUP), s_final


def _layernorm(x, g, b):
    mu = jnp.mean(x, axis=-1, keepdims=True)
    xc = x - mu
    var = jnp.mean(xc * xc, axis=-1, keepdims=True)
    return xc * lax.rsqrt(var + LN_EPS) * g + b


def _out_router_kernel(*refs, tm, n_sub, n_prompt):
    tiles = [refs[6 * u:6 * u + 6] for u in range(n_sub)]
    wo_ref, g_ref, b_ref, rw_ref, rb_ref, x1_ref, idx_ref, gate_ref, rank_ref, cnt_ref, carry_ref = refs[6 * n_sub:]
    step = pl.program_id(0)
    units = range(n_sub)

    @pl.when(step == 0)
    def _():
        carry_ref[...] = jnp.zeros_like(carry_ref)

    x1s = []
    for u in units:
        attp_ref, atts_ref, op_ref, os_ref, xp_ref, xs_ref = tiles[u]
        prompt = step * n_sub + u < n_prompt
        att = jnp.where(prompt, attp_ref[...], atts_ref[...])
        o = jnp.where(prompt, op_ref[...], os_ref[...])
        x = jnp.where(prompt, xp_ref[...], xs_ref[...])
        mix = (jnp.dot(att, wo_ref[:GROUP, :], preferred_element_type=F32)
               + jnp.dot(o, wo_ref[GROUP:, :], preferred_element_type=F32))
        x1 = _layernorm(DEEPNORM_ALPHA * x + mix, g_ref[...], b_ref[...])
        x1_ref[u * tm:(u + 1) * tm, :] = x1
        x1s.append(x1)

    works = []
    lane = lax.broadcasted_iota(jnp.int32, (tm, LANES), 1)
    for x1 in x1s:
        x_hi = x1.astype(BF16)
        x_lo = (x1 - x_hi.astype(F32)).astype(BF16)
        logits = (jnp.dot(x_hi, rw_ref[0], preferred_element_type=F32)
                  + jnp.dot(x_lo, rw_ref[0], preferred_element_type=F32)
                  + jnp.dot(x_hi, rw_ref[1], preferred_element_type=F32)) + rb_ref[...]
        works.append(jnp.where(lane < N_EXPERTS, logits, -jnp.inf))

    vals, idxs = [[] for _ in units], [[] for _ in units]
    for _ in range(TOP_K):
        for u in units:
            m = jnp.max(works[u], axis=-1, keepdims=True)
            am = jnp.min(jnp.where(works[u] == m, lane, LANES), axis=-1, keepdims=True)
            vals[u].append(m)
            idxs[u].append(am)
            works[u] = jnp.where(lane == am, -jnp.inf, works[u])

    ri = lax.broadcasted_iota(jnp.int32, (tm, tm), 0)
    ci = lax.broadcasted_iota(jnp.int32, (tm, tm), 1)
    before = jnp.where(ri > ci, 1.0, 0.0).astype(BF16)
    base = carry_ref[...]
    for u in units:
        exps = [jnp.exp(v - vals[u][0]) for v in vals[u]]
        denom = exps[0] + exps[1] + exps[2] + exps[3]
        chosen = jnp.zeros((tm, LANES), F32)
        gate_out = jnp.zeros((tm, LANES), F32)
        idx_out = jnp.zeros((tm, LANES), jnp.int32)
        for k in range(TOP_K):
            chosen = jnp.where(lane == idxs[u][k], 1.0, chosen)
            gate_out = jnp.where(lane == k, exps[k] / denom, gate_out)
            idx_out = jnp.where(lane == k, idxs[u][k], idx_out)
        prefix = jnp.dot(before, chosen.astype(BF16), preferred_element_type=F32) + base
        base = base + jnp.sum(chosen, axis=0, keepdims=True)
        rank_out = jnp.zeros((tm, LANES), F32)
        for k in range(TOP_K):
            r = jnp.sum(jnp.where(lane == idxs[u][k], prefix, 0.0), axis=-1, keepdims=True)
            rank_out = jnp.where(lane == k, r, rank_out)
        idx_ref[0, :, u * tm:(u + 1) * tm] = idx_out.T[:SUBLANES, :]
        gate_ref[u * tm:(u + 1) * tm, :] = gate_out
        rank_ref[0, :, u * tm:(u + 1) * tm] = rank_out.astype(jnp.int32).T[:SUBLANES, :]
    carry_ref[...] = base
    cnt_ref[...] = base.astype(jnp.int32)


def _out_router(streams, w_out_bf, ln_g, ln_b, router_w, router_b, tm, n_sub):
    (att_p, o_p, x_p), (att_s, o_s, x_s) = streams
    n_prompt = x_p.shape[0] // tm
    n_sample = x_s.shape[0] // tm
    t = x_p.shape[0] + x_s.shape[0]
    step_rows = tm * n_sub
    row = lambda i: (i, 0)
    const = lambda i: (0, 0)
    slots = pl.BlockSpec((1, SUBLANES, step_rows), lambda i: (i, 0, 0))
    tile_specs, tile_args = [], []
    for u in range(n_sub):
        prow = lambda i, u=u: (jnp.minimum(i * n_sub + u, n_prompt - 1), 0)
        srow = lambda i, u=u: (jnp.clip(i * n_sub + u - n_prompt, 0, n_sample - 1), 0)
        tile_specs += [pl.BlockSpec((tm, GROUP), prow), pl.BlockSpec((tm, GROUP), srow),
                       pl.BlockSpec((tm, GROUP), prow), pl.BlockSpec((tm, GROUP), srow),
                       pl.BlockSpec((tm, D_MODEL), prow), pl.BlockSpec((tm, D_MODEL), srow)]
        tile_args += [att_p, att_s, o_p, o_s, x_p, x_s]
    rw = jnp.pad(router_w, ((0, 0), (0, LANES - N_EXPERTS)))
    rw_hi = rw.astype(BF16)
    rw = jnp.stack([rw_hi, (rw - rw_hi.astype(F32)).astype(BF16)])
    rb = jnp.pad(router_b.reshape(1, N_EXPERTS), ((0, 0), (0, LANES - N_EXPERTS)))
    kern = functools.partial(_out_router_kernel, tm=tm, n_sub=n_sub, n_prompt=n_prompt)
    return pl.pallas_call(
        kern,
        grid=(t // step_rows,),
        in_specs=tile_specs + [pl.BlockSpec((2 * GROUP, D_MODEL), const),
                               pl.BlockSpec((1, D_MODEL), const), pl.BlockSpec((1, D_MODEL), const),
                               pl.BlockSpec((2, D_MODEL, LANES), lambda i: (0, 0, 0)),
                               pl.BlockSpec((1, LANES), const)],
        out_specs=[pl.BlockSpec((step_rows, D_MODEL), row), slots, pl.BlockSpec((step_rows, LANES), row), slots,
                   pl.BlockSpec((1, LANES), const)],
        out_shape=[jax.ShapeDtypeStruct((t, D_MODEL), F32),
                   jax.ShapeDtypeStruct((t // step_rows, SUBLANES, step_rows), jnp.int32),
                   jax.ShapeDtypeStruct((t, LANES), F32),
                   jax.ShapeDtypeStruct((t // step_rows, SUBLANES, step_rows), jnp.int32),
                   jax.ShapeDtypeStruct((1, LANES), jnp.int32)],
        scratch_shapes=[pltpu.VMEM((1, LANES), F32)],
        compiler_params=_params("arbitrary"),
        name="out_router",
    )(*tile_args, w_out_bf, ln_g.reshape(1, D_MODEL), ln_b.reshape(1, D_MODEL), rw, rb)


def _row_copy(src, src_row, dst, dst_row, sem):
    return pltpu.make_async_copy(src.at[pl.ds(src_row, 1), :], dst.at[pl.ds(dst_row, 1), :], sem)


def _dispatch_kernel(ps_ref, pe_ref, dest_ref, x_ref, xb_hbm, zero_ref, sem, zsem, *, tm, bm):
    i = pl.program_id(0)

    def zero_block(row):
        return pltpu.make_async_copy(zero_ref, xb_hbm.at[pl.ds(pl.multiple_of(row, bm), bm), :], zsem)

    def fill(e):
        return zero_block(pe_ref[e] - bm)

    @pl.when(i == 0)
    def _():
        zero_ref[...] = jnp.zeros_like(zero_ref)
        for e in range(N_EXPERTS):
            @pl.when(pe_ref[e] > ps_ref[e])
            def _():
                fill(e).start()
        first_unused = pe_ref[N_EXPERTS - 1] // bm
        n_blocks = xb_hbm.shape[0] // bm
        lax.fori_loop(first_unused, n_blocks, lambda b, c: (zero_block(b * bm).start(), c)[1], 0)
        for e in range(N_EXPERTS):
            @pl.when(pe_ref[e] > ps_ref[e])
            def _():
                fill(e).wait()
        lax.fori_loop(first_unused, n_blocks, lambda b, c: (zero_block(b * bm).wait(), c)[1], 0)

    def issue(r, carry):
        for k in range(TOP_K):
            _row_copy(x_ref, r, xb_hbm, dest_ref[0, 0, r * TOP_K + k], sem).start(priority=k % N_DMA_PRIORITIES)
        return carry

    lax.fori_loop(0, tm, issue, 0)
    pltpu.make_async_copy(xb_hbm.at[pl.ds(0, tm * TOP_K), :], xb_hbm.at[pl.ds(0, tm * TOP_K), :], sem).wait()


def _dispatch(x1, dest, pad_start, pad_end, rows, tm, bm):
    t = x1.shape[0]
    return pl.pallas_call(
        functools.partial(_dispatch_kernel, tm=tm, bm=bm),
        grid_spec=pltpu.PrefetchScalarGridSpec(
            num_scalar_prefetch=2,
            grid=(t // tm,),
            in_specs=[pl.BlockSpec((1, 1, tm * TOP_K), lambda i, ps, pe: (i, 0, 0), memory_space=pltpu.SMEM),
                      pl.BlockSpec((tm, D_MODEL), lambda i, ps, pe: (i, 0))],
            out_specs=pl.BlockSpec(memory_space=pl.ANY),
            scratch_shapes=[pltpu.VMEM((bm, D_MODEL), F32),
                            pltpu.SemaphoreType.DMA(()), pltpu.SemaphoreType.DMA(())]),
        out_shape=jax.ShapeDtypeStruct((rows, D_MODEL), F32),
        compiler_params=_params("arbitrary"),
        name="dispatch",
    )(pad_start, pad_end, dest.reshape(t // tm, 1, tm * TOP_K), x1)


def _expert_kernel(ps_ref, pe_ref, x_ref, wgu_hbm, bgu_ref, wd_hbm, bd_ref, y_ref,
                   wgu_buf, wd_buf, sem, state_ref, *, bm):
    blk = pl.program_id(0)
    row0 = blk * bm

    def fetch(e, s):
        return (pltpu.make_async_copy(wgu_hbm.at[e], wgu_buf.at[s], sem.at[s]),
                pltpu.make_async_copy(wd_hbm.at[e], wd_buf.at[s], sem.at[s]))

    def next_with_rows(e):
        return lax.while_loop(lambda n: (n < N_EXPERTS) & (pe_ref[jnp.minimum(n, N_EXPERTS - 1)] <= row0),
                              lambda n: n + 1, e)

    @pl.when(row0 < pe_ref[N_EXPERTS - 1])
    def _():
        @pl.when(blk == 0)
        def _():
            first = next_with_rows(0)
            state_ref[0] = first
            state_ref[1] = 0
            for copy in fetch(first, 0):
                copy.start()

        @pl.when((blk > 0) & (row0 >= pe_ref[state_ref[0]]))
        def _():
            state_ref[0] = next_with_rows(state_ref[0])
            state_ref[1] = 1 - state_ref[1]

        e, s = state_ref[0], state_ref[1]

        @pl.when(row0 == ps_ref[e])
        def _():
            for copy in fetch(e, s):
                copy.wait()
            nxt = lax.while_loop(lambda n: (n < N_EXPERTS) & (pe_ref[jnp.minimum(n, N_EXPERTS - 1)] <= pe_ref[e]),
                                 lambda n: n + 1, e + 1)

            @pl.when(nxt < N_EXPERTS)
            def _():
                for copy in fetch(nxt, 1 - s):
                    copy.start()

        x = x_ref[...].astype(BF16)
        h = jnp.dot(x, wgu_buf[s].astype(BF16), preferred_element_type=F32) + bgu_ref[pl.ds(e, 1), :]
        gate = jnp.minimum(h[:, :D_FF], SWIGLU_LIMIT)
        up = jnp.clip(h[:, D_FF:], -SWIGLU_LIMIT, SWIGLU_LIMIT)
        act = (up + 1.0) * (gate * jax.nn.sigmoid(SWIGLU_ALPHA * gate))
        y_ref[...] = (jnp.dot(act.astype(BF16), wd_buf[s].astype(BF16), preferred_element_type=F32)
                      + bd_ref[pl.ds(e, 1), :])

    @pl.when(row0 >= pe_ref[N_EXPERTS - 1])
    def _():
        y_ref[...] = jnp.zeros_like(y_ref)


def _experts(xb, pad_start, pad_end, w_gu, b_gu, w_down, b_down, bm):
    rows = xb.shape[0]
    n_blocks = rows // bm
    used = lambda i, ps, pe: (jnp.maximum(jnp.minimum(i, pe[N_EXPERTS - 1] // bm - 1), 0), 0)
    whole = lambda i, ps, pe: (0, 0)
    return pl.pallas_call(
        functools.partial(_expert_kernel, bm=bm),
        grid_spec=pltpu.PrefetchScalarGridSpec(
            num_scalar_prefetch=2,
            grid=(n_blocks,),
            in_specs=[pl.BlockSpec((bm, D_MODEL), used),
                      pl.BlockSpec(memory_space=pl.ANY),
                      pl.BlockSpec((N_EXPERTS, 2 * D_FF), whole),
                      pl.BlockSpec(memory_space=pl.ANY),
                      pl.BlockSpec((N_EXPERTS, D_MODEL), whole)],
            out_specs=pl.BlockSpec((bm, D_MODEL), lambda i, ps, pe: (i, 0)),
            scratch_shapes=[pltpu.VMEM((2, D_MODEL, 2 * D_FF), F32), pltpu.VMEM((2, D_FF, D_MODEL), F32),
                            pltpu.SemaphoreType.DMA((2,)), pltpu.SMEM((2,), jnp.int32)]),
        out_shape=jax.ShapeDtypeStruct((rows, D_MODEL), F32),
        compiler_params=_params("arbitrary"),
        name="experts",
    )(pad_start, pad_end, xb, w_gu, b_gu, w_down, b_down)


def _combine_ln_kernel(dcur_ref, dnext_ref, x_ref, gate_ref, g_ref, b_ref, yb_hbm, op_ref, os_ref, ybuf, sem, *,
                       tm, n_prompt):
    i = pl.program_id(0)
    slot = i % 2

    def issue(dref, s):
        def body(r, carry):
            for k in range(TOP_K):
                _row_copy(yb_hbm, dref[0, 0, r * TOP_K + k], ybuf.at[s, k], r,
                          sem.at[s]).start(priority=k % N_DMA_PRIORITIES)
            return carry
        lax.fori_loop(0, tm, body, 0)

    @pl.when(i == 0)
    def _():
        issue(dcur_ref, 0)

    for s in range(2):
        @pl.when((i + 1 < pl.num_programs(0)) & (slot != s))
        def _():
            issue(dnext_ref, s)

    pltpu.make_async_copy(ybuf.at[slot], ybuf.at[slot], sem.at[slot]).wait()
    gates = gate_ref[...]
    y = sum(gates[:, k:k + 1] * ybuf[slot, k] for k in range(TOP_K))
    out = _layernorm(DEEPNORM_ALPHA * x_ref[...] + y, g_ref[...], b_ref[...])

    @pl.when(i < n_prompt)
    def _():
        op_ref[...] = out

    @pl.when(i >= n_prompt)
    def _():
        os_ref[...] = out


def _combine_ln(x1, gates, dest, yb, ln_g, ln_b, tm, t_prompt):
    t = x1.shape[0]
    n = t // tm
    n_prompt = t_prompt // tm
    row = lambda i: (i, 0)
    const = lambda i: (0, 0)
    d2 = dest.reshape(n, 1, tm * TOP_K)
    return pl.pallas_call(
        functools.partial(_combine_ln_kernel, tm=tm, n_prompt=n_prompt),
        grid=(n,),
        in_specs=[pl.BlockSpec((1, 1, tm * TOP_K), lambda i: (i, 0, 0), memory_space=pltpu.SMEM),
                  pl.BlockSpec((1, 1, tm * TOP_K), lambda i: (jnp.minimum(i + 1, n - 1), 0, 0),
                               memory_space=pltpu.SMEM),
                  pl.BlockSpec((tm, D_MODEL), row),
                  pl.BlockSpec((tm, LANES), row),
                  pl.BlockSpec((1, D_MODEL), const), pl.BlockSpec((1, D_MODEL), const),
                  pl.BlockSpec(memory_space=pl.ANY)],
        out_specs=[pl.BlockSpec((tm, D_MODEL), lambda i: (jnp.minimum(i, n_prompt - 1), 0)),
                   pl.BlockSpec((tm, D_MODEL), lambda i: (jnp.maximum(i - n_prompt, 0), 0))],
        out_shape=[jax.ShapeDtypeStruct((t_prompt, D_MODEL), F32),
                   jax.ShapeDtypeStruct((t - t_prompt, D_MODEL), F32)],
        scratch_shapes=[pltpu.VMEM((2, TOP_K, tm, D_MODEL), F32), pltpu.SemaphoreType.DMA((2,))],
        compiler_params=_params("arbitrary"),
        name="combine_ln",
    )(d2, d2, x1, gates, ln_g.reshape(1, D_MODEL), ln_b.reshape(1, D_MODEL), yb)


MOE_BM = 512
ATTN_BLK = 512
QUERY_PART = 256
GDN_SUB = 4
ROUTE_TM = 128


def _mixers(x2d, bsz, seq, cache_k, cache_v, conv_past, s0, lam, lam_init, w_in_bf, conv_w, a_log, dt_bias,
            delta_norm_w, subln_w):
    prompt = cache_k is None
    tm = ATTN_BLK if prompt else x2d.shape[0]
    past8 = jnp.pad(conv_past, ((0, 0), (SUBLANES - (CONV_W - 1), 0), (0, 0)))
    q_bf, k_f, v_f, k_bf, v_bf, cin, z, ab, *ctail = _in_proj(x2d, w_in_bf, past8, conv_w, tm,
                                                              seq // tm if prompt else 1, prompt)
    if prompt:
        att = _attn_prompt(q_bf, k_bf, v_bf, lam, subln_w, bsz, seq, ATTN_BLK, lam_init)
        conv_new = ctail[0][:, SUBLANES - (CONV_W - 1):]
    else:
        att = _attn_sample(q_bf, k_bf, v_bf, cache_k, cache_v, 0, lam, subln_w.reshape(1, HEAD), bsz, seq, lam_init)
        conv_new = cin.reshape(bsz, seq, CONV_CH)[:, seq - (CONV_W - 1):]
    chunk = CHUNK if seq % CHUNK == 0 else seq
    n_sub = GDN_SUB if (seq // chunk) % GDN_SUB == 0 else 1
    u, w, qd, kd, qk, gl = _gdn_prep(cin, ab, past8, conv_w, a_log, dt_bias, bsz, seq, chunk, n_sub, prompt)
    o, s_new = _gdn_scan(u, w, qd, kd, qk, gl, z, s0, delta_norm_w.reshape(1, HEAD), bsz, seq, chunk)
    return att, o, k_f, v_f, conv_new, s_new


def _moe(x1, idx, gates, rank, counts, w_gu, b_gu, w_down, b_down, ln_g, ln_b, bm, tm, t_prompt):
    t = x1.shape[0]
    n = t * TOP_K
    counts = counts[0, :N_EXPERTS]
    padded = (counts + bm - 1) // bm * bm
    pad_end = jnp.cumsum(padded).astype(jnp.int32)
    pad_start = (pad_end - padded).astype(jnp.int32)
    idx = idx[:, :TOP_K, :]
    dest = rank[:, :TOP_K, :] + sum(jnp.where(idx == e, pad_start[e], 0) for e in range(N_EXPERTS))
    dest = dest.transpose(0, 2, 1).reshape(t, TOP_K)
    n_blocks = -(-n // bm) + N_EXPERTS
    xb = _dispatch(x1, dest, pad_start, pad_end, n_blocks * bm, tm, bm)
    yb = _experts(xb, pad_start, pad_end, w_gu, b_gu, w_down, b_down, bm)
    return _combine_ln(x1, gates, dest, yb, ln_g, ln_b, ROUTE_TM, t_prompt)


def kernel(x_prompt, x_sample, cache_k, cache_v, state_conv, state_delta, w_in, conv_w, a_log, dt_bias,
           delta_norm_w, lambda_q1, lambda_k1, lambda_q2, lambda_k2, subln_w, w_out, ln1_g, ln1_b,
           router_w, router_b, w_gu, b_gu, w_down, b_down, ln2_g, ln2_b):
    bp, lp, _ = x_prompt.shape
    bs, ls, _ = x_sample.shape
    l = 0
    lam_init = 0.8 - 0.6 * math.exp(-0.3 * l)
    lam = (jnp.exp(jnp.sum(lambda_q1[l] * lambda_k1[l])) - jnp.exp(jnp.sum(lambda_q2[l] * lambda_k2[l]))
           + lam_init).reshape(1).astype(F32)
    w_in_bf = jnp.pad(w_in[l], ((0, 0), (0, IN_COLS_PAD - IN_COLS))).astype(BF16)
    shared = (lam, lam_init, w_in_bf, conv_w[l], a_log[l], dt_bias[l], delta_norm_w[l], subln_w[l])

    xp = x_prompt.reshape(bp * lp, D_MODEL)
    xs = x_sample.reshape(bs * ls, D_MODEL)
    zero_conv = jnp.zeros((bp, CONV_W - 1, CONV_CH), F32)
    zero_s = jnp.zeros((bp, N_HEADS, HEAD, HEAD), F32)
    att_p, o_p, k_p, v_p, cin_p, s_p = _mixers(xp, bp, lp, None, None, zero_conv, zero_s, *shared)
    att_s, o_s, k_s, v_s, cin_s, s_s = _mixers(xs, bs, ls, cache_k, cache_v, state_conv[l],
                                               state_delta[l], *shared)

    n_sub = next(c for c in (4, 3, 2, 1) if ((bp * lp + bs * ls) // ROUTE_TM) % c == 0)
    x1, idx, gates, rank, counts = _out_router(((att_p, o_p, xp), (att_s, o_s, xs)), w_out[l].astype(BF16),
                                               ln1_g[l], ln1_b[l], router_w[l], router_b[l], ROUTE_TM, n_sub)
    tp = bp * lp
    tm = n_sub * ROUTE_TM
    y_p, y_s = _moe(x1, idx, gates, rank, counts, w_gu[l], b_gu[l], w_down[l], b_down[l], ln2_g[l], ln2_b[l],
                    MOE_BM, tm, tp)
    return (y_p.reshape(bp, lp, D_MODEL), y_s.reshape(bs, ls, D_MODEL),
            k_p.reshape(1, bp, lp, N_HEADS, HEAD), v_p.reshape(1, bp, lp, N_HEADS, HEAD),
            cin_p[None], s_p[None].astype(state_delta.dtype),
            k_s.reshape(1, bs, ls, N_HEADS, HEAD), v_s.reshape(1, bs, ls, N_HEADS, HEAD),
            cin_s[None], s_s[None].astype(state_delta.dtype))
```

```python
import functools
import math

import jax
import jax.numpy as jnp
from jax import lax
from jax.experimental import pallas as pl
from jax.experimental.pallas import tpu as pltpu

F32 = jnp.float32
BF16 = jnp.bfloat16

D_MODEL = 1024
HEAD = 128
N_HEADS = 4
DQK = HEAD // 2
GROUP = N_HEADS * HEAD
CONV_W = 4
CONV_CH = 3 * GROUP
CHUNK = 64
ALIBI_MAX = 8.0
N_EXPERTS = 32
TOP_K = 4
D_FF = D_MODEL
SWIGLU_LIMIT = 7.0
SWIGLU_ALPHA = 1.702
DEPTH = 1
DEEPNORM_ALPHA = (2 * DEPTH) ** 0.25
LN_EPS = 1e-5
SUBLN_EPS = 1e-5
GATED_NORM_EPS = 1e-6
L2_EPS = 1e-6

LANES = 128
SUBLANES = 8
BF16_EXACT_INT = 256
BF16_ROWS = 16
LOG2E = 1.4426950408889634
N_POS = 6
ONES_ROWS = BF16_ROWS
VMEM_LIMIT = 56 * 1024 * 1024

COL_Q, COL_K, COL_V, COL_CONV = 0, GROUP, 2 * GROUP, 3 * GROUP
COL_Z = COL_CONV + CONV_CH
COL_AB = COL_Z + GROUP
IN_COLS = COL_AB + 2 * N_HEADS
IN_COLS_PAD = COL_AB + LANES


def _params(*sem):
    return pltpu.CompilerParams(dimension_semantics=sem, vmem_limit_bytes=VMEM_LIMIT)


def _conv_activation(prev, raw, cw_ref):
    rows = raw.shape[0]
    xin = jnp.concatenate([prev, raw], axis=0)
    conv = sum(xin[SUBLANES - (CONV_W - 1) + j: SUBLANES - (CONV_W - 1) + j + rows] * cw_ref[j:j + 1, :]
               for j in range(CONV_W))
    conv = conv * jax.nn.sigmoid(conv)
    parts = []
    for h in range(2 * N_HEADS):
        x = conv[:, h * HEAD:(h + 1) * HEAD]
        scale = HEAD ** -0.5 if h < N_HEADS else 1.0
        parts.append(x * (lax.rsqrt(jnp.sum(x * x, axis=-1, keepdims=True) + L2_EPS) * scale))
    return jnp.concatenate(parts + [conv[:, 2 * GROUP:]], axis=1)


def _in_proj_kernel(x_ref, w_ref, past_ref, cw_ref, q_ref, kf_ref, vf_ref, kb_ref, vb_ref, c_ref, z_ref, ab_ref,
                    *conv_refs, tm, tiles_per_stream, prompt):
    xb = x_ref[...].astype(BF16)

    def section(lo, hi):
        return jnp.dot(xb, w_ref[:, lo:hi], preferred_element_type=F32)

    q = section(COL_Q, COL_K) * (DQK ** -0.5 * (LOG2E if prompt else 1.0))
    k = section(COL_K, COL_V)
    kf_ref[0] = k.reshape(tm, N_HEADS, HEAD)
    kb_ref[...] = k.astype(BF16)
    v = section(COL_V, COL_CONV)
    vf_ref[0] = v.reshape(tm, N_HEADS, HEAD)
    raw = section(COL_CONV, COL_Z)
    if prompt:
        ctail_ref, tail_ref = conv_refs
        q_ref[0] = q.T.astype(BF16)
        vb_ref[0] = v.T.astype(BF16)
        prev = jnp.where(pl.program_id(0) % tiles_per_stream == 0, past_ref[0], tail_ref[...])
        c_ref[...] = _conv_activation(prev, raw, cw_ref)
        tail_ref[...] = raw[tm - SUBLANES:, :]
        ctail_ref[0] = raw[tm - SUBLANES:, :]
    else:
        q_ref[...] = q.astype(BF16)
        vb_ref[...] = v.astype(BF16)
        c_ref[...] = raw
    z_ref[...] = section(COL_Z, COL_AB)
    ab_ref[...] = section(COL_AB, IN_COLS_PAD)


def _in_proj(x2d, w_bf, past8, conv_w, tm, tiles_per_stream, prompt):
    t = x2d.shape[0]
    row = lambda i: (i, 0)
    stream = lambda i: (i // tiles_per_stream, 0, 0)
    widths = (GROUP, GROUP, GROUP, GROUP, GROUP, CONV_CH, GROUP, LANES)
    dtypes = (BF16, F32, F32, BF16, BF16, F32, F32, F32)
    out_specs = [pl.BlockSpec((tm, w), row) for w in widths]
    out_shape = [jax.ShapeDtypeStruct((t, w), d) for w, d in zip(widths, dtypes)]
    for slot in (1, 2):
        out_specs[slot] = pl.BlockSpec((1, tm, N_HEADS, HEAD), lambda i: (i, 0, 0, 0))
        out_shape[slot] = jax.ShapeDtypeStruct((t // tm, tm, N_HEADS, HEAD), F32)
    if prompt:
        for slot in (0, 4):
            out_specs[slot] = pl.BlockSpec((1, GROUP, tm), lambda i: (i, 0, 0))
            out_shape[slot] = jax.ShapeDtypeStruct((t // tm, GROUP, tm), BF16)
        out_specs.append(pl.BlockSpec((1, SUBLANES, CONV_CH), stream))
        out_shape.append(jax.ShapeDtypeStruct((past8.shape[0], SUBLANES, CONV_CH), F32))
    return pl.pallas_call(
        functools.partial(_in_proj_kernel, tm=tm, tiles_per_stream=tiles_per_stream, prompt=prompt),
        grid=(t // tm,),
        in_specs=[pl.BlockSpec((tm, D_MODEL), row),
                  pl.BlockSpec((D_MODEL, IN_COLS_PAD), lambda i: (0, 0)),
                  pl.BlockSpec((1, SUBLANES, CONV_CH), stream),
                  pl.BlockSpec((CONV_W, CONV_CH), lambda i: (0, 0))],
        out_specs=out_specs,
        out_shape=out_shape,
        scratch_shapes=[pltpu.VMEM((SUBLANES, CONV_CH), F32)] if prompt else [],
        compiler_params=_params("arbitrary"),
        name="in_proj",
    )(x2d, w_bf, past8, conv_w)


def _alibi_slopes():
    return [2.0 ** (-ALIBI_MAX * (h + 1) / N_HEADS) for h in range(N_HEADS)]


def _stack_halves(q):
    lane = lax.broadcasted_iota(jnp.int32, q.shape, 1)
    zero = jnp.zeros_like(q)
    return jnp.concatenate([jnp.where(lane < DQK, q, zero), jnp.where(lane < DQK, zero, q)], axis=0)


def _diff_norm(acc, l, lam, w, lam_init, rows):
    o = acc[:rows] / l[:rows] - lam * (acc[rows:] / l[rows:])
    ms = jnp.mean(o * o, axis=-1, keepdims=True)
    return o * lax.rsqrt(ms + SUBLN_EPS) * w * (1.0 - lam_init)


def _head_slope(h):
    s = _alibi_slopes()
    return jnp.where(h == 0, s[0], jnp.where(h == 1, s[1], jnp.where(h == 2, s[2], s[3]))).astype(F32)


def _attn_prompt_kernel(lam_ref, qt_ref, k_ref, vt_ref, w_ref, o_ref,
                        diag_ref, kaug_ref, vaug_ref, qz_ref, s0_ref, s1_ref, p0_ref, p1_ref, mx0_ref, mx1_ref,
                        m_ref, acc_ref, *, blk, lam_init):
    i = pl.program_id(2)
    slope = _head_slope(pl.program_id(1)) * LOG2E
    rows = 2 * blk
    n_kv = k_ref.shape[0] // blk

    def pieces(x):
        lo = x % BF16_EXACT_INT
        return _split3(slope * lo.astype(F32)) + _split3(slope * (x - lo).astype(F32))

    @pl.when(i == 0)
    def _():
        lane = lax.broadcasted_iota(jnp.int32, (blk, HEAD), 1)
        extra = jnp.where(lane < N_POS, 1.0, 0.0).astype(BF16)
        for n, piece in enumerate(pieces(lax.broadcasted_iota(jnp.int32, (blk, 1), 0))):
            extra = jnp.where(lane == N_POS + n, piece, extra)
        ones = jnp.ones((ONES_ROWS, blk), BF16)
        c = lax.broadcasted_iota(jnp.int32, (blk, rows), 0)
        a = lax.broadcasted_iota(jnp.int32, (blk, rows), 1) % blk
        diag_ref[...] = jnp.where(c // CHUNK <= a // CHUNK, -slope * jnp.abs(a - c).astype(F32), -jnp.inf)
        r = lax.broadcasted_iota(jnp.int32, (HEAD, rows), 0)
        qx = jnp.where((r >= N_POS) & (r < 2 * N_POS), 1.0, 0.0).astype(BF16)
        for n, piece in enumerate(pieces(-(lax.broadcasted_iota(jnp.int32, (1, rows), 1) % blk))):
            qx = jnp.where(r == n, piece, qx)
        qz_ref[HEAD:, :] = qx

        def fill(j, carry):
            j0 = pl.multiple_of(j * blk, blk)
            kaug_ref[pl.ds(j0, blk), :HEAD] = k_ref[pl.ds(j0, blk), :]
            kaug_ref[pl.ds(j0, blk), HEAD:] = extra
            vaug_ref[j, :HEAD, :] = vt_ref[j]
            vaug_ref[j, HEAD:, :] = ones
            return carry

        lax.fori_loop(0, n_kv, fill, 0)

    qt = qt_ref[0]
    d = lax.broadcasted_iota(jnp.int32, qt.shape, 0)
    zero = jnp.zeros_like(qt)
    qz_ref[:HEAD, :] = jnp.concatenate([jnp.where(d < DQK, qt, zero), jnp.where(d < DQK, zero, qt)], axis=1)

    def tile_rows(j):
        return pl.ds(pl.multiple_of(j * blk, blk), blk)

    acc_ref[...] = jnp.zeros_like(acc_ref)
    for part in range(rows // QUERY_PART):
        cols = slice(part * QUERY_PART, (part + 1) * QUERY_PART)
        s = jnp.dot(k_ref[tile_rows(i), :], qz_ref[:HEAD, cols], preferred_element_type=F32) + diag_ref[:, cols]
        m = jnp.max(s, axis=0, keepdims=True)
        p1_ref[:, cols] = jnp.exp2(s - m).astype(BF16)
        m_ref[:, cols] = m
        s_first = jnp.dot(kaug_ref[tile_rows(0), :], qz_ref[:, cols], preferred_element_type=F32)
        s0_ref[:, cols] = s_first
        mx0_ref[:, cols] = jnp.max(s_first, axis=0, keepdims=True)

    def step(j, s_cur, s_nxt, p_cur, p_nxt, mx_cur, mx_nxt):
        shift = -slope * ((i - j) * blk).astype(F32)
        k_next = kaug_ref[tile_rows(jnp.minimum(j + 1, i - 1)), :]
        v_prev = vaug_ref[jnp.where(j == 0, i, j - 1)]
        for part in range(rows // QUERY_PART):
            cols = slice(part * QUERY_PART, (part + 1) * QUERY_PART)
            s_next = jnp.dot(k_next, qz_ref[:, cols], preferred_element_type=F32)
            s_nxt[:, cols] = s_next
            mx_nxt[:, cols] = jnp.max(s_next, axis=0, keepdims=True)
            pv = jnp.dot(v_prev, p_nxt[:, cols], preferred_element_type=F32)
            m_old = m_ref[:, cols]
            m_new = jnp.maximum(m_old, mx_cur[:, cols] + shift)
            p_cur[:, cols] = jnp.exp2(s_cur[:, cols] - (m_new - shift)).astype(BF16)
            acc_ref[:, cols] = (acc_ref[:, cols] + pv) * jnp.exp2(m_old - m_new)
            m_ref[:, cols] = m_new

    def body(jj, carry):
        step(2 * jj, s0_ref, s1_ref, p0_ref, p1_ref, mx0_ref, mx1_ref)

        @pl.when(2 * jj + 1 < i)
        def _():
            step(2 * jj + 1, s1_ref, s0_ref, p1_ref, p0_ref, mx1_ref, mx0_ref)

        return carry

    lax.fori_loop(0, (i + 1) // 2, body, 0)
    p_last = jnp.where(i % 2 == 1, p0_ref[...], p1_ref[...])
    acc = acc_ref[...] + jnp.dot(vaug_ref[jnp.where(i > 0, i - 1, i)], p_last, preferred_element_type=F32)
    l = acc[HEAD:HEAD + 1, :]
    num = acc[:HEAD, :]
    ot = num[:, :blk] / l[:, :blk] - lam_ref[0] * (num[:, blk:] / l[:, blk:])
    ms = jnp.mean(ot * ot, axis=0, keepdims=True)
    ot = ot * lax.rsqrt(ms + SUBLN_EPS) * w_ref[...] * (1.0 - lam_init)
    o_ref[...] = ot.T.astype(o_ref.dtype)


def _attn_prompt(qt_bf, k_bf, vt_bf, lam, subln_w, bsz, seq, blk, lam_init):
    nq = seq // blk
    kern = functools.partial(_attn_prompt_kernel, blk=blk, lam_init=lam_init)
    return pl.pallas_call(
        kern,
        grid=(bsz, N_HEADS, nq),
        in_specs=[pl.BlockSpec(memory_space=pltpu.SMEM),
                  pl.BlockSpec((1, HEAD, blk), lambda b, h, i: (b * nq + i, h, 0)),
                  pl.BlockSpec((seq, HEAD), lambda b, h, i: (b, h)),
                  pl.BlockSpec((nq, HEAD, blk), lambda b, h, i: (b, h, 0)),
                  pl.BlockSpec((HEAD, 1), lambda b, h, i: (0, 0))],
        out_specs=pl.BlockSpec((blk, HEAD), lambda b, h, i: (b * nq + i, h)),
        scratch_shapes=[pltpu.VMEM((blk, 2 * blk), F32),
                        pltpu.VMEM((seq, 2 * HEAD), BF16),
                        pltpu.VMEM((nq, HEAD + ONES_ROWS, blk), BF16),
                        pltpu.VMEM((2 * HEAD, 2 * blk), BF16),
                        pltpu.VMEM((blk, 2 * blk), F32), pltpu.VMEM((blk, 2 * blk), F32),
                        pltpu.VMEM((blk, 2 * blk), BF16), pltpu.VMEM((blk, 2 * blk), BF16),
                        pltpu.VMEM((1, 2 * blk), F32), pltpu.VMEM((1, 2 * blk), F32),
                        pltpu.VMEM((1, 2 * blk), F32),
                        pltpu.VMEM((HEAD + ONES_ROWS, 2 * blk), F32)],
        out_shape=jax.ShapeDtypeStruct((bsz * seq, GROUP), BF16),
        compiler_params=_params("parallel", "parallel", "arbitrary"),
        name="attn_prompt",
    )(lam, qt_bf, k_bf, vt_bf, subln_w.reshape(HEAD, 1))


def _attn_sample_kernel(lam_ref, q_ref, kn_ref, vn_ref, kc_ref, vc_ref, w_ref, o_ref, *, seq, past, lam_init):
    nt = (((1,), (1,)), ((), ()))
    per_head = 2 * seq
    qz = jnp.concatenate([_stack_halves(q_ref[:, h * HEAD:(h + 1) * HEAD]) for h in range(N_HEADS)], axis=0)
    row = lax.broadcasted_iota(jnp.int32, (N_HEADS * per_head, 1), 0)
    q_head = row // per_head
    q_pos = past + row % seq
    slopes = _alibi_slopes()
    slope = jnp.where(q_head == 0, slopes[0], jnp.where(q_head == 1, slopes[1],
                                                        jnp.where(q_head == 2, slopes[2], slopes[3]))).astype(F32)

    def scores(keys, first_pos):
        n = lax.broadcasted_iota(jnp.int32, (1, keys.shape[0]), 1)
        s = lax.dot_general(qz, keys, nt, preferred_element_type=F32)
        bias = -slope * jnp.abs(q_pos - (first_pos + n // N_HEADS)).astype(F32)
        return jnp.where(n % N_HEADS == q_head, s + bias, -jnp.inf)

    s_c = scores(kc_ref[0, 0].reshape(past * N_HEADS, HEAD).astype(BF16), 0)
    s_n = scores(kn_ref[...].reshape(seq * N_HEADS, HEAD), past)
    m = jnp.maximum(jnp.max(s_c, axis=-1, keepdims=True), jnp.max(s_n, axis=-1, keepdims=True))
    p_c = jnp.exp(s_c - m)
    p_n = jnp.exp(s_n - m)
    l = jnp.sum(p_c, axis=-1, keepdims=True) + jnp.sum(p_n, axis=-1, keepdims=True)
    acc = (jnp.dot(p_c.astype(BF16), vc_ref[0, 0].reshape(past * N_HEADS, HEAD).astype(BF16),
                   preferred_element_type=F32)
           + jnp.dot(p_n.astype(BF16), vn_ref[...].reshape(seq * N_HEADS, HEAD), preferred_element_type=F32))
    for h in range(N_HEADS):
        rows = slice(h * per_head, (h + 1) * per_head)
        o_ref[:, h * HEAD:(h + 1) * HEAD] = _diff_norm(acc[rows], l[rows], lam_ref[0], w_ref[...], lam_init,
                                                       seq).astype(o_ref.dtype)


def _attn_sample(q_bf, k_bf, v_bf, cache_k, cache_v, layer, lam, subln_w, bsz, seq, lam_init):
    past = cache_k.shape[2]
    kern = functools.partial(_attn_sample_kernel, seq=seq, past=past, lam_init=lam_init)
    new = pl.BlockSpec((seq, GROUP), lambda b: (b, 0))
    cache = pl.BlockSpec((1, 1, past, N_HEADS, HEAD), lambda b: (layer, b, 0, 0, 0))
    return pl.pallas_call(
        kern,
        grid=(bsz,),
        in_specs=[pl.BlockSpec(memory_space=pltpu.SMEM), new, new, new, cache, cache,
                  pl.BlockSpec((1, HEAD), lambda b: (0, 0))],
        out_specs=new,
        out_shape=jax.ShapeDtypeStruct((bsz * seq, GROUP), BF16),
        compiler_params=_params("parallel"),
        name="attn_sample",
    )(lam, q_bf, k_bf, v_bf, cache_k, cache_v, subln_w)


def _split3(x):
    hi = x.astype(BF16)
    r1 = x - hi.astype(F32)
    mid = r1.astype(BF16)
    lo = (r1 - mid.astype(F32)).astype(BF16)
    return hi, mid, lo


def _gdn_prep_kernel(cin_ref, halo_ref, past_ref, ab_ref, cw_ref, alog_ref, dtb_ref,
                     u_ref, w_ref, qd_ref, kd_ref, qk_ref, gl_ref, *, chunk, n_sub, activated):
    c_idx = pl.program_id(1)
    rows = chunk * n_sub
    if activated:
        conv = cin_ref[...]
    else:
        prev = jnp.where(c_idx == 0, past_ref[0], halo_ref[...])
        conv = _conv_activation(prev, cin_ref[...], cw_ref)

    ab = ab_ref[...]
    lane = lax.broadcasted_iota(jnp.int32, ab.shape, 1)
    pre = ab + dtb_ref[...]
    softplus = jnp.maximum(pre, 0.0) + jnp.log(1.0 + jnp.exp(-jnp.abs(pre)))
    g = jnp.where(lane < N_HEADS, -jnp.exp(alog_ref[...]) * softplus, 0.0)
    beta_all = jax.nn.sigmoid(ab)

    ri = lax.broadcasted_iota(jnp.int32, (rows, rows), 0)
    ci = lax.broadcasted_iota(jnp.int32, (rows, rows), 1)
    same = (ri // chunk) == (ci // chunk)
    incl = same & (ri >= ci)
    strict = same & (ri > ci)
    eye = jnp.where(ri == ci, 1.0, 0.0).astype(F32)
    nt = (((1,), (1,)), ((), ()))
    g_parts = _split3(g)
    ones_incl = jnp.where(incl, 1.0, 0.0).astype(BF16)
    ones_same = jnp.where(same, 1.0, 0.0).astype(BF16)
    gc = sum(jnp.dot(ones_incl, part, preferred_element_type=F32) for part in g_parts)
    g_end = sum(jnp.dot(ones_same, part, preferred_element_type=F32) for part in g_parts)
    gct = gc.T
    for sc in range(n_sub):
        gl_ref[sc] = g_end[sc * chunk:sc * chunk + 1, :]

    t_mats, p_mats, rhs = [], [], []
    for h in range(N_HEADS):
        col = slice(h * HEAD, (h + 1) * HEAD)
        qh = conv[:, h * HEAD:(h + 1) * HEAD]
        kh = conv[:, GROUP + h * HEAD:GROUP + (h + 1) * HEAD]
        vh = conv[:, 2 * GROUP + h * HEAD:2 * GROUP + (h + 1) * HEAD]
        beta = beta_all[:, N_HEADS + h:N_HEADS + h + 1]
        gcol = gc[:, h:h + 1]
        grow = gct[h:h + 1, :]
        gamma = jnp.exp(jnp.where(incl, gcol - grow, -jnp.inf))
        egc = jnp.exp(gcol)
        kb = kh * beta
        khb = kh.astype(BF16)
        a = jnp.where(strict, lax.dot_general(kb.astype(BF16), khb, nt, preferred_element_type=F32) * gamma, 0.0)
        qk = (lax.dot_general(qh.astype(BF16), khb, nt, preferred_element_type=F32) * gamma).astype(BF16)
        for sc in range(n_sub):
            blk = slice(sc * chunk, (sc + 1) * chunk)
            qk_ref[h, blk, :] = qk[blk, blk]
        qd_ref[:, col] = (qh * egc).astype(BF16)
        kd_ref[:, col] = (kh * jnp.exp(g_end[:, h:h + 1] - gcol)).astype(BF16)
        t_mats.append(eye - a)
        p_mats.append(a)
        rhs.append(((vh * beta).astype(BF16), (kb * egc).astype(BF16)))

    for _ in range(int(math.log2(chunk)) - 1):
        for h in range(N_HEADS):
            pb = p_mats[h].astype(BF16)
            p_mats[h] = jnp.dot(pb, pb, preferred_element_type=F32)
        for h in range(N_HEADS):
            t_mats[h] = t_mats[h] + jnp.dot(t_mats[h].astype(BF16), p_mats[h].astype(BF16),
                                            preferred_element_type=F32)

    for h in range(N_HEADS):
        col = slice(h * HEAD, (h + 1) * HEAD)
        t_inv = t_mats[h].astype(BF16)
        u_ref[:, col] = jnp.dot(t_inv, rhs[h][0], preferred_element_type=F32)
        w_ref[:, col] = jnp.dot(t_inv, rhs[h][1], preferred_element_type=F32).astype(BF16)


def _gdn_prep(cin, ab, past8, conv_w, a_log, dt_bias, bsz, seq, chunk, n_sub, activated):
    rows = chunk * n_sub
    nblk = seq // rows
    t = bsz * seq
    lanes = lambda v: jnp.pad(v.reshape(1, N_HEADS).astype(F32), ((0, 0), (0, LANES - N_HEADS)))
    kern = functools.partial(_gdn_prep_kernel, chunk=chunk, n_sub=n_sub, activated=activated)
    rowblk = lambda b, c: (b * nblk + c, 0)
    halo = lambda b, c: (jnp.maximum((b * nblk + c) * (rows // SUBLANES) - 1, 0), 0)
    const = lambda b, c: (0, 0)
    return pl.pallas_call(
        kern,
        grid=(bsz, nblk),
        in_specs=[pl.BlockSpec((rows, CONV_CH), rowblk),
                  pl.BlockSpec((SUBLANES, CONV_CH), halo),
                  pl.BlockSpec((1, SUBLANES, CONV_CH), lambda b, c: (b, 0, 0)),
                  pl.BlockSpec((rows, LANES), rowblk),
                  pl.BlockSpec((CONV_W, CONV_CH), const),
                  pl.BlockSpec((1, LANES), const),
                  pl.BlockSpec((1, LANES), const)],
        out_specs=[pl.BlockSpec((rows, GROUP), rowblk),
                   pl.BlockSpec((rows, GROUP), rowblk),
                   pl.BlockSpec((rows, GROUP), rowblk),
                   pl.BlockSpec((rows, GROUP), rowblk),
                   pl.BlockSpec((N_HEADS, rows, chunk), lambda b, c: (0, b * nblk + c, 0)),
                   pl.BlockSpec((n_sub, 1, LANES), lambda b, c: (b * nblk + c, 0, 0))],
        out_shape=[jax.ShapeDtypeStruct((t, GROUP), F32),
                   jax.ShapeDtypeStruct((t, GROUP), BF16),
                   jax.ShapeDtypeStruct((t, GROUP), BF16),
                   jax.ShapeDtypeStruct((t, GROUP), BF16),
                   jax.ShapeDtypeStruct((N_HEADS, t, chunk), BF16),
                   jax.ShapeDtypeStruct((t // chunk, 1, LANES), F32)],
        compiler_params=_params("parallel", "parallel"),
        name="gdn_prep",
    )(cin, cin, past8, ab, conv_w, lanes(a_log), lanes(dt_bias))


def _gdn_scan_kernel(u_ref, w_ref, qd_ref, kd_ref, qk_ref, gl_ref, z_ref, s0_ref, nw_ref,
                     o_ref, sf_ref, s_ref, *, bsz, chunk):
    c_idx = pl.program_id(0)

    @pl.when(c_idx == 0)
    def _():
        s_ref[...] = s0_ref[...]

    tn = (((0,), (0,)), ((), ()))
    chains = [(b, h, slice(h * HEAD, (h + 1) * HEAD)) for b in range(bsz) for h in range(N_HEADS)]
    states = [s_ref[b, h] for b, h, _ in chains]
    prods = [jnp.dot(jnp.concatenate([w_ref[b, :, col], qd_ref[b, :, col]], axis=0), s.astype(BF16),
                     preferred_element_type=F32) for (b, _, col), s in zip(chains, states)]
    v_news = [(u_ref[b, :, col] - r[:chunk]).astype(BF16) for (b, _, col), r in zip(chains, prods)]
    outs = [r[chunk:] + jnp.dot(qk_ref[h, b], v, preferred_element_type=F32)
            for (b, h, _), r, v in zip(chains, prods, v_news)]
    for (b, h, col), s, v in zip(chains, states, v_news):
        glast = jnp.exp(gl_ref[b, 0])
        s_ref[b, h] = s * glast[:, h:h + 1] + lax.dot_general(kd_ref[b, :, col], v, tn,
                                                             preferred_element_type=F32)
    for (b, h, col), o in zip(chains, outs):
        zh = z_ref[b, :, col]
        ms = jnp.mean(o * o, axis=-1, keepdims=True)
        o = o * lax.rsqrt(ms + GATED_NORM_EPS) * nw_ref[...] * (zh * jax.nn.sigmoid(zh))
        o_ref[b, :, col] = o.astype(o_ref.dtype)

    @pl.when(c_idx == pl.num_programs(0) - 1)
    def _():
        sf_ref[...] = s_ref[...]


def _gdn_scan(u, w, qd, kd, qk, gl, z, s0, norm_w, bsz, seq, chunk):
    nc = seq // chunk
    kern = functools.partial(_gdn_scan_kernel, bsz=bsz, chunk=chunk)
    tok = pl.BlockSpec((bsz, chunk, GROUP), lambda c: (0, c, 0))
    state = pl.BlockSpec((bsz, N_HEADS, HEAD, HEAD), lambda c: (0, 0, 0, 0))
    o, s_final = pl.pallas_call(
        kern,
        grid=(nc,),
        in_specs=[tok, tok, tok, tok,
                  pl.BlockSpec((N_HEADS, bsz, chunk, chunk), lambda c: (0, 0, c, 0)),
                  pl.BlockSpec((bsz, 1, 1, LANES), lambda c: (0, c, 0, 0)),
                  tok, state,
                  pl.BlockSpec((1, HEAD), lambda c: (0, 0))],
        out_specs=[tok, state],
        out_shape=[jax.ShapeDtypeStruct((bsz, seq, GROUP), BF16),
                   jax.ShapeDtypeStruct((bsz, N_HEADS, HEAD, HEAD), F32)],
        scratch_shapes=[pltpu.VMEM((bsz, N_HEADS, HEAD, HEAD), F32)],
        compiler_params=_params("arbitrary"),
        name="gdn_scan",
    )(u.reshape(bsz, seq, GROUP), w.reshape(bsz, seq, GROUP), qd.reshape(bsz, seq, GROUP),
      kd.reshape(bsz, seq, GROUP), qk.reshape(N_HEADS, bsz, seq, chunk), gl.reshape(bsz, nc, 1, LANES),
      z.reshape(bsz, seq, GROUP), s0, norm_w)
    return o.reshape(bsz * seq, GROUP), s_final


def _layernorm(x, g, b):
    mu = jnp.mean(x, axis=-1, keepdims=True)
    xc = x - mu
    var = jnp.mean(xc * xc, axis=-1, keepdims=True)
    return xc * lax.rsqrt(var + LN_EPS) * g + b


def _out_router_kernel(*refs, tm, n_sub, n_prompt):
    tiles = [refs[6 * u:6 * u + 6] for u in range(n_sub)]
    wo_ref, g_ref, b_ref, rw_ref, rb_ref, x1_ref, idx_ref, gate_ref, rank_ref, cnt_ref, carry_ref = refs[6 * n_sub:]
    step = pl.program_id(0)
    units = range(n_sub)

    @pl.when(step == 0)
    def _():
        carry_ref[...] = jnp.zeros_like(carry_ref)

    x1s = []
    for u in units:
        attp_ref, atts_ref, op_ref, os_ref, xp_ref, xs_ref = tiles[u]
        prompt = step * n_sub + u < n_prompt
        att = jnp.where(prompt, attp_ref[...], atts_ref[...])
        o = jnp.where(prompt, op_ref[...], os_ref[...])
        x = jnp.where(prompt, xp_ref[...], xs_ref[...])
        mix = (jnp.dot(att, wo_ref[:GROUP, :], preferred_element_type=F32)
               + jnp.dot(o, wo_ref[GROUP:, :], preferred_element_type=F32))
        x1 = _layernorm(DEEPNORM_ALPHA * x + mix, g_ref[...], b_ref[...])
        x1_ref[u * tm:(u + 1) * tm, :] = x1
        x1s.append(x1)

    works = []
    lane = lax.broadcasted_iota(jnp.int32, (tm, LANES), 1)
    for x1 in x1s:
        x_hi = x1.astype(BF16)
        x_lo = (x1 - x_hi.astype(F32)).astype(BF16)
        logits = (jnp.dot(x_hi, rw_ref[0], preferred_element_type=F32)
                  + jnp.dot(x_lo, rw_ref[0], preferred_element_type=F32)
                  + jnp.dot(x_hi, rw_ref[1], preferred_element_type=F32)) + rb_ref[...]
        works.append(jnp.where(lane < N_EXPERTS, logits, -jnp.inf))

    vals, idxs = [[] for _ in units], [[] for _ in units]
    for _ in range(TOP_K):
        for u in units:
            m = jnp.max(works[u], axis=-1, keepdims=True)
            am = jnp.min(jnp.where(works[u] == m, lane, LANES), axis=-1, keepdims=True)
            vals[u].append(m)
            idxs[u].append(am)
            works[u] = jnp.where(lane == am, -jnp.inf, works[u])

    ri = lax.broadcasted_iota(jnp.int32, (tm, tm), 0)
    ci = lax.broadcasted_iota(jnp.int32, (tm, tm), 1)
    before = jnp.where(ri > ci, 1.0, 0.0).astype(BF16)
    base = carry_ref[...]
    for u in units:
        exps = [jnp.exp(v - vals[u][0]) for v in vals[u]]
        denom = exps[0] + exps[1] + exps[2] + exps[3]
        chosen = jnp.zeros((tm, LANES), F32)
        gate_out = jnp.zeros((tm, LANES), F32)
        idx_out = jnp.zeros((tm, LANES), jnp.int32)
        for k in range(TOP_K):
            chosen = jnp.where(lane == idxs[u][k], 1.0, chosen)
            gate_out = jnp.where(lane == k, exps[k] / denom, gate_out)
            idx_out = jnp.where(lane == k, idxs[u][k], idx_out)
        prefix = jnp.dot(before, chosen.astype(BF16), preferred_element_type=F32) + base
        base = base + jnp.sum(chosen, axis=0, keepdims=True)
        rank_out = jnp.zeros((tm, LANES), F32)
        for k in range(TOP_K):
            r = jnp.sum(jnp.where(lane == idxs[u][k], prefix, 0.0), axis=-1, keepdims=True)
            rank_out = jnp.where(lane == k, r, rank_out)
        idx_ref[0, :, u * tm:(u + 1) * tm] = idx_out.T[:SUBLANES, :]
        gate_ref[u * tm:(u + 1) * tm, :] = gate_out
        rank_ref[0, :, u * tm:(u + 1) * tm] = rank_out.astype(jnp.int32).T[:SUBLANES, :]
    carry_ref[...] = base
    cnt_ref[...] = base.astype(jnp.int32)


def _out_router(streams, w_out_bf, ln_g, ln_b, router_w, router_b, tm, n_sub):
    (att_p, o_p, x_p), (att_s, o_s, x_s) = streams
    n_prompt = x_p.shape[0] // tm
    n_sample = x_s.shape[0] // tm
    t = x_p.shape[0] + x_s.shape[0]
    step_rows = tm * n_sub
    row = lambda i: (i, 0)
    const = lambda i: (0, 0)
    slots = pl.BlockSpec((1, SUBLANES, step_rows), lambda i: (i, 0, 0))
    tile_specs, tile_args = [], []
    for u in range(n_sub):
        prow = lambda i, u=u: (jnp.minimum(i * n_sub + u, n_prompt - 1), 0)
        srow = lambda i, u=u: (jnp.clip(i * n_sub + u - n_prompt, 0, n_sample - 1), 0)
        tile_specs += [pl.BlockSpec((tm, GROUP), prow), pl.BlockSpec((tm, GROUP), srow),
                       pl.BlockSpec((tm, GROUP), prow), pl.BlockSpec((tm, GROUP), srow),
                       pl.BlockSpec((tm, D_MODEL), prow), pl.BlockSpec((tm, D_MODEL), srow)]
        tile_args += [att_p, att_s, o_p, o_s, x_p, x_s]
    rw = jnp.pad(router_w, ((0, 0), (0, LANES - N_EXPERTS)))
    rw_hi = rw.astype(BF16)
    rw = jnp.stack([rw_hi, (rw - rw_hi.astype(F32)).astype(BF16)])
    rb = jnp.pad(router_b.reshape(1, N_EXPERTS), ((0, 0), (0, LANES - N_EXPERTS)))
    kern = functools.partial(_out_router_kernel, tm=tm, n_sub=n_sub, n_prompt=n_prompt)
    return pl.pallas_call(
        kern,
        grid=(t // step_rows,),
        in_specs=tile_specs + [pl.BlockSpec((2 * GROUP, D_MODEL), const),
                               pl.BlockSpec((1, D_MODEL), const), pl.BlockSpec((1, D_MODEL), const),
                               pl.BlockSpec((2, D_MODEL, LANES), lambda i: (0, 0, 0)),
                               pl.BlockSpec((1, LANES), const)],
        out_specs=[pl.BlockSpec((step_rows, D_MODEL), row), slots, pl.BlockSpec((step_rows, LANES), row), slots,
                   pl.BlockSpec((1, LANES), const)],
        out_shape=[jax.ShapeDtypeStruct((t, D_MODEL), F32),
                   jax.ShapeDtypeStruct((t // step_rows, SUBLANES, step_rows), jnp.int32),
                   jax.ShapeDtypeStruct((t, LANES), F32),
                   jax.ShapeDtypeStruct((t // step_rows, SUBLANES, step_rows), jnp.int32),
                   jax.ShapeDtypeStruct((1, LANES), jnp.int32)],
        scratch_shapes=[pltpu.VMEM((1, LANES), F32)],
        compiler_params=_params("arbitrary"),
        name="out_router",
    )(*tile_args, w_out_bf, ln_g.reshape(1, D_MODEL), ln_b.reshape(1, D_MODEL), rw, rb)


def _row_copy(src, src_row, dst, dst_row, sem):
    return pltpu.make_async_copy(src.at[pl.ds(src_row, 1), :], dst.at[pl.ds(dst_row, 1), :], sem)


def _dispatch_kernel(ps_ref, pe_ref, dest_ref, x_ref, xb_hbm, zero_ref, sem, zsem, *, tm, bm):
    i = pl.program_id(0)

    def zero_block(row):
        return pltpu.make_async_copy(zero_ref, xb_hbm.at[pl.ds(pl.multiple_of(row, bm), bm), :], zsem)

    def fill(e):
        return zero_block(pe_ref[e] - bm)

    @pl.when(i == 0)
    def _():
        zero_ref[...] = jnp.zeros_like(zero_ref)
        for e in range(N_EXPERTS):
            @pl.when(pe_ref[e] > ps_ref[e])
            def _():
                fill(e).start()
        first_unused = pe_ref[N_EXPERTS - 1] // bm
        n_blocks = xb_hbm.shape[0] // bm
        lax.fori_loop(first_unused, n_blocks, lambda b, c: (zero_block(b * bm).start(), c)[1], 0)
        for e in range(N_EXPERTS):
            @pl.when(pe_ref[e] > ps_ref[e])
            def _():
                fill(e).wait()
        lax.fori_loop(first_unused, n_blocks, lambda b, c: (zero_block(b * bm).wait(), c)[1], 0)

    def issue(r, carry):
        for k in range(TOP_K):
            _row_copy(x_ref, r, xb_hbm, dest_ref[0, 0, r * TOP_K + k], sem).start()
        return carry

    lax.fori_loop(0, tm, issue, 0)
    pltpu.make_async_copy(xb_hbm.at[pl.ds(0, tm * TOP_K), :], xb_hbm.at[pl.ds(0, tm * TOP_K), :], sem).wait()


def _dispatch(x1, dest, pad_start, pad_end, rows, tm, bm):
    t = x1.shape[0]
    return pl.pallas_call(
        functools.partial(_dispatch_kernel, tm=tm, bm=bm),
        grid_spec=pltpu.PrefetchScalarGridSpec(
            num_scalar_prefetch=2,
            grid=(t // tm,),
            in_specs=[pl.BlockSpec((1, 1, tm * TOP_K), lambda i, ps, pe: (i, 0, 0), memory_space=pltpu.SMEM),
                      pl.BlockSpec((tm, D_MODEL), lambda i, ps, pe: (i, 0))],
            out_specs=pl.BlockSpec(memory_space=pl.ANY),
            scratch_shapes=[pltpu.VMEM((bm, D_MODEL), F32),
                            pltpu.SemaphoreType.DMA(()), pltpu.SemaphoreType.DMA(())]),
        out_shape=jax.ShapeDtypeStruct((rows, D_MODEL), F32),
        compiler_params=_params("arbitrary"),
        name="dispatch",
    )(pad_start, pad_end, dest.reshape(t // tm, 1, tm * TOP_K), x1)


def _expert_kernel(ps_ref, pe_ref, x_ref, wgu_hbm, bgu_ref, wd_hbm, bd_ref, y_ref,
                   wgu_buf, wd_buf, sem, state_ref, *, bm):
    blk = pl.program_id(0)
    row0 = blk * bm

    def fetch(e, s):
        return (pltpu.make_async_copy(wgu_hbm.at[e], wgu_buf.at[s], sem.at[s]),
                pltpu.make_async_copy(wd_hbm.at[e], wd_buf.at[s], sem.at[s]))

    def next_with_rows(e):
        return lax.while_loop(lambda n: (n < N_EXPERTS) & (pe_ref[jnp.minimum(n, N_EXPERTS - 1)] <= row0),
                              lambda n: n + 1, e)

    @pl.when(row0 < pe_ref[N_EXPERTS - 1])
    def _():
        @pl.when(blk == 0)
        def _():
            first = next_with_rows(0)
            state_ref[0] = first
            state_ref[1] = 0
            for copy in fetch(first, 0):
                copy.start()

        @pl.when((blk > 0) & (row0 >= pe_ref[state_ref[0]]))
        def _():
            state_ref[0] = next_with_rows(state_ref[0])
            state_ref[1] = 1 - state_ref[1]

        e, s = state_ref[0], state_ref[1]

        @pl.when(row0 == ps_ref[e])
        def _():
            for copy in fetch(e, s):
                copy.wait()
            nxt = lax.while_loop(lambda n: (n < N_EXPERTS) & (pe_ref[jnp.minimum(n, N_EXPERTS - 1)] <= pe_ref[e]),
                                 lambda n: n + 1, e + 1)

            @pl.when(nxt < N_EXPERTS)
            def _():
                for copy in fetch(nxt, 1 - s):
                    copy.start()

        x = x_ref[...].astype(BF16)
        h = jnp.dot(x, wgu_buf[s].astype(BF16), preferred_element_type=F32) + bgu_ref[pl.ds(e, 1), :]
        gate = jnp.minimum(h[:, :D_FF], SWIGLU_LIMIT)
        up = jnp.clip(h[:, D_FF:], -SWIGLU_LIMIT, SWIGLU_LIMIT)
        act = (up + 1.0) * (gate * jax.nn.sigmoid(SWIGLU_ALPHA * gate))
        y_ref[...] = (jnp.dot(act.astype(BF16), wd_buf[s].astype(BF16), preferred_element_type=F32)
                      + bd_ref[pl.ds(e, 1), :])

    @pl.when(row0 >= pe_ref[N_EXPERTS - 1])
    def _():
        y_ref[...] = jnp.zeros_like(y_ref)


def _experts(xb, pad_start, pad_end, w_gu, b_gu, w_down, b_down, bm):
    rows = xb.shape[0]
    n_blocks = rows // bm
    used = lambda i, ps, pe: (jnp.maximum(jnp.minimum(i, pe[N_EXPERTS - 1] // bm - 1), 0), 0)
    whole = lambda i, ps, pe: (0, 0)
    return pl.pallas_call(
        functools.partial(_expert_kernel, bm=bm),
        grid_spec=pltpu.PrefetchScalarGridSpec(
            num_scalar_prefetch=2,
            grid=(n_blocks,),
            in_specs=[pl.BlockSpec((bm, D_MODEL), used),
                      pl.BlockSpec(memory_space=pl.ANY),
                      pl.BlockSpec((N_EXPERTS, 2 * D_FF), whole),
                      pl.BlockSpec(memory_space=pl.ANY),
                      pl.BlockSpec((N_EXPERTS, D_MODEL), whole)],
            out_specs=pl.BlockSpec((bm, D_MODEL), lambda i, ps, pe: (i, 0)),
            scratch_shapes=[pltpu.VMEM((2, D_MODEL, 2 * D_FF), F32), pltpu.VMEM((2, D_FF, D_MODEL), F32),
                            pltpu.SemaphoreType.DMA((2,)), pltpu.SMEM((2,), jnp.int32)]),
        out_shape=jax.ShapeDtypeStruct((rows, D_MODEL), F32),
        compiler_params=_params("arbitrary"),
        name="experts",
    )(pad_start, pad_end, xb, w_gu, b_gu, w_down, b_down)


def _combine_ln_kernel(dcur_ref, dnext_ref, x_ref, gate_ref, g_ref, b_ref, yb_hbm, op_ref, os_ref, ybuf, sem, *,
                       tm, n_prompt):
    i = pl.program_id(0)
    slot = i % 2

    def issue(dref, s):
        def body(r, carry):
            for k in range(TOP_K):
                _row_copy(yb_hbm, dref[0, 0, r * TOP_K + k], ybuf.at[s, k], r, sem.at[s]).start()
            return carry
        lax.fori_loop(0, tm, body, 0)

    @pl.when(i == 0)
    def _():
        issue(dcur_ref, 0)

    @pl.when(i + 1 < pl.num_programs(0))
    def _():
        issue(dnext_ref, 1 - slot)

    pltpu.make_async_copy(ybuf.at[slot], ybuf.at[slot], sem.at[slot]).wait()
    gates = gate_ref[...]
    y = sum(gates[:, k:k + 1] * ybuf[slot, k] for k in range(TOP_K))
    out = _layernorm(DEEPNORM_ALPHA * x_ref[...] + y, g_ref[...], b_ref[...])

    @pl.when(i < n_prompt)
    def _():
        op_ref[...] = out

    @pl.when(i >= n_prompt)
    def _():
        os_ref[...] = out


def _combine_ln(x1, gates, dest, yb, ln_g, ln_b, tm, t_prompt):
    t = x1.shape[0]
    n = t // tm
    n_prompt = t_prompt // tm
    row = lambda i: (i, 0)
    const = lambda i: (0, 0)
    d2 = dest.reshape(n, 1, tm * TOP_K)
    return pl.pallas_call(
        functools.partial(_combine_ln_kernel, tm=tm, n_prompt=n_prompt),
        grid=(n,),
        in_specs=[pl.BlockSpec((1, 1, tm * TOP_K), lambda i: (i, 0, 0), memory_space=pltpu.SMEM),
                  pl.BlockSpec((1, 1, tm * TOP_K), lambda i: (jnp.minimum(i + 1, n - 1), 0, 0),
                               memory_space=pltpu.SMEM),
                  pl.BlockSpec((tm, D_MODEL), row),
                  pl.BlockSpec((tm, LANES), row),
                  pl.BlockSpec((1, D_MODEL), const), pl.BlockSpec((1, D_MODEL), const),
                  pl.BlockSpec(memory_space=pl.ANY)],
        out_specs=[pl.BlockSpec((tm, D_MODEL), lambda i: (jnp.minimum(i, n_prompt - 1), 0)),
                   pl.BlockSpec((tm, D_MODEL), lambda i: (jnp.maximum(i - n_prompt, 0), 0))],
        out_shape=[jax.ShapeDtypeStruct((t_prompt, D_MODEL), F32),
                   jax.ShapeDtypeStruct((t - t_prompt, D_MODEL), F32)],
        scratch_shapes=[pltpu.VMEM((2, TOP_K, tm, D_MODEL), F32), pltpu.SemaphoreType.DMA((2,))],
        compiler_params=_params("arbitrary"),
        name="combine_ln",
    )(d2, d2, x1, gates, ln_g.reshape(1, D_MODEL), ln_b.reshape(1, D_MODEL), yb)


MOE_BM = 512
ATTN_BLK = 512
QUERY_PART = 256
GDN_SUB = 4
ROUTE_TM = 128


def _mixers(x2d, bsz, seq, cache_k, cache_v, conv_past, s0, lam, lam_init, w_in_bf, conv_w, a_log, dt_bias,
            delta_norm_w, subln_w):
    prompt = cache_k is None
    tm = ATTN_BLK if prompt else x2d.shape[0]
    past8 = jnp.pad(conv_past, ((0, 0), (SUBLANES - (CONV_W - 1), 0), (0, 0)))
    q_bf, k_f, v_f, k_bf, v_bf, cin, z, ab, *ctail = _in_proj(x2d, w_in_bf, past8, conv_w, tm,
                                                              seq // tm if prompt else 1, prompt)
    if prompt:
        att = _attn_prompt(q_bf, k_bf, v_bf, lam, subln_w, bsz, seq, ATTN_BLK, lam_init)
        conv_new = ctail[0][:, SUBLANES - (CONV_W - 1):]
    else:
        att = _attn_sample(q_bf, k_bf, v_bf, cache_k, cache_v, 0, lam, subln_w.reshape(1, HEAD), bsz, seq, lam_init)
        conv_new = cin.reshape(bsz, seq, CONV_CH)[:, seq - (CONV_W - 1):]
    chunk = CHUNK if seq % CHUNK == 0 else seq
    n_sub = GDN_SUB if (seq // chunk) % GDN_SUB == 0 else 1
    u, w, qd, kd, qk, gl = _gdn_prep(cin, ab, past8, conv_w, a_log, dt_bias, bsz, seq, chunk, n_sub, prompt)
    o, s_new = _gdn_scan(u, w, qd, kd, qk, gl, z, s0, delta_norm_w.reshape(1, HEAD), bsz, seq, chunk)
    return att, o, k_f, v_f, conv_new, s_new


def _moe(x1, idx, gates, rank, counts, w_gu, b_gu, w_down, b_down, ln_g, ln_b, bm, tm, t_prompt):
    t = x1.shape[0]
    n = t * TOP_K
    counts = counts[0, :N_EXPERTS]
    padded = (counts + bm - 1) // bm * bm
    pad_end = jnp.cumsum(padded).astype(jnp.int32)
    pad_start = (pad_end - padded).astype(jnp.int32)
    idx = idx[:, :TOP_K, :]
    dest = rank[:, :TOP_K, :] + sum(jnp.where(idx == e, pad_start[e], 0) for e in range(N_EXPERTS))
    dest = dest.transpose(0, 2, 1).reshape(t, TOP_K)
    n_blocks = -(-n // bm) + N_EXPERTS
    xb = _dispatch(x1, dest, pad_start, pad_end, n_blocks * bm, tm, bm)
    yb = _experts(xb, pad_start, pad_end, w_gu, b_gu, w_down, b_down, bm)
    return _combine_ln(x1, gates, dest, yb, ln_g, ln_b, ROUTE_TM, t_prompt)


def kernel(x_prompt, x_sample, cache_k, cache_v, state_conv, state_delta, w_in, conv_w, a_log, dt_bias,
           delta_norm_w, lambda_q1, lambda_k1, lambda_q2, lambda_k2, subln_w, w_out, ln1_g, ln1_b,
           router_w, router_b, w_gu, b_gu, w_down, b_down, ln2_g, ln2_b):
    bp, lp, _ = x_prompt.shape
    bs, ls, _ = x_sample.shape
    l = 0
    lam_init = 0.8 - 0.6 * math.exp(-0.3 * l)
    lam = (jnp.exp(jnp.sum(lambda_q1[l] * lambda_k1[l])) - jnp.exp(jnp.sum(lambda_q2[l] * lambda_k2[l]))
           + lam_init).reshape(1).astype(F32)
    w_in_bf = jnp.pad(w_in[l], ((0, 0), (0, IN_COLS_PAD - IN_COLS))).astype(BF16)
    shared = (lam, lam_init, w_in_bf, conv_w[l], a_log[l], dt_bias[l], delta_norm_w[l], subln_w[l])

    xp = x_prompt.reshape(bp * lp, D_MODEL)
    xs = x_sample.reshape(bs * ls, D_MODEL)
    zero_conv = jnp.zeros((bp, CONV_W - 1, CONV_CH), F32)
    zero_s = jnp.zeros((bp, N_HEADS, HEAD, HEAD), F32)
    att_p, o_p, k_p, v_p, cin_p, s_p = _mixers(xp, bp, lp, None, None, zero_conv, zero_s, *shared)
    att_s, o_s, k_s, v_s, cin_s, s_s = _mixers(xs, bs, ls, cache_k, cache_v, state_conv[l],
                                               state_delta[l], *shared)

    n_sub = next(c for c in (4, 3, 2, 1) if ((bp * lp + bs * ls) // ROUTE_TM) % c == 0)
    x1, idx, gates, rank, counts = _out_router(((att_p, o_p, xp), (att_s, o_s, xs)), w_out[l].astype(BF16),
                                               ln1_g[l], ln1_b[l], router_w[l], router_b[l], ROUTE_TM, n_sub)
    tp = bp * lp
    tm = n_sub * ROUTE_TM
    y_p, y_s = _moe(x1, idx, gates, rank, counts, w_gu[l], b_gu[l], w_down[l], b_down[l], ln2_g[l], ln2_b[l],
                    MOE_BM, tm, tp)
    return (y_p.reshape(bp, lp, D_MODEL), y_s.reshape(bs, ls, D_MODEL),
            k_p.reshape(1, bp, lp, N_HEADS, HEAD), v_p.reshape(1, bp, lp, N_HEADS, HEAD),
            cin_p[None], s_p[None].astype(state_delta.dtype),
            k_s.reshape(1, bs, ls, N_HEADS, HEAD), v_s.reshape(1, bs, ls, N_HEADS, HEAD),
            cin_s[None], s_s[None].astype(state_delta.dtype))
```

```python
import functools
import math

import jax
import jax.numpy as jnp
from jax import lax
from jax.experimental import pallas as pl
from jax.experimental.pallas import tpu as pltpu

F32 = jnp.float32
BF16 = jnp.bfloat16

D_MODEL = 1024
HEAD = 128
N_HEADS = 4
DQK = HEAD // 2
GROUP = N_HEADS * HEAD
CONV_W = 4
CONV_CH = 3 * GROUP
CHUNK = 64
ALIBI_MAX = 8.0
N_EXPERTS = 32
TOP_K = 4
D_FF = D_MODEL
SWIGLU_LIMIT = 7.0
SWIGLU_ALPHA = 1.702
DEPTH = 1
DEEPNORM_ALPHA = (2 * DEPTH) ** 0.25
LN_EPS = 1e-5
SUBLN_EPS = 1e-5
GATED_NORM_EPS = 1e-6
L2_EPS = 1e-6

LANES = 128
SUBLANES = 8
BF16_EXACT_INT = 256
BF16_ROWS = 16
LOG2E = 1.4426950408889634
N_POS = 6
ONES_ROWS = BF16_ROWS
N_DMA_PRIORITIES = 2
VMEM_LIMIT = 56 * 1024 * 1024

COL_Q, COL_K, COL_V, COL_CONV = 0, GROUP, 2 * GROUP, 3 * GROUP
COL_Z = COL_CONV + CONV_CH
COL_AB = COL_Z + GROUP
IN_COLS = COL_AB + 2 * N_HEADS
IN_COLS_PAD = COL_AB + LANES


def _params(*sem):
    return pltpu.CompilerParams(dimension_semantics=sem, vmem_limit_bytes=VMEM_LIMIT)


def _conv_activation(prev, raw, cw_ref):
    rows = raw.shape[0]
    xin = jnp.concatenate([prev, raw], axis=0)
    conv = sum(xin[SUBLANES - (CONV_W - 1) + j: SUBLANES - (CONV_W - 1) + j + rows] * cw_ref[j:j + 1, :]
               for j in range(CONV_W))
    conv = conv * jax.nn.sigmoid(conv)
    parts = []
    for h in range(2 * N_HEADS):
        x = conv[:, h * HEAD:(h + 1) * HEAD]
        scale = HEAD ** -0.5 if h < N_HEADS else 1.0
        parts.append(x * (lax.rsqrt(jnp.sum(x * x, axis=-1, keepdims=True) + L2_EPS) * scale))
    return jnp.concatenate(parts + [conv[:, 2 * GROUP:]], axis=1)


def _in_proj_kernel(x_ref, w_ref, past_ref, cw_ref, q_ref, kf_ref, vf_ref, kb_ref, vb_ref, c_ref, z_ref, ab_ref,
                    *conv_refs, tm, tiles_per_stream, prompt):
    xb = x_ref[...].astype(BF16)

    def section(lo, hi):
        return jnp.dot(xb, w_ref[:, lo:hi], preferred_element_type=F32)

    q = section(COL_Q, COL_K) * (DQK ** -0.5 * (LOG2E if prompt else 1.0))
    k = section(COL_K, COL_V)
    kf_ref[0] = k.reshape(tm, N_HEADS, HEAD)
    kb_ref[...] = k.astype(BF16)
    v = section(COL_V, COL_CONV)
    vf_ref[0] = v.reshape(tm, N_HEADS, HEAD)
    raw = section(COL_CONV, COL_Z)
    if prompt:
        ctail_ref, tail_ref = conv_refs
        q_ref[0] = q.T.astype(BF16)
        vb_ref[0] = v.T.astype(BF16)
        prev = jnp.where(pl.program_id(0) % tiles_per_stream == 0, past_ref[0], tail_ref[...])
        c_ref[...] = _conv_activation(prev, raw, cw_ref)
        tail_ref[...] = raw[tm - SUBLANES:, :]
        ctail_ref[0] = raw[tm - SUBLANES:, :]
    else:
        q_ref[...] = q.astype(BF16)
        vb_ref[...] = v.astype(BF16)
        c_ref[...] = raw
    z_ref[...] = section(COL_Z, COL_AB)
    ab_ref[...] = section(COL_AB, IN_COLS_PAD)


def _in_proj(x2d, w_bf, past8, conv_w, tm, tiles_per_stream, prompt):
    t = x2d.shape[0]
    row = lambda i: (i, 0)
    stream = lambda i: (i // tiles_per_stream, 0, 0)
    widths = (GROUP, GROUP, GROUP, GROUP, GROUP, CONV_CH, GROUP, LANES)
    dtypes = (BF16, F32, F32, BF16, BF16, F32, F32, F32)
    out_specs = [pl.BlockSpec((tm, w), row) for w in widths]
    out_shape = [jax.ShapeDtypeStruct((t, w), d) for w, d in zip(widths, dtypes)]
    for slot in (1, 2):
        out_specs[slot] = pl.BlockSpec((1, tm, N_HEADS, HEAD), lambda i: (i, 0, 0, 0))
        out_shape[slot] = jax.ShapeDtypeStruct((t // tm, tm, N_HEADS, HEAD), F32)
    if prompt:
        for slot in (0, 4):
            out_specs[slot] = pl.BlockSpec((1, GROUP, tm), lambda i: (i, 0, 0))
            out_shape[slot] = jax.ShapeDtypeStruct((t // tm, GROUP, tm), BF16)
        out_specs.append(pl.BlockSpec((1, SUBLANES, CONV_CH), stream))
        out_shape.append(jax.ShapeDtypeStruct((past8.shape[0], SUBLANES, CONV_CH), F32))
    return pl.pallas_call(
        functools.partial(_in_proj_kernel, tm=tm, tiles_per_stream=tiles_per_stream, prompt=prompt),
        grid=(t // tm,),
        in_specs=[pl.BlockSpec((tm, D_MODEL), row),
                  pl.BlockSpec((D_MODEL, IN_COLS_PAD), lambda i: (0, 0)),
                  pl.BlockSpec((1, SUBLANES, CONV_CH), stream),
                  pl.BlockSpec((CONV_W, CONV_CH), lambda i: (0, 0))],
        out_specs=out_specs,
        out_shape=out_shape,
        scratch_shapes=[pltpu.VMEM((SUBLANES, CONV_CH), F32)] if prompt else [],
        compiler_params=_params("arbitrary"),
        name="in_proj",
    )(x2d, w_bf, past8, conv_w)


def _alibi_slopes():
    return [2.0 ** (-ALIBI_MAX * (h + 1) / N_HEADS) for h in range(N_HEADS)]


def _stack_halves(q):
    lane = lax.broadcasted_iota(jnp.int32, q.shape, 1)
    zero = jnp.zeros_like(q)
    return jnp.concatenate([jnp.where(lane < DQK, q, zero), jnp.where(lane < DQK, zero, q)], axis=0)


def _diff_norm(acc, l, lam, w, lam_init, rows):
    o = acc[:rows] / l[:rows] - lam * (acc[rows:] / l[rows:])
    ms = jnp.mean(o * o, axis=-1, keepdims=True)
    return o * lax.rsqrt(ms + SUBLN_EPS) * w * (1.0 - lam_init)


def _head_slope(h):
    s = _alibi_slopes()
    return jnp.where(h == 0, s[0], jnp.where(h == 1, s[1], jnp.where(h == 2, s[2], s[3]))).astype(F32)


def _attn_prompt_kernel(lam_ref, qt_ref, k_ref, vt_ref, w_ref, o_ref,
                        diag_ref, kaug_ref, vaug_ref, qz_ref, s0_ref, s1_ref, p0_ref, p1_ref, mx0_ref, mx1_ref, m_ref, acc_ref, *, blk, lam_init):
    i = pl.program_id(2)
    slope = _head_slope(pl.program_id(1)) * LOG2E
    rows = 2 * blk
    n_kv = k_ref.shape[0] // blk

    def pieces(x):
        lo = x % BF16_EXACT_INT
        return _split3(slope * lo.astype(F32)) + _split3(slope * (x - lo).astype(F32))

    @pl.when(i == 0)
    def _():
        lane = lax.broadcasted_iota(jnp.int32, (blk, HEAD), 1)
        extra = jnp.where(lane < N_POS, 1.0, 0.0).astype(BF16)
        for n, piece in enumerate(pieces(lax.broadcasted_iota(jnp.int32, (blk, 1), 0))):
            extra = jnp.where(lane == N_POS + n, piece, extra)
        ones = jnp.ones((ONES_ROWS, blk), BF16)
        c = lax.broadcasted_iota(jnp.int32, (blk, rows), 0)
        a = lax.broadcasted_iota(jnp.int32, (blk, rows), 1) % blk
        diag_ref[...] = jnp.where(c // CHUNK <= a // CHUNK, -slope * jnp.abs(a - c).astype(F32), -jnp.inf)
        r = lax.broadcasted_iota(jnp.int32, (HEAD, rows), 0)
        qx = jnp.where((r >= N_POS) & (r < 2 * N_POS), 1.0, 0.0).astype(BF16)
        for n, piece in enumerate(pieces(-(lax.broadcasted_iota(jnp.int32, (1, rows), 1) % blk))):
            qx = jnp.where(r == n, piece, qx)
        qz_ref[HEAD:, :] = qx

        def fill(j, carry):
            j0 = pl.multiple_of(j * blk, blk)
            kaug_ref[pl.ds(j0, blk), :HEAD] = k_ref[pl.ds(j0, blk), :]
            kaug_ref[pl.ds(j0, blk), HEAD:] = extra
            vaug_ref[j, :HEAD, :] = vt_ref[j]
            vaug_ref[j, HEAD:, :] = ones
            return carry

        lax.fori_loop(0, n_kv, fill, 0)

    qt = qt_ref[0]
    d = lax.broadcasted_iota(jnp.int32, qt.shape, 0)
    zero = jnp.zeros_like(qt)
    qz_ref[:HEAD, :] = jnp.concatenate([jnp.where(d < DQK, qt, zero), jnp.where(d < DQK, zero, qt)], axis=1)

    def tile_rows(j):
        return pl.ds(pl.multiple_of(j * blk, blk), blk)

    acc_ref[...] = jnp.zeros_like(acc_ref)
    for part in range(rows // QUERY_PART):
        cols = slice(part * QUERY_PART, (part + 1) * QUERY_PART)
        s = jnp.dot(k_ref[tile_rows(i), :], qz_ref[:HEAD, cols], preferred_element_type=F32) + diag_ref[:, cols]
        m = jnp.max(s, axis=0, keepdims=True)
        p1_ref[:, cols] = jnp.exp2(s - m).astype(BF16)
        m_ref[:, cols] = m
        s_first = jnp.dot(kaug_ref[tile_rows(0), :], qz_ref[:, cols], preferred_element_type=F32)
        s0_ref[:, cols] = s_first
        mx0_ref[:, cols] = jnp.max(s_first, axis=0, keepdims=True)

    def step(j, s_cur, s_nxt, p_cur, p_nxt, mx_cur, mx_nxt):
        shift = -slope * ((i - j) * blk).astype(F32)
        k_next = kaug_ref[tile_rows(jnp.minimum(j + 1, i - 1)), :]
        v_prev = vaug_ref[jnp.where(j == 0, i, j - 1)]
        for part in range(rows // QUERY_PART):
            cols = slice(part * QUERY_PART, (part + 1) * QUERY_PART)
            s_next = jnp.dot(k_next, qz_ref[:, cols], preferred_element_type=F32)
            s_nxt[:, cols] = s_next
            mx_nxt[:, cols] = jnp.max(s_next, axis=0, keepdims=True)
            pv = jnp.dot(v_prev, p_nxt[:, cols], preferred_element_type=F32)
            m_old = m_ref[:, cols]
            m_new = jnp.maximum(m_old, mx_cur[:, cols] + shift)
            p_cur[:, cols] = jnp.exp2(s_cur[:, cols] - (m_new - shift)).astype(BF16)
            acc_ref[:, cols] = (acc_ref[:, cols] + pv) * jnp.exp2(m_old - m_new)
            m_ref[:, cols] = m_new

    def body(jj, carry):
        step(2 * jj, s0_ref, s1_ref, p0_ref, p1_ref, mx0_ref, mx1_ref)

        @pl.when(2 * jj + 1 < i)
        def _():
            step(2 * jj + 1, s1_ref, s0_ref, p1_ref, p0_ref, mx1_ref, mx0_ref)

        return carry

    lax.fori_loop(0, (i + 1) // 2, body, 0)
    p_last = jnp.where(i % 2 == 1, p0_ref[...], p1_ref[...])
    acc = acc_ref[...] + jnp.dot(vaug_ref[jnp.where(i > 0, i - 1, i)], p_last, preferred_element_type=F32)
    l = acc[HEAD:HEAD + 1, :]
    num = acc[:HEAD, :]
    ot = num[:, :blk] / l[:, :blk] - lam_ref[0] * (num[:, blk:] / l[:, blk:])
    ms = jnp.mean(ot * ot, axis=0, keepdims=True)
    ot = ot * lax.rsqrt(ms + SUBLN_EPS) * w_ref[...] * (1.0 - lam_init)
    o_ref[...] = ot.T.astype(o_ref.dtype)


def _attn_prompt(qt_bf, k_bf, vt_bf, lam, subln_w, bsz, seq, blk, lam_init):
    nq = seq // blk
    kern = functools.partial(_attn_prompt_kernel, blk=blk, lam_init=lam_init)
    return pl.pallas_call(
        kern,
        grid=(bsz, N_HEADS, nq),
        in_specs=[pl.BlockSpec(memory_space=pltpu.SMEM),
                  pl.BlockSpec((1, HEAD, blk), lambda b, h, i: (b * nq + i, h, 0)),
                  pl.BlockSpec((seq, HEAD), lambda b, h, i: (b, h)),
                  pl.BlockSpec((nq, HEAD, blk), lambda b, h, i: (b, h, 0)),
                  pl.BlockSpec((HEAD, 1), lambda b, h, i: (0, 0))],
        out_specs=pl.BlockSpec((blk, HEAD), lambda b, h, i: (b * nq + i, h)),
        scratch_shapes=[pltpu.VMEM((blk, 2 * blk), F32),
                        pltpu.VMEM((seq, 2 * HEAD), BF16),
                        pltpu.VMEM((nq, HEAD + ONES_ROWS, blk), BF16),
                        pltpu.VMEM((2 * HEAD, 2 * blk), BF16),
                        pltpu.VMEM((blk, 2 * blk), F32), pltpu.VMEM((blk, 2 * blk), F32),
                        pltpu.VMEM((blk, 2 * blk), BF16), pltpu.VMEM((blk, 2 * blk), BF16),
                        pltpu.VMEM((1, 2 * blk), F32), pltpu.VMEM((1, 2 * blk), F32),
                        pltpu.VMEM((1, 2 * blk), F32),
                        pltpu.VMEM((HEAD + ONES_ROWS, 2 * blk), F32)],
        out_shape=jax.ShapeDtypeStruct((bsz * seq, GROUP), BF16),
        compiler_params=_params("parallel", "parallel", "arbitrary"),
        name="attn_prompt",
    )(lam, qt_bf, k_bf, vt_bf, subln_w.reshape(HEAD, 1))


def _attn_sample_kernel(lam_ref, q_ref, kn_ref, vn_ref, kc_ref, vc_ref, w_ref, o_ref, *, seq, past, lam_init):
    nt = (((1,), (1,)), ((), ()))
    per_head = 2 * seq
    qz = jnp.concatenate([_stack_halves(q_ref[:, h * HEAD:(h + 1) * HEAD]) for h in range(N_HEADS)], axis=0)
    row = lax.broadcasted_iota(jnp.int32, (N_HEADS * per_head, 1), 0)
    q_head = row // per_head
    q_pos = past + row % seq
    slopes = _alibi_slopes()
    slope = jnp.where(q_head == 0, slopes[0], jnp.where(q_head == 1, slopes[1],
                                                        jnp.where(q_head == 2, slopes[2], slopes[3]))).astype(F32)

    def scores(keys, first_pos):
        n = lax.broadcasted_iota(jnp.int32, (1, keys.shape[0]), 1)
        s = lax.dot_general(qz, keys, nt, preferred_element_type=F32)
        bias = -slope * jnp.abs(q_pos - (first_pos + n // N_HEADS)).astype(F32)
        return jnp.where(n % N_HEADS == q_head, s + bias, -jnp.inf)

    s_c = scores(kc_ref[0, 0].reshape(past * N_HEADS, HEAD).astype(BF16), 0)
    s_n = scores(kn_ref[...].reshape(seq * N_HEADS, HEAD), past)
    m = jnp.maximum(jnp.max(s_c, axis=-1, keepdims=True), jnp.max(s_n, axis=-1, keepdims=True))
    p_c = jnp.exp(s_c - m)
    p_n = jnp.exp(s_n - m)
    l = jnp.sum(p_c, axis=-1, keepdims=True) + jnp.sum(p_n, axis=-1, keepdims=True)
    acc = (jnp.dot(p_c.astype(BF16), vc_ref[0, 0].reshape(past * N_HEADS, HEAD).astype(BF16),
                   preferred_element_type=F32)
           + jnp.dot(p_n.astype(BF16), vn_ref[...].reshape(seq * N_HEADS, HEAD), preferred_element_type=F32))
    for h in range(N_HEADS):
        rows = slice(h * per_head, (h + 1) * per_head)
        o_ref[:, h * HEAD:(h + 1) * HEAD] = _diff_norm(acc[rows], l[rows], lam_ref[0], w_ref[...], lam_init,
                                                       seq).astype(o_ref.dtype)


def _attn_sample(q_bf, k_bf, v_bf, cache_k, cache_v, layer, lam, subln_w, bsz, seq, lam_init):
    past = cache_k.shape[2]
    kern = functools.partial(_attn_sample_kernel, seq=seq, past=past, lam_init=lam_init)
    new = pl.BlockSpec((seq, GROUP), lambda b: (b, 0))
    cache = pl.BlockSpec((1, 1, past, N_HEADS, HEAD), lambda b: (layer, b, 0, 0, 0))
    return pl.pallas_call(
        kern,
        grid=(bsz,),
        in_specs=[pl.BlockSpec(memory_space=pltpu.SMEM), new, new, new, cache, cache,
                  pl.BlockSpec((1, HEAD), lambda b: (0, 0))],
        out_specs=new,
        out_shape=jax.ShapeDtypeStruct((bsz * seq, GROUP), BF16),
        compiler_params=_params("parallel"),
        name="attn_sample",
    )(lam, q_bf, k_bf, v_bf, cache_k, cache_v, subln_w)


def _split3(x):
    hi = x.astype(BF16)
    r1 = x - hi.astype(F32)
    mid = r1.astype(BF16)
    lo = (r1 - mid.astype(F32)).astype(BF16)
    return hi, mid, lo


def _gdn_prep_kernel(cin_ref, halo_ref, past_ref, ab_ref, cw_ref, alog_ref, dtb_ref,
                     u_ref, w_ref, qd_ref, kd_ref, qk_ref, gl_ref, *, chunk, n_sub, activated):
    c_idx = pl.program_id(1)
    rows = chunk * n_sub
    if activated:
        conv = cin_ref[...]
    else:
        prev = jnp.where(c_idx == 0, past_ref[0], halo_ref[...])
        conv = _conv_activation(prev, cin_ref[...], cw_ref)

    ab = ab_ref[...]
    lane = lax.broadcasted_iota(jnp.int32, ab.shape, 1)
    pre = ab + dtb_ref[...]
    softplus = jnp.maximum(pre, 0.0) + jnp.log(1.0 + jnp.exp(-jnp.abs(pre)))
    g = jnp.where(lane < N_HEADS, -jnp.exp(alog_ref[...]) * softplus, 0.0)
    beta_all = jax.nn.sigmoid(ab)

    ri = lax.broadcasted_iota(jnp.int32, (rows, rows), 0)
    ci = lax.broadcasted_iota(jnp.int32, (rows, rows), 1)
    same = (ri // chunk) == (ci // chunk)
    incl = same & (ri >= ci)
    strict = same & (ri > ci)
    eye = jnp.where(ri == ci, 1.0, 0.0).astype(F32)
    nt = (((1,), (1,)), ((), ()))
    g_parts = _split3(g)
    ones_incl = jnp.where(incl, 1.0, 0.0).astype(BF16)
    ones_same = jnp.where(same, 1.0, 0.0).astype(BF16)
    gc = sum(jnp.dot(ones_incl, part, preferred_element_type=F32) for part in g_parts)
    g_end = sum(jnp.dot(ones_same, part, preferred_element_type=F32) for part in g_parts)
    gct = gc.T
    for sc in range(n_sub):
        gl_ref[sc] = g_end[sc * chunk:sc * chunk + 1, :]

    t_mats, p_mats, rhs = [], [], []
    for h in range(N_HEADS):
        col = slice(h * HEAD, (h + 1) * HEAD)
        qh = conv[:, h * HEAD:(h + 1) * HEAD]
        kh = conv[:, GROUP + h * HEAD:GROUP + (h + 1) * HEAD]
        vh = conv[:, 2 * GROUP + h * HEAD:2 * GROUP + (h + 1) * HEAD]
        beta = beta_all[:, N_HEADS + h:N_HEADS + h + 1]
        gcol = gc[:, h:h + 1]
        grow = gct[h:h + 1, :]
        gamma = jnp.exp(jnp.where(incl, gcol - grow, -jnp.inf))
        egc = jnp.exp(gcol)
        kb = kh * beta
        khb = kh.astype(BF16)
        a = jnp.where(strict, lax.dot_general(kb.astype(BF16), khb, nt, preferred_element_type=F32) * gamma, 0.0)
        qk = (lax.dot_general(qh.astype(BF16), khb, nt, preferred_element_type=F32) * gamma).astype(BF16)
        for sc in range(n_sub):
            blk = slice(sc * chunk, (sc + 1) * chunk)
            qk_ref[h, blk, :] = qk[blk, blk]
        qd_ref[:, col] = (qh * egc).astype(BF16)
        kd_ref[:, col] = (kh * jnp.exp(g_end[:, h:h + 1] - gcol)).astype(BF16)
        t_mats.append(eye - a)
        p_mats.append(a)
        rhs.append(((vh * beta).astype(BF16), (kb * egc).astype(BF16)))

    for _ in range(int(math.log2(chunk)) - 1):
        for h in range(N_HEADS):
            pb = p_mats[h].astype(BF16)
            p_mats[h] = jnp.dot(pb, pb, preferred_element_type=F32)
        for h in range(N_HEADS):
            t_mats[h] = t_mats[h] + jnp.dot(t_mats[h].astype(BF16), p_mats[h].astype(BF16),
                                            preferred_element_type=F32)

    for h in range(N_HEADS):
        col = slice(h * HEAD, (h + 1) * HEAD)
        t_inv = t_mats[h].astype(BF16)
        u_ref[:, col] = jnp.dot(t_inv, rhs[h][0], preferred_element_type=F32)
        w_ref[:, col] = jnp.dot(t_inv, rhs[h][1], preferred_element_type=F32).astype(BF16)


def _gdn_prep(cin, ab, past8, conv_w, a_log, dt_bias, bsz, seq, chunk, n_sub, activated):
    rows = chunk * n_sub
    nblk = seq // rows
    t = bsz * seq
    lanes = lambda v: jnp.pad(v.reshape(1, N_HEADS).astype(F32), ((0, 0), (0, LANES - N_HEADS)))
    kern = functools.partial(_gdn_prep_kernel, chunk=chunk, n_sub=n_sub, activated=activated)
    rowblk = lambda b, c: (b * nblk + c, 0)
    halo = lambda b, c: (jnp.maximum((b * nblk + c) * (rows // SUBLANES) - 1, 0), 0)
    const = lambda b, c: (0, 0)
    return pl.pallas_call(
        kern,
        grid=(bsz, nblk),
        in_specs=[pl.BlockSpec((rows, CONV_CH), rowblk),
                  pl.BlockSpec((SUBLANES, CONV_CH), halo),
                  pl.BlockSpec((1, SUBLANES, CONV_CH), lambda b, c: (b, 0, 0)),
                  pl.BlockSpec((rows, LANES), rowblk),
                  pl.BlockSpec((CONV_W, CONV_CH), const),
                  pl.BlockSpec((1, LANES), const),
                  pl.BlockSpec((1, LANES), const)],
        out_specs=[pl.BlockSpec((rows, GROUP), rowblk),
                   pl.BlockSpec((rows, GROUP), rowblk),
                   pl.BlockSpec((rows, GROUP), rowblk),
                   pl.BlockSpec((rows, GROUP), rowblk),
                   pl.BlockSpec((N_HEADS, rows, chunk), lambda b, c: (0, b * nblk + c, 0)),
                   pl.BlockSpec((n_sub, 1, LANES), lambda b, c: (b * nblk + c, 0, 0))],
        out_shape=[jax.ShapeDtypeStruct((t, GROUP), F32),
                   jax.ShapeDtypeStruct((t, GROUP), BF16),
                   jax.ShapeDtypeStruct((t, GROUP), BF16),
                   jax.ShapeDtypeStruct((t, GROUP), BF16),
                   jax.ShapeDtypeStruct((N_HEADS, t, chunk), BF16),
                   jax.ShapeDtypeStruct((t // chunk, 1, LANES), F32)],
        compiler_params=_params("parallel", "parallel"),
        name="gdn_prep",
    )(cin, cin, past8, ab, conv_w, lanes(a_log), lanes(dt_bias))


def _gdn_scan_kernel(u_ref, w_ref, qd_ref, kd_ref, qk_ref, gl_ref, z_ref, s0_ref, nw_ref,
                     o_ref, sf_ref, s_ref, *, bsz, chunk, n_c):
    c_idx = pl.program_id(0)

    @pl.when(c_idx == 0)
    def _():
        s_ref[...] = s0_ref[...]

    tn = (((0,), (0,)), ((), ()))
    chains = [(b, h, slice(h * HEAD, (h + 1) * HEAD)) for b in range(bsz) for h in range(N_HEADS)]
    states = [s_ref[b, h] for b, h, _ in chains]
    for cc in range(n_c):
        rows = slice(cc * chunk, (cc + 1) * chunk)
        prods = [jnp.dot(jnp.concatenate([w_ref[b, rows, col], qd_ref[b, rows, col]], axis=0), s.astype(BF16),
                         preferred_element_type=F32) for (b, _, col), s in zip(chains, states)]
        v_news = [(u_ref[b, rows, col] - r[:chunk]).astype(BF16) for (b, _, col), r in zip(chains, prods)]
        outs = [r[chunk:] + jnp.dot(qk_ref[h, b, rows], v, preferred_element_type=F32)
                for (b, h, _), r, v in zip(chains, prods, v_news)]
        decays = [jnp.exp(gl_ref[b, cc]) for b in range(bsz)]
        states = [s * decays[b][:, h:h + 1] + lax.dot_general(kd_ref[b, rows, col], v, tn,
                                                             preferred_element_type=F32)
                  for (b, h, col), s, v in zip(chains, states, v_news)]
        for (b, h, col), o in zip(chains, outs):
            zh = z_ref[b, rows, col]
            ms = jnp.mean(o * o, axis=-1, keepdims=True)
            o = o * lax.rsqrt(ms + GATED_NORM_EPS) * nw_ref[...] * (zh * jax.nn.sigmoid(zh))
            o_ref[b, rows, col] = o.astype(o_ref.dtype)
    for (b, h, _), s in zip(chains, states):
        s_ref[b, h] = s

    @pl.when(c_idx == pl.num_programs(0) - 1)
    def _():
        sf_ref[...] = s_ref[...]


def _gdn_scan(u, w, qd, kd, qk, gl, z, s0, norm_w, bsz, seq, chunk, n_c):
    nc = seq // chunk
    kern = functools.partial(_gdn_scan_kernel, bsz=bsz, chunk=chunk, n_c=n_c)
    tok = pl.BlockSpec((bsz, n_c * chunk, GROUP), lambda c: (0, c, 0))
    state = pl.BlockSpec((bsz, N_HEADS, HEAD, HEAD), lambda c: (0, 0, 0, 0))
    o, s_final = pl.pallas_call(
        kern,
        grid=(nc // n_c,),
        in_specs=[tok, tok, tok, tok,
                  pl.BlockSpec((N_HEADS, bsz, n_c * chunk, chunk), lambda c: (0, 0, c, 0)),
                  pl.BlockSpec((bsz, n_c, 1, LANES), lambda c: (0, c, 0, 0)),
                  tok, state,
                  pl.BlockSpec((1, HEAD), lambda c: (0, 0))],
        out_specs=[tok, state],
        out_shape=[jax.ShapeDtypeStruct((bsz, seq, GROUP), BF16),
                   jax.ShapeDtypeStruct((bsz, N_HEADS, HEAD, HEAD), F32)],
        scratch_shapes=[pltpu.VMEM((bsz, N_HEADS, HEAD, HEAD), F32)],
        compiler_params=_params("arbitrary"),
        name="gdn_scan",
    )(u.reshape(bsz, seq, GROUP), w.reshape(bsz, seq, GROUP), qd.reshape(bsz, seq, GROUP),
      kd.reshape(bsz, seq, GROUP), qk.reshape(N_HEADS, bsz, seq, chunk), gl.reshape(bsz, nc, 1, LANES),
      z.reshape(bsz, seq, GROUP), s0, norm_w)
    return o.reshape(bsz * seq, GROUP), s_final


def _layernorm(x, g, b):
    mu = jnp.mean(x, axis=-1, keepdims=True)
    xc = x - mu
    var = jnp.mean(xc * xc, axis=-1, keepdims=True)
    return xc * lax.rsqrt(var + LN_EPS) * g + b


def _out_router_kernel(*refs, tm, n_sub, n_prompt):
    tiles = [refs[6 * u:6 * u + 6] for u in range(n_sub)]
    wo_ref, g_ref, b_ref, rw_ref, rb_ref, x1_ref, idx_ref, gate_ref, rank_ref, cnt_ref, carry_ref = refs[6 * n_sub:]
    step = pl.program_id(0)
    units = range(n_sub)

    @pl.when(step == 0)
    def _():
        carry_ref[...] = jnp.zeros_like(carry_ref)

    x1s = []
    for u in units:
        attp_ref, atts_ref, op_ref, os_ref, xp_ref, xs_ref = tiles[u]
        prompt = step * n_sub + u < n_prompt
        att = jnp.where(prompt, attp_ref[...], atts_ref[...])
        o = jnp.where(prompt, op_ref[...], os_ref[...])
        x = jnp.where(prompt, xp_ref[...], xs_ref[...])
        mix = (jnp.dot(att, wo_ref[:GROUP, :], preferred_element_type=F32)
               + jnp.dot(o, wo_ref[GROUP:, :], preferred_element_type=F32))
        x1 = _layernorm(DEEPNORM_ALPHA * x + mix, g_ref[...], b_ref[...])
        x1_ref[u * tm:(u + 1) * tm, :] = x1
        x1s.append(x1)

    works = []
    lane = lax.broadcasted_iota(jnp.int32, (tm, LANES), 1)
    for x1 in x1s:
        x_hi = x1.astype(BF16)
        x_lo = (x1 - x_hi.astype(F32)).astype(BF16)
        logits = (jnp.dot(x_hi, rw_ref[0], preferred_element_type=F32)
                  + jnp.dot(x_lo, rw_ref[0], preferred_element_type=F32)
                  + jnp.dot(x_hi, rw_ref[1], preferred_element_type=F32)) + rb_ref[...]
        works.append(jnp.where(lane < N_EXPERTS, logits, -jnp.inf))

    vals, idxs = [[] for _ in units], [[] for _ in units]
    for _ in range(TOP_K):
        for u in units:
            m = jnp.max(works[u], axis=-1, keepdims=True)
            am = jnp.min(jnp.where(works[u] == m, lane, LANES), axis=-1, keepdims=True)
            vals[u].append(m)
            idxs[u].append(am)
            works[u] = jnp.where(lane == am, -jnp.inf, works[u])

    ri = lax.broadcasted_iota(jnp.int32, (tm, tm), 0)
    ci = lax.broadcasted_iota(jnp.int32, (tm, tm), 1)
    before = jnp.where(ri > ci, 1.0, 0.0).astype(BF16)
    base = carry_ref[...]
    for u in units:
        exps = [jnp.exp(v - vals[u][0]) for v in vals[u]]
        denom = exps[0] + exps[1] + exps[2] + exps[3]
        chosen = jnp.zeros((tm, LANES), F32)
        gate_out = jnp.zeros((tm, LANES), F32)
        idx_out = jnp.zeros((tm, LANES), jnp.int32)
        for k in range(TOP_K):
            chosen = jnp.where(lane == idxs[u][k], 1.0, chosen)
            gate_out = jnp.where(lane == k, exps[k] / denom, gate_out)
            idx_out = jnp.where(lane == k, idxs[u][k], idx_out)
        prefix = jnp.dot(before, chosen.astype(BF16), preferred_element_type=F32) + base
        base = base + jnp.sum(chosen, axis=0, keepdims=True)
        rank_out = jnp.zeros((tm, LANES), F32)
        for k in range(TOP_K):
            r = jnp.sum(jnp.where(lane == idxs[u][k], prefix, 0.0), axis=-1, keepdims=True)
            rank_out = jnp.where(lane == k, r, rank_out)
        idx_ref[0, :, u * tm:(u + 1) * tm] = idx_out.T[:SUBLANES, :]
        gate_ref[u * tm:(u + 1) * tm, :] = gate_out
        rank_ref[0, :, u * tm:(u + 1) * tm] = rank_out.astype(jnp.int32).T[:SUBLANES, :]
    carry_ref[...] = base
    cnt_ref[...] = base.astype(jnp.int32)


def _out_router(streams, w_out_bf, ln_g, ln_b, router_w, router_b, tm, n_sub):
    (att_p, o_p, x_p), (att_s, o_s, x_s) = streams
    n_prompt = x_p.shape[0] // tm
    n_sample = x_s.shape[0] // tm
    t = x_p.shape[0] + x_s.shape[0]
    step_rows = tm * n_sub
    row = lambda i: (i, 0)
    const = lambda i: (0, 0)
    slots = pl.BlockSpec((1, SUBLANES, step_rows), lambda i: (i, 0, 0))
    tile_specs, tile_args = [], []
    for u in range(n_sub):
        prow = lambda i, u=u: (jnp.minimum(i * n_sub + u, n_prompt - 1), 0)
        srow = lambda i, u=u: (jnp.clip(i * n_sub + u - n_prompt, 0, n_sample - 1), 0)
        tile_specs += [pl.BlockSpec((tm, GROUP), prow), pl.BlockSpec((tm, GROUP), srow),
                       pl.BlockSpec((tm, GROUP), prow), pl.BlockSpec((tm, GROUP), srow),
                       pl.BlockSpec((tm, D_MODEL), prow), pl.BlockSpec((tm, D_MODEL), srow)]
        tile_args += [att_p, att_s, o_p, o_s, x_p, x_s]
    rw = jnp.pad(router_w, ((0, 0), (0, LANES - N_EXPERTS)))
    rw_hi = rw.astype(BF16)
    rw = jnp.stack([rw_hi, (rw - rw_hi.astype(F32)).astype(BF16)])
    rb = jnp.pad(router_b.reshape(1, N_EXPERTS), ((0, 0), (0, LANES - N_EXPERTS)))
    kern = functools.partial(_out_router_kernel, tm=tm, n_sub=n_sub, n_prompt=n_prompt)
    return pl.pallas_call(
        kern,
        grid=(t // step_rows,),
        in_specs=tile_specs + [pl.BlockSpec((2 * GROUP, D_MODEL), const),
                               pl.BlockSpec((1, D_MODEL), const), pl.BlockSpec((1, D_MODEL), const),
                               pl.BlockSpec((2, D_MODEL, LANES), lambda i: (0, 0, 0)),
                               pl.BlockSpec((1, LANES), const)],
        out_specs=[pl.BlockSpec((step_rows, D_MODEL), row), slots, pl.BlockSpec((step_rows, LANES), row), slots,
                   pl.BlockSpec((1, LANES), const)],
        out_shape=[jax.ShapeDtypeStruct((t, D_MODEL), F32),
                   jax.ShapeDtypeStruct((t // step_rows, SUBLANES, step_rows), jnp.int32),
                   jax.ShapeDtypeStruct((t, LANES), F32),
                   jax.ShapeDtypeStruct((t // step_rows, SUBLANES, step_rows), jnp.int32),
                   jax.ShapeDtypeStruct((1, LANES), jnp.int32)],
        scratch_shapes=[pltpu.VMEM((1, LANES), F32)],
        compiler_params=_params("arbitrary"),
        name="out_router",
    )(*tile_args, w_out_bf, ln_g.reshape(1, D_MODEL), ln_b.reshape(1, D_MODEL), rw, rb)


def _row_copy(src, src_row, dst, dst_row, sem):
    return pltpu.make_async_copy(src.at[pl.ds(src_row, 1), :], dst.at[pl.ds(dst_row, 1), :], sem)


def _dispatch_kernel(ps_ref, pe_ref, dest_ref, x_ref, xb_hbm, zero_ref, sem, zsem, *, tm, bm):
    i = pl.program_id(0)

    def zero_block(row):
        return pltpu.make_async_copy(zero_ref, xb_hbm.at[pl.ds(pl.multiple_of(row, bm), bm), :], zsem)

    def fill(e):
        return zero_block(pe_ref[e] - bm)

    @pl.when(i == 0)
    def _():
        zero_ref[...] = jnp.zeros_like(zero_ref)
        for e in range(N_EXPERTS):
            @pl.when(pe_ref[e] > ps_ref[e])
            def _():
                fill(e).start()
        first_unused = pe_ref[N_EXPERTS - 1] // bm
        n_blocks = xb_hbm.shape[0] // bm
        lax.fori_loop(first_unused, n_blocks, lambda b, c: (zero_block(b * bm).start(), c)[1], 0)
        for e in range(N_EXPERTS):
            @pl.when(pe_ref[e] > ps_ref[e])
            def _():
                fill(e).wait()
        lax.fori_loop(first_unused, n_blocks, lambda b, c: (zero_block(b * bm).wait(), c)[1], 0)

    def issue(r, carry):
        for k in range(TOP_K):
            _row_copy(x_ref, r, xb_hbm, dest_ref[0, 0, r * TOP_K + k], sem).start(priority=k % N_DMA_PRIORITIES)
        return carry

    lax.fori_loop(0, tm, issue, 0)
    pltpu.make_async_copy(xb_hbm.at[pl.ds(0, tm * TOP_K), :], xb_hbm.at[pl.ds(0, tm * TOP_K), :], sem).wait()


def _dispatch(x1, dest, pad_start, pad_end, rows, tm, bm):
    t = x1.shape[0]
    return pl.pallas_call(
        functools.partial(_dispatch_kernel, tm=tm, bm=bm),
        grid_spec=pltpu.PrefetchScalarGridSpec(
            num_scalar_prefetch=2,
            grid=(t // tm,),
            in_specs=[pl.BlockSpec((1, 1, tm * TOP_K), lambda i, ps, pe: (i, 0, 0), memory_space=pltpu.SMEM),
                      pl.BlockSpec((tm, D_MODEL), lambda i, ps, pe: (i, 0))],
            out_specs=pl.BlockSpec(memory_space=pl.ANY),
            scratch_shapes=[pltpu.VMEM((bm, D_MODEL), F32),
                            pltpu.SemaphoreType.DMA(()), pltpu.SemaphoreType.DMA(())]),
        out_shape=jax.ShapeDtypeStruct((rows, D_MODEL), F32),
        compiler_params=_params("arbitrary"),
        name="dispatch",
    )(pad_start, pad_end, dest.reshape(t // tm, 1, tm * TOP_K), x1)


def _expert_kernel(ps_ref, pe_ref, x_ref, wgu_hbm, bgu_ref, wd_hbm, bd_ref, y_ref,
                   wgu_buf, wd_buf, sem, state_ref, *, bm):
    blk = pl.program_id(0)
    row0 = blk * bm

    def fetch(e, s):
        return (pltpu.make_async_copy(wgu_hbm.at[e], wgu_buf.at[s], sem.at[s]),
                pltpu.make_async_copy(wd_hbm.at[e], wd_buf.at[s], sem.at[s]))

    def next_with_rows(e):
        return lax.while_loop(lambda n: (n < N_EXPERTS) & (pe_ref[jnp.minimum(n, N_EXPERTS - 1)] <= row0),
                              lambda n: n + 1, e)

    @pl.when(row0 < pe_ref[N_EXPERTS - 1])
    def _():
        @pl.when(blk == 0)
        def _():
            first = next_with_rows(0)
            state_ref[0] = first
            state_ref[1] = 0
            for copy in fetch(first, 0):
                copy.start()

        @pl.when((blk > 0) & (row0 >= pe_ref[state_ref[0]]))
        def _():
            state_ref[0] = next_with_rows(state_ref[0])
            state_ref[1] = 1 - state_ref[1]

        e, s = state_ref[0], state_ref[1]

        @pl.when(row0 == ps_ref[e])
        def _():
            for copy in fetch(e, s):
                copy.wait()
            nxt = lax.while_loop(lambda n: (n < N_EXPERTS) & (pe_ref[jnp.minimum(n, N_EXPERTS - 1)] <= pe_ref[e]),
                                 lambda n: n + 1, e + 1)

            @pl.when(nxt < N_EXPERTS)
            def _():
                for copy in fetch(nxt, 1 - s):
                    copy.start()

        x = x_ref[...].astype(BF16)
        h = jnp.dot(x, wgu_buf[s].astype(BF16), preferred_element_type=F32) + bgu_ref[pl.ds(e, 1), :]
        gate = jnp.minimum(h[:, :D_FF], SWIGLU_LIMIT)
        up = jnp.clip(h[:, D_FF:], -SWIGLU_LIMIT, SWIGLU_LIMIT)
        act = (up + 1.0) * (gate * jax.nn.sigmoid(SWIGLU_ALPHA * gate))
        y_ref[...] = (jnp.dot(act.astype(BF16), wd_buf[s].astype(BF16), preferred_element_type=F32)
                      + bd_ref[pl.ds(e, 1), :])

    @pl.when(row0 >= pe_ref[N_EXPERTS - 1])
    def _():
        y_ref[...] = jnp.zeros_like(y_ref)


def _experts(xb, pad_start, pad_end, w_gu, b_gu, w_down, b_down, bm):
    rows = xb.shape[0]
    n_blocks = rows // bm
    used = lambda i, ps, pe: (jnp.maximum(jnp.minimum(i, pe[N_EXPERTS - 1] // bm - 1), 0), 0)
    whole = lambda i, ps, pe: (0, 0)
    return pl.pallas_call(
        functools.partial(_expert_kernel, bm=bm),
        grid_spec=pltpu.PrefetchScalarGridSpec(
            num_scalar_prefetch=2,
            grid=(n_blocks,),
            in_specs=[pl.BlockSpec((bm, D_MODEL), used),
                      pl.BlockSpec(memory_space=pl.ANY),
                      pl.BlockSpec((N_EXPERTS, 2 * D_FF), whole),
                      pl.BlockSpec(memory_space=pl.ANY),
                      pl.BlockSpec((N_EXPERTS, D_MODEL), whole)],
            out_specs=pl.BlockSpec((bm, D_MODEL), lambda i, ps, pe: (i, 0)),
            scratch_shapes=[pltpu.VMEM((2, D_MODEL, 2 * D_FF), F32), pltpu.VMEM((2, D_FF, D_MODEL), F32),
                            pltpu.SemaphoreType.DMA((2,)), pltpu.SMEM((2,), jnp.int32)]),
        out_shape=jax.ShapeDtypeStruct((rows, D_MODEL), F32),
        compiler_params=_params("arbitrary"),
        name="experts",
    )(pad_start, pad_end, xb, w_gu, b_gu, w_down, b_down)


def _combine_ln_kernel(dcur_ref, dnext_ref, x_ref, gate_ref, g_ref, b_ref, yb_hbm, op_ref, os_ref, ybuf, sem, *,
                       tm, n_prompt):
    i = pl.program_id(0)
    slot = i % 2

    def issue(dref, s):
        def body(r, carry):
            for k in range(TOP_K):
                _row_copy(yb_hbm, dref[0, 0, r * TOP_K + k], ybuf.at[s, k], r,
                          sem.at[s]).start(priority=k % N_DMA_PRIORITIES)
            return carry
        lax.fori_loop(0, tm, body, 0)

    @pl.when(i == 0)
    def _():
        issue(dcur_ref, 0)

    for s in range(2):
        @pl.when((i + 1 < pl.num_programs(0)) & (slot != s))
        def _():
            issue(dnext_ref, s)

    pltpu.make_async_copy(ybuf.at[slot], ybuf.at[slot], sem.at[slot]).wait()
    gates = gate_ref[...]
    y = sum(gates[:, k:k + 1] * ybuf[slot, k] for k in range(TOP_K))
    out = _layernorm(DEEPNORM_ALPHA * x_ref[...] + y, g_ref[...], b_ref[...])

    @pl.when(i < n_prompt)
    def _():
        op_ref[...] = out

    @pl.when(i >= n_prompt)
    def _():
        os_ref[...] = out


def _combine_ln(x1, gates, dest, yb, ln_g, ln_b, tm, t_prompt):
    t = x1.shape[0]
    n = t // tm
    n_prompt = t_prompt // tm
    row = lambda i: (i, 0)
    const = lambda i: (0, 0)
    d2 = dest.reshape(n, 1, tm * TOP_K)
    return pl.pallas_call(
        functools.partial(_combine_ln_kernel, tm=tm, n_prompt=n_prompt),
        grid=(n,),
        in_specs=[pl.BlockSpec((1, 1, tm * TOP_K), lambda i: (i, 0, 0), memory_space=pltpu.SMEM),
                  pl.BlockSpec((1, 1, tm * TOP_K), lambda i: (jnp.minimum(i + 1, n - 1), 0, 0),
                               memory_space=pltpu.SMEM),
                  pl.BlockSpec((tm, D_MODEL), row),
                  pl.BlockSpec((tm, LANES), row),
                  pl.BlockSpec((1, D_MODEL), const), pl.BlockSpec((1, D_MODEL), const),
                  pl.BlockSpec(memory_space=pl.ANY)],
        out_specs=[pl.BlockSpec((tm, D_MODEL), lambda i: (jnp.minimum(i, n_prompt - 1), 0)),
                   pl.BlockSpec((tm, D_MODEL), lambda i: (jnp.maximum(i - n_prompt, 0), 0))],
        out_shape=[jax.ShapeDtypeStruct((t_prompt, D_MODEL), F32),
                   jax.ShapeDtypeStruct((t - t_prompt, D_MODEL), F32)],
        scratch_shapes=[pltpu.VMEM((2, TOP_K, tm, D_MODEL), F32), pltpu.SemaphoreType.DMA((2,))],
        compiler_params=_params("arbitrary"),
        name="combine_ln",
    )(d2, d2, x1, gates, ln_g.reshape(1, D_MODEL), ln_b.reshape(1, D_MODEL), yb)


MOE_BM = 512
ATTN_BLK = 512
QUERY_PART = 256
GDN_SUB = 4
ROUTE_TM = 128


def _mixers(x2d, bsz, seq, cache_k, cache_v, conv_past, s0, lam, lam_init, w_in_bf, conv_w, a_log, dt_bias,
            delta_norm_w, subln_w):
    prompt = cache_k is None
    tm = ATTN_BLK if prompt else x2d.shape[0]
    past8 = jnp.pad(conv_past, ((0, 0), (SUBLANES - (CONV_W - 1), 0), (0, 0)))
    q_bf, k_f, v_f, k_bf, v_bf, cin, z, ab, *ctail = _in_proj(x2d, w_in_bf, past8, conv_w, tm,
                                                              seq // tm if prompt else 1, prompt)
    if prompt:
        att = _attn_prompt(q_bf, k_bf, v_bf, lam, subln_w, bsz, seq, ATTN_BLK, lam_init)
        conv_new = ctail[0][:, SUBLANES - (CONV_W - 1):]
    else:
        att = _attn_sample(q_bf, k_bf, v_bf, cache_k, cache_v, 0, lam, subln_w.reshape(1, HEAD), bsz, seq, lam_init)
        conv_new = cin.reshape(bsz, seq, CONV_CH)[:, seq - (CONV_W - 1):]
    chunk = CHUNK if seq % CHUNK == 0 else seq
    n_sub = GDN_SUB if (seq // chunk) % GDN_SUB == 0 else 1
    u, w, qd, kd, qk, gl = _gdn_prep(cin, ab, past8, conv_w, a_log, dt_bias, bsz, seq, chunk, n_sub, prompt)
    o, s_new = _gdn_scan(u, w, qd, kd, qk, gl, z, s0, delta_norm_w.reshape(1, HEAD), bsz, seq, chunk, n_sub)
    return att, o, k_f, v_f, conv_new, s_new


def _moe(x1, idx, gates, rank, counts, w_gu, b_gu, w_down, b_down, ln_g, ln_b, bm, tm, t_prompt):
    t = x1.shape[0]
    n = t * TOP_K
    counts = counts[0, :N_EXPERTS]
    padded = (counts + bm - 1) // bm * bm
    pad_end = jnp.cumsum(padded).astype(jnp.int32)
    pad_start = (pad_end - padded).astype(jnp.int32)
    idx = idx[:, :TOP_K, :]
    dest = rank[:, :TOP_K, :] + sum(jnp.where(idx == e, pad_start[e], 0) for e in range(N_EXPERTS))
    dest = dest.transpose(0, 2, 1).reshape(t, TOP_K)
    n_blocks = -(-n // bm) + N_EXPERTS
    xb = _dispatch(x1, dest, pad_start, pad_end, n_blocks * bm, tm, bm)
    yb = _experts(xb, pad_start, pad_end, w_gu, b_gu, w_down, b_down, bm)
    return _combine_ln(x1, gates, dest, yb, ln_g, ln_b, ROUTE_TM, t_prompt)


def kernel(x_prompt, x_sample, cache_k, cache_v, state_conv, state_delta, w_in, conv_w, a_log, dt_bias,
           delta_norm_w, lambda_q1, lambda_k1, lambda_q2, lambda_k2, subln_w, w_out, ln1_g, ln1_b,
           router_w, router_b, w_gu, b_gu, w_down, b_down, ln2_g, ln2_b):
    bp, lp, _ = x_prompt.shape
    bs, ls, _ = x_sample.shape
    l = 0
    lam_init = 0.8 - 0.6 * math.exp(-0.3 * l)
    lam = (jnp.exp(jnp.sum(lambda_q1[l] * lambda_k1[l])) - jnp.exp(jnp.sum(lambda_q2[l] * lambda_k2[l]))
           + lam_init).reshape(1).astype(F32)
    w_in_bf = jnp.pad(w_in[l], ((0, 0), (0, IN_COLS_PAD - IN_COLS))).astype(BF16)
    shared = (lam, lam_init, w_in_bf, conv_w[l], a_log[l], dt_bias[l], delta_norm_w[l], subln_w[l])

    xp = x_prompt.reshape(bp * lp, D_MODEL)
    xs = x_sample.reshape(bs * ls, D_MODEL)
    zero_conv = jnp.zeros((bp, CONV_W - 1, CONV_CH), F32)
    zero_s = jnp.zeros((bp, N_HEADS, HEAD, HEAD), F32)
    att_p, o_p, k_p, v_p, cin_p, s_p = _mixers(xp, bp, lp, None, None, zero_conv, zero_s, *shared)
    att_s, o_s, k_s, v_s, cin_s, s_s = _mixers(xs, bs, ls, cache_k, cache_v, state_conv[l],
                                               state_delta[l], *shared)

    n_sub = next(c for c in (4, 3, 2, 1) if ((bp * lp + bs * ls) // ROUTE_TM) % c == 0)
    x1, idx, gates, rank, counts = _out_router(((att_p, o_p, xp), (att_s, o_s, xs)), w_out[l].astype(BF16),
                                               ln1_g[l], ln1_b[l], router_w[l], router_b[l], ROUTE_TM, n_sub)
    tp = bp * lp
    tm = n_sub * ROUTE_TM
    y_p, y_s = _moe(x1, idx, gates, rank, counts, w_gu[l], b_gu[l], w_down[l], b_down[l], ln2_g[l], ln2_b[l],
                    MOE_BM, tm, tp)
    return (y_p.reshape(bp, lp, D_MODEL), y_s.reshape(bs, ls, D_MODEL),
            k_p.reshape(1, bp, lp, N_HEADS, HEAD), v_p.reshape(1, bp, lp, N_HEADS, HEAD),
            cin_p[None], s_p[None].astype(state_delta.dtype),
            k_s.reshape(1, bs, ls, N_HEADS, HEAD), v_s.reshape(1, bs, ls, N_HEADS, HEAD),
            cin_s[None], s_s[None].astype(state_delta.dtype))
```

```python
import functools
import math

import jax
import jax.numpy as jnp
from jax import lax
from jax.experimental import pallas as pl
from jax.experimental.pallas import tpu as pltpu

F32 = jnp.float32
BF16 = jnp.bfloat16

D_MODEL = 1024
HEAD = 128
N_HEADS = 4
DQK = HEAD // 2
GROUP = N_HEADS * HEAD
CONV_W = 4
CONV_CH = 3 * GROUP
CHUNK = 64
ALIBI_MAX = 8.0
N_EXPERTS = 32
TOP_K = 4
D_FF = D_MODEL
SWIGLU_LIMIT = 7.0
SWIGLU_ALPHA = 1.702
DEPTH = 1
DEEPNORM_ALPHA = (2 * DEPTH) ** 0.25
LN_EPS = 1e-5
SUBLN_EPS = 1e-5
GATED_NORM_EPS = 1e-6
L2_EPS = 1e-6

LANES = 128
SUBLANES = 8
BF16_EXACT_INT = 256
BF16_ROWS = 16
LOG2E = 1.4426950408889634
N_POS = 6
ONES_ROWS = BF16_ROWS
N_DMA_PRIORITIES = 2
VMEM_LIMIT = 56 * 1024 * 1024

COL_Q, COL_K, COL_V, COL_CONV = 0, GROUP, 2 * GROUP, 3 * GROUP
COL_Z = COL_CONV + CONV_CH
COL_AB = COL_Z + GROUP
IN_COLS = COL_AB + 2 * N_HEADS
IN_COLS_PAD = COL_AB + LANES


def _params(*sem):
    return pltpu.CompilerParams(dimension_semantics=sem, vmem_limit_bytes=VMEM_LIMIT)


def _conv_activation(prev, raw, cw_ref):
    rows = raw.shape[0]
    xin = jnp.concatenate([prev, raw], axis=0)
    conv = sum(xin[SUBLANES - (CONV_W - 1) + j: SUBLANES - (CONV_W - 1) + j + rows] * cw_ref[j:j + 1, :]
               for j in range(CONV_W))
    conv = conv * jax.nn.sigmoid(conv)
    parts = []
    for h in range(2 * N_HEADS):
        x = conv[:, h * HEAD:(h + 1) * HEAD]
        scale = HEAD ** -0.5 if h < N_HEADS else 1.0
        parts.append(x * (lax.rsqrt(jnp.sum(x * x, axis=-1, keepdims=True) + L2_EPS) * scale))
    return jnp.concatenate(parts + [conv[:, 2 * GROUP:]], axis=1)


def _in_proj_kernel(x_ref, w_ref, past_ref, cw_ref, q_ref, kf_ref, vf_ref, kb_ref, vb_ref, c_ref, z_ref, ab_ref,
                    *conv_refs, tm, tiles_per_stream, prompt):
    xb = x_ref[...].astype(BF16)

    def section(lo, hi):
        return jnp.dot(xb, w_ref[:, lo:hi], preferred_element_type=F32)

    q = section(COL_Q, COL_K) * (DQK ** -0.5 * (LOG2E if prompt else 1.0))
    k = section(COL_K, COL_V)
    kf_ref[0] = k.reshape(tm, N_HEADS, HEAD)
    kb_ref[...] = k.astype(BF16)
    v = section(COL_V, COL_CONV)
    vf_ref[0] = v.reshape(tm, N_HEADS, HEAD)
    raw = section(COL_CONV, COL_Z)
    if prompt:
        ctail_ref, tail_ref = conv_refs
        q_ref[0] = q.T.astype(BF16)
        vb_ref[0] = v.T.astype(BF16)
        prev = jnp.where(pl.program_id(0) % tiles_per_stream == 0, past_ref[0], tail_ref[...])
        c_ref[...] = _conv_activation(prev, raw, cw_ref)
        tail_ref[...] = raw[tm - SUBLANES:, :]
        ctail_ref[0] = raw[tm - SUBLANES:, :]
    else:
        q_ref[...] = q.astype(BF16)
        vb_ref[...] = v.astype(BF16)
        c_ref[...] = raw
    z_ref[...] = section(COL_Z, COL_AB)
    ab_ref[...] = section(COL_AB, IN_COLS_PAD)


def _in_proj(x2d, w_bf, past8, conv_w, tm, tiles_per_stream, prompt):
    t = x2d.shape[0]
    row = lambda i: (i, 0)
    stream = lambda i: (i // tiles_per_stream, 0, 0)
    widths = (GROUP, GROUP, GROUP, GROUP, GROUP, CONV_CH, GROUP, LANES)
    dtypes = (BF16, F32, F32, BF16, BF16, F32, F32, F32)
    out_specs = [pl.BlockSpec((tm, w), row) for w in widths]
    out_shape = [jax.ShapeDtypeStruct((t, w), d) for w, d in zip(widths, dtypes)]
    for slot in (1, 2):
        out_specs[slot] = pl.BlockSpec((1, tm, N_HEADS, HEAD), lambda i: (i, 0, 0, 0))
        out_shape[slot] = jax.ShapeDtypeStruct((t // tm, tm, N_HEADS, HEAD), F32)
    if prompt:
        for slot in (0, 4):
            out_specs[slot] = pl.BlockSpec((1, GROUP, tm), lambda i: (i, 0, 0))
            out_shape[slot] = jax.ShapeDtypeStruct((t // tm, GROUP, tm), BF16)
        out_specs.append(pl.BlockSpec((1, SUBLANES, CONV_CH), stream))
        out_shape.append(jax.ShapeDtypeStruct((past8.shape[0], SUBLANES, CONV_CH), F32))
    return pl.pallas_call(
        functools.partial(_in_proj_kernel, tm=tm, tiles_per_stream=tiles_per_stream, prompt=prompt),
        grid=(t // tm,),
        in_specs=[pl.BlockSpec((tm, D_MODEL), row),
                  pl.BlockSpec((D_MODEL, IN_COLS_PAD), lambda i: (0, 0)),
                  pl.BlockSpec((1, SUBLANES, CONV_CH), stream),
                  pl.BlockSpec((CONV_W, CONV_CH), lambda i: (0, 0))],
        out_specs=out_specs,
        out_shape=out_shape,
        scratch_shapes=[pltpu.VMEM((SUBLANES, CONV_CH), F32)] if prompt else [],
        compiler_params=_params("arbitrary"),
        name="in_proj",
    )(x2d, w_bf, past8, conv_w)


def _alibi_slopes():
    return [2.0 ** (-ALIBI_MAX * (h + 1) / N_HEADS) for h in range(N_HEADS)]


def _stack_halves(q):
    lane = lax.broadcasted_iota(jnp.int32, q.shape, 1)
    zero = jnp.zeros_like(q)
    return jnp.concatenate([jnp.where(lane < DQK, q, zero), jnp.where(lane < DQK, zero, q)], axis=0)


def _diff_norm(acc, l, lam, w, lam_init, rows):
    o = acc[:rows] / l[:rows] - lam * (acc[rows:] / l[rows:])
    ms = jnp.mean(o * o, axis=-1, keepdims=True)
    return o * lax.rsqrt(ms + SUBLN_EPS) * w * (1.0 - lam_init)


def _head_slope(h):
    s = _alibi_slopes()
    return jnp.where(h == 0, s[0], jnp.where(h == 1, s[1], jnp.where(h == 2, s[2], s[3]))).astype(F32)


def _attn_prompt_kernel(lam_ref, qt_ref, k_ref, vt_ref, w_ref, o_ref,
                        diag_ref, kaug_ref, vaug_ref, qz_ref, s0_ref, s1_ref, p0_ref, p1_ref, mx0_ref, mx1_ref, m_ref, acc_ref, *, blk, lam_init):
    i = pl.program_id(2)
    slope = _head_slope(pl.program_id(1)) * LOG2E
    rows = 2 * blk
    n_kv = k_ref.shape[0] // blk

    def pieces(x):
        lo = x % BF16_EXACT_INT
        return _split3(slope * lo.astype(F32)) + _split3(slope * (x - lo).astype(F32))

    @pl.when(i == 0)
    def _():
        lane = lax.broadcasted_iota(jnp.int32, (blk, HEAD), 1)
        extra = jnp.where(lane < N_POS, 1.0, 0.0).astype(BF16)
        for n, piece in enumerate(pieces(lax.broadcasted_iota(jnp.int32, (blk, 1), 0))):
            extra = jnp.where(lane == N_POS + n, piece, extra)
        ones = jnp.ones((ONES_ROWS, blk), BF16)
        c = lax.broadcasted_iota(jnp.int32, (blk, rows), 0)
        a = lax.broadcasted_iota(jnp.int32, (blk, rows), 1) % blk
        diag_ref[...] = jnp.where(c // CHUNK <= a // CHUNK, -slope * jnp.abs(a - c).astype(F32), -jnp.inf)
        r = lax.broadcasted_iota(jnp.int32, (HEAD, rows), 0)
        qx = jnp.where((r >= N_POS) & (r < 2 * N_POS), 1.0, 0.0).astype(BF16)
        for n, piece in enumerate(pieces(-(lax.broadcasted_iota(jnp.int32, (1, rows), 1) % blk))):
            qx = jnp.where(r == n, piece, qx)
        qz_ref[HEAD:, :] = qx

        def fill(j, carry):
            j0 = pl.multiple_of(j * blk, blk)
            kaug_ref[pl.ds(j0, blk), :HEAD] = k_ref[pl.ds(j0, blk), :]
            kaug_ref[pl.ds(j0, blk), HEAD:] = extra
            vaug_ref[j, :HEAD, :] = vt_ref[j]
            vaug_ref[j, HEAD:, :] = ones
            return carry

        lax.fori_loop(0, n_kv, fill, 0)

    qt = qt_ref[0]
    d = lax.broadcasted_iota(jnp.int32, qt.shape, 0)
    zero = jnp.zeros_like(qt)
    qz_ref[:HEAD, :] = jnp.concatenate([jnp.where(d < DQK, qt, zero), jnp.where(d < DQK, zero, qt)], axis=1)

    def tile_rows(j):
        return pl.ds(pl.multiple_of(j * blk, blk), blk)

    acc_ref[...] = jnp.zeros_like(acc_ref)
    for part in range(rows // QUERY_PART):
        cols = slice(part * QUERY_PART, (part + 1) * QUERY_PART)
        s = jnp.dot(k_ref[tile_rows(i), :], qz_ref[:HEAD, cols], preferred_element_type=F32) + diag_ref[:, cols]
        m = jnp.max(s, axis=0, keepdims=True)
        p1_ref[:, cols] = jnp.exp2(s - m).astype(BF16)
        m_ref[:, cols] = m
        s_first = jnp.dot(kaug_ref[tile_rows(0), :], qz_ref[:, cols], preferred_element_type=F32)
        s0_ref[:, cols] = s_first
        mx0_ref[:, cols] = jnp.max(s_first, axis=0, keepdims=True)

    def step(j, s_cur, s_nxt, p_cur, p_nxt, mx_cur, mx_nxt):
        shift = -slope * ((i - j) * blk).astype(F32)
        k_next = kaug_ref[tile_rows(jnp.minimum(j + 1, i - 1)), :]
        v_prev = vaug_ref[jnp.where(j == 0, i, j - 1)]
        for part in range(rows // QUERY_PART):
            cols = slice(part * QUERY_PART, (part + 1) * QUERY_PART)
            s_next = jnp.dot(k_next, qz_ref[:, cols], preferred_element_type=F32)
            s_nxt[:, cols] = s_next
            mx_nxt[:, cols] = jnp.max(s_next, axis=0, keepdims=True)
            pv = jnp.dot(v_prev, p_nxt[:, cols], preferred_element_type=F32)
            m_old = m_ref[:, cols]
            m_new = jnp.maximum(m_old, mx_cur[:, cols] + shift)
            p_cur[:, cols] = jnp.exp2(s_cur[:, cols] - (m_new - shift)).astype(BF16)
            acc_ref[:, cols] = (acc_ref[:, cols] + pv) * jnp.exp2(m_old - m_new)
            m_ref[:, cols] = m_new

    def body(jj, carry):
        step(2 * jj, s0_ref, s1_ref, p0_ref, p1_ref, mx0_ref, mx1_ref)

        @pl.when(2 * jj + 1 < i)
        def _():
            step(2 * jj + 1, s1_ref, s0_ref, p1_ref, p0_ref, mx1_ref, mx0_ref)

        return carry

    lax.fori_loop(0, (i + 1) // 2, body, 0)
    p_last = jnp.where(i % 2 == 1, p0_ref[...], p1_ref[...])
    acc = acc_ref[...] + jnp.dot(vaug_ref[jnp.where(i > 0, i - 1, i)], p_last, preferred_element_type=F32)
    l = acc[HEAD:HEAD + 1, :]
    num = acc[:HEAD, :]
    ot = num[:, :blk] / l[:, :blk] - lam_ref[0] * (num[:, blk:] / l[:, blk:])
    ms = jnp.mean(ot * ot, axis=0, keepdims=True)
    ot = ot * lax.rsqrt(ms + SUBLN_EPS) * w_ref[...] * (1.0 - lam_init)
    o_ref[...] = ot.T.astype(o_ref.dtype)


def _attn_prompt(qt_bf, k_bf, vt_bf, lam, subln_w, bsz, seq, blk, lam_init):
    nq = seq // blk
    kern = functools.partial(_attn_prompt_kernel, blk=blk, lam_init=lam_init)
    return pl.pallas_call(
        kern,
        grid=(bsz, N_HEADS, nq),
        in_specs=[pl.BlockSpec(memory_space=pltpu.SMEM),
                  pl.BlockSpec((1, HEAD, blk), lambda b, h, i: (b * nq + i, h, 0)),
                  pl.BlockSpec((seq, HEAD), lambda b, h, i: (b, h)),
                  pl.BlockSpec((nq, HEAD, blk), lambda b, h, i: (b, h, 0)),
                  pl.BlockSpec((HEAD, 1), lambda b, h, i: (0, 0))],
        out_specs=pl.BlockSpec((blk, HEAD), lambda b, h, i: (b * nq + i, h)),
        scratch_shapes=[pltpu.VMEM((blk, 2 * blk), F32),
                        pltpu.VMEM((seq, 2 * HEAD), BF16),
                        pltpu.VMEM((nq, HEAD + ONES_ROWS, blk), BF16),
                        pltpu.VMEM((2 * HEAD, 2 * blk), BF16),
                        pltpu.VMEM((blk, 2 * blk), F32), pltpu.VMEM((blk, 2 * blk), F32),
                        pltpu.VMEM((blk, 2 * blk), BF16), pltpu.VMEM((blk, 2 * blk), BF16),
                        pltpu.VMEM((1, 2 * blk), F32), pltpu.VMEM((1, 2 * blk), F32),
                        pltpu.VMEM((1, 2 * blk), F32),
                        pltpu.VMEM((HEAD + ONES_ROWS, 2 * blk), F32)],
        out_shape=jax.ShapeDtypeStruct((bsz * seq, GROUP), BF16),
        compiler_params=_params("parallel", "parallel", "arbitrary"),
        name="attn_prompt",
    )(lam, qt_bf, k_bf, vt_bf, subln_w.reshape(HEAD, 1))


def _attn_sample_kernel(lam_ref, q_ref, kn_ref, vn_ref, kc_ref, vc_ref, w_ref, o_ref, *, seq, past, lam_init):
    nt = (((1,), (1,)), ((), ()))
    per_head = 2 * seq
    qz = jnp.concatenate([_stack_halves(q_ref[:, h * HEAD:(h + 1) * HEAD]) for h in range(N_HEADS)], axis=0)
    row = lax.broadcasted_iota(jnp.int32, (N_HEADS * per_head, 1), 0)
    q_head = row // per_head
    q_pos = past + row % seq
    slopes = _alibi_slopes()
    slope = jnp.where(q_head == 0, slopes[0], jnp.where(q_head == 1, slopes[1],
                                                        jnp.where(q_head == 2, slopes[2], slopes[3]))).astype(F32)

    def scores(keys, first_pos):
        n = lax.broadcasted_iota(jnp.int32, (1, keys.shape[0]), 1)
        s = lax.dot_general(qz, keys, nt, preferred_element_type=F32)
        bias = -slope * jnp.abs(q_pos - (first_pos + n // N_HEADS)).astype(F32)
        return jnp.where(n % N_HEADS == q_head, s + bias, -jnp.inf)

    s_c = scores(kc_ref[0, 0].reshape(past * N_HEADS, HEAD).astype(BF16), 0)
    s_n = scores(kn_ref[...].reshape(seq * N_HEADS, HEAD), past)
    m = jnp.maximum(jnp.max(s_c, axis=-1, keepdims=True), jnp.max(s_n, axis=-1, keepdims=True))
    p_c = jnp.exp(s_c - m)
    p_n = jnp.exp(s_n - m)
    l = jnp.sum(p_c, axis=-1, keepdims=True) + jnp.sum(p_n, axis=-1, keepdims=True)
    acc = (jnp.dot(p_c.astype(BF16), vc_ref[0, 0].reshape(past * N_HEADS, HEAD).astype(BF16),
                   preferred_element_type=F32)
           + jnp.dot(p_n.astype(BF16), vn_ref[...].reshape(seq * N_HEADS, HEAD), preferred_element_type=F32))
    for h in range(N_HEADS):
        rows = slice(h * per_head, (h + 1) * per_head)
        o_ref[:, h * HEAD:(h + 1) * HEAD] = _diff_norm(acc[rows], l[rows], lam_ref[0], w_ref[...], lam_init,
                                                       seq).astype(o_ref.dtype)


def _attn_sample(q_bf, k_bf, v_bf, cache_k, cache_v, layer, lam, subln_w, bsz, seq, lam_init):
    past = cache_k.shape[2]
    kern = functools.partial(_attn_sample_kernel, seq=seq, past=past, lam_init=lam_init)
    new = pl.BlockSpec((seq, GROUP), lambda b: (b, 0))
    cache = pl.BlockSpec((1, 1, past, N_HEADS, HEAD), lambda b: (layer, b, 0, 0, 0))
    return pl.pallas_call(
        kern,
        grid=(bsz,),
        in_specs=[pl.BlockSpec(memory_space=pltpu.SMEM), new, new, new, cache, cache,
                  pl.BlockSpec((1, HEAD), lambda b: (0, 0))],
        out_specs=new,
        out_shape=jax.ShapeDtypeStruct((bsz * seq, GROUP), BF16),
        compiler_params=_params("parallel"),
        name="attn_sample",
    )(lam, q_bf, k_bf, v_bf, cache_k, cache_v, subln_w)


def _split3(x):
    hi = x.astype(BF16)
    r1 = x - hi.astype(F32)
    mid = r1.astype(BF16)
    lo = (r1 - mid.astype(F32)).astype(BF16)
    return hi, mid, lo


def _gdn_prep_kernel(cin_ref, halo_ref, past_ref, ab_ref, cw_ref, alog_ref, dtb_ref,
                     u_ref, w_ref, qd_ref, kd_ref, qk_ref, gl_ref, *, chunk, n_sub, n_grp, activated):
    c_idx = pl.program_id(1)
    rows = chunk * n_sub
    if activated:
        conv = cin_ref[...]
    else:
        prev = jnp.where(c_idx == 0, past_ref[0], halo_ref[...])
        conv = _conv_activation(prev, cin_ref[...], cw_ref)

    ab = ab_ref[...]
    lane = lax.broadcasted_iota(jnp.int32, ab.shape, 1)
    pre = ab + dtb_ref[...]
    softplus = jnp.maximum(pre, 0.0) + jnp.log(1.0 + jnp.exp(-jnp.abs(pre)))
    g = jnp.where(lane < N_HEADS, -jnp.exp(alog_ref[...]) * softplus, 0.0)
    beta_all = jax.nn.sigmoid(ab)

    ri = lax.broadcasted_iota(jnp.int32, (rows, rows), 0)
    ci = lax.broadcasted_iota(jnp.int32, (rows, rows), 1)
    same = (ri // chunk) == (ci // chunk)
    incl = same & (ri >= ci)
    strict = same & (ri > ci)
    eye = jnp.where(ri == ci, 1.0, 0.0).astype(F32)
    nt = (((1,), (1,)), ((), ()))
    ones_incl = jnp.where(incl, 1.0, 0.0).astype(BF16)
    ones_same = jnp.where(same, 1.0, 0.0).astype(BF16)

    units, t_mats, p_mats, rhs = [], [], [], []
    for grp in range(n_grp):
        rws = slice(grp * rows, (grp + 1) * rows)
        g_parts = _split3(g[rws])
        gc = sum(jnp.dot(ones_incl, part, preferred_element_type=F32) for part in g_parts)
        g_end = sum(jnp.dot(ones_same, part, preferred_element_type=F32) for part in g_parts)
        gct = gc.T
        for sc in range(n_sub):
            gl_ref[grp * n_sub + sc] = g_end[sc * chunk:sc * chunk + 1, :]
        for h in range(N_HEADS):
            col = slice(h * HEAD, (h + 1) * HEAD)
            qh = conv[rws, h * HEAD:(h + 1) * HEAD]
            kh = conv[rws, GROUP + h * HEAD:GROUP + (h + 1) * HEAD]
            vh = conv[rws, 2 * GROUP + h * HEAD:2 * GROUP + (h + 1) * HEAD]
            beta = beta_all[rws, N_HEADS + h:N_HEADS + h + 1]
            gcol = gc[:, h:h + 1]
            grow = gct[h:h + 1, :]
            gamma = jnp.exp(jnp.where(incl, gcol - grow, -jnp.inf))
            egc = jnp.exp(gcol)
            kb = kh * beta
            khb = kh.astype(BF16)
            a = jnp.where(strict, lax.dot_general(kb.astype(BF16), khb, nt, preferred_element_type=F32) * gamma,
                          0.0)
            qk = (lax.dot_general(qh.astype(BF16), khb, nt, preferred_element_type=F32) * gamma).astype(BF16)
            for sc in range(n_sub):
                blk = slice(sc * chunk, (sc + 1) * chunk)
                qk_ref[h, grp * rows + sc * chunk:grp * rows + (sc + 1) * chunk, :] = qk[blk, blk]
            qd_ref[rws, col] = (qh * egc).astype(BF16)
            kd_ref[rws, col] = (kh * jnp.exp(g_end[:, h:h + 1] - gcol)).astype(BF16)
            units.append((rws, col))
            t_mats.append(eye - a)
            p_mats.append(a)
            rhs.append(((vh * beta).astype(BF16), (kb * egc).astype(BF16)))

    for _ in range(int(math.log2(chunk)) - 1):
        for n in range(len(units)):
            pb = p_mats[n].astype(BF16)
            p_mats[n] = jnp.dot(pb, pb, preferred_element_type=F32)
        for n in range(len(units)):
            t_mats[n] = t_mats[n] + jnp.dot(t_mats[n].astype(BF16), p_mats[n].astype(BF16),
                                            preferred_element_type=F32)

    for n, (rws, col) in enumerate(units):
        t_inv = t_mats[n].astype(BF16)
        u_ref[rws, col] = jnp.dot(t_inv, rhs[n][0], preferred_element_type=F32)
        w_ref[rws, col] = jnp.dot(t_inv, rhs[n][1], preferred_element_type=F32).astype(BF16)


def _gdn_prep(cin, ab, past8, conv_w, a_log, dt_bias, bsz, seq, chunk, n_sub, n_grp, activated):
    rows = chunk * n_sub * n_grp
    nblk = seq // rows
    t = bsz * seq
    lanes = lambda v: jnp.pad(v.reshape(1, N_HEADS).astype(F32), ((0, 0), (0, LANES - N_HEADS)))
    kern = functools.partial(_gdn_prep_kernel, chunk=chunk, n_sub=n_sub, n_grp=n_grp, activated=activated)
    rowblk = lambda b, c: (b * nblk + c, 0)
    halo = lambda b, c: (jnp.maximum((b * nblk + c) * (rows // SUBLANES) - 1, 0), 0)
    const = lambda b, c: (0, 0)
    return pl.pallas_call(
        kern,
        grid=(bsz, nblk),
        in_specs=[pl.BlockSpec((rows, CONV_CH), rowblk),
                  pl.BlockSpec((SUBLANES, CONV_CH), halo),
                  pl.BlockSpec((1, SUBLANES, CONV_CH), lambda b, c: (b, 0, 0)),
                  pl.BlockSpec((rows, LANES), rowblk),
                  pl.BlockSpec((CONV_W, CONV_CH), const),
                  pl.BlockSpec((1, LANES), const),
                  pl.BlockSpec((1, LANES), const)],
        out_specs=[pl.BlockSpec((rows, GROUP), rowblk),
                   pl.BlockSpec((rows, GROUP), rowblk),
                   pl.BlockSpec((rows, GROUP), rowblk),
                   pl.BlockSpec((rows, GROUP), rowblk),
                   pl.BlockSpec((N_HEADS, rows, chunk), lambda b, c: (0, b * nblk + c, 0)),
                   pl.BlockSpec((n_sub * n_grp, 1, LANES), lambda b, c: (b * nblk + c, 0, 0))],
        out_shape=[jax.ShapeDtypeStruct((t, GROUP), F32),
                   jax.ShapeDtypeStruct((t, GROUP), BF16),
                   jax.ShapeDtypeStruct((t, GROUP), BF16),
                   jax.ShapeDtypeStruct((t, GROUP), BF16),
                   jax.ShapeDtypeStruct((N_HEADS, t, chunk), BF16),
                   jax.ShapeDtypeStruct((t // chunk, 1, LANES), F32)],
        compiler_params=_params("parallel", "parallel"),
        name="gdn_prep",
    )(cin, cin, past8, ab, conv_w, lanes(a_log), lanes(dt_bias))


def _gdn_scan_kernel(u_ref, w_ref, qd_ref, kd_ref, qk_ref, gl_ref, z_ref, s0_ref, nw_ref,
                     o_ref, sf_ref, s_ref, *, bsz, chunk, n_c):
    c_idx = pl.program_id(0)

    @pl.when(c_idx == 0)
    def _():
        s_ref[...] = s0_ref[...]

    tn = (((0,), (0,)), ((), ()))
    chains = [(b, h, slice(h * HEAD, (h + 1) * HEAD)) for b in range(bsz) for h in range(N_HEADS)]
    states = [s_ref[b, h] for b, h, _ in chains]
    for cc in range(n_c):
        rows = slice(cc * chunk, (cc + 1) * chunk)
        prods = [jnp.dot(jnp.concatenate([w_ref[b, rows, col], qd_ref[b, rows, col]], axis=0), s.astype(BF16),
                         preferred_element_type=F32) for (b, _, col), s in zip(chains, states)]
        v_news = [(u_ref[b, rows, col] - r[:chunk]).astype(BF16) for (b, _, col), r in zip(chains, prods)]
        outs = [r[chunk:] + jnp.dot(qk_ref[h, b, rows], v, preferred_element_type=F32)
                for (b, h, _), r, v in zip(chains, prods, v_news)]
        decays = [jnp.exp(gl_ref[b, cc]) for b in range(bsz)]
        states = [s * decays[b][:, h:h + 1] + lax.dot_general(kd_ref[b, rows, col], v, tn,
                                                             preferred_element_type=F32)
                  for (b, h, col), s, v in zip(chains, states, v_news)]
        for (b, h, col), o in zip(chains, outs):
            zh = z_ref[b, rows, col]
            ms = jnp.mean(o * o, axis=-1, keepdims=True)
            o = o * lax.rsqrt(ms + GATED_NORM_EPS) * nw_ref[...] * (zh * jax.nn.sigmoid(zh))
            o_ref[b, rows, col] = o.astype(o_ref.dtype)
    for (b, h, _), s in zip(chains, states):
        s_ref[b, h] = s

    @pl.when(c_idx == pl.num_programs(0) - 1)
    def _():
        sf_ref[...] = s_ref[...]


def _gdn_scan(u, w, qd, kd, qk, gl, z, s0, norm_w, bsz, seq, chunk, n_c):
    nc = seq // chunk
    kern = functools.partial(_gdn_scan_kernel, bsz=bsz, chunk=chunk, n_c=n_c)
    tok = pl.BlockSpec((bsz, n_c * chunk, GROUP), lambda c: (0, c, 0))
    state = pl.BlockSpec((bsz, N_HEADS, HEAD, HEAD), lambda c: (0, 0, 0, 0))
    o, s_final = pl.pallas_call(
        kern,
        grid=(nc // n_c,),
        in_specs=[tok, tok, tok, tok,
                  pl.BlockSpec((N_HEADS, bsz, n_c * chunk, chunk), lambda c: (0, 0, c, 0)),
                  pl.BlockSpec((bsz, n_c, 1, LANES), lambda c: (0, c, 0, 0)),
                  tok, state,
                  pl.BlockSpec((1, HEAD), lambda c: (0, 0))],
        out_specs=[tok, state],
        out_shape=[jax.ShapeDtypeStruct((bsz, seq, GROUP), BF16),
                   jax.ShapeDtypeStruct((bsz, N_HEADS, HEAD, HEAD), F32)],
        scratch_shapes=[pltpu.VMEM((bsz, N_HEADS, HEAD, HEAD), F32)],
        compiler_params=_params("arbitrary"),
        name="gdn_scan",
    )(u.reshape(bsz, seq, GROUP), w.reshape(bsz, seq, GROUP), qd.reshape(bsz, seq, GROUP),
      kd.reshape(bsz, seq, GROUP), qk.reshape(N_HEADS, bsz, seq, chunk), gl.reshape(bsz, nc, 1, LANES),
      z.reshape(bsz, seq, GROUP), s0, norm_w)
    return o.reshape(bsz * seq, GROUP), s_final


def _layernorm(x, g, b):
    mu = jnp.mean(x, axis=-1, keepdims=True)
    xc = x - mu
    var = jnp.mean(xc * xc, axis=-1, keepdims=True)
    return xc * lax.rsqrt(var + LN_EPS) * g + b


def _out_router_kernel(*refs, tm, n_sub, n_prompt):
    tiles = [refs[6 * u:6 * u + 6] for u in range(n_sub)]
    wo_ref, g_ref, b_ref, rw_ref, rb_ref, x1_ref, idx_ref, gate_ref, rank_ref, cnt_ref, carry_ref = refs[6 * n_sub:]
    step = pl.program_id(0)
    units = range(n_sub)

    @pl.when(step == 0)
    def _():
        carry_ref[...] = jnp.zeros_like(carry_ref)

    x1s = []
    for u in units:
        attp_ref, atts_ref, op_ref, os_ref, xp_ref, xs_ref = tiles[u]
        prompt = step * n_sub + u < n_prompt
        att = jnp.where(prompt, attp_ref[...], atts_ref[...])
        o = jnp.where(prompt, op_ref[...], os_ref[...])
        x = jnp.where(prompt, xp_ref[...], xs_ref[...])
        mix = (jnp.dot(att, wo_ref[:GROUP, :], preferred_element_type=F32)
               + jnp.dot(o, wo_ref[GROUP:, :], preferred_element_type=F32))
        x1 = _layernorm(DEEPNORM_ALPHA * x + mix, g_ref[...], b_ref[...])
        x1_ref[u * tm:(u + 1) * tm, :] = x1
        x1s.append(x1)

    works = []
    lane = lax.broadcasted_iota(jnp.int32, (tm, LANES), 1)
    for x1 in x1s:
        x_hi = x1.astype(BF16)
        x_lo = (x1 - x_hi.astype(F32)).astype(BF16)
        logits = (jnp.dot(x_hi, rw_ref[0], preferred_element_type=F32)
                  + jnp.dot(x_lo, rw_ref[0], preferred_element_type=F32)
                  + jnp.dot(x_hi, rw_ref[1], preferred_element_type=F32)) + rb_ref[...]
        works.append(jnp.where(lane < N_EXPERTS, logits, -jnp.inf))

    vals, idxs = [[] for _ in units], [[] for _ in units]
    for _ in range(TOP_K):
        for u in units:
            m = jnp.max(works[u], axis=-1, keepdims=True)
            am = jnp.min(jnp.where(works[u] == m, lane, LANES), axis=-1, keepdims=True)
            vals[u].append(m)
            idxs[u].append(am)
            works[u] = jnp.where(lane == am, -jnp.inf, works[u])

    ri = lax.broadcasted_iota(jnp.int32, (tm, tm), 0)
    ci = lax.broadcasted_iota(jnp.int32, (tm, tm), 1)
    before = jnp.where(ri > ci, 1.0, 0.0).astype(BF16)
    base = carry_ref[...]
    for u in units:
        exps = [jnp.exp(v - vals[u][0]) for v in vals[u]]
        denom = exps[0] + exps[1] + exps[2] + exps[3]
        chosen = jnp.zeros((tm, LANES), F32)
        gate_out = jnp.zeros((tm, LANES), F32)
        idx_out = jnp.zeros((tm, LANES), jnp.int32)
        for k in range(TOP_K):
            chosen = jnp.where(lane == idxs[u][k], 1.0, chosen)
            gate_out = jnp.where(lane == k, exps[k] / denom, gate_out)
            idx_out = jnp.where(lane == k, idxs[u][k], idx_out)
        prefix = jnp.dot(before, chosen.astype(BF16), preferred_element_type=F32) + base
        base = base + jnp.sum(chosen, axis=0, keepdims=True)
        rank_out = jnp.zeros((tm, LANES), F32)
        for k in range(TOP_K):
            r = jnp.sum(jnp.where(lane == idxs[u][k], prefix, 0.0), axis=-1, keepdims=True)
            rank_out = jnp.where(lane == k, r, rank_out)
        idx_ref[0, :, u * tm:(u + 1) * tm] = idx_out.T[:SUBLANES, :]
        gate_ref[u * tm:(u + 1) * tm, :] = gate_out
        rank_ref[0, :, u * tm:(u + 1) * tm] = rank_out.astype(jnp.int32).T[:SUBLANES, :]
    carry_ref[...] = base
    cnt_ref[...] = base.astype(jnp.int32)


def _out_router(streams, w_out_bf, ln_g, ln_b, router_w, router_b, tm, n_sub):
    (att_p, o_p, x_p), (att_s, o_s, x_s) = streams
    n_prompt = x_p.shape[0] // tm
    n_sample = x_s.shape[0] // tm
    t = x_p.shape[0] + x_s.shape[0]
    step_rows = tm * n_sub
    row = lambda i: (i, 0)
    const = lambda i: (0, 0)
    slots = pl.BlockSpec((1, SUBLANES, step_rows), lambda i: (i, 0, 0))
    tile_specs, tile_args = [], []
    for u in range(n_sub):
        prow = lambda i, u=u: (jnp.minimum(i * n_sub + u, n_prompt - 1), 0)
        srow = lambda i, u=u: (jnp.clip(i * n_sub + u - n_prompt, 0, n_sample - 1), 0)
        tile_specs += [pl.BlockSpec((tm, GROUP), prow), pl.BlockSpec((tm, GROUP), srow),
                       pl.BlockSpec((tm, GROUP), prow), pl.BlockSpec((tm, GROUP), srow),
                       pl.BlockSpec((tm, D_MODEL), prow), pl.BlockSpec((tm, D_MODEL), srow)]
        tile_args += [att_p, att_s, o_p, o_s, x_p, x_s]
    rw = jnp.pad(router_w, ((0, 0), (0, LANES - N_EXPERTS)))
    rw_hi = rw.astype(BF16)
    rw = jnp.stack([rw_hi, (rw - rw_hi.astype(F32)).astype(BF16)])
    rb = jnp.pad(router_b.reshape(1, N_EXPERTS), ((0, 0), (0, LANES - N_EXPERTS)))
    kern = functools.partial(_out_router_kernel, tm=tm, n_sub=n_sub, n_prompt=n_prompt)
    return pl.pallas_call(
        kern,
        grid=(t // step_rows,),
        in_specs=tile_specs + [pl.BlockSpec((2 * GROUP, D_MODEL), const),
                               pl.BlockSpec((1, D_MODEL), const), pl.BlockSpec((1, D_MODEL), const),
                               pl.BlockSpec((2, D_MODEL, LANES), lambda i: (0, 0, 0)),
                               pl.BlockSpec((1, LANES), const)],
        out_specs=[pl.BlockSpec((step_rows, D_MODEL), row), slots, pl.BlockSpec((step_rows, LANES), row), slots,
                   pl.BlockSpec((1, LANES), const)],
        out_shape=[jax.ShapeDtypeStruct((t, D_MODEL), F32),
                   jax.ShapeDtypeStruct((t // step_rows, SUBLANES, step_rows), jnp.int32),
                   jax.ShapeDtypeStruct((t, LANES), F32),
                   jax.ShapeDtypeStruct((t // step_rows, SUBLANES, step_rows), jnp.int32),
                   jax.ShapeDtypeStruct((1, LANES), jnp.int32)],
        scratch_shapes=[pltpu.VMEM((1, LANES), F32)],
        compiler_params=_params("arbitrary"),
        name="out_router",
    )(*tile_args, w_out_bf, ln_g.reshape(1, D_MODEL), ln_b.reshape(1, D_MODEL), rw, rb)


def _row_copy(src, src_row, dst, dst_row, sem):
    return pltpu.make_async_copy(src.at[pl.ds(src_row, 1), :], dst.at[pl.ds(dst_row, 1), :], sem)


def _dispatch_kernel(ps_ref, pe_ref, dest_ref, x_ref, xb_hbm, zero_ref, sem, zsem, *, tm, bm):
    i = pl.program_id(0)

    def zero_block(row):
        return pltpu.make_async_copy(zero_ref, xb_hbm.at[pl.ds(pl.multiple_of(row, bm), bm), :], zsem)

    def fill(e):
        return zero_block(pe_ref[e] - bm)

    @pl.when(i == 0)
    def _():
        zero_ref[...] = jnp.zeros_like(zero_ref)
        for e in range(N_EXPERTS):
            @pl.when(pe_ref[e] > ps_ref[e])
            def _():
                fill(e).start()
        first_unused = pe_ref[N_EXPERTS - 1] // bm
        n_blocks = xb_hbm.shape[0] // bm
        lax.fori_loop(first_unused, n_blocks, lambda b, c: (zero_block(b * bm).start(), c)[1], 0)
        for e in range(N_EXPERTS):
            @pl.when(pe_ref[e] > ps_ref[e])
            def _():
                fill(e).wait()
        lax.fori_loop(first_unused, n_blocks, lambda b, c: (zero_block(b * bm).wait(), c)[1], 0)

    def issue(r, carry):
        for k in range(TOP_K):
            _row_copy(x_ref, r, xb_hbm, dest_ref[0, 0, r * TOP_K + k], sem).start(priority=k % N_DMA_PRIORITIES)
        return carry

    lax.fori_loop(0, tm, issue, 0)
    pltpu.make_async_copy(xb_hbm.at[pl.ds(0, tm * TOP_K), :], xb_hbm.at[pl.ds(0, tm * TOP_K), :], sem).wait()


def _dispatch(x1, dest, pad_start, pad_end, rows, tm, bm):
    t = x1.shape[0]
    return pl.pallas_call(
        functools.partial(_dispatch_kernel, tm=tm, bm=bm),
        grid_spec=pltpu.PrefetchScalarGridSpec(
            num_scalar_prefetch=2,
            grid=(t // tm,),
            in_specs=[pl.BlockSpec((1, 1, tm * TOP_K), lambda i, ps, pe: (i, 0, 0), memory_space=pltpu.SMEM),
                      pl.BlockSpec((tm, D_MODEL), lambda i, ps, pe: (i, 0))],
            out_specs=pl.BlockSpec(memory_space=pl.ANY),
            scratch_shapes=[pltpu.VMEM((bm, D_MODEL), F32),
                            pltpu.SemaphoreType.DMA(()), pltpu.SemaphoreType.DMA(())]),
        out_shape=jax.ShapeDtypeStruct((rows, D_MODEL), F32),
        compiler_params=_params("arbitrary"),
        name="dispatch",
    )(pad_start, pad_end, dest.reshape(t // tm, 1, tm * TOP_K), x1)


def _expert_kernel(ps_ref, pe_ref, x_ref, wgu_hbm, bgu_ref, wd_hbm, bd_ref, y_ref,
                   wgu_buf, wd_buf, sem, state_ref, *, bm):
    blk = pl.program_id(0)
    row0 = blk * bm

    def fetch(e, s):
        return (pltpu.make_async_copy(wgu_hbm.at[e], wgu_buf.at[s], sem.at[s]),
                pltpu.make_async_copy(wd_hbm.at[e], wd_buf.at[s], sem.at[s]))

    def next_with_rows(e):
        return lax.while_loop(lambda n: (n < N_EXPERTS) & (pe_ref[jnp.minimum(n, N_EXPERTS - 1)] <= row0),
                              lambda n: n + 1, e)

    @pl.when(row0 < pe_ref[N_EXPERTS - 1])
    def _():
        @pl.when(blk == 0)
        def _():
            first = next_with_rows(0)
            state_ref[0] = first
            state_ref[1] = 0
            for copy in fetch(first, 0):
                copy.start()

        @pl.when((blk > 0) & (row0 >= pe_ref[state_ref[0]]))
        def _():
            state_ref[0] = next_with_rows(state_ref[0])
            state_ref[1] = 1 - state_ref[1]

        e, s = state_ref[0], state_ref[1]

        @pl.when(row0 == ps_ref[e])
        def _():
            for copy in fetch(e, s):
                copy.wait()
            nxt = lax.while_loop(lambda n: (n < N_EXPERTS) & (pe_ref[jnp.minimum(n, N_EXPERTS - 1)] <= pe_ref[e]),
                                 lambda n: n + 1, e + 1)

            @pl.when(nxt < N_EXPERTS)
            def _():
                for copy in fetch(nxt, 1 - s):
                    copy.start()

        x = x_ref[...].astype(BF16)
        h = jnp.dot(x, wgu_buf[s].astype(BF16), preferred_element_type=F32) + bgu_ref[pl.ds(e, 1), :]
        gate = jnp.minimum(h[:, :D_FF], SWIGLU_LIMIT)
        up = jnp.clip(h[:, D_FF:], -SWIGLU_LIMIT, SWIGLU_LIMIT)
        act = (up + 1.0) * (gate * jax.nn.sigmoid(SWIGLU_ALPHA * gate))
        y_ref[...] = (jnp.dot(act.astype(BF16), wd_buf[s].astype(BF16), preferred_element_type=F32)
                      + bd_ref[pl.ds(e, 1), :])

    @pl.when(row0 >= pe_ref[N_EXPERTS - 1])
    def _():
        y_ref[...] = jnp.zeros_like(y_ref)


def _experts(xb, pad_start, pad_end, w_gu, b_gu, w_down, b_down, bm):
    rows = xb.shape[0]
    n_blocks = rows // bm
    used = lambda i, ps, pe: (jnp.maximum(jnp.minimum(i, pe[N_EXPERTS - 1] // bm - 1), 0), 0)
    whole = lambda i, ps, pe: (0, 0)
    return pl.pallas_call(
        functools.partial(_expert_kernel, bm=bm),
        grid_spec=pltpu.PrefetchScalarGridSpec(
            num_scalar_prefetch=2,
            grid=(n_blocks,),
            in_specs=[pl.BlockSpec((bm, D_MODEL), used),
                      pl.BlockSpec(memory_space=pl.ANY),
                      pl.BlockSpec((N_EXPERTS, 2 * D_FF), whole),
                      pl.BlockSpec(memory_space=pl.ANY),
                      pl.BlockSpec((N_EXPERTS, D_MODEL), whole)],
            out_specs=pl.BlockSpec((bm, D_MODEL), lambda i, ps, pe: (i, 0)),
            scratch_shapes=[pltpu.VMEM((2, D_MODEL, 2 * D_FF), F32), pltpu.VMEM((2, D_FF, D_MODEL), F32),
                            pltpu.SemaphoreType.DMA((2,)), pltpu.SMEM((2,), jnp.int32)]),
        out_shape=jax.ShapeDtypeStruct((rows, D_MODEL), F32),
        compiler_params=_params("arbitrary"),
        name="experts",
    )(pad_start, pad_end, xb, w_gu, b_gu, w_down, b_down)


def _combine_ln_kernel(dcur_ref, dnext_ref, x_ref, gate_ref, g_ref, b_ref, yb_hbm, op_ref, os_ref, ybuf, sem, *,
                       tm, n_prompt):
    i = pl.program_id(0)
    slot = i % 2

    def issue(dref, s):
        def body(r, carry):
            for k in range(TOP_K):
                _row_copy(yb_hbm, dref[0, 0, r * TOP_K + k], ybuf.at[s, k], r,
                          sem.at[s]).start(priority=k % N_DMA_PRIORITIES)
            return carry
        lax.fori_loop(0, tm, body, 0)

    @pl.when(i == 0)
    def _():
        issue(dcur_ref, 0)

    for s in range(2):
        @pl.when((i + 1 < pl.num_programs(0)) & (slot != s))
        def _():
            issue(dnext_ref, s)

    pltpu.make_async_copy(ybuf.at[slot], ybuf.at[slot], sem.at[slot]).wait()
    gates = gate_ref[...]
    y = sum(gates[:, k:k + 1] * ybuf[slot, k] for k in range(TOP_K))
    out = _layernorm(DEEPNORM_ALPHA * x_ref[...] + y, g_ref[...], b_ref[...])

    @pl.when(i < n_prompt)
    def _():
        op_ref[...] = out

    @pl.when(i >= n_prompt)
    def _():
        os_ref[...] = out


def _combine_ln(x1, gates, dest, yb, ln_g, ln_b, tm, t_prompt):
    t = x1.shape[0]
    n = t // tm
    n_prompt = t_prompt // tm
    row = lambda i: (i, 0)
    const = lambda i: (0, 0)
    d2 = dest.reshape(n, 1, tm * TOP_K)
    return pl.pallas_call(
        functools.partial(_combine_ln_kernel, tm=tm, n_prompt=n_prompt),
        grid=(n,),
        in_specs=[pl.BlockSpec((1, 1, tm * TOP_K), lambda i: (i, 0, 0), memory_space=pltpu.SMEM),
                  pl.BlockSpec((1, 1, tm * TOP_K), lambda i: (jnp.minimum(i + 1, n - 1), 0, 0),
                               memory_space=pltpu.SMEM),
                  pl.BlockSpec((tm, D_MODEL), row),
                  pl.BlockSpec((tm, LANES), row),
                  pl.BlockSpec((1, D_MODEL), const), pl.BlockSpec((1, D_MODEL), const),
                  pl.BlockSpec(memory_space=pl.ANY)],
        out_specs=[pl.BlockSpec((tm, D_MODEL), lambda i: (jnp.minimum(i, n_prompt - 1), 0)),
                   pl.BlockSpec((tm, D_MODEL), lambda i: (jnp.maximum(i - n_prompt, 0), 0))],
        out_shape=[jax.ShapeDtypeStruct((t_prompt, D_MODEL), F32),
                   jax.ShapeDtypeStruct((t - t_prompt, D_MODEL), F32)],
        scratch_shapes=[pltpu.VMEM((2, TOP_K, tm, D_MODEL), F32), pltpu.SemaphoreType.DMA((2,))],
        compiler_params=_params("arbitrary"),
        name="combine_ln",
    )(d2, d2, x1, gates, ln_g.reshape(1, D_MODEL), ln_b.reshape(1, D_MODEL), yb)


MOE_BM = 512
ATTN_BLK = 512
QUERY_PART = 256
GDN_SUB = 4
GDN_GROUPS = 2
ROUTE_TM = 128


def _mixers(x2d, bsz, seq, cache_k, cache_v, conv_past, s0, lam, lam_init, w_in_bf, conv_w, a_log, dt_bias,
            delta_norm_w, subln_w):
    prompt = cache_k is None
    tm = ATTN_BLK if prompt else x2d.shape[0]
    past8 = jnp.pad(conv_past, ((0, 0), (SUBLANES - (CONV_W - 1), 0), (0, 0)))
    q_bf, k_f, v_f, k_bf, v_bf, cin, z, ab, *ctail = _in_proj(x2d, w_in_bf, past8, conv_w, tm,
                                                              seq // tm if prompt else 1, prompt)
    if prompt:
        att = _attn_prompt(q_bf, k_bf, v_bf, lam, subln_w, bsz, seq, ATTN_BLK, lam_init)
        conv_new = ctail[0][:, SUBLANES - (CONV_W - 1):]
    else:
        att = _attn_sample(q_bf, k_bf, v_bf, cache_k, cache_v, 0, lam, subln_w.reshape(1, HEAD), bsz, seq, lam_init)
        conv_new = cin.reshape(bsz, seq, CONV_CH)[:, seq - (CONV_W - 1):]
    chunk = CHUNK if seq % CHUNK == 0 else seq
    n_sub = GDN_SUB if (seq // chunk) % GDN_SUB == 0 else 1
    n_grp = GDN_GROUPS if (seq // (chunk * n_sub)) % GDN_GROUPS == 0 else 1
    u, w, qd, kd, qk, gl = _gdn_prep(cin, ab, past8, conv_w, a_log, dt_bias, bsz, seq, chunk, n_sub, n_grp, prompt)
    o, s_new = _gdn_scan(u, w, qd, kd, qk, gl, z, s0, delta_norm_w.reshape(1, HEAD), bsz, seq, chunk, n_sub)
    return att, o, k_f, v_f, conv_new, s_new


def _moe(x1, idx, gates, rank, counts, w_gu, b_gu, w_down, b_down, ln_g, ln_b, bm, tm, t_prompt):
    t = x1.shape[0]
    n = t * TOP_K
    counts = counts[0, :N_EXPERTS]
    padded = (counts + bm - 1) // bm * bm
    pad_end = jnp.cumsum(padded).astype(jnp.int32)
    pad_start = (pad_end - padded).astype(jnp.int32)
    idx = idx[:, :TOP_K, :]
    dest = rank[:, :TOP_K, :] + sum(jnp.where(idx == e, pad_start[e], 0) for e in range(N_EXPERTS))
    dest = dest.transpose(0, 2, 1).reshape(t, TOP_K)
    n_blocks = -(-n // bm) + N_EXPERTS
    xb = _dispatch(x1, dest, pad_start, pad_end, n_blocks * bm, tm, bm)
    yb = _experts(xb, pad_start, pad_end, w_gu, b_gu, w_down, b_down, bm)
    return _combine_ln(x1, gates, dest, yb, ln_g, ln_b, ROUTE_TM, t_prompt)


def kernel(x_prompt, x_sample, cache_k, cache_v, state_conv, state_delta, w_in, conv_w, a_log, dt_bias,
           delta_norm_w, lambda_q1, lambda_k1, lambda_q2, lambda_k2, subln_w, w_out, ln1_g, ln1_b,
           router_w, router_b, w_gu, b_gu, w_down, b_down, ln2_g, ln2_b):
    bp, lp, _ = x_prompt.shape
    bs, ls, _ = x_sample.shape
    l = 0
    lam_init = 0.8 - 0.6 * math.exp(-0.3 * l)
    lam = (jnp.exp(jnp.sum(lambda_q1[l] * lambda_k1[l])) - jnp.exp(jnp.sum(lambda_q2[l] * lambda_k2[l]))
           + lam_init).reshape(1).astype(F32)
    w_in_bf = jnp.pad(w_in[l], ((0, 0), (0, IN_COLS_PAD - IN_COLS))).astype(BF16)
    shared = (lam, lam_init, w_in_bf, conv_w[l], a_log[l], dt_bias[l], delta_norm_w[l], subln_w[l])

    xp = x_prompt.reshape(bp * lp, D_MODEL)
    xs = x_sample.reshape(bs * ls, D_MODEL)
    zero_conv = jnp.zeros((bp, CONV_W - 1, CONV_CH), F32)
    zero_s = jnp.zeros((bp, N_HEADS, HEAD, HEAD), F32)
    att_p, o_p, k_p, v_p, cin_p, s_p = _mixers(xp, bp, lp, None, None, zero_conv, zero_s, *shared)
    att_s, o_s, k_s, v_s, cin_s, s_s = _mixers(xs, bs, ls, cache_k, cache_v, state_conv[l],
                                               state_delta[l], *shared)

    n_sub = next(c for c in (4, 3, 2, 1) if ((bp * lp + bs * ls) // ROUTE_TM) % c == 0)
    x1, idx, gates, rank, counts = _out_router(((att_p, o_p, xp), (att_s, o_s, xs)), w_out[l].astype(BF16),
                                               ln1_g[l], ln1_b[l], router_w[l], router_b[l], ROUTE_TM, n_sub)
    tp = bp * lp
    tm = n_sub * ROUTE_TM
    y_p, y_s = _moe(x1, idx, gates, rank, counts, w_gu[l], b_gu[l], w_down[l], b_down[l], ln2_g[l], ln2_b[l],
                    MOE_BM, tm, tp)
    return (y_p.reshape(bp, lp, D_MODEL), y_s.reshape(bs, ls, D_MODEL),
            k_p.reshape(1, bp, lp, N_HEADS, HEAD), v_p.reshape(1, bp, lp, N_HEADS, HEAD),
            cin_p[None], s_p[None].astype(state_delta.dtype),
            k_s.reshape(1, bs, ls, N_HEADS, HEAD), v_s.reshape(1, bs, ls, N_HEADS, HEAD),
            cin_s[None], s_s[None].astype(state_delta.dtype))
```

```python
import functools
import math

import jax
import jax.numpy as jnp
from jax import lax
from jax.experimental import pallas as pl
from jax.experimental.pallas import tpu as pltpu

F32 = jnp.float32
BF16 = jnp.bfloat16

D_MODEL = 1024
HEAD = 128
N_HEADS = 4
DQK = HEAD // 2
GROUP = N_HEADS * HEAD
CONV_W = 4
CONV_CH = 3 * GROUP
CHUNK = 64
ALIBI_MAX = 8.0
N_EXPERTS = 32
TOP_K = 4
D_FF = D_MODEL
SWIGLU_LIMIT = 7.0
SWIGLU_ALPHA = 1.702
DEPTH = 1
DEEPNORM_ALPHA = (2 * DEPTH) ** 0.25
LN_EPS = 1e-5
SUBLN_EPS = 1e-5
GATED_NORM_EPS = 1e-6
L2_EPS = 1e-6

LANES = 128
SUBLANES = 8
BF16_EXACT_INT = 256
BF16_ROWS = 16
LOG2E = 1.4426950408889634
N_POS = 6
ONES_ROWS = BF16_ROWS
N_DMA_PRIORITIES = 2
VMEM_LIMIT = 56 * 1024 * 1024

COL_Q, COL_K, COL_V, COL_CONV = 0, GROUP, 2 * GROUP, 3 * GROUP
COL_Z = COL_CONV + CONV_CH
COL_AB = COL_Z + GROUP
IN_COLS = COL_AB + 2 * N_HEADS
IN_COLS_PAD = COL_AB + LANES


def _params(*sem):
    return pltpu.CompilerParams(dimension_semantics=sem, vmem_limit_bytes=VMEM_LIMIT)


def _conv_activation(prev, raw, cw_ref):
    rows = raw.shape[0]
    xin = jnp.concatenate([prev, raw], axis=0)
    conv = sum(xin[SUBLANES - (CONV_W - 1) + j: SUBLANES - (CONV_W - 1) + j + rows] * cw_ref[j:j + 1, :]
               for j in range(CONV_W))
    conv = conv * jax.nn.sigmoid(conv)
    parts = []
    for h in range(2 * N_HEADS):
        x = conv[:, h * HEAD:(h + 1) * HEAD]
        scale = HEAD ** -0.5 if h < N_HEADS else 1.0
        parts.append(x * (lax.rsqrt(jnp.sum(x * x, axis=-1, keepdims=True) + L2_EPS) * scale))
    return jnp.concatenate(parts + [conv[:, 2 * GROUP:]], axis=1)


def _in_proj_kernel(x_ref, w_ref, past_ref, cw_ref, q_ref, kf_ref, vf_ref, kb_ref, vb_ref, c_ref, z_ref, ab_ref,
                    *conv_refs, tm, tiles_per_stream, prompt):
    xb = x_ref[...].astype(BF16)

    def section(lo, hi):
        return jnp.dot(xb, w_ref[:, lo:hi], preferred_element_type=F32)

    q = section(COL_Q, COL_K) * (DQK ** -0.5 * (LOG2E if prompt else 1.0))
    k = section(COL_K, COL_V)
    kf_ref[0] = k.reshape(tm, N_HEADS, HEAD)
    kb_ref[...] = k.astype(BF16)
    v = section(COL_V, COL_CONV)
    vf_ref[0] = v.reshape(tm, N_HEADS, HEAD)
    raw = section(COL_CONV, COL_Z)
    if prompt:
        ctail_ref, tail_ref = conv_refs
        q_ref[0] = q.T.astype(BF16)
        vb_ref[0] = v.T.astype(BF16)
        prev = jnp.where(pl.program_id(0) % tiles_per_stream == 0, past_ref[0], tail_ref[...])
        c_ref[...] = _conv_activation(prev, raw, cw_ref)
        tail_ref[...] = raw[tm - SUBLANES:, :]
        ctail_ref[0] = raw[tm - SUBLANES:, :]
    else:
        q_ref[...] = q.astype(BF16)
        vb_ref[...] = v.astype(BF16)
        c_ref[...] = raw
    z_ref[...] = section(COL_Z, COL_AB)
    ab_ref[...] = section(COL_AB, IN_COLS_PAD)


def _in_proj(x2d, w_bf, past8, conv_w, tm, tiles_per_stream, prompt):
    t = x2d.shape[0]
    row = lambda i: (i, 0)
    stream = lambda i: (i // tiles_per_stream, 0, 0)
    widths = (GROUP, GROUP, GROUP, GROUP, GROUP, CONV_CH, GROUP, LANES)
    dtypes = (BF16, F32, F32, BF16, BF16, F32, F32, F32)
    out_specs = [pl.BlockSpec((tm, w), row) for w in widths]
    out_shape = [jax.ShapeDtypeStruct((t, w), d) for w, d in zip(widths, dtypes)]
    for slot in (1, 2):
        out_specs[slot] = pl.BlockSpec((1, tm, N_HEADS, HEAD), lambda i: (i, 0, 0, 0))
        out_shape[slot] = jax.ShapeDtypeStruct((t // tm, tm, N_HEADS, HEAD), F32)
    if prompt:
        for slot in (0, 4):
            out_specs[slot] = pl.BlockSpec((1, GROUP, tm), lambda i: (i, 0, 0))
            out_shape[slot] = jax.ShapeDtypeStruct((t // tm, GROUP, tm), BF16)
        out_specs.append(pl.BlockSpec((1, SUBLANES, CONV_CH), stream))
        out_shape.append(jax.ShapeDtypeStruct((past8.shape[0], SUBLANES, CONV_CH), F32))
    return pl.pallas_call(
        functools.partial(_in_proj_kernel, tm=tm, tiles_per_stream=tiles_per_stream, prompt=prompt),
        grid=(t // tm,),
        in_specs=[pl.BlockSpec((tm, D_MODEL), row),
                  pl.BlockSpec((D_MODEL, IN_COLS_PAD), lambda i: (0, 0)),
                  pl.BlockSpec((1, SUBLANES, CONV_CH), stream),
                  pl.BlockSpec((CONV_W, CONV_CH), lambda i: (0, 0))],
        out_specs=out_specs,
        out_shape=out_shape,
        scratch_shapes=[pltpu.VMEM((SUBLANES, CONV_CH), F32)] if prompt else [],
        compiler_params=_params("arbitrary"),
        name="in_proj",
    )(x2d, w_bf, past8, conv_w)


def _alibi_slopes():
    return [2.0 ** (-ALIBI_MAX * (h + 1) / N_HEADS) for h in range(N_HEADS)]


def _stack_halves(q):
    lane = lax.broadcasted_iota(jnp.int32, q.shape, 1)
    zero = jnp.zeros_like(q)
    return jnp.concatenate([jnp.where(lane < DQK, q, zero), jnp.where(lane < DQK, zero, q)], axis=0)


def _diff_norm(acc, l, lam, w, lam_init, rows):
    o = acc[:rows] / l[:rows] - lam * (acc[rows:] / l[rows:])
    ms = jnp.mean(o * o, axis=-1, keepdims=True)
    return o * lax.rsqrt(ms + SUBLN_EPS) * w * (1.0 - lam_init)


def _head_slope(h):
    s = _alibi_slopes()
    return jnp.where(h == 0, s[0], jnp.where(h == 1, s[1], jnp.where(h == 2, s[2], s[3]))).astype(F32)


def _attn_prompt_kernel(lam_ref, qt_ref, k_ref, vt_ref, w_ref, o_ref,
                        diag_ref, kaug_ref, vaug_ref, qz_ref, s0_ref, s1_ref, p0_ref, p1_ref, mx0_ref, mx1_ref, m_ref, acc_ref, *, blk, lam_init):
    i = pl.program_id(2)
    slope = _head_slope(pl.program_id(1)) * LOG2E
    rows = 2 * blk
    n_kv = k_ref.shape[0] // blk

    def pieces(x):
        lo = x % BF16_EXACT_INT
        return _split3(slope * lo.astype(F32)) + _split3(slope * (x - lo).astype(F32))

    @pl.when(i == 0)
    def _():
        lane = lax.broadcasted_iota(jnp.int32, (blk, HEAD), 1)
        extra = jnp.where(lane < N_POS, 1.0, 0.0).astype(BF16)
        for n, piece in enumerate(pieces(lax.broadcasted_iota(jnp.int32, (blk, 1), 0))):
            extra = jnp.where(lane == N_POS + n, piece, extra)
        ones = jnp.ones((ONES_ROWS, blk), BF16)
        c = lax.broadcasted_iota(jnp.int32, (blk, rows), 0)
        a = lax.broadcasted_iota(jnp.int32, (blk, rows), 1) % blk
        diag_ref[...] = jnp.where(c // CHUNK <= a // CHUNK, -slope * jnp.abs(a - c).astype(F32), -jnp.inf)
        r = lax.broadcasted_iota(jnp.int32, (HEAD, rows), 0)
        qx = jnp.where((r >= N_POS) & (r < 2 * N_POS), 1.0, 0.0).astype(BF16)
        for n, piece in enumerate(pieces(-(lax.broadcasted_iota(jnp.int32, (1, rows), 1) % blk))):
            qx = jnp.where(r == n, piece, qx)
        qz_ref[HEAD:, :] = qx

        def fill(j, carry):
            j0 = pl.multiple_of(j * blk, blk)
            kaug_ref[pl.ds(j0, blk), :HEAD] = k_ref[pl.ds(j0, blk), :]
            kaug_ref[pl.ds(j0, blk), HEAD:] = extra
            vaug_ref[j, :HEAD, :] = vt_ref[j]
            vaug_ref[j, HEAD:, :] = ones
            return carry

        lax.fori_loop(0, n_kv, fill, 0)

    qt = qt_ref[0]
    d = lax.broadcasted_iota(jnp.int32, qt.shape, 0)
    zero = jnp.zeros_like(qt)
    qz_ref[:HEAD, :] = jnp.concatenate([jnp.where(d < DQK, qt, zero), jnp.where(d < DQK, zero, qt)], axis=1)

    def tile_rows(j):
        return pl.ds(pl.multiple_of(j * blk, blk), blk)

    acc_ref[...] = jnp.zeros_like(acc_ref)
    for part in range(rows // QUERY_PART):
        cols = slice(part * QUERY_PART, (part + 1) * QUERY_PART)
        s = jnp.dot(k_ref[tile_rows(i), :], qz_ref[:HEAD, cols], preferred_element_type=F32) + diag_ref[:, cols]
        m = jnp.max(s, axis=0, keepdims=True)
        p1_ref[:, cols] = jnp.exp2(s - m).astype(BF16)
        m_ref[:, cols] = m
        s_first = jnp.dot(kaug_ref[tile_rows(0), :], qz_ref[:, cols], preferred_element_type=F32)
        s0_ref[:, cols] = s_first
        mx0_ref[:, cols] = jnp.max(s_first, axis=0, keepdims=True)

    def step(j, s_cur, s_nxt, p_cur, p_nxt, mx_cur, mx_nxt):
        shift = -slope * ((i - j) * blk).astype(F32)
        k_next = kaug_ref[tile_rows(jnp.minimum(j + 1, i - 1)), :]
        v_prev = vaug_ref[jnp.where(j == 0, i, j - 1)]
        for part in range(rows // QUERY_PART):
            cols = slice(part * QUERY_PART, (part + 1) * QUERY_PART)
            s_next = jnp.dot(k_next, qz_ref[:, cols], preferred_element_type=F32)
            s_nxt[:, cols] = s_next
            mx_nxt[:, cols] = jnp.max(s_next, axis=0, keepdims=True)
            pv = jnp.dot(v_prev, p_nxt[:, cols], preferred_element_type=F32)
            m_old = m_ref[:, cols]
            m_new = jnp.maximum(m_old, mx_cur[:, cols] + shift)
            p_cur[:, cols] = jnp.exp2(s_cur[:, cols] - (m_new - shift)).astype(BF16)
            acc_ref[:, cols] = (acc_ref[:, cols] + pv) * jnp.exp2(m_old - m_new)
            m_ref[:, cols] = m_new

    def body(jj, carry):
        step(2 * jj, s0_ref, s1_ref, p0_ref, p1_ref, mx0_ref, mx1_ref)

        @pl.when(2 * jj + 1 < i)
        def _():
            step(2 * jj + 1, s1_ref, s0_ref, p1_ref, p0_ref, mx1_ref, mx0_ref)

        return carry

    lax.fori_loop(0, (i + 1) // 2, body, 0)
    p_last = jnp.where(i % 2 == 1, p0_ref[...], p1_ref[...])
    acc = acc_ref[...] + jnp.dot(vaug_ref[jnp.where(i > 0, i - 1, i)], p_last, preferred_element_type=F32)
    l = acc[HEAD:HEAD + 1, :]
    num = acc[:HEAD, :]
    ot = num[:, :blk] / l[:, :blk] - lam_ref[0] * (num[:, blk:] / l[:, blk:])
    ms = jnp.mean(ot * ot, axis=0, keepdims=True)
    ot = ot * lax.rsqrt(ms + SUBLN_EPS) * w_ref[...] * (1.0 - lam_init)
    o_ref[...] = ot.T.astype(o_ref.dtype)


def _attn_prompt(qt_bf, k_bf, vt_bf, lam, subln_w, bsz, seq, blk, lam_init):
    nq = seq // blk
    kern = functools.partial(_attn_prompt_kernel, blk=blk, lam_init=lam_init)
    return pl.pallas_call(
        kern,
        grid=(bsz, N_HEADS, nq),
        in_specs=[pl.BlockSpec(memory_space=pltpu.SMEM),
                  pl.BlockSpec((1, HEAD, blk), lambda b, h, i: (b * nq + i, h, 0)),
                  pl.BlockSpec((seq, HEAD), lambda b, h, i: (b, h)),
                  pl.BlockSpec((nq, HEAD, blk), lambda b, h, i: (b, h, 0)),
                  pl.BlockSpec((HEAD, 1), lambda b, h, i: (0, 0))],
        out_specs=pl.BlockSpec((blk, HEAD), lambda b, h, i: (b * nq + i, h)),
        scratch_shapes=[pltpu.VMEM((blk, 2 * blk), F32),
                        pltpu.VMEM((seq, 2 * HEAD), BF16),
                        pltpu.VMEM((nq, HEAD + ONES_ROWS, blk), BF16),
                        pltpu.VMEM((2 * HEAD, 2 * blk), BF16),
                        pltpu.VMEM((blk, 2 * blk), F32), pltpu.VMEM((blk, 2 * blk), F32),
                        pltpu.VMEM((blk, 2 * blk), BF16), pltpu.VMEM((blk, 2 * blk), BF16),
                        pltpu.VMEM((1, 2 * blk), F32), pltpu.VMEM((1, 2 * blk), F32),
                        pltpu.VMEM((1, 2 * blk), F32),
                        pltpu.VMEM((HEAD + ONES_ROWS, 2 * blk), F32)],
        out_shape=jax.ShapeDtypeStruct((bsz * seq, GROUP), BF16),
        compiler_params=_params("parallel", "parallel", "arbitrary"),
        name="attn_prompt",
    )(lam, qt_bf, k_bf, vt_bf, subln_w.reshape(HEAD, 1))


def _attn_sample_kernel(lam_ref, q_ref, kn_ref, vn_ref, kc_ref, vc_ref, w_ref, o_ref, *, seq, past, lam_init):
    nt = (((1,), (1,)), ((), ()))
    per_head = 2 * seq
    qz = jnp.concatenate([_stack_halves(q_ref[:, h * HEAD:(h + 1) * HEAD]) for h in range(N_HEADS)], axis=0)
    row = lax.broadcasted_iota(jnp.int32, (N_HEADS * per_head, 1), 0)
    q_head = row // per_head
    q_pos = past + row % seq
    slopes = _alibi_slopes()
    slope = jnp.where(q_head == 0, slopes[0], jnp.where(q_head == 1, slopes[1],
                                                        jnp.where(q_head == 2, slopes[2], slopes[3]))).astype(F32)

    def scores(keys, first_pos):
        n = lax.broadcasted_iota(jnp.int32, (1, keys.shape[0]), 1)
        s = lax.dot_general(qz, keys, nt, preferred_element_type=F32)
        bias = -slope * jnp.abs(q_pos - (first_pos + n // N_HEADS)).astype(F32)
        return jnp.where(n % N_HEADS == q_head, s + bias, -jnp.inf)

    s_c = scores(kc_ref[0, 0].reshape(past * N_HEADS, HEAD).astype(BF16), 0)
    s_n = scores(kn_ref[...].reshape(seq * N_HEADS, HEAD), past)
    m = jnp.maximum(jnp.max(s_c, axis=-1, keepdims=True), jnp.max(s_n, axis=-1, keepdims=True))
    p_c = jnp.exp(s_c - m)
    p_n = jnp.exp(s_n - m)
    l = jnp.sum(p_c, axis=-1, keepdims=True) + jnp.sum(p_n, axis=-1, keepdims=True)
    acc = (jnp.dot(p_c.astype(BF16), vc_ref[0, 0].reshape(past * N_HEADS, HEAD).astype(BF16),
                   preferred_element_type=F32)
           + jnp.dot(p_n.astype(BF16), vn_ref[...].reshape(seq * N_HEADS, HEAD), preferred_element_type=F32))
    for h in range(N_HEADS):
        rows = slice(h * per_head, (h + 1) * per_head)
        o_ref[:, h * HEAD:(h + 1) * HEAD] = _diff_norm(acc[rows], l[rows], lam_ref[0], w_ref[...], lam_init,
                                                       seq).astype(o_ref.dtype)


def _attn_sample(q_bf, k_bf, v_bf, cache_k, cache_v, layer, lam, subln_w, bsz, seq, lam_init):
    past = cache_k.shape[2]
    kern = functools.partial(_attn_sample_kernel, seq=seq, past=past, lam_init=lam_init)
    new = pl.BlockSpec((seq, GROUP), lambda b: (b, 0))
    cache = pl.BlockSpec((1, 1, past, N_HEADS, HEAD), lambda b: (layer, b, 0, 0, 0))
    return pl.pallas_call(
        kern,
        grid=(bsz,),
        in_specs=[pl.BlockSpec(memory_space=pltpu.SMEM), new, new, new, cache, cache,
                  pl.BlockSpec((1, HEAD), lambda b: (0, 0))],
        out_specs=new,
        out_shape=jax.ShapeDtypeStruct((bsz * seq, GROUP), BF16),
        compiler_params=_params("parallel"),
        name="attn_sample",
    )(lam, q_bf, k_bf, v_bf, cache_k, cache_v, subln_w)


def _split3(x):
    hi = x.astype(BF16)
    r1 = x - hi.astype(F32)
    mid = r1.astype(BF16)
    lo = (r1 - mid.astype(F32)).astype(BF16)
    return hi, mid, lo


def _gdn_prep_kernel(cin_ref, halo_ref, past_ref, ab_ref, cw_ref, alog_ref, dtb_ref,
                     u_ref, w_ref, qd_ref, kd_ref, qk_ref, gl_ref, *, chunk, n_sub, n_grp, activated):
    c_idx = pl.program_id(1)
    rows = chunk * n_sub
    if activated:
        conv = cin_ref[...]
    else:
        prev = jnp.where(c_idx == 0, past_ref[0], halo_ref[...])
        conv = _conv_activation(prev, cin_ref[...], cw_ref)

    ab = ab_ref[...]
    lane = lax.broadcasted_iota(jnp.int32, ab.shape, 1)
    pre = ab + dtb_ref[...]
    softplus = jnp.maximum(pre, 0.0) + jnp.log(1.0 + jnp.exp(-jnp.abs(pre)))
    g = jnp.where(lane < N_HEADS, -jnp.exp(alog_ref[...]) * softplus, 0.0)
    beta_all = jax.nn.sigmoid(ab)

    ri = lax.broadcasted_iota(jnp.int32, (rows, rows), 0)
    ci = lax.broadcasted_iota(jnp.int32, (rows, rows), 1)
    same = (ri // chunk) == (ci // chunk)
    incl = same & (ri >= ci)
    strict = same & (ri > ci)
    eye = jnp.where(ri == ci, 1.0, 0.0).astype(F32)
    nt = (((1,), (1,)), ((), ()))
    ones_incl = jnp.where(incl, 1.0, 0.0).astype(BF16)
    ones_same = jnp.where(same, 1.0, 0.0).astype(BF16)

    units, t_mats, p_mats, rhs = [], [], [], []
    for grp in range(n_grp):
        rws = slice(grp * rows, (grp + 1) * rows)
        g_parts = _split3(g[rws])
        gc = sum(jnp.dot(ones_incl, part, preferred_element_type=F32) for part in g_parts)
        g_end = sum(jnp.dot(ones_same, part, preferred_element_type=F32) for part in g_parts)
        gct = gc.T
        for sc in range(n_sub):
            gl_ref[grp * n_sub + sc] = g_end[sc * chunk:sc * chunk + 1, :]
        for h in range(N_HEADS):
            col = slice(h * HEAD, (h + 1) * HEAD)
            qh = conv[rws, h * HEAD:(h + 1) * HEAD]
            kh = conv[rws, GROUP + h * HEAD:GROUP + (h + 1) * HEAD]
            vh = conv[rws, 2 * GROUP + h * HEAD:2 * GROUP + (h + 1) * HEAD]
            beta = beta_all[rws, N_HEADS + h:N_HEADS + h + 1]
            gcol = gc[:, h:h + 1]
            grow = gct[h:h + 1, :]
            gamma = jnp.exp(jnp.where(incl, gcol - grow, -jnp.inf))
            egc = jnp.exp(gcol)
            kb = kh * beta
            khb = kh.astype(BF16)
            a = jnp.where(strict, lax.dot_general(kb.astype(BF16), khb, nt, preferred_element_type=F32) * gamma,
                          0.0)
            qk = (lax.dot_general(qh.astype(BF16), khb, nt, preferred_element_type=F32) * gamma).astype(BF16)
            for sc in range(n_sub):
                blk = slice(sc * chunk, (sc + 1) * chunk)
                qk_ref[h, grp * rows + sc * chunk:grp * rows + (sc + 1) * chunk, :] = qk[blk, blk]
            qd_ref[rws, col] = (qh * egc).astype(BF16)
            kd_ref[rws, col] = (kh * jnp.exp(g_end[:, h:h + 1] - gcol)).astype(BF16)
            units.append((rws, col))
            t_mats.append(eye - a)
            p_mats.append(a)
            rhs.append(((vh * beta).astype(BF16), (kb * egc).astype(BF16)))

    for _ in range(int(math.log2(chunk)) - 1):
        for n in range(len(units)):
            pb = p_mats[n].astype(BF16)
            p_mats[n] = jnp.dot(pb, pb, preferred_element_type=F32)
        for n in range(len(units)):
            t_mats[n] = t_mats[n] + jnp.dot(t_mats[n].astype(BF16), p_mats[n].astype(BF16),
                                            preferred_element_type=F32)

    for n, (rws, col) in enumerate(units):
        t_inv = t_mats[n].astype(BF16)
        u_ref[rws, col] = jnp.dot(t_inv, rhs[n][0], preferred_element_type=F32)
        w_ref[rws, col] = jnp.dot(t_inv, rhs[n][1], preferred_element_type=F32).astype(BF16)


def _gdn_prep(cin, ab, past8, conv_w, a_log, dt_bias, bsz, seq, chunk, n_sub, n_grp, activated):
    rows = chunk * n_sub * n_grp
    nblk = seq // rows
    t = bsz * seq
    lanes = lambda v: jnp.pad(v.reshape(1, N_HEADS).astype(F32), ((0, 0), (0, LANES - N_HEADS)))
    kern = functools.partial(_gdn_prep_kernel, chunk=chunk, n_sub=n_sub, n_grp=n_grp, activated=activated)
    rowblk = lambda b, c: (b * nblk + c, 0)
    halo = lambda b, c: (jnp.maximum((b * nblk + c) * (rows // SUBLANES) - 1, 0), 0)
    const = lambda b, c: (0, 0)
    return pl.pallas_call(
        kern,
        grid=(bsz, nblk),
        in_specs=[pl.BlockSpec((rows, CONV_CH), rowblk),
                  pl.BlockSpec((SUBLANES, CONV_CH), halo),
                  pl.BlockSpec((1, SUBLANES, CONV_CH), lambda b, c: (b, 0, 0)),
                  pl.BlockSpec((rows, LANES), rowblk),
                  pl.BlockSpec((CONV_W, CONV_CH), const),
                  pl.BlockSpec((1, LANES), const),
                  pl.BlockSpec((1, LANES), const)],
        out_specs=[pl.BlockSpec((rows, GROUP), rowblk),
                   pl.BlockSpec((rows, GROUP), rowblk),
                   pl.BlockSpec((rows, GROUP), rowblk),
                   pl.BlockSpec((rows, GROUP), rowblk),
                   pl.BlockSpec((N_HEADS, rows, chunk), lambda b, c: (0, b * nblk + c, 0)),
                   pl.BlockSpec((n_sub * n_grp, 1, LANES), lambda b, c: (b * nblk + c, 0, 0))],
        out_shape=[jax.ShapeDtypeStruct((t, GROUP), F32),
                   jax.ShapeDtypeStruct((t, GROUP), BF16),
                   jax.ShapeDtypeStruct((t, GROUP), BF16),
                   jax.ShapeDtypeStruct((t, GROUP), BF16),
                   jax.ShapeDtypeStruct((N_HEADS, t, chunk), BF16),
                   jax.ShapeDtypeStruct((t // chunk, 1, LANES), F32)],
        compiler_params=_params("parallel", "parallel"),
        name="gdn_prep",
    )(cin, cin, past8, ab, conv_w, lanes(a_log), lanes(dt_bias))


def _gdn_scan_kernel(u_ref, w_ref, qd_ref, kd_ref, qk_ref, gl_ref, z_ref, s0_ref, nw_ref,
                     o_ref, sf_ref, s_ref, *, bsz, chunk, n_c):
    c_idx = pl.program_id(0)

    @pl.when(c_idx == 0)
    def _():
        s_ref[...] = s0_ref[...]

    tn = (((0,), (0,)), ((), ()))
    chains = [(b, h, slice(h * HEAD, (h + 1) * HEAD)) for b in range(bsz) for h in range(N_HEADS)]
    states = [s_ref[b, h] for b, h, _ in chains]
    for cc in range(n_c):
        rows = slice(cc * chunk, (cc + 1) * chunk)
        prods = [jnp.dot(jnp.concatenate([w_ref[b, rows, col], qd_ref[b, rows, col]], axis=0), s.astype(BF16),
                         preferred_element_type=F32) for (b, _, col), s in zip(chains, states)]
        v_news = [(u_ref[b, rows, col] - r[:chunk]).astype(BF16) for (b, _, col), r in zip(chains, prods)]
        outs = [r[chunk:] + jnp.dot(qk_ref[h, b, rows], v, preferred_element_type=F32)
                for (b, h, _), r, v in zip(chains, prods, v_news)]
        decays = [jnp.exp(gl_ref[b, cc]) for b in range(bsz)]
        states = [s * decays[b][:, h:h + 1] + lax.dot_general(kd_ref[b, rows, col], v, tn,
                                                             preferred_element_type=F32)
                  for (b, h, col), s, v in zip(chains, states, v_news)]
        for (b, h, col), o in zip(chains, outs):
            zh = z_ref[b, rows, col]
            ms = jnp.mean(o * o, axis=-1, keepdims=True)
            o = o * lax.rsqrt(ms + GATED_NORM_EPS) * nw_ref[...] * (zh * jax.nn.sigmoid(zh))
            o_ref[b, rows, col] = o.astype(o_ref.dtype)
    for (b, h, _), s in zip(chains, states):
        s_ref[b, h] = s

    @pl.when(c_idx == pl.num_programs(0) - 1)
    def _():
        sf_ref[...] = s_ref[...]


def _gdn_scan(u, w, qd, kd, qk, gl, z, s0, norm_w, bsz, seq, chunk, n_c):
    nc = seq // chunk
    kern = functools.partial(_gdn_scan_kernel, bsz=bsz, chunk=chunk, n_c=n_c)
    tok = pl.BlockSpec((bsz, n_c * chunk, GROUP), lambda c: (0, c, 0))
    state = pl.BlockSpec((bsz, N_HEADS, HEAD, HEAD), lambda c: (0, 0, 0, 0))
    o, s_final = pl.pallas_call(
        kern,
        grid=(nc // n_c,),
        in_specs=[tok, tok, tok, tok,
                  pl.BlockSpec((N_HEADS, bsz, n_c * chunk, chunk), lambda c: (0, 0, c, 0)),
                  pl.BlockSpec((bsz, n_c, 1, LANES), lambda c: (0, c, 0, 0)),
                  tok, state,
                  pl.BlockSpec((1, HEAD), lambda c: (0, 0))],
        out_specs=[tok, state],
        out_shape=[jax.ShapeDtypeStruct((bsz, seq, GROUP), BF16),
                   jax.ShapeDtypeStruct((bsz, N_HEADS, HEAD, HEAD), F32)],
        scratch_shapes=[pltpu.VMEM((bsz, N_HEADS, HEAD, HEAD), F32)],
        compiler_params=_params("arbitrary"),
        name="gdn_scan",
    )(u.reshape(bsz, seq, GROUP), w.reshape(bsz, seq, GROUP), qd.reshape(bsz, seq, GROUP),
      kd.reshape(bsz, seq, GROUP), qk.reshape(N_HEADS, bsz, seq, chunk), gl.reshape(bsz, nc, 1, LANES),
      z.reshape(bsz, seq, GROUP), s0, norm_w)
    return o.reshape(bsz * seq, GROUP), s_final


def _layernorm(x, g, b):
    mu = jnp.mean(x, axis=-1, keepdims=True)
    xc = x - mu
    var = jnp.mean(xc * xc, axis=-1, keepdims=True)
    return xc * lax.rsqrt(var + LN_EPS) * g + b


def _out_router_kernel(*refs, tm, n_sub, n_prompt):
    tiles = [refs[6 * u:6 * u + 6] for u in range(n_sub)]
    wo_ref, g_ref, b_ref, rw_ref, rb_ref, x1_ref, idx_ref, gate_ref, rank_ref, cnt_ref, carry_ref = refs[6 * n_sub:]
    step = pl.program_id(0)
    units = range(n_sub)

    @pl.when(step == 0)
    def _():
        carry_ref[...] = jnp.zeros_like(carry_ref)

    x1s = []
    for u in units:
        attp_ref, atts_ref, op_ref, os_ref, xp_ref, xs_ref = tiles[u]
        prompt = step * n_sub + u < n_prompt
        att = jnp.where(prompt, attp_ref[...], atts_ref[...])
        o = jnp.where(prompt, op_ref[...], os_ref[...])
        x = jnp.where(prompt, xp_ref[...], xs_ref[...])
        mix = (jnp.dot(att, wo_ref[:GROUP, :], preferred_element_type=F32)
               + jnp.dot(o, wo_ref[GROUP:, :], preferred_element_type=F32))
        x1 = _layernorm(DEEPNORM_ALPHA * x + mix, g_ref[...], b_ref[...])
        x1_ref[u * tm:(u + 1) * tm, :] = x1
        x1s.append(x1)

    works = []
    lane = lax.broadcasted_iota(jnp.int32, (tm, LANES), 1)
    for x1 in x1s:
        x_hi = x1.astype(BF16)
        x_lo = (x1 - x_hi.astype(F32)).astype(BF16)
        logits = (jnp.dot(x_hi, rw_ref[0], preferred_element_type=F32)
                  + jnp.dot(x_lo, rw_ref[0], preferred_element_type=F32)
                  + jnp.dot(x_hi, rw_ref[1], preferred_element_type=F32)) + rb_ref[...]
        works.append(jnp.where(lane < N_EXPERTS, logits, -jnp.inf))

    vals, idxs = [[] for _ in units], [[] for _ in units]
    for _ in range(TOP_K):
        for u in units:
            m = jnp.max(works[u], axis=-1, keepdims=True)
            am = jnp.min(jnp.where(works[u] == m, lane, LANES), axis=-1, keepdims=True)
            vals[u].append(m)
            idxs[u].append(am)
            works[u] = jnp.where(lane == am, -jnp.inf, works[u])

    ri = lax.broadcasted_iota(jnp.int32, (tm, tm), 0)
    ci = lax.broadcasted_iota(jnp.int32, (tm, tm), 1)
    before = jnp.where(ri > ci, 1.0, 0.0).astype(BF16)
    base = carry_ref[...]
    for u in units:
        exps = [jnp.exp(v - vals[u][0]) for v in vals[u]]
        denom = exps[0] + exps[1] + exps[2] + exps[3]
        chosen = jnp.zeros((tm, LANES), F32)
        gate_out = jnp.zeros((tm, LANES), F32)
        idx_out = jnp.zeros((tm, LANES), jnp.int32)
        for k in range(TOP_K):
            chosen = jnp.where(lane == idxs[u][k], 1.0, chosen)
            gate_out = jnp.where(lane == k, exps[k] / denom, gate_out)
            idx_out = jnp.where(lane == k, idxs[u][k], idx_out)
        prefix = jnp.dot(before, chosen.astype(BF16), preferred_element_type=F32) + base
        base = base + jnp.sum(chosen, axis=0, keepdims=True)
        rank_out = jnp.zeros((tm, LANES), F32)
        for k in range(TOP_K):
            r = jnp.sum(jnp.where(lane == idxs[u][k], prefix, 0.0), axis=-1, keepdims=True)
            rank_out = jnp.where(lane == k, r, rank_out)
        idx_ref[0, :, u * tm:(u + 1) * tm] = idx_out.T[:SUBLANES, :]
        gate_ref[u * tm:(u + 1) * tm, :] = gate_out
        rank_ref[0, :, u * tm:(u + 1) * tm] = rank_out.astype(jnp.int32).T[:SUBLANES, :]
    carry_ref[...] = base
    cnt_ref[...] = base.astype(jnp.int32)


def _out_router(streams, w_out_bf, ln_g, ln_b, router_w, router_b, tm, n_sub):
    (att_p, o_p, x_p), (att_s, o_s, x_s) = streams
    n_prompt = x_p.shape[0] // tm
    n_sample = x_s.shape[0] // tm
    t = x_p.shape[0] + x_s.shape[0]
    step_rows = tm * n_sub
    row = lambda i: (i, 0)
    const = lambda i: (0, 0)
    slots = pl.BlockSpec((1, SUBLANES, step_rows), lambda i: (i, 0, 0))
    tile_specs, tile_args = [], []
    for u in range(n_sub):
        prow = lambda i, u=u: (jnp.minimum(i * n_sub + u, n_prompt - 1), 0)
        srow = lambda i, u=u: (jnp.clip(i * n_sub + u - n_prompt, 0, n_sample - 1), 0)
        tile_specs += [pl.BlockSpec((tm, GROUP), prow), pl.BlockSpec((tm, GROUP), srow),
                       pl.BlockSpec((tm, GROUP), prow), pl.BlockSpec((tm, GROUP), srow),
                       pl.BlockSpec((tm, D_MODEL), prow), pl.BlockSpec((tm, D_MODEL), srow)]
        tile_args += [att_p, att_s, o_p, o_s, x_p, x_s]
    rw = jnp.pad(router_w, ((0, 0), (0, LANES - N_EXPERTS)))
    rw_hi = rw.astype(BF16)
    rw = jnp.stack([rw_hi, (rw - rw_hi.astype(F32)).astype(BF16)])
    rb = jnp.pad(router_b.reshape(1, N_EXPERTS), ((0, 0), (0, LANES - N_EXPERTS)))
    kern = functools.partial(_out_router_kernel, tm=tm, n_sub=n_sub, n_prompt=n_prompt)
    return pl.pallas_call(
        kern,
        grid=(t // step_rows,),
        in_specs=tile_specs + [pl.BlockSpec((2 * GROUP, D_MODEL), const),
                               pl.BlockSpec((1, D_MODEL), const), pl.BlockSpec((1, D_MODEL), const),
                               pl.BlockSpec((2, D_MODEL, LANES), lambda i: (0, 0, 0)),
                               pl.BlockSpec((1, LANES), const)],
        out_specs=[pl.BlockSpec((step_rows, D_MODEL), row), slots, pl.BlockSpec((step_rows, LANES), row), slots,
                   pl.BlockSpec((1, LANES), const)],
        out_shape=[jax.ShapeDtypeStruct((t, D_MODEL), F32),
                   jax.ShapeDtypeStruct((t // step_rows, SUBLANES, step_rows), jnp.int32),
                   jax.ShapeDtypeStruct((t, LANES), F32),
                   jax.ShapeDtypeStruct((t // step_rows, SUBLANES, step_rows), jnp.int32),
                   jax.ShapeDtypeStruct((1, LANES), jnp.int32)],
        scratch_shapes=[pltpu.VMEM((1, LANES), F32)],
        compiler_params=_params("arbitrary"),
        name="out_router",
    )(*tile_args, w_out_bf, ln_g.reshape(1, D_MODEL), ln_b.reshape(1, D_MODEL), rw, rb)


def _row_copy(src, src_row, dst, dst_row, sem):
    return pltpu.make_async_copy(src.at[pl.ds(src_row, 1), :], dst.at[pl.ds(dst_row, 1), :], sem)


def _dispatch_kernel(ps_ref, pe_ref, dest_ref, x_ref, xb_hbm, zero_ref, sem, zsem, *, tm, bm):
    i = pl.program_id(0)

    def zero_block(row):
        return pltpu.make_async_copy(zero_ref, xb_hbm.at[pl.ds(pl.multiple_of(row, bm), bm), :], zsem)

    def fill(e):
        return zero_block(pe_ref[e] - bm)

    @pl.when(i == 0)
    def _():
        zero_ref[...] = jnp.zeros_like(zero_ref)
        for e in range(N_EXPERTS):
            @pl.when(pe_ref[e] > ps_ref[e])
            def _():
                fill(e).start()
        first_unused = pe_ref[N_EXPERTS - 1] // bm
        n_blocks = xb_hbm.shape[0] // bm
        lax.fori_loop(first_unused, n_blocks, lambda b, c: (zero_block(b * bm).start(), c)[1], 0)
        for e in range(N_EXPERTS):
            @pl.when(pe_ref[e] > ps_ref[e])
            def _():
                fill(e).wait()
        lax.fori_loop(first_unused, n_blocks, lambda b, c: (zero_block(b * bm).wait(), c)[1], 0)

    def issue(r, carry):
        for k in range(TOP_K):
            _row_copy(x_ref, r, xb_hbm, dest_ref[0, 0, r * TOP_K + k], sem).start(priority=k % N_DMA_PRIORITIES)
        return carry

    lax.fori_loop(0, tm, issue, 0)
    pltpu.make_async_copy(xb_hbm.at[pl.ds(0, tm * TOP_K), :], xb_hbm.at[pl.ds(0, tm * TOP_K), :], sem).wait()


def _dispatch(x1, dest, pad_start, pad_end, rows, tm, bm):
    t = x1.shape[0]
    return pl.pallas_call(
        functools.partial(_dispatch_kernel, tm=tm, bm=bm),
        grid_spec=pltpu.PrefetchScalarGridSpec(
            num_scalar_prefetch=2,
            grid=(t // tm,),
            in_specs=[pl.BlockSpec((1, 1, tm * TOP_K), lambda i, ps, pe: (i, 0, 0), memory_space=pltpu.SMEM),
                      pl.BlockSpec((tm, D_MODEL), lambda i, ps, pe: (i, 0))],
            out_specs=pl.BlockSpec(memory_space=pl.ANY),
            scratch_shapes=[pltpu.VMEM((bm, D_MODEL), F32),
                            pltpu.SemaphoreType.DMA(()), pltpu.SemaphoreType.DMA(())]),
        out_shape=jax.ShapeDtypeStruct((rows, D_MODEL), F32),
        compiler_params=_params("arbitrary"),
        name="dispatch",
    )(pad_start, pad_end, dest.reshape(t // tm, 1, tm * TOP_K), x1)


def _expert_kernel(ps_ref, pe_ref, x_ref, wgu_hbm, bgu_ref, wd_hbm, bd_ref, y_ref,
                   wgu_buf, wd_buf, sem, state_ref, *, bm):
    blk = pl.program_id(0)
    row0 = blk * bm

    def fetch(e, s):
        return (pltpu.make_async_copy(wgu_hbm.at[e], wgu_buf.at[s], sem.at[s]),
                pltpu.make_async_copy(wd_hbm.at[e], wd_buf.at[s], sem.at[s]))

    def next_with_rows(e):
        return lax.while_loop(lambda n: (n < N_EXPERTS) & (pe_ref[jnp.minimum(n, N_EXPERTS - 1)] <= row0),
                              lambda n: n + 1, e)

    @pl.when(row0 < pe_ref[N_EXPERTS - 1])
    def _():
        @pl.when(blk == 0)
        def _():
            first = next_with_rows(0)
            state_ref[0] = first
            state_ref[1] = 0
            for copy in fetch(first, 0):
                copy.start()

        @pl.when((blk > 0) & (row0 >= pe_ref[state_ref[0]]))
        def _():
            state_ref[0] = next_with_rows(state_ref[0])
            state_ref[1] = 1 - state_ref[1]

        e, s = state_ref[0], state_ref[1]

        @pl.when(row0 == ps_ref[e])
        def _():
            for copy in fetch(e, s):
                copy.wait()
            nxt = lax.while_loop(lambda n: (n < N_EXPERTS) & (pe_ref[jnp.minimum(n, N_EXPERTS - 1)] <= pe_ref[e]),
                                 lambda n: n + 1, e + 1)

            @pl.when(nxt < N_EXPERTS)
            def _():
                for copy in fetch(nxt, 1 - s):
                    copy.start()

        x = x_ref[...].astype(BF16)
        h = jnp.dot(x, wgu_buf[s].astype(BF16), preferred_element_type=F32) + bgu_ref[pl.ds(e, 1), :]
        gate = jnp.minimum(h[:, :D_FF], SWIGLU_LIMIT)
        up = jnp.clip(h[:, D_FF:], -SWIGLU_LIMIT, SWIGLU_LIMIT)
        act = (up + 1.0) * (gate * jax.nn.sigmoid(SWIGLU_ALPHA * gate))
        y_ref[...] = (jnp.dot(act.astype(BF16), wd_buf[s].astype(BF16), preferred_element_type=F32)
                      + bd_ref[pl.ds(e, 1), :])

    @pl.when(row0 >= pe_ref[N_EXPERTS - 1])
    def _():
        y_ref[...] = jnp.zeros_like(y_ref)


def _experts(xb, pad_start, pad_end, w_gu, b_gu, w_down, b_down, bm):
    rows = xb.shape[0]
    n_blocks = rows // bm
    used = lambda i, ps, pe: (jnp.maximum(jnp.minimum(i, pe[N_EXPERTS - 1] // bm - 1), 0), 0)
    whole = lambda i, ps, pe: (0, 0)
    return pl.pallas_call(
        functools.partial(_expert_kernel, bm=bm),
        grid_spec=pltpu.PrefetchScalarGridSpec(
            num_scalar_prefetch=2,
            grid=(n_blocks,),
            in_specs=[pl.BlockSpec((bm, D_MODEL), used),
                      pl.BlockSpec(memory_space=pl.ANY),
                      pl.BlockSpec((N_EXPERTS, 2 * D_FF), whole),
                      pl.BlockSpec(memory_space=pl.ANY),
                      pl.BlockSpec((N_EXPERTS, D_MODEL), whole)],
            out_specs=pl.BlockSpec((bm, D_MODEL), lambda i, ps, pe: (i, 0)),
            scratch_shapes=[pltpu.VMEM((2, D_MODEL, 2 * D_FF), F32), pltpu.VMEM((2, D_FF, D_MODEL), F32),
                            pltpu.SemaphoreType.DMA((2,)), pltpu.SMEM((2,), jnp.int32)]),
        out_shape=jax.ShapeDtypeStruct((rows, D_MODEL), F32),
        compiler_params=_params("arbitrary"),
        name="experts",
    )(pad_start, pad_end, xb, w_gu, b_gu, w_down, b_down)


def _combine_ln_kernel(dcur_ref, dnext_ref, x_ref, gate_ref, g_ref, b_ref, yb_hbm, op_ref, os_ref, ybuf, sem, *,
                       tm, n_prompt):
    i = pl.program_id(0)
    slot = i % 2

    def issue(dref, s):
        def body(r, carry):
            for k in range(TOP_K):
                _row_copy(yb_hbm, dref[0, 0, r * TOP_K + k], ybuf.at[s, k], r,
                          sem.at[s]).start(priority=k % N_DMA_PRIORITIES)
            return carry
        lax.fori_loop(0, tm, body, 0)

    @pl.when(i == 0)
    def _():
        issue(dcur_ref, 0)

    for s in range(2):
        @pl.when((i + 1 < pl.num_programs(0)) & (slot != s))
        def _():
            issue(dnext_ref, s)

    pltpu.make_async_copy(ybuf.at[slot], ybuf.at[slot], sem.at[slot]).wait()
    gates = gate_ref[...]
    y = sum(gates[:, k:k + 1] * ybuf[slot, k] for k in range(TOP_K))
    out = _layernorm(DEEPNORM_ALPHA * x_ref[...] + y, g_ref[...], b_ref[...])

    @pl.when(i < n_prompt)
    def _():
        op_ref[...] = out

    @pl.when(i >= n_prompt)
    def _():
        os_ref[...] = out


def _combine_ln(x1, gates, dest, yb, ln_g, ln_b, tm, t_prompt):
    t = x1.shape[0]
    n = t // tm
    n_prompt = t_prompt // tm
    row = lambda i: (i, 0)
    const = lambda i: (0, 0)
    d2 = dest.reshape(n, 1, tm * TOP_K)
    return pl.pallas_call(
        functools.partial(_combine_ln_kernel, tm=tm, n_prompt=n_prompt),
        grid=(n,),
        in_specs=[pl.BlockSpec((1, 1, tm * TOP_K), lambda i: (i, 0, 0), memory_space=pltpu.SMEM),
                  pl.BlockSpec((1, 1, tm * TOP_K), lambda i: (jnp.minimum(i + 1, n - 1), 0, 0),
                               memory_space=pltpu.SMEM),
                  pl.BlockSpec((tm, D_MODEL), row),
                  pl.BlockSpec((tm, LANES), row),
                  pl.BlockSpec((1, D_MODEL), const), pl.BlockSpec((1, D_MODEL), const),
                  pl.BlockSpec(memory_space=pl.ANY)],
        out_specs=[pl.BlockSpec((tm, D_MODEL), lambda i: (jnp.minimum(i, n_prompt - 1), 0)),
                   pl.BlockSpec((tm, D_MODEL), lambda i: (jnp.maximum(i - n_prompt, 0), 0))],
        out_shape=[jax.ShapeDtypeStruct((t_prompt, D_MODEL), F32),
                   jax.ShapeDtypeStruct((t - t_prompt, D_MODEL), F32)],
        scratch_shapes=[pltpu.VMEM((2, TOP_K, tm, D_MODEL), F32), pltpu.SemaphoreType.DMA((2,))],
        compiler_params=_params("arbitrary"),
        name="combine_ln",
    )(d2, d2, x1, gates, ln_g.reshape(1, D_MODEL), ln_b.reshape(1, D_MODEL), yb)


MOE_BM = 512
ATTN_BLK = 512
QUERY_PART = 256
GDN_SUB = 4
GDN_GROUPS = 2
SCAN_CHUNKS = 8
ROUTE_TM = 128


def _mixers(x2d, bsz, seq, cache_k, cache_v, conv_past, s0, lam, lam_init, w_in_bf, conv_w, a_log, dt_bias,
            delta_norm_w, subln_w):
    prompt = cache_k is None
    tm = ATTN_BLK if prompt else x2d.shape[0]
    past8 = jnp.pad(conv_past, ((0, 0), (SUBLANES - (CONV_W - 1), 0), (0, 0)))
    q_bf, k_f, v_f, k_bf, v_bf, cin, z, ab, *ctail = _in_proj(x2d, w_in_bf, past8, conv_w, tm,
                                                              seq // tm if prompt else 1, prompt)
    if prompt:
        att = _attn_prompt(q_bf, k_bf, v_bf, lam, subln_w, bsz, seq, ATTN_BLK, lam_init)
        conv_new = ctail[0][:, SUBLANES - (CONV_W - 1):]
    else:
        att = _attn_sample(q_bf, k_bf, v_bf, cache_k, cache_v, 0, lam, subln_w.reshape(1, HEAD), bsz, seq, lam_init)
        conv_new = cin.reshape(bsz, seq, CONV_CH)[:, seq - (CONV_W - 1):]
    chunk = CHUNK if seq % CHUNK == 0 else seq
    n_sub = GDN_SUB if (seq // chunk) % GDN_SUB == 0 else 1
    n_grp = GDN_GROUPS if (seq // (chunk * n_sub)) % GDN_GROUPS == 0 else 1
    u, w, qd, kd, qk, gl = _gdn_prep(cin, ab, past8, conv_w, a_log, dt_bias, bsz, seq, chunk, n_sub, n_grp, prompt)
    n_scan = SCAN_CHUNKS if (seq // chunk) % SCAN_CHUNKS == 0 else 1
    o, s_new = _gdn_scan(u, w, qd, kd, qk, gl, z, s0, delta_norm_w.reshape(1, HEAD), bsz, seq, chunk, n_scan)
    return att, o, k_f, v_f, conv_new, s_new


def _moe(x1, idx, gates, rank, counts, w_gu, b_gu, w_down, b_down, ln_g, ln_b, bm, tm, t_prompt):
    t = x1.shape[0]
    n = t * TOP_K
    counts = counts[0, :N_EXPERTS]
    padded = (counts + bm - 1) // bm * bm
    pad_end = jnp.cumsum(padded).astype(jnp.int32)
    pad_start = (pad_end - padded).astype(jnp.int32)
    idx = idx[:, :TOP_K, :]
    dest = rank[:, :TOP_K, :] + sum(jnp.where(idx == e, pad_start[e], 0) for e in range(N_EXPERTS))
    dest = dest.transpose(0, 2, 1).reshape(t, TOP_K)
    n_blocks = -(-n // bm) + N_EXPERTS
    xb = _dispatch(x1, dest, pad_start, pad_end, n_blocks * bm, tm, bm)
    yb = _experts(xb, pad_start, pad_end, w_gu, b_gu, w_down, b_down, bm)
    return _combine_ln(x1, gates, dest, yb, ln_g, ln_b, ROUTE_TM, t_prompt)


def kernel(x_prompt, x_sample, cache_k, cache_v, state_conv, state_delta, w_in, conv_w, a_log, dt_bias,
           delta_norm_w, lambda_q1, lambda_k1, lambda_q2, lambda_k2, subln_w, w_out, ln1_g, ln1_b,
           router_w, router_b, w_gu, b_gu, w_down, b_down, ln2_g, ln2_b):
    bp, lp, _ = x_prompt.shape
    bs, ls, _ = x_sample.shape
    l = 0
    lam_init = 0.8 - 0.6 * math.exp(-0.3 * l)
    lam = (jnp.exp(jnp.sum(lambda_q1[l] * lambda_k1[l])) - jnp.exp(jnp.sum(lambda_q2[l] * lambda_k2[l]))
           + lam_init).reshape(1).astype(F32)
    w_in_bf = jnp.pad(w_in[l], ((0, 0), (0, IN_COLS_PAD - IN_COLS))).astype(BF16)
    shared = (lam, lam_init, w_in_bf, conv_w[l], a_log[l], dt_bias[l], delta_norm_w[l], subln_w[l])

    xp = x_prompt.reshape(bp * lp, D_MODEL)
    xs = x_sample.reshape(bs * ls, D_MODEL)
    zero_conv = jnp.zeros((bp, CONV_W - 1, CONV_CH), F32)
    zero_s = jnp.zeros((bp, N_HEADS, HEAD, HEAD), F32)
    att_p, o_p, k_p, v_p, cin_p, s_p = _mixers(xp, bp, lp, None, None, zero_conv, zero_s, *shared)
    att_s, o_s, k_s, v_s, cin_s, s_s = _mixers(xs, bs, ls, cache_k, cache_v, state_conv[l],
                                               state_delta[l], *shared)

    n_sub = next(c for c in (4, 3, 2, 1) if ((bp * lp + bs * ls) // ROUTE_TM) % c == 0)
    x1, idx, gates, rank, counts = _out_router(((att_p, o_p, xp), (att_s, o_s, xs)), w_out[l].astype(BF16),
                                               ln1_g[l], ln1_b[l], router_w[l], router_b[l], ROUTE_TM, n_sub)
    tp = bp * lp
    tm = n_sub * ROUTE_TM
    y_p, y_s = _moe(x1, idx, gates, rank, counts, w_gu[l], b_gu[l], w_down[l], b_down[l], ln2_g[l], ln2_b[l],
                    MOE_BM, tm, tp)
    return (y_p.reshape(bp, lp, D_MODEL), y_s.reshape(bs, ls, D_MODEL),
            k_p.reshape(1, bp, lp, N_HEADS, HEAD), v_p.reshape(1, bp, lp, N_HEADS, HEAD),
            cin_p[None], s_p[None].astype(state_delta.dtype),
            k_s.reshape(1, bs, ls, N_HEADS, HEAD), v_s.reshape(1, bs, ls, N_HEADS, HEAD),
            cin_s[None], s_s[None].astype(state_delta.dtype))
```
